```python
import math
import jax
import jax.numpy as jnp
from jax import lax
import numpy as np

D_MODEL = 1024
BATCH = 2
SEQ = 16384
DEPTH = 1
DEC_BATCH = 16
DEC_SEQ = 32
PAST_LEN = 1024

CHUNK = 64
MIX = D_MODEL
DN_WIDTH = MIX // 2
S5_WIDTH = MIX - DN_WIDTH
DN_HEADS = 4
DN_DK = DN_WIDTH // DN_HEADS
DN_DV = DN_WIDTH // DN_HEADS
CONV_K = 4
CONV_CH = 2 * DN_HEADS * DN_DK + DN_HEADS * DN_DV
S5_GROUP = 16
S5_GROUPS = S5_WIDTH // S5_GROUP
S5_STATE = 64
IN_COLS = CONV_CH + 2 * DN_HEADS + DN_WIDTH + 2 * S5_WIDTH
EPS = 1e-6
L2_EPS = 1e-6

kernel_name = "hymba_gdn_s5_stream_step"


def rms_norm(x, w):
    xf = x.astype(jnp.float32)
    return xf * lax.rsqrt(jnp.mean(xf * xf, axis=-1, keepdims=True) + EPS) * w.astype(jnp.float32)


def l2_normalize(x):
    return x * lax.rsqrt(jnp.sum(x * x, axis=-1, keepdims=True) + L2_EPS)


def causal_short_conv(u, buf, w):
    T = u.shape[1]
    up = jnp.concatenate([buf, u], axis=1)
    out = up[:, 0:T] * w[0]
    for j in range(1, CONV_K):
        out = out + up[:, j:j + T] * w[j]
    return jax.nn.silu(out), up[:, T:]


def gated_delta_rule(q, k, v, g, beta, s0):
    bsz, T, H, _ = q.shape
    blk = min(CHUNK, T)
    nb = T // blk

    def to_blocks(t):
        return t.reshape(bsz, nb, blk, H, -1).transpose(1, 0, 3, 2, 4)

    q, k, v = to_blocks(q), to_blocks(k), to_blocks(v)
    g = g.reshape(bsz, nb, blk, H).transpose(1, 0, 3, 2)
    beta = beta.reshape(bsz, nb, blk, H).transpose(1, 0, 3, 2)
    gc = jnp.cumsum(g, axis=-1)
    incl = jnp.tril(jnp.ones((blk, blk), dtype=bool))
    strict = jnp.tril(jnp.ones((blk, blk), dtype=bool), k=-1)
    diff = gc[..., :, None] - gc[..., None, :]
    decay = jnp.where(incl, jnp.exp(jnp.where(incl, diff, 0.0)), 0.0)
    kb = k * beta[..., None]
    eye = jnp.eye(blk, dtype=jnp.float32)
    lower = jnp.where(strict, jnp.einsum('nbhid,nbhjd->nbhij', kb, k) * decay, 0.0) + eye
    u = lax.linalg.triangular_solve(lower, v * beta[..., None], left_side=True, lower=True,
                                    unit_diagonal=True)
    w = lax.linalg.triangular_solve(lower, kb * jnp.exp(gc)[..., None], left_side=True, lower=True,
                                    unit_diagonal=True)
    attn = jnp.where(incl, jnp.einsum('nbhid,nbhjd->nbhij', q, k) * decay, 0.0)

    def step(S, xs):
        q_b, k_b, u_b, w_b, a_b, gc_b = xs
        v_new = u_b - jnp.einsum('bhcd,bhde->bhce', w_b, S)
        o = (jnp.einsum('bhcd,bhde->bhce', q_b * jnp.exp(gc_b)[..., None], S)
             + jnp.einsum('bhij,bhje->bhie', a_b, v_new))
        g_last = gc_b[..., -1]
        k_dec = k_b * jnp.exp(g_last[..., None] - gc_b)[..., None]
        S = S * jnp.exp(g_last)[..., None, None] + jnp.einsum('bhcd,bhce->bhde', k_dec, v_new)
        return S, o

    s_final, o = lax.scan(step, s0, (q, k, u, w, attn, gc))
    o = o.transpose(1, 0, 3, 2, 4).reshape(bsz, T, H, -1)
    return o, s_final


def s5_ssm(u, x0_re, x0_im, A_re, A_im, log_dt, B_re, B_im, C_re, C_im, D):
    bsz, T, _ = u.shape
    ug = u.reshape(bsz, T, S5_GROUPS, S5_GROUP)
    lam_re = jnp.minimum(A_re.astype(jnp.float32), -1e-4)
    lam_im = A_im.astype(jnp.float32)
    dt = jnp.exp(log_dt.astype(jnp.float32))[:, None]
    ldt_re, ldt_im = lam_re * dt, lam_im * dt
    mag = jnp.exp(ldt_re)
    lb_re, lb_im = mag * jnp.cos(ldt_im), mag * jnp.sin(ldt_im)
    den = lam_re * lam_re + lam_im * lam_im
    f_re = ((lb_re - 1.0) * lam_re + lb_im * lam_im) / den
    f_im = (lb_im * lam_re - (lb_re - 1.0) * lam_im) / den
    B_re = B_re.astype(jnp.float32)
    B_im = B_im.astype(jnp.float32)
    bb_re = f_re[..., None] * B_re - f_im[..., None] * B_im
    bb_im = f_re[..., None] * B_im + f_im[..., None] * B_re
    bu_re = jnp.einsum('btgc,gnc->btgn', ug, bb_re)
    bu_im = jnp.einsum('btgc,gnc->btgn', ug, bb_im)
    a_re = jnp.broadcast_to(lb_re, (1, T, S5_GROUPS, S5_STATE))
    a_im = jnp.broadcast_to(lb_im, (1, T, S5_GROUPS, S5_STATE))

    def combine(e1, e2):
        a1r, a1i, b1r, b1i = e1
        a2r, a2i, b2r, b2i = e2
        return (a1r * a2r - a1i * a2i,
                a1r * a2i + a1i * a2r,
                a2r * b1r - a2i * b1i + b2r,
                a2r * b1i + a2i * b1r + b2i)

    _, _, xr, xi = lax.associative_scan(combine, (a_re, a_im, bu_re, bu_im), axis=1)
    kpow = jnp.arange(1, T + 1, dtype=jnp.float32)[:, None, None]
    p_mag = jnp.exp(ldt_re * kpow)
    p_re, p_im = p_mag * jnp.cos(ldt_im * kpow), p_mag * jnp.sin(ldt_im * kpow)
    x0_re = x0_re[:, None]
    x0_im = x0_im[:, None]
    xr = xr + p_re * x0_re - p_im * x0_im
    xi = xi + p_re * x0_im + p_im * x0_re
    y = (jnp.einsum('btgn,gcn->btgc', xr, C_re.astype(jnp.float32))
         - jnp.einsum('btgn,gcn->btgc', xi, C_im.astype(jnp.float32))
         + D.astype(jnp.float32).reshape(S5_GROUPS, S5_GROUP) * ug)
    return y.reshape(bsz, T, S5_WIDTH), xr[:, -1], xi[:, -1]


def hybrid_layer(x, conv_buf, s_dn, s5_re, s5_im, norm_w, w_in, conv_w, dn_A_log, dn_dt_bias,
                 dn_norm_w, s5_A_re, s5_A_im, s5_log_dt, s5_B_re, s5_B_im, s5_C_re, s5_C_im, s5_D,
                 glu_w, glu_b, w_out):
    bsz, T, _ = x.shape
    h = rms_norm(x, norm_w).astype(x.dtype)
    p = jnp.einsum('btd,de->bte', h, w_in).astype(jnp.float32)
    o0 = CONV_CH
    o1 = o0 + DN_HEADS
    o2 = o1 + DN_HEADS
    o3 = o2 + DN_WIDTH
    o4 = o3 + S5_WIDTH
    qkv_raw, a_logit, b_logit = p[..., :o0], p[..., o0:o1], p[..., o1:o2]
    z_dn, u_s5, z_s5 = p[..., o2:o3], p[..., o3:o4], p[..., o4:]

    qkv, new_buf = causal_short_conv(qkv_raw, conv_buf.astype(jnp.float32), conv_w.astype(jnp.float32))
    nqk = DN_HEADS * DN_DK
    q = qkv[..., :nqk].reshape(bsz, T, DN_HEADS, DN_DK)
    k = qkv[..., nqk:2 * nqk].reshape(bsz, T, DN_HEADS, DN_DK)
    v = qkv[..., 2 * nqk:].reshape(bsz, T, DN_HEADS, DN_DV)
    q = l2_normalize(q) * (DN_DK ** -0.5)
    k = l2_normalize(k)
    beta = jax.nn.sigmoid(b_logit)
    g = -jnp.exp(dn_A_log.astype(jnp.float32)) * jax.nn.softplus(a_logit + dn_dt_bias.astype(jnp.float32))
    o_dn, s_dn_new = gated_delta_rule(q, k, v, g, beta, s_dn.astype(jnp.float32))
    o_dn = rms_norm(o_dn, dn_norm_w) * jax.nn.silu(z_dn.reshape(bsz, T, DN_HEADS, DN_DV))

    y_s5, re_new, im_new = s5_ssm(u_s5, s5_re.astype(jnp.float32), s5_im.astype(jnp.float32),
                                  s5_A_re, s5_A_im, s5_log_dt, s5_B_re, s5_B_im, s5_C_re, s5_C_im, s5_D)
    gy = jax.nn.gelu(y_s5)
    o_s5 = gy * jax.nn.sigmoid(gy @ glu_w.astype(jnp.float32) + glu_b.astype(jnp.float32)) * jax.nn.silu(z_s5)

    mixed = jnp.concatenate([o_dn.reshape(bsz, T, DN_WIDTH), o_s5], axis=-1).astype(x.dtype)
    x_new = x + jnp.einsum('bte,ed->btd', mixed, w_out)
    return x_new, new_buf, s_dn_new, re_new, im_new


def setup_inputs(seed: int = 0) -> dict:
    key = jax.random.key(seed)
    ks = jax.random.split(key, 24)
    f32 = jnp.float32
    nrm = lambda k, s, sc: jax.random.normal(k, s, f32) * sc
    x_prompt = nrm(ks[0], (BATCH, SEQ, D_MODEL), 1.0)
    x_sample = nrm(ks[1], (DEC_BATCH, DEC_SEQ, D_MODEL), 1.0)
    cache_conv = nrm(ks[2], (DEPTH, DEC_BATCH, CONV_K - 1, CONV_CH), 1.0)
    state_dn = nrm(ks[3], (DEPTH, DEC_BATCH, DN_HEADS, DN_DK, DN_DV), 0.1)
    state_s5_re = nrm(ks[4], (DEPTH, DEC_BATCH, S5_GROUPS, S5_STATE), 0.1)
    state_s5_im = nrm(ks[5], (DEPTH, DEC_BATCH, S5_GROUPS, S5_STATE), 0.1)
    norm_w = 1.0 + nrm(ks[6], (DEPTH, D_MODEL), 0.02)
    w_in = nrm(ks[7], (DEPTH, D_MODEL, IN_COLS), D_MODEL ** -0.5)
    conv_w = nrm(ks[8], (DEPTH, CONV_K, CONV_CH), CONV_K ** -0.5)
    dn_A_log = jnp.log(jax.random.uniform(ks[9], (DEPTH, DN_HEADS), f32, 1.0, 16.0))
    dt0 = jnp.exp(jax.random.uniform(ks[10], (DEPTH, DN_HEADS), f32, math.log(1e-3), math.log(1e-1)))
    dn_dt_bias = dt0 + jnp.log(-jnp.expm1(-dt0))
    dn_norm_w = 1.0 + nrm(ks[11], (DEPTH, DN_DV), 0.02)
    s5_A_re = -0.5 + nrm(ks[12], (DEPTH, S5_GROUPS, S5_STATE), 0.01)
    s5_A_im = (math.pi * jnp.arange(S5_STATE, dtype=f32))[None, None, :] + nrm(ks[13], (DEPTH, S5_GROUPS, S5_STATE), 0.01)
    s5_log_dt = jax.random.uniform(ks[14], (DEPTH, S5_GROUPS), f32, math.log(1e-3), math.log(1e-1))
    s5_B_re = nrm(ks[15], (DEPTH, S5_GROUPS, S5_STATE, S5_GROUP), (2 * S5_GROUP) ** -0.5)
    s5_B_im = nrm(ks[16], (DEPTH, S5_GROUPS, S5_STATE, S5_GROUP), (2 * S5_GROUP) ** -0.5)
    s5_C_re = nrm(ks[17], (DEPTH, S5_GROUPS, S5_GROUP, S5_STATE), (2 * S5_STATE) ** -0.5)
    s5_C_im = nrm(ks[18], (DEPTH, S5_GROUPS, S5_GROUP, S5_STATE), (2 * S5_STATE) ** -0.5)
    s5_D = nrm(ks[19], (DEPTH, S5_WIDTH), 1.0)
    glu_w = nrm(ks[20], (DEPTH, S5_WIDTH, S5_WIDTH), S5_WIDTH ** -0.5)
    glu_b = nrm(ks[21], (DEPTH, S5_WIDTH), 0.01)
    w_out = nrm(ks[22], (DEPTH, MIX, D_MODEL), MIX ** -0.5)
    final_norm_w = 1.0 + nrm(ks[23], (D_MODEL,), 0.02)
    return {"x_prompt": x_prompt, "x_sample": x_sample, "cache_conv": cache_conv, "state_dn": state_dn,
            "state_s5_re": state_s5_re, "state_s5_im": state_s5_im, "norm_w": norm_w, "w_in": w_in,
            "conv_w": conv_w, "dn_A_log": dn_A_log, "dn_dt_bias": dn_dt_bias, "dn_norm_w": dn_norm_w,
            "s5_A_re": s5_A_re, "s5_A_im": s5_A_im, "s5_log_dt": s5_log_dt, "s5_B_re": s5_B_re,
            "s5_B_im": s5_B_im, "s5_C_re": s5_C_re, "s5_C_im": s5_C_im, "s5_D": s5_D, "glu_w": glu_w,
            "glu_b": glu_b, "w_out": w_out, "final_norm_w": final_norm_w}


def reference(x_prompt, x_sample, cache_conv, state_dn, state_s5_re, state_s5_im, norm_w, w_in, conv_w,
              dn_A_log, dn_dt_bias, dn_norm_w, s5_A_re, s5_A_im, s5_log_dt, s5_B_re, s5_B_im, s5_C_re,
              s5_C_im, s5_D, glu_w, glu_b, w_out, final_norm_w):
    bp = x_prompt.shape[0]
    yp, ys = x_prompt, x_sample
    conv_p, dn_p, re_p, im_p = [], [], [], []
    conv_s, dn_s, re_s, im_s = [], [], [], []
    for l in range(DEPTH):
        lw = (norm_w[l], w_in[l], conv_w[l], dn_A_log[l], dn_dt_bias[l], dn_norm_w[l], s5_A_re[l],
              s5_A_im[l], s5_log_dt[l], s5_B_re[l], s5_B_im[l], s5_C_re[l], s5_C_im[l], s5_D[l],
              glu_w[l], glu_b[l], w_out[l])
        yp, c1, d1, r1, i1 = hybrid_layer(
            yp, jnp.zeros((bp, CONV_K - 1, CONV_CH), jnp.float32),
            jnp.zeros((bp, DN_HEADS, DN_DK, DN_DV), jnp.float32),
            jnp.zeros((bp, S5_GROUPS, S5_STATE), jnp.float32),
            jnp.zeros((bp, S5_GROUPS, S5_STATE), jnp.float32), *lw)
        ys, c2, d2, r2, i2 = hybrid_layer(ys, cache_conv[l], state_dn[l], state_s5_re[l], state_s5_im[l], *lw)
        conv_p.append(c1); dn_p.append(d1); re_p.append(r1); im_p.append(i1)
        conv_s.append(c2); dn_s.append(d2); re_s.append(r2); im_s.append(i2)
    y_prompt = rms_norm(yp, final_norm_w).astype(x_prompt.dtype)
    y_sample = rms_norm(ys, final_norm_w).astype(x_sample.dtype)
    new_conv_prompt = jnp.stack(conv_p).astype(cache_conv.dtype)
    new_dn_prompt = jnp.stack(dn_p).astype(state_dn.dtype)
    new_s5_re_prompt = jnp.stack(re_p).astype(state_s5_re.dtype)
    new_s5_im_prompt = jnp.stack(im_p).astype(state_s5_im.dtype)
    new_conv_sample = jnp.stack(conv_s).astype(cache_conv.dtype)
    new_dn_sample = jnp.stack(dn_s).astype(state_dn.dtype)
    new_s5_re_sample = jnp.stack(re_s).astype(state_s5_re.dtype)
    new_s5_im_sample = jnp.stack(im_s).astype(state_s5_im.dtype)
    return (y_prompt, y_sample, new_conv_prompt, new_dn_prompt, new_s5_re_prompt, new_s5_im_prompt,
            new_conv_sample, new_dn_sample, new_s5_re_sample, new_s5_im_sample)
```

```python
import functools
import math

import jax
import jax.numpy as jnp
from jax import lax
from jax.experimental import pallas as pl
from jax.experimental.pallas import tpu as pltpu

bf16 = jnp.bfloat16
f32 = jnp.float32

LANES = 128
SUBLANES = 8

D_MODEL = 1024
DN_HEADS = 4
DN_D = 128
DN_WIDTH = DN_HEADS * DN_D
CONV_K = 4
CONV_CH = 3 * DN_WIDTH
S5_WIDTH = 512
S5_GROUP = 16
S5_GROUPS = 32
S5_STATE = 64
S5_PAIRS = S5_GROUPS // 2
S5_L = 16
S5_GL = S5_GROUP * S5_L
S5_PLANES = S5_WIDTH // LANES
EPS = 1e-6
L2_EPS = 1e-6

OFF_QKV = 0
OFF_ZDN = CONV_CH
OFF_US5 = OFF_ZDN + DN_WIDTH
OFF_ZS5 = OFF_US5 + S5_WIDTH
OFF_AB = OFF_ZS5 + S5_WIDTH
IN_COLS_PAD = OFF_AB + 128

VMEM_LIMIT = 56 * 1024 * 1024

_NT = (((1,), (1,)), ((), ()))
_TN = (((0,), (0,)), ((), ()))


def _bdot(a, b):
    return jnp.dot(a.astype(bf16), b.astype(bf16), preferred_element_type=f32)


def _bdot_g(a, b, dims):
    return lax.dot_general(a.astype(bf16), b.astype(bf16), dims, preferred_element_type=f32)


def _xdot(a, b):
    return jnp.dot(a, b, precision=lax.Precision.HIGHEST, preferred_element_type=f32)


def _sigmoid(x):
    return 1.0 / (1.0 + jnp.exp(-x))


def _silu(x):
    return x * _sigmoid(x)


def _softplus(x):
    return jnp.maximum(x, 0.0) + jnp.log1p(jnp.exp(-jnp.abs(x)))


def _gelu_tanh(x):
    c = math.sqrt(2.0 / math.pi)
    return 0.5 * x * (1.0 + jnp.tanh(c * (x + 0.044715 * (x * x * x))))


def _split16(a):
    hi = a.astype(bf16)
    return hi, (a - hi.astype(f32)).astype(bf16)


def _dot3(a, b):
    ah, al = _split16(a)
    bh, bl = _split16(b)
    d = functools.partial(jnp.dot, preferred_element_type=f32)
    return d(ah, bh) + (d(ah, bl) + d(al, bh))


def _unit_lower_solve(nmat, rhs, off_masks):
    inv = None
    for lvl, mask in enumerate(off_masks):
        off = nmat * mask
        if lvl == 0:
            c = nmat.shape[0]
            ri = lax.broadcasted_iota(jnp.int32, (c, c), 0)
            ci = lax.broadcasted_iota(jnp.int32, (c, c), 1)
            inv = (ri == ci).astype(f32) - off
        else:
            inv = inv - _bdot(_bdot(inv, off), inv)
    x0 = _bdot(inv, rhs)
    resid = rhs - x0 - _dot3(nmat, x0)
    return x0 + _bdot(inv, resid)


def _delta_body(x_ref, w_ref, nw_ref, cw_ref, alog_ref, dtb_ref, dnw_ref, conv0_ref, s0_ref,
                odn_ref, us5_ref, zs5_ref, convout_ref, s_ref, cbuf, *, nb, tb, chunk):
    C = chunk
    rows = nb * tb
    n_chunks = tb // C
    log2c = C.bit_length() - 1

    @pl.when(pl.program_id(1) == 0)
    def _init():
        cbuf[:, 0:SUBLANES, :] = conv0_ref[...]
        s_ref[...] = s0_ref[...]

    x = x_ref[...].reshape(rows, D_MODEL)
    ms = jnp.mean(x * x, axis=-1, keepdims=True)
    h = (x * lax.rsqrt(ms + EPS) * nw_ref[...]).astype(bf16)

    qkv_raw = jnp.dot(h, w_ref[:, OFF_QKV:OFF_QKV + CONV_CH], preferred_element_type=f32)
    qkv_parts = []
    for b in range(nb):
        cbuf[b, SUBLANES:SUBLANES + tb, :] = qkv_raw[b * tb:(b + 1) * tb, :]
        acc = cbuf[b, 5:5 + tb, :] * cw_ref[0:1, :]
        for j in range(1, CONV_K):
            acc = acc + cbuf[b, 5 + j:5 + j + tb, :] * cw_ref[j:j + 1, :]
        qkv_parts.append(_silu(acc))
        tail = cbuf[b, tb:tb + SUBLANES, :]
        cbuf[b, 0:SUBLANES, :] = tail
        convout_ref[b] = tail
    qkv = qkv_parts[0] if nb == 1 else jnp.concatenate(qkv_parts, axis=0)

    for j in range(S5_PLANES):
        lo = OFF_US5 + j * LANES
        us5_ref[:, j] = jnp.dot(h, w_ref[:, lo:lo + LANES], preferred_element_type=f32).reshape(nb, tb, LANES)
    zs5_ref[...] = jnp.dot(h, w_ref[:, OFF_ZS5:OFF_ZS5 + S5_WIDTH],
                           preferred_element_type=f32).reshape(nb, tb, S5_WIDTH)
    zdn = jnp.dot(h, w_ref[:, OFF_ZDN:OFF_ZDN + DN_WIDTH], preferred_element_type=f32)
    ab = jnp.dot(h, w_ref[:, OFF_AB:OFF_AB + LANES], preferred_element_type=f32)

    g_full = -jnp.exp(alog_ref[...]) * _softplus(ab + dtb_ref[...])
    beta_full = _sigmoid(ab)

    ri = lax.broadcasted_iota(jnp.int32, (rows, rows), 0)
    ci = lax.broadcasted_iota(jnp.int32, (rows, rows), 1)
    tri = ((ri >= ci) & ((ri >> log2c) == (ci >> log2c))).astype(f32)
    gc_full = _xdot(tri, g_full)

    ri = lax.broadcasted_iota(jnp.int32, (C, C), 0)
    ci = lax.broadcasted_iota(jnp.int32, (C, C), 1)
    incl = ri >= ci
    strict = ri > ci
    eye = (ri == ci).astype(f32)
    ones_cc = jnp.ones((C, C), f32)
    dnw = dnw_ref[...]
    off_masks = [(((ri >> (l + 1)) == (ci >> (l + 1))) & (((ri >> l) & 1) == 1)
                  & (((ci >> l) & 1) == 0)).astype(f32) for l in range(log2c)]

    for b in range(nb):
        for c in range(n_chunks):
            r0 = b * tb + c * C
            for hd in range(DN_HEADS):
                lo = hd * DN_D
                q = qkv[r0:r0 + C, lo:lo + DN_D]
                k = qkv[r0:r0 + C, DN_WIDTH + lo:DN_WIDTH + lo + DN_D]
                v = qkv[r0:r0 + C, 2 * DN_WIDTH + lo:2 * DN_WIDTH + lo + DN_D]
                q = q * lax.rsqrt(jnp.sum(q * q, axis=-1, keepdims=True) + L2_EPS) * (DN_D ** -0.5)
                k = k * lax.rsqrt(jnp.sum(k * k, axis=-1, keepdims=True) + L2_EPS)
                beta = beta_full[r0:r0 + C, DN_HEADS + hd:DN_HEADS + hd + 1]
                gcc = gc_full[r0:r0 + C, hd:hd + 1]
                glast = gc_full[r0 + C - 1:r0 + C, hd:hd + 1]
                eg = jnp.exp(gcc)
                kb = k * beta
                vb = v * beta

                gcb = jnp.broadcast_to(gcc, (C, C))
                diff = gcb - _xdot(ones_cc, gcb * eye)
                dmat = jnp.where(incl, jnp.exp(jnp.where(incl, diff, 0.0)), 0.0)

                k16 = k.astype(bf16)
                nmat = jnp.where(strict, _bdot_g(kb, k16, _NT) * dmat, 0.0)
                attn = jnp.where(incl, _bdot_g(q, k16, _NT) * dmat, 0.0)

                sol = _unit_lower_solve(nmat, jnp.concatenate([vb, kb * eg], axis=1), off_masks)
                u = sol[:, :DN_D]
                w = sol[:, DN_D:]

                s_old = s_ref[b, hd]
                s16 = s_old.astype(bf16)
                v_new = u - _bdot(w, s16)
                o = _bdot(q * eg, s16) + _bdot(attn, v_new)
                k_dec = k * jnp.exp(glast - gcc)
                s_ref[b, hd] = s_old * jnp.exp(glast) + _bdot_g(k_dec, v_new, _TN)

                on = o * lax.rsqrt(jnp.mean(o * o, axis=-1, keepdims=True) + EPS) * dnw
                zd = zdn[r0:r0 + C, lo:lo + DN_D]
                odn_ref[b, c * C:(c + 1) * C, lo:lo + DN_D] = (on * _silu(zd)).astype(bf16)


def _delta_call(x, w_in_r, norm_w, conv_w, alog, dtb, dnw, conv0, s0, *, nb, tb, chunk):
    B, T, _ = x.shape
    grid = (B // nb, T // tb)
    const2 = lambda b, t: (0, 0)
    body = functools.partial(_delta_body, nb=nb, tb=tb, chunk=chunk)
    return pl.pallas_call(
        body,
        grid=grid,
        in_specs=[
            pl.BlockSpec((nb, tb, D_MODEL), lambda b, t: (b, t, 0)),
            pl.BlockSpec((D_MODEL, IN_COLS_PAD), const2),
            pl.BlockSpec((1, D_MODEL), const2),
            pl.BlockSpec((CONV_K, CONV_CH), const2),
            pl.BlockSpec((1, LANES), const2),
            pl.BlockSpec((1, LANES), const2),
            pl.BlockSpec((1, DN_D), const2),
            pl.BlockSpec((nb, SUBLANES, CONV_CH), lambda b, t: (b, 0, 0)),
            pl.BlockSpec((nb, DN_HEADS, DN_D, DN_D), lambda b, t: (b, 0, 0, 0)),
        ],
        out_specs=[
            pl.BlockSpec((nb, tb, DN_WIDTH), lambda b, t: (b, t, 0)),
            pl.BlockSpec((nb, S5_PLANES, tb, LANES), lambda b, t: (b, 0, t, 0)),
            pl.BlockSpec((nb, tb, S5_WIDTH), lambda b, t: (b, t, 0)),
            pl.BlockSpec((nb, SUBLANES, CONV_CH), lambda b, t: (b, 0, 0)),
            pl.BlockSpec((nb, DN_HEADS, DN_D, DN_D), lambda b, t: (b, 0, 0, 0)),
        ],
        out_shape=[
            jax.ShapeDtypeStruct((B, T, DN_WIDTH), bf16),
            jax.ShapeDtypeStruct((B, S5_PLANES, T, LANES), f32),
            jax.ShapeDtypeStruct((B, T, S5_WIDTH), f32),
            jax.ShapeDtypeStruct((B, SUBLANES, CONV_CH), f32),
            jax.ShapeDtypeStruct((B, DN_HEADS, DN_D, DN_D), f32),
        ],
        scratch_shapes=[pltpu.VMEM((nb, tb + SUBLANES, CONV_CH), f32)],
        compiler_params=pltpu.CompilerParams(
            dimension_semantics=("arbitrary", "arbitrary"), vmem_limit_bytes=VMEM_LIMIT),
        name="delta",
    )(x, w_in_r, norm_w, conv_w, alog, dtb, dnw, conv0, s0)


def _s5mat_body(cpr_ref, cpi_ref, bpr_ref, bpi_ref, t_ref):
    t = _xdot(cpr_ref[0], bpr_ref[0]) - _xdot(cpi_ref[0], bpi_ref[0])
    ri = lax.broadcasted_iota(jnp.int32, (S5_GL, S5_GL), 0) >> 4
    ci = lax.broadcasted_iota(jnp.int32, (S5_GL, S5_GL), 1) >> 4
    t_ref[0] = jnp.where(ri >= ci, t, 0.0).astype(bf16)


def _s5mat_call(cpr, cpi, bpr, bpi):
    return pl.pallas_call(
        _s5mat_body,
        grid=(S5_GROUPS,),
        in_specs=[
            pl.BlockSpec((1, S5_GL, S5_STATE), lambda g: (g, 0, 0)),
            pl.BlockSpec((1, S5_GL, S5_STATE), lambda g: (g, 0, 0)),
            pl.BlockSpec((1, S5_STATE, S5_GL), lambda g: (g, 0, 0)),
            pl.BlockSpec((1, S5_STATE, S5_GL), lambda g: (g, 0, 0)),
        ],
        out_specs=pl.BlockSpec((1, S5_GL, S5_GL), lambda g: (g, 0, 0)),
        out_shape=jax.ShapeDtypeStruct((S5_GROUPS, S5_GL, S5_GL), bf16),
        compiler_params=pltpu.CompilerParams(dimension_semantics=("arbitrary",)),
        name="s5mat",
    )(cpr, cpi, bpr, bpi)


def _s5_body(u_ref, z_ref, xr0_ref, xi0_ref, t_ref, wp_ref, vt_ref, ar_ref, ai_ref, d_ref,
             gw_ref, gb_ref,
             o_ref, xr_ref, xi_ref,
             at_scr, er_scr, ei_scr, xinr_scr, xini_scr, yt_scr, y_scr, *, tb, n_streams, n_chunks):
    R = tb // S5_L
    used = n_streams * n_chunks

    @pl.when(pl.program_id(1) == 0)
    def _init():
        xr_ref[...] = xr0_ref[...]
        xi_ref[...] = xi0_ref[...]

    for f in range(S5_L):
        for j in range(S5_PLANES):
            ut = u_ref[0, j, pl.ds(f, R, stride=S5_L), :]
            ut_t = ut.T.astype(bf16)
            for gl in range(LANES // S5_GROUP):
                g = j * (LANES // S5_GROUP) + gl
                at_scr[g, f * S5_GROUP:(f + 1) * S5_GROUP, :] = ut_t[gl * S5_GROUP:(gl + 1) * S5_GROUP, :]

    for p in range(S5_PAIRS):
        a_t = at_scr[2 * p:2 * p + 2].reshape(2 * S5_GL, R)
        e = lax.dot_general(a_t, wp_ref[p], _TN, preferred_element_type=f32)
        er_scr[:, p * LANES:(p + 1) * LANES] = e[:, :LANES]
        ei_scr[:, p * LANES:(p + 1) * LANES] = e[:, LANES:]

    if used < R:
        xinr_scr[...] = jnp.zeros_like(xinr_scr)
        xini_scr[...] = jnp.zeros_like(xini_scr)

    a_r = ar_ref[...]
    a_i = ai_ref[...]
    for s in range(n_streams):
        def step(c, carry):
            xr, xi = carry
            row = s * n_chunks + c
            xinr_scr[pl.ds(row, 1), :] = xr
            xini_scr[pl.ds(row, 1), :] = xi
            er = er_scr[pl.ds(row, 1), :]
            ei = ei_scr[pl.ds(row, 1), :]
            return a_r * xr - a_i * xi + er, a_r * xi + a_i * xr + ei

        xr, xi = lax.fori_loop(0, n_chunks, step, (xr_ref[0, s:s + 1, :], xi_ref[0, s:s + 1, :]))
        xr_ref[0, s:s + 1, :] = xr
        xi_ref[0, s:s + 1, :] = xi

    for g in range(S5_GROUPS):
        p = g // 2
        xin = jnp.concatenate([xinr_scr[:, p * LANES:(p + 1) * LANES],
                               xini_scr[:, p * LANES:(p + 1) * LANES]], axis=1).astype(bf16)
        yt_scr[g] = (jnp.dot(t_ref[g], at_scr[g], preferred_element_type=f32)
                     + lax.dot_general(vt_ref[g], xin, _NT, preferred_element_type=f32))

    for f in range(S5_L):
        gpp = LANES // S5_GROUP
        for j in range(S5_PLANES):
            yt = yt_scr[j * gpp:(j + 1) * gpp, f * S5_GROUP:(f + 1) * S5_GROUP, :].reshape(LANES, R)
            y_scr[j, pl.ds(f, R, stride=S5_L), :] = yt.T

    sub = 256
    for rb in range(tb // sub):
        sl = pl.ds(rb * sub, sub)
        y_intra = jnp.concatenate([y_scr[j, sl, :] for j in range(S5_PLANES)], axis=1)
        u_nat = jnp.concatenate([u_ref[0, j, sl, :] for j in range(S5_PLANES)], axis=1)
        y = y_intra + d_ref[...] * u_nat
        gy = _gelu_tanh(y)
        gate = _sigmoid(_bdot(gy, gw_ref[...]) + gb_ref[...])
        o_ref[0, sl, :] = (gy * gate * _silu(z_ref[0, sl, :])).astype(bf16)


def _s5_call(u, z, xr0, xi0, tmat, wp, vt, ar, ai, dvec, gw, gb, *, tb, n_streams, n_chunks):
    G, T, _ = z.shape
    grid = (G, T // tb)
    R = tb // S5_L
    nstate = S5_GROUPS * S5_STATE
    c2 = lambda g, t: (0, 0)
    c3 = lambda g, t: (0, 0, 0)
    body = functools.partial(_s5_body, tb=tb, n_streams=n_streams, n_chunks=n_chunks)
    one = pl.Buffered(1)
    return pl.pallas_call(
        body,
        grid=grid,
        in_specs=[
            pl.BlockSpec((1, S5_PLANES, tb, LANES), lambda g, t: (g, 0, t, 0)),
            pl.BlockSpec((1, tb, S5_WIDTH), lambda g, t: (g, t, 0)),
            pl.BlockSpec((1, n_streams, nstate), lambda g, t: (g, 0, 0)),
            pl.BlockSpec((1, n_streams, nstate), lambda g, t: (g, 0, 0)),
            pl.BlockSpec((S5_GROUPS, S5_GL, S5_GL), c3, pipeline_mode=one),
            pl.BlockSpec((S5_PAIRS, 2 * S5_GL, 2 * LANES), c3, pipeline_mode=one),
            pl.BlockSpec((S5_GROUPS, S5_GL, 2 * LANES), c3, pipeline_mode=one),
            pl.BlockSpec((1, nstate), c2),
            pl.BlockSpec((1, nstate), c2),
            pl.BlockSpec((1, S5_WIDTH), c2),
            pl.BlockSpec((S5_WIDTH, S5_WIDTH), c2),
            pl.BlockSpec((1, S5_WIDTH), c2),
        ],
        out_specs=[
            pl.BlockSpec((1, tb, S5_WIDTH), lambda g, t: (g, t, 0)),
            pl.BlockSpec((1, n_streams, nstate), lambda g, t: (g, 0, 0)),
            pl.BlockSpec((1, n_streams, nstate), lambda g, t: (g, 0, 0)),
        ],
        out_shape=[
            jax.ShapeDtypeStruct((G, T, S5_WIDTH), bf16),
            jax.ShapeDtypeStruct((G, n_streams, nstate), f32),
            jax.ShapeDtypeStruct((G, n_streams, nstate), f32),
        ],
        scratch_shapes=[
            pltpu.VMEM((S5_GROUPS, S5_GL, R), bf16),
            pltpu.VMEM((R, nstate), f32),
            pltpu.VMEM((R, nstate), f32),
            pltpu.VMEM((R, nstate), f32),
            pltpu.VMEM((R, nstate), f32),
            pltpu.VMEM((S5_GROUPS, S5_GL, R), f32),
            pltpu.VMEM((S5_PLANES, tb, LANES), f32),
        ],
        compiler_params=pltpu.CompilerParams(
            dimension_semantics=("arbitrary", "arbitrary"), vmem_limit_bytes=VMEM_LIMIT),
        name="s5",
    )(u, z, xr0, xi0, tmat, wp, vt, ar, ai, dvec, gw, gb)


def _out_body(x_ref, odn_ref, os5_ref, wo_ref, fw_ref, y_ref):
    acc = (x_ref[...]
           + jnp.dot(odn_ref[...], wo_ref[0:DN_WIDTH, :], preferred_element_type=f32)
           + jnp.dot(os5_ref[...], wo_ref[DN_WIDTH:, :], preferred_element_type=f32))
    ms = jnp.mean(acc * acc, axis=-1, keepdims=True)
    y_ref[...] = acc * lax.rsqrt(ms + EPS) * fw_ref[...]


def _out_call(x2, odn2, os52, w_out16, fw, *, rows):
    n = x2.shape[0]
    return pl.pallas_call(
        _out_body,
        grid=(n // rows,),
        in_specs=[
            pl.BlockSpec((rows, D_MODEL), lambda i: (i, 0)),
            pl.BlockSpec((rows, DN_WIDTH), lambda i: (i, 0)),
            pl.BlockSpec((rows, S5_WIDTH), lambda i: (i, 0)),
            pl.BlockSpec((D_MODEL, D_MODEL), lambda i: (0, 0)),
            pl.BlockSpec((1, D_MODEL), lambda i: (0, 0)),
        ],
        out_specs=pl.BlockSpec((rows, D_MODEL), lambda i: (i, 0)),
        out_shape=jax.ShapeDtypeStruct((n, D_MODEL), f32),
        compiler_params=pltpu.CompilerParams(
            dimension_semantics=("arbitrary",), vmem_limit_bytes=VMEM_LIMIT),
        name="outproj",
    )(x2, odn2, os52, w_out16, fw)


def _s5_operands(a_re, a_im, log_dt, b_re, b_im, c_re, c_im):
    lam_re = jnp.minimum(a_re, -1e-4)
    lam_im = a_im
    dt = jnp.exp(log_dt)[:, None]
    ldt_re, ldt_im = lam_re * dt, lam_im * dt

    def lpow(k):
        mag = jnp.exp(ldt_re * k)
        return mag * jnp.cos(ldt_im * k), mag * jnp.sin(ldt_im * k)

    lb_re, lb_im = lpow(1.0)
    den = lam_re * lam_re + lam_im * lam_im
    f_re = ((lb_re - 1.0) * lam_re + lb_im * lam_im) / den
    f_im = (lb_im * lam_re - (lb_re - 1.0) * lam_im) / den
    bb_re = f_re[..., None] * b_re - f_im[..., None] * b_im
    bb_im = f_re[..., None] * b_im + f_im[..., None] * b_re

    fr = jnp.arange(S5_L, dtype=f32)
    G, N = S5_GROUPS, S5_STATE

    def cmul(ar, ai, br, bi):
        return ar * br - ai * bi, ar * bi + ai * br

    pr, pi = lpow(fr[:, None, None])
    pr, pi = pr.transpose(1, 0, 2), pi.transpose(1, 0, 2)
    cpr, cpi = cmul(c_re[:, None], c_im[:, None], pr[:, :, None], pi[:, :, None])
    cpr, cpi = cpr.reshape(G, S5_GL, N), cpi.reshape(G, S5_GL, N)
    qr, qi = lpow(-fr[:, None, None])
    qr, qi = qr.transpose(1, 2, 0), qi.transpose(1, 2, 0)
    bpr, bpi = cmul(qr[..., None], qi[..., None], bb_re[:, :, None], bb_im[:, :, None])
    bpr, bpi = bpr.reshape(G, N, S5_GL), bpi.reshape(G, N, S5_GL)

    wr_, wi_ = lpow((S5_L - 1.0) - fr[:, None, None])
    wr_, wi_ = wr_.transpose(1, 0, 2), wi_.transpose(1, 0, 2)
    bbr_t, bbi_t = bb_re.transpose(0, 2, 1), bb_im.transpose(0, 2, 1)
    w_re, w_im = cmul(wr_[:, :, None], wi_[:, :, None], bbr_t[:, None], bbi_t[:, None])
    w_re, w_im = w_re.reshape(G, S5_GL, N), w_im.reshape(G, S5_GL, N)
    vr_, vi_ = lpow(fr[:, None, None] + 1.0)
    vr_, vi_ = vr_.transpose(1, 0, 2), vi_.transpose(1, 0, 2)
    v_re, v_im = cmul(c_re[:, None], c_im[:, None], vr_[:, :, None], vi_[:, :, None])
    v_re, v_im = v_re.reshape(G, S5_GL, N), v_im.reshape(G, S5_GL, N)

    zeros = jnp.zeros((S5_PAIRS, S5_GL, N), f32)

    def pair_cols(re, im):
        re, im = re.reshape(S5_PAIRS, 2, S5_GL, N), im.reshape(S5_PAIRS, 2, S5_GL, N)
        even = jnp.concatenate([re[:, 0], zeros, im[:, 0], zeros], axis=-1)
        odd = jnp.concatenate([zeros, re[:, 1], zeros, im[:, 1]], axis=-1)
        return jnp.stack([even, odd], axis=1)

    wp = pair_cols(w_re, w_im).reshape(S5_PAIRS, 2 * S5_GL, 2 * LANES).astype(bf16)
    vt = pair_cols(v_re, -v_im).reshape(S5_GROUPS, S5_GL, 2 * LANES).astype(bf16)
    a16r, a16i = lpow(float(S5_L))
    return cpr, cpi, bpr, bpi, wp, vt, a16r.reshape(1, G * N), a16i.reshape(1, G * N)


def _pad_lanes(v, start=0):
    out = jnp.zeros((1, LANES), f32)
    return out.at[0, start:start + v.shape[0]].set(v.astype(f32))


def _layer(x, conv0, s0, xr0, xi0, prm, *, delta_cfg, s5_cfg, out_rows):
    B, T, _ = x.shape
    conv0p = jnp.concatenate([jnp.zeros((B, SUBLANES - (CONV_K - 1), CONV_CH), f32), conv0], axis=1)
    odn, us5, zs5, convout, s_new = _delta_call(
        x, prm["w_in_r"], prm["norm_w"], prm["conv_w"], prm["alog"], prm["dtb"], prm["dnw"],
        conv0p, s0, **delta_cfg)

    nstate = S5_GROUPS * S5_STATE
    if s5_cfg["flatten"]:
        tb = s5_cfg["tb"]
        uf = us5.transpose(1, 0, 2, 3).reshape(1, S5_PLANES, B * T, LANES)
        zf = zs5.reshape(1, B * T, S5_WIDTH)
        pad = tb - B * T
        uf = jnp.pad(uf, ((0, 0), (0, 0), (0, pad), (0, 0)))
        zf = jnp.pad(zf, ((0, 0), (0, pad), (0, 0)))
        os5, xr, xi = _s5_call(uf, zf, xr0.reshape(1, B, nstate), xi0.reshape(1, B, nstate),
                               prm["tmat"], prm["wp"], prm["vt"], prm["ar"], prm["ai"], prm["dvec"],
                               prm["gw"], prm["gb"], tb=tb, n_streams=B, n_chunks=T // S5_L)
        os5 = os5[0, :B * T].reshape(B, T, S5_WIDTH)
    else:
        tb = s5_cfg["tb"]
        os5, xr, xi = _s5_call(us5, zs5, xr0.reshape(B, 1, nstate), xi0.reshape(B, 1, nstate),
                               prm["tmat"], prm["wp"], prm["vt"], prm["ar"], prm["ai"], prm["dvec"],
                               prm["gw"], prm["gb"], tb=tb, n_streams=1, n_chunks=tb // S5_L)

    y = _out_call(x.reshape(B * T, D_MODEL), odn.reshape(B * T, DN_WIDTH), os5.reshape(B * T, S5_WIDTH),
                  prm["w_out"], prm["fw"], rows=out_rows)
    return (y.reshape(B, T, D_MODEL), convout[:, SUBLANES - (CONV_K - 1):, :], s_new,
            xr.reshape(B, S5_GROUPS, S5_STATE), xi.reshape(B, S5_GROUPS, S5_STATE))


def kernel(x_prompt, x_sample, cache_conv, state_dn, state_s5_re, state_s5_im, norm_w, w_in, conv_w, dn_A_log, dn_dt_bias, dn_norm_w, s5_A_re, s5_A_im, s5_log_dt, s5_B_re, s5_B_im, s5_C_re, s5_C_im, s5_D, glu_w, glu_b, w_out, final_norm_w):
    depth = norm_w.shape[0]
    assert depth == 1
    l = 0
    wi = w_in[l]
    o0 = CONV_CH
    o2 = o0 + 2 * DN_HEADS
    w_in_r = jnp.concatenate(
        [wi[:, :o0], wi[:, o2:], wi[:, o0:o2], jnp.zeros((D_MODEL, LANES - 2 * DN_HEADS), f32)],
        axis=1).astype(bf16)
    cpr, cpi, bpr, bpi, wp, vt, ar, ai = _s5_operands(
        s5_A_re[l].astype(f32), s5_A_im[l].astype(f32), s5_log_dt[l].astype(f32),
        s5_B_re[l].astype(f32), s5_B_im[l].astype(f32), s5_C_re[l].astype(f32), s5_C_im[l].astype(f32))
    prm = dict(
        w_in_r=w_in_r,
        norm_w=norm_w[l].reshape(1, D_MODEL).astype(f32),
        conv_w=conv_w[l].astype(f32),
        alog=_pad_lanes(dn_A_log[l]),
        dtb=_pad_lanes(dn_dt_bias[l]),
        dnw=dn_norm_w[l].reshape(1, DN_D).astype(f32),
        tmat=_s5mat_call(cpr, cpi, bpr, bpi),
        wp=wp, vt=vt, ar=ar, ai=ai,
        dvec=s5_D[l].reshape(1, S5_WIDTH).astype(f32),
        gw=glu_w[l].astype(bf16),
        gb=glu_b[l].reshape(1, S5_WIDTH).astype(f32),
        w_out=w_out[l].astype(bf16),
        fw=final_norm_w.reshape(1, D_MODEL).astype(f32),
    )

    bp = x_prompt.shape[0]
    yp, c1, d1, r1, i1 = _layer(
        x_prompt,
        jnp.zeros((bp, CONV_K - 1, CONV_CH), f32),
        jnp.zeros((bp, DN_HEADS, DN_D, DN_D), f32),
        jnp.zeros((bp, S5_GROUPS, S5_STATE), f32),
        jnp.zeros((bp, S5_GROUPS, S5_STATE), f32),
        prm,
        delta_cfg=dict(nb=1, tb=256, chunk=64),
        s5_cfg=dict(flatten=False, tb=2048),
        out_rows=512)
    ys, c2, d2, r2, i2 = _layer(
        x_sample, cache_conv[l].astype(f32), state_dn[l].astype(f32),
        state_s5_re[l].astype(f32), state_s5_im[l].astype(f32),
        prm,
        delta_cfg=dict(nb=4, tb=32, chunk=32),
        s5_cfg=dict(flatten=True, tb=2048),
        out_rows=512)

    return (yp, ys, c1[None], d1[None], r1[None], i1[None], c2[None], d2[None], r2[None], i2[None])
```

```python
import functools
import math

import jax
import jax.numpy as jnp
import numpy as np
from jax import lax
from jax.experimental import pallas as pl
from jax.experimental.pallas import tpu as pltpu

bf16 = jnp.bfloat16
f32 = jnp.float32

LANES = 128
SUBLANES = 8

D_MODEL = 1024
DN_HEADS = 4
DN_D = 128
DN_WIDTH = DN_HEADS * DN_D
CONV_K = 4
CONV_CH = 3 * DN_WIDTH
S5_WIDTH = 512
S5_GROUP = 16
S5_GROUPS = 32
S5_STATE = 64
S5_PAIRS = S5_GROUPS // 2
S5_L = 16
S5_GL = S5_GROUP * S5_L
S5_PLANES = S5_WIDTH // LANES
EPS = 1e-6
L2_EPS = 1e-6

OFF_QKV = 0
OFF_ZDN = CONV_CH
OFF_US5 = OFF_ZDN + DN_WIDTH
OFF_ZS5 = OFF_US5 + S5_WIDTH
OFF_AB = OFF_ZS5 + S5_WIDTH
IN_COLS_PAD = OFF_AB + 128

GROUP = 256
MASK_INCL, MASK_STRICT, MASK_OFF = 0, 1, 2
VMEM_LIMIT =56 * 1024 * 1024

_NT = (((1,), (1,)), ((), ()))
_TN = (((0,), (0,)), ((), ()))


def _bdot(a, b):
    return jnp.dot(a.astype(bf16), b.astype(bf16), preferred_element_type=f32)


def _bdot_g(a, b, dims):
    return lax.dot_general(a.astype(bf16), b.astype(bf16), dims, preferred_element_type=f32)


def _xdot(a, b):
    return jnp.dot(a, b, precision=lax.Precision.HIGHEST, preferred_element_type=f32)


def _sigmoid(x):
    return 1.0 / (1.0 + jnp.exp(-x))


def _silu(x):
    return x * _sigmoid(x)


def _softplus(x):
    return jnp.maximum(x, 0.0) + jnp.log1p(jnp.exp(-jnp.abs(x)))


def _gelu_tanh(x):
    c = math.sqrt(2.0 / math.pi)
    return 0.5 * x * (1.0 + jnp.tanh(c * (x + 0.044715 * (x * x * x))))


def _split16(a):
    hi = a.astype(bf16)
    return hi, (a - hi.astype(f32)).astype(bf16)


def _dot3(a, b):
    ah, al = _split16(a)
    bh, bl = _split16(b)
    d = functools.partial(jnp.dot, preferred_element_type=f32)
    return d(ah, bh) + (d(ah, bl) + d(al, bh))


def _unit_lower_solve(nmat, rhs, eye, m_ref, n_levels):
    inv = eye - nmat * m_ref[MASK_OFF]
    for lvl in range(1, n_levels):
        off = nmat * m_ref[MASK_OFF + lvl]
        inv = inv - _bdot(_bdot(inv, off), inv)
    x0 = _bdot(inv, rhs)
    resid = rhs - x0 - _dot3(nmat, x0)
    return x0 + _bdot(inv, resid)


def _segment_cumsum(tri16, g):
    g1 = g.astype(bf16)
    r1 = g - g1.astype(f32)
    g2 = r1.astype(bf16)
    g3 = (r1 - g2.astype(f32)).astype(bf16)
    s = jnp.dot(tri16, jnp.concatenate([g1, g2, g3], axis=1), preferred_element_type=f32)
    n = g.shape[1]
    return s[:, :n] + (s[:, n:2 * n] + s[:, 2 * n:])


def _delta_body(x_ref, w_ref, nw_ref, cw_ref, alog_ref, dtb_ref, dnw_ref, m_ref, conv0_ref, s0_ref,
                odn_ref, us5_ref, zs5_ref, convout_ref, s_ref, cbuf, *, nb, tb):
    rows = nb * tb
    n_groups = rows // GROUP
    segs = GROUP // tb
    n_levels = tb.bit_length() - 1

    @pl.when(pl.program_id(1) == 0)
    def _init():
        cbuf[:, 0:SUBLANES, :] = conv0_ref[...]
        s_ref[...] = s0_ref[...]

    x = x_ref[...].reshape(rows, D_MODEL)
    ms = jnp.mean(x * x, axis=-1, keepdims=True)
    h = (x * lax.rsqrt(ms + EPS) * nw_ref[...]).astype(bf16)

    qkv_raw = jnp.dot(h, w_ref[:, OFF_QKV:OFF_QKV + CONV_CH], preferred_element_type=f32)
    qkv_parts = []
    tails = []
    for b in range(nb):
        cbuf[b, SUBLANES:SUBLANES + tb, :] = qkv_raw[b * tb:(b + 1) * tb, :]
        acc = cbuf[b, 5:5 + tb, :] * cw_ref[0:1, :]
        for j in range(1, CONV_K):
            acc = acc + cbuf[b, 5 + j:5 + j + tb, :] * cw_ref[j:j + 1, :]
        qkv_parts.append(_silu(acc))
        tail = cbuf[b, tb:tb + SUBLANES, :]
        cbuf[b, 0:SUBLANES, :] = tail
        tails.append(tail)
    convout_ref[...] = jnp.stack(tails)
    qkv = qkv_parts[0] if nb == 1 else jnp.concatenate(qkv_parts, axis=0)

    for j in range(S5_PLANES):
        lo = OFF_US5 + j * LANES
        us5_ref[:, j] = jnp.dot(h, w_ref[:, lo:lo + LANES], preferred_element_type=f32).reshape(nb, tb, LANES)
    zs5_ref[...] = jnp.dot(h, w_ref[:, OFF_ZS5:OFF_ZS5 + S5_WIDTH],
                           preferred_element_type=f32).reshape(nb, tb, S5_WIDTH)
    zdn = jnp.dot(h, w_ref[:, OFF_ZDN:OFF_ZDN + DN_WIDTH], preferred_element_type=f32)
    ab = jnp.dot(h, w_ref[:, OFF_AB:OFF_AB + LANES], preferred_element_type=f32)

    g_full = -jnp.exp(alog_ref[...]) * _softplus(ab + dtb_ref[...])
    beta_full = _sigmoid(ab)

    incl = m_ref[MASK_INCL]
    strict = m_ref[MASK_STRICT]
    eye = incl - strict
    tri16 = incl.astype(bf16)
    dnw = dnw_ref[...]

    s_cur = {(b, hd): s_ref[b, hd] for b in range(nb) for hd in range(DN_HEADS)}
    o_groups = []
    for gi in range(n_groups):
        r0 = gi * GROUP
        gc_blk = _segment_cumsum(tri16, g_full[r0:r0 + GROUP])
        o_heads = []
        for hd in range(DN_HEADS):
            lo = hd * DN_D
            q = qkv[r0:r0 + GROUP, lo:lo + DN_D]
            k = qkv[r0:r0 + GROUP, DN_WIDTH + lo:DN_WIDTH + lo + DN_D]
            v = qkv[r0:r0 + GROUP, 2 * DN_WIDTH + lo:2 * DN_WIDTH + lo + DN_D]
            q = q * lax.rsqrt(jnp.sum(q * q, axis=-1, keepdims=True) + L2_EPS) * (DN_D ** -0.5)
            k = k * lax.rsqrt(jnp.sum(k * k, axis=-1, keepdims=True) + L2_EPS)
            beta = beta_full[r0:r0 + GROUP, DN_HEADS + hd:DN_HEADS + hd + 1]
            gcc = gc_blk[:, hd:hd + 1]
            eg = jnp.exp(gcc)
            kb = k * beta
            vb = v * beta

            gcb = jnp.broadcast_to(gcc, (GROUP, GROUP))
            dmat = jnp.exp((gcb - gcb.T) * incl) * incl
            k16 = k.astype(bf16)
            nmat = _bdot_g(kb, k16, _NT) * (dmat * strict)
            attn = _bdot_g(q, k16, _NT) * dmat

            sol = _unit_lower_solve(nmat, jnp.concatenate([vb, kb * eg], axis=1), eye, m_ref, n_levels)
            u = sol[:, :DN_D]
            w = sol[:, DN_D:]
            qe = q * eg

            v_new_parts, os_parts, s16 = [], [], []
            for sg in range(segs):
                a0 = sg * tb
                b = (r0 + a0) // tb
                s16.append(s_cur[(b, hd)].astype(bf16))
                v_new_parts.append(u[a0:a0 + tb] - _bdot(w[a0:a0 + tb], s16[sg]))
                os_parts.append(_bdot(qe[a0:a0 + tb], s16[sg]))
            v_new = v_new_parts[0] if segs == 1 else jnp.concatenate(v_new_parts, axis=0)
            o_state = os_parts[0] if segs == 1 else jnp.concatenate(os_parts, axis=0)
            o = o_state + _bdot(attn, v_new)
            for sg in range(segs):
                a0 = sg * tb
                b = (r0 + a0) // tb
                glast = gcc[a0 + tb - 1:a0 + tb, :]
                k_dec = k[a0:a0 + tb] * jnp.exp(glast - gcc[a0:a0 + tb])
                s_cur[(b, hd)] = (s_cur[(b, hd)] * jnp.exp(glast)
                                  + _bdot_g(k_dec, v_new[a0:a0 + tb], _TN))

            on = o * lax.rsqrt(jnp.mean(o * o, axis=-1, keepdims=True) + EPS) * dnw
            o_heads.append(on * _silu(zdn[r0:r0 + GROUP, lo:lo + DN_D]))
        o_groups.append(jnp.concatenate(o_heads, axis=1))
    o_all = o_groups[0] if n_groups == 1 else jnp.concatenate(o_groups, axis=0)
    odn_ref[...] = o_all.astype(bf16).reshape(nb, tb, DN_WIDTH)
    s_ref[...] = jnp.stack([jnp.stack([s_cur[(b, hd)] for hd in range(DN_HEADS)]) for b in range(nb)])


def _delta_masks(tb):
    r = np.arange(GROUP)[:, None]
    c = np.arange(GROUP)[None, :]
    same = (r // tb) == (c // tb)
    ms = [same & (r >= c), same & (r > c)]
    for l in range(tb.bit_length() - 1):
        ms.append(((r >> (l + 1)) == (c >> (l + 1))) & (((r >> l) & 1) == 1) & (((c >> l) & 1) == 0))
    return jnp.asarray(np.stack(ms).astype(np.float32))


def _delta_call(x, w_in_r, norm_w, conv_w, alog, dtb, dnw, conv0, s0, *, nb, tb):
    B, T, _ = x.shape
    assert GROUP % tb == 0 and (nb * tb) % GROUP == 0 and B % nb == 0 and T % tb == 0
    grid = (B // nb, T // tb)
    const2 = lambda b, t: (0, 0)
    body = functools.partial(_delta_body, nb=nb, tb=tb)
    masks = _delta_masks(tb)
    one = pl.Buffered(1)
    return pl.pallas_call(
        body,
        grid=grid,
        in_specs=[
            pl.BlockSpec((nb, tb, D_MODEL), lambda b, t: (b, t, 0)),
            pl.BlockSpec((D_MODEL, IN_COLS_PAD), const2, pipeline_mode=one),
            pl.BlockSpec((1, D_MODEL), const2),
            pl.BlockSpec((CONV_K, CONV_CH), const2),
            pl.BlockSpec((1, LANES), const2),
            pl.BlockSpec((1, LANES), const2),
            pl.BlockSpec((1, DN_D), const2),
            pl.BlockSpec(masks.shape, lambda b, t: (0, 0, 0), pipeline_mode=one),
            pl.BlockSpec((nb, SUBLANES, CONV_CH), lambda b, t: (b, 0, 0)),
            pl.BlockSpec((nb, DN_HEADS, DN_D, DN_D), lambda b, t: (b, 0, 0, 0)),
        ],
        out_specs=[
            pl.BlockSpec((nb, tb, DN_WIDTH), lambda b, t: (b, t, 0)),
            pl.BlockSpec((nb, S5_PLANES, tb, LANES), lambda b, t: (b, 0, t, 0)),
            pl.BlockSpec((nb, tb, S5_WIDTH), lambda b, t: (b, t, 0)),
            pl.BlockSpec((nb, SUBLANES, CONV_CH), lambda b, t: (b, 0, 0)),
            pl.BlockSpec((nb, DN_HEADS, DN_D, DN_D), lambda b, t: (b, 0, 0, 0)),
        ],
        out_shape=[
            jax.ShapeDtypeStruct((B, T, DN_WIDTH), bf16),
            jax.ShapeDtypeStruct((B, S5_PLANES, T, LANES), f32),
            jax.ShapeDtypeStruct((B, T, S5_WIDTH), f32),
            jax.ShapeDtypeStruct((B, SUBLANES, CONV_CH), f32),
            jax.ShapeDtypeStruct((B, DN_HEADS, DN_D, DN_D), f32),
        ],
        scratch_shapes=[pltpu.VMEM((nb, tb + SUBLANES, CONV_CH), f32)],
        compiler_params=pltpu.CompilerParams(
            dimension_semantics=("arbitrary", "arbitrary"), vmem_limit_bytes=VMEM_LIMIT),
        name="delta",
    )(x, w_in_r, norm_w, conv_w, alog, dtb, dnw, masks, conv0, s0)


def _s5mat_body(cpr_ref, cpi_ref, bpr_ref, bpi_ref, t_ref):
    t = _xdot(cpr_ref[0], bpr_ref[0]) - _xdot(cpi_ref[0], bpi_ref[0])
    ri = lax.broadcasted_iota(jnp.int32, (S5_GL, S5_GL), 0) >> 4
    ci = lax.broadcasted_iota(jnp.int32, (S5_GL, S5_GL), 1) >> 4
    t_ref[0] = jnp.where(ri >= ci, t, 0.0).astype(bf16)


def _s5mat_call(cpr, cpi, bpr, bpi):
    return pl.pallas_call(
        _s5mat_body,
        grid=(S5_GROUPS,),
        in_specs=[
            pl.BlockSpec((1, S5_GL, S5_STATE), lambda g: (g, 0, 0)),
            pl.BlockSpec((1, S5_GL, S5_STATE), lambda g: (g, 0, 0)),
            pl.BlockSpec((1, S5_STATE, S5_GL), lambda g: (g, 0, 0)),
            pl.BlockSpec((1, S5_STATE, S5_GL), lambda g: (g, 0, 0)),
        ],
        out_specs=pl.BlockSpec((1, S5_GL, S5_GL), lambda g: (g, 0, 0)),
        out_shape=jax.ShapeDtypeStruct((S5_GROUPS, S5_GL, S5_GL), bf16),
        compiler_params=pltpu.CompilerParams(dimension_semantics=("arbitrary",)),
        name="s5mat",
    )(cpr, cpi, bpr, bpi)


def _s5_body(u_ref, z_ref, xr0_ref, xi0_ref, t_ref, wp_ref, vt_ref, ar_ref, ai_ref, d_ref,
             gw_ref, gb_ref,
             o_ref, xr_ref, xi_ref,
             at_scr, er_scr, ei_scr, xinr_scr, xini_scr, yt_scr, y_scr, *, tb, n_streams, n_chunks):
    R = tb // S5_L
    used = n_streams * n_chunks

    @pl.when(pl.program_id(1) == 0)
    def _init():
        xr_ref[...] = xr0_ref[...]
        xi_ref[...] = xi0_ref[...]

    for f in range(S5_L):
        for j in range(S5_PLANES):
            ut = u_ref[0, j, pl.ds(f, R, stride=S5_L), :]
            ut_t = ut.T.astype(bf16)
            for gl in range(LANES // S5_GROUP):
                g = j * (LANES // S5_GROUP) + gl
                at_scr[g, f * S5_GROUP:(f + 1) * S5_GROUP, :] = ut_t[gl * S5_GROUP:(gl + 1) * S5_GROUP, :]

    for p in range(S5_PAIRS):
        a_t = at_scr[2 * p:2 * p + 2].reshape(2 * S5_GL, R)
        e = lax.dot_general(a_t, wp_ref[p], _TN, preferred_element_type=f32)
        er_scr[:, p * LANES:(p + 1) * LANES] = e[:, :LANES]
        ei_scr[:, p * LANES:(p + 1) * LANES] = e[:, LANES:]

    if used < R:
        xinr_scr[...] = jnp.zeros_like(xinr_scr)
        xini_scr[...] = jnp.zeros_like(xini_scr)

    a_r = ar_ref[...]
    a_i = ai_ref[...]
    for s in range(n_streams):
        def step(c, carry):
            xr, xi = carry
            row = s * n_chunks + c
            xinr_scr[pl.ds(row, 1), :] = xr
            xini_scr[pl.ds(row, 1), :] = xi
            er = er_scr[pl.ds(row, 1), :]
            ei = ei_scr[pl.ds(row, 1), :]
            return a_r * xr - a_i * xi + er, a_r * xi + a_i * xr + ei

        xr, xi = lax.fori_loop(0, n_chunks, step, (xr_ref[0, s:s + 1, :], xi_ref[0, s:s + 1, :]))
        xr_ref[0, s:s + 1, :] = xr
        xi_ref[0, s:s + 1, :] = xi

    for g in range(S5_GROUPS):
        p = g // 2
        xin = jnp.concatenate([xinr_scr[:, p * LANES:(p + 1) * LANES],
                               xini_scr[:, p * LANES:(p + 1) * LANES]], axis=1).astype(bf16)
        yt_scr[g] = (jnp.dot(t_ref[g], at_scr[g], preferred_element_type=f32)
                     + lax.dot_general(vt_ref[g], xin, _NT, preferred_element_type=f32))

    for f in range(S5_L):
        gpp = LANES // S5_GROUP
        for j in range(S5_PLANES):
            yt = yt_scr[j * gpp:(j + 1) * gpp, f * S5_GROUP:(f + 1) * S5_GROUP, :].reshape(LANES, R)
            y_scr[j, pl.ds(f, R, stride=S5_L), :] = yt.T

    sub = 256
    for rb in range(tb // sub):
        sl = pl.ds(rb * sub, sub)
        y_intra = jnp.concatenate([y_scr[j, sl, :] for j in range(S5_PLANES)], axis=1)
        u_nat = jnp.concatenate([u_ref[0, j, sl, :] for j in range(S5_PLANES)], axis=1)
        y = y_intra + d_ref[...] * u_nat
        gy = _gelu_tanh(y)
        gate = _sigmoid(_bdot(gy, gw_ref[...]) + gb_ref[...])
        o_ref[0, sl, :] = (gy * gate * _silu(z_ref[0, sl, :])).astype(bf16)


def _s5_call(u, z, xr0, xi0, tmat, wp, vt, ar, ai, dvec, gw, gb, *, tb, n_streams, n_chunks):
    G, T, _ = z.shape
    grid = (G, T // tb)
    R = tb // S5_L
    nstate = S5_GROUPS * S5_STATE
    c2 = lambda g, t: (0, 0)
    c3 = lambda g, t: (0, 0, 0)
    body = functools.partial(_s5_body, tb=tb, n_streams=n_streams, n_chunks=n_chunks)
    one = pl.Buffered(1)
    return pl.pallas_call(
        body,
        grid=grid,
        in_specs=[
            pl.BlockSpec((1, S5_PLANES, tb, LANES), lambda g, t: (g, 0, t, 0)),
            pl.BlockSpec((1, tb, S5_WIDTH), lambda g, t: (g, t, 0)),
            pl.BlockSpec((1, n_streams, nstate), lambda g, t: (g, 0, 0)),
            pl.BlockSpec((1, n_streams, nstate), lambda g, t: (g, 0, 0)),
            pl.BlockSpec((S5_GROUPS, S5_GL, S5_GL), c3, pipeline_mode=one),
            pl.BlockSpec((S5_PAIRS, 2 * S5_GL, 2 * LANES), c3, pipeline_mode=one),
            pl.BlockSpec((S5_GROUPS, S5_GL, 2 * LANES), c3, pipeline_mode=one),
            pl.BlockSpec((1, nstate), c2),
            pl.BlockSpec((1, nstate), c2),
            pl.BlockSpec((1, S5_WIDTH), c2),
            pl.BlockSpec((S5_WIDTH, S5_WIDTH), c2),
            pl.BlockSpec((1, S5_WIDTH), c2),
        ],
        out_specs=[
            pl.BlockSpec((1, tb, S5_WIDTH), lambda g, t: (g, t, 0)),
            pl.BlockSpec((1, n_streams, nstate), lambda g, t: (g, 0, 0)),
            pl.BlockSpec((1, n_streams, nstate), lambda g, t: (g, 0, 0)),
        ],
        out_shape=[
            jax.ShapeDtypeStruct((G, T, S5_WIDTH), bf16),
            jax.ShapeDtypeStruct((G, n_streams, nstate), f32),
            jax.ShapeDtypeStruct((G, n_streams, nstate), f32),
        ],
        scratch_shapes=[
            pltpu.VMEM((S5_GROUPS, S5_GL, R), bf16),
            pltpu.VMEM((R, nstate), f32),
            pltpu.VMEM((R, nstate), f32),
            pltpu.VMEM((R, nstate), f32),
            pltpu.VMEM((R, nstate), f32),
            pltpu.VMEM((S5_GROUPS, S5_GL, R), f32),
            pltpu.VMEM((S5_PLANES, tb, LANES), f32),
        ],
        compiler_params=pltpu.CompilerParams(
            dimension_semantics=("arbitrary", "arbitrary"), vmem_limit_bytes=VMEM_LIMIT),
        name="s5",
    )(u, z, xr0, xi0, tmat, wp, vt, ar, ai, dvec, gw, gb)


def _out_body(x_ref, odn_ref, os5_ref, wo_ref, fw_ref, y_ref):
    acc = (x_ref[...]
           + jnp.dot(odn_ref[...], wo_ref[0:DN_WIDTH, :], preferred_element_type=f32)
           + jnp.dot(os5_ref[...], wo_ref[DN_WIDTH:, :], preferred_element_type=f32))
    ms = jnp.mean(acc * acc, axis=-1, keepdims=True)
    y_ref[...] = acc * lax.rsqrt(ms + EPS) * fw_ref[...]


def _out_call(x2, odn2, os52, w_out16, fw, *, rows):
    n = x2.shape[0]
    return pl.pallas_call(
        _out_body,
        grid=(n // rows,),
        in_specs=[
            pl.BlockSpec((rows, D_MODEL), lambda i: (i, 0)),
            pl.BlockSpec((rows, DN_WIDTH), lambda i: (i, 0)),
            pl.BlockSpec((rows, S5_WIDTH), lambda i: (i, 0)),
            pl.BlockSpec((D_MODEL, D_MODEL), lambda i: (0, 0)),
            pl.BlockSpec((1, D_MODEL), lambda i: (0, 0)),
        ],
        out_specs=pl.BlockSpec((rows, D_MODEL), lambda i: (i, 0)),
        out_shape=jax.ShapeDtypeStruct((n, D_MODEL), f32),
        compiler_params=pltpu.CompilerParams(
            dimension_semantics=("arbitrary",), vmem_limit_bytes=VMEM_LIMIT),
        name="outproj",
    )(x2, odn2, os52, w_out16, fw)


def _s5_operands(a_re, a_im, log_dt, b_re, b_im, c_re, c_im):
    lam_re = jnp.minimum(a_re, -1e-4)
    lam_im = a_im
    dt = jnp.exp(log_dt)[:, None]
    ldt_re, ldt_im = lam_re * dt, lam_im * dt

    def lpow(k):
        mag = jnp.exp(ldt_re * k)
        return mag * jnp.cos(ldt_im * k), mag * jnp.sin(ldt_im * k)

    lb_re, lb_im = lpow(1.0)
    den = lam_re * lam_re + lam_im * lam_im
    f_re = ((lb_re - 1.0) * lam_re + lb_im * lam_im) / den
    f_im = (lb_im * lam_re - (lb_re - 1.0) * lam_im) / den
    bb_re = f_re[..., None] * b_re - f_im[..., None] * b_im
    bb_im = f_re[..., None] * b_im + f_im[..., None] * b_re

    fr = jnp.arange(S5_L, dtype=f32)
    G, N = S5_GROUPS, S5_STATE

    def cmul(ar, ai, br, bi):
        return ar * br - ai * bi, ar * bi + ai * br

    pr, pi = lpow(fr[:, None, None])
    pr, pi = pr.transpose(1, 0, 2), pi.transpose(1, 0, 2)
    cpr, cpi = cmul(c_re[:, None], c_im[:, None], pr[:, :, None], pi[:, :, None])
    cpr, cpi = cpr.reshape(G, S5_GL, N), cpi.reshape(G, S5_GL, N)
    qr, qi = lpow(-fr[:, None, None])
    qr, qi = qr.transpose(1, 2, 0), qi.transpose(1, 2, 0)
    bpr, bpi = cmul(qr[..., None], qi[..., None], bb_re[:, :, None], bb_im[:, :, None])
    bpr, bpi = bpr.reshape(G, N, S5_GL), bpi.reshape(G, N, S5_GL)

    wr_, wi_ = lpow((S5_L - 1.0) - fr[:, None, None])
    wr_, wi_ = wr_.transpose(1, 0, 2), wi_.transpose(1, 0, 2)
    bbr_t, bbi_t = bb_re.transpose(0, 2, 1), bb_im.transpose(0, 2, 1)
    w_re, w_im = cmul(wr_[:, :, None], wi_[:, :, None], bbr_t[:, None], bbi_t[:, None])
    w_re, w_im = w_re.reshape(G, S5_GL, N), w_im.reshape(G, S5_GL, N)
    vr_, vi_ = lpow(fr[:, None, None] + 1.0)
    vr_, vi_ = vr_.transpose(1, 0, 2), vi_.transpose(1, 0, 2)
    v_re, v_im = cmul(c_re[:, None], c_im[:, None], vr_[:, :, None], vi_[:, :, None])
    v_re, v_im = v_re.reshape(G, S5_GL, N), v_im.reshape(G, S5_GL, N)

    zeros = jnp.zeros((S5_PAIRS, S5_GL, N), f32)

    def pair_cols(re, im):
        re, im = re.reshape(S5_PAIRS, 2, S5_GL, N), im.reshape(S5_PAIRS, 2, S5_GL, N)
        even = jnp.concatenate([re[:, 0], zeros, im[:, 0], zeros], axis=-1)
        odd = jnp.concatenate([zeros, re[:, 1], zeros, im[:, 1]], axis=-1)
        return jnp.stack([even, odd], axis=1)

    wp = pair_cols(w_re, w_im).reshape(S5_PAIRS, 2 * S5_GL, 2 * LANES).astype(bf16)
    vt = pair_cols(v_re, -v_im).reshape(S5_GROUPS, S5_GL, 2 * LANES).astype(bf16)
    a16r, a16i = lpow(float(S5_L))
    return cpr, cpi, bpr, bpi, wp, vt, a16r.reshape(1, G * N), a16i.reshape(1, G * N)


def _pad_lanes(v, start=0):
    out = jnp.zeros((1, LANES), f32)
    return out.at[0, start:start + v.shape[0]].set(v.astype(f32))


def _layer(x, conv0, s0, xr0, xi0, prm, *, delta_cfg, s5_cfg, out_rows):
    B, T, _ = x.shape
    conv0p = jnp.concatenate([jnp.zeros((B, SUBLANES - (CONV_K - 1), CONV_CH), f32), conv0], axis=1)
    odn, us5, zs5, convout, s_new = _delta_call(
        x, prm["w_in_r"], prm["norm_w"], prm["conv_w"], prm["alog"], prm["dtb"], prm["dnw"],
        conv0p, s0, **delta_cfg)

    nstate = S5_GROUPS * S5_STATE
    if s5_cfg["flatten"]:
        tb = s5_cfg["tb"]
        uf = us5.transpose(1, 0, 2, 3).reshape(1, S5_PLANES, B * T, LANES)
        zf = zs5.reshape(1, B * T, S5_WIDTH)
        pad = tb - B * T
        uf = jnp.pad(uf, ((0, 0), (0, 0), (0, pad), (0, 0)))
        zf = jnp.pad(zf, ((0, 0), (0, pad), (0, 0)))
        os5, xr, xi = _s5_call(uf, zf, xr0.reshape(1, B, nstate), xi0.reshape(1, B, nstate),
                               prm["tmat"], prm["wp"], prm["vt"], prm["ar"], prm["ai"], prm["dvec"],
                               prm["gw"], prm["gb"], tb=tb, n_streams=B, n_chunks=T // S5_L)
        os5 = os5[0, :B * T].reshape(B, T, S5_WIDTH)
    else:
        tb = s5_cfg["tb"]
        os5, xr, xi = _s5_call(us5, zs5, xr0.reshape(B, 1, nstate), xi0.reshape(B, 1, nstate),
                               prm["tmat"], prm["wp"], prm["vt"], prm["ar"], prm["ai"], prm["dvec"],
                               prm["gw"], prm["gb"], tb=tb, n_streams=1, n_chunks=tb // S5_L)

    y = _out_call(x.reshape(B * T, D_MODEL), odn.reshape(B * T, DN_WIDTH), os5.reshape(B * T, S5_WIDTH),
                  prm["w_out"], prm["fw"], rows=out_rows)
    return (y.reshape(B, T, D_MODEL), convout[:, SUBLANES - (CONV_K - 1):, :], s_new,
            xr.reshape(B, S5_GROUPS, S5_STATE), xi.reshape(B, S5_GROUPS, S5_STATE))


def kernel(x_prompt, x_sample, cache_conv, state_dn, state_s5_re, state_s5_im, norm_w, w_in, conv_w, dn_A_log, dn_dt_bias, dn_norm_w, s5_A_re, s5_A_im, s5_log_dt, s5_B_re, s5_B_im, s5_C_re, s5_C_im, s5_D, glu_w, glu_b, w_out, final_norm_w):
    depth = norm_w.shape[0]
    assert depth == 1
    l = 0
    wi = w_in[l]
    o0 = CONV_CH
    o2 = o0 + 2 * DN_HEADS
    w_in_r = jnp.concatenate(
        [wi[:, :o0], wi[:, o2:], wi[:, o0:o2], jnp.zeros((D_MODEL, LANES - 2 * DN_HEADS), f32)],
        axis=1).astype(bf16)
    cpr, cpi, bpr, bpi, wp, vt, ar, ai = _s5_operands(
        s5_A_re[l].astype(f32), s5_A_im[l].astype(f32), s5_log_dt[l].astype(f32),
        s5_B_re[l].astype(f32), s5_B_im[l].astype(f32), s5_C_re[l].astype(f32), s5_C_im[l].astype(f32))
    prm = dict(
        w_in_r=w_in_r,
        norm_w=norm_w[l].reshape(1, D_MODEL).astype(f32),
        conv_w=conv_w[l].astype(f32),
        alog=_pad_lanes(dn_A_log[l]),
        dtb=_pad_lanes(dn_dt_bias[l]),
        dnw=dn_norm_w[l].reshape(1, DN_D).astype(f32),
        tmat=_s5mat_call(cpr, cpi, bpr, bpi),
        wp=wp, vt=vt, ar=ar, ai=ai,
        dvec=s5_D[l].reshape(1, S5_WIDTH).astype(f32),
        gw=glu_w[l].astype(bf16),
        gb=glu_b[l].reshape(1, S5_WIDTH).astype(f32),
        w_out=w_out[l].astype(bf16),
        fw=final_norm_w.reshape(1, D_MODEL).astype(f32),
    )

    bp = x_prompt.shape[0]
    yp, c1, d1, r1, i1 = _layer(
        x_prompt,
        jnp.zeros((bp, CONV_K - 1, CONV_CH), f32),
        jnp.zeros((bp, DN_HEADS, DN_D, DN_D), f32),
        jnp.zeros((bp, S5_GROUPS, S5_STATE), f32),
        jnp.zeros((bp, S5_GROUPS, S5_STATE), f32),
        prm,
        delta_cfg=dict(nb=2, tb=256),
        s5_cfg=dict(flatten=False, tb=2048),
        out_rows=512)
    ys, c2, d2, r2, i2 = _layer(
        x_sample, cache_conv[l].astype(f32), state_dn[l].astype(f32),
        state_s5_re[l].astype(f32), state_s5_im[l].astype(f32),
        prm,
        delta_cfg=dict(nb=16, tb=32),
        s5_cfg=dict(flatten=True, tb=2048),
        out_rows=512)

    return (yp, ys, c1[None], d1[None], r1[None], i1[None], c2[None], d2[None], r2[None], i2[None])
```

```python
import functools
import math

import jax
import jax.numpy as jnp
import numpy as np
from jax import lax
from jax.experimental import pallas as pl
from jax.experimental.pallas import tpu as pltpu

bf16 = jnp.bfloat16
f32 = jnp.float32

LANES = 128
SUBLANES = 8

D_MODEL = 1024
DN_HEADS = 4
DN_D = 128
DN_WIDTH = DN_HEADS * DN_D
CONV_K = 4
CONV_CH = 3 * DN_WIDTH
S5_WIDTH = 512
S5_GROUP = 16
S5_GROUPS = 32
S5_STATE = 64
S5_PAIRS = S5_GROUPS // 2
S5_L = 16
S5_GL = S5_GROUP * S5_L
S5_PLANES = S5_WIDTH // LANES
EPS = 1e-6
L2_EPS = 1e-6

OFF_QKV = 0
OFF_ZDN = CONV_CH
OFF_US5 = OFF_ZDN + DN_WIDTH
OFF_ZS5 = OFF_US5 + S5_WIDTH
OFF_AB = OFF_ZS5 + S5_WIDTH
IN_COLS_PAD = OFF_AB + 128

GROUP = 256
MASK_INCL, MASK_STRICT, MASK_OFF = 0, 1, 2
VMEM_LIMIT =56 * 1024 * 1024

_NT = (((1,), (1,)), ((), ()))
_TN = (((0,), (0,)), ((), ()))


def _bdot(a, b):
    return jnp.dot(a.astype(bf16), b.astype(bf16), preferred_element_type=f32)


def _bdot_g(a, b, dims):
    return lax.dot_general(a.astype(bf16), b.astype(bf16), dims, preferred_element_type=f32)


def _xdot(a, b):
    return jnp.dot(a, b, precision=lax.Precision.HIGHEST, preferred_element_type=f32)


def _sigmoid(x):
    return 1.0 / (1.0 + jnp.exp(-x))


def _silu(x):
    return x * _sigmoid(x)


def _softplus(x):
    return jnp.maximum(x, 0.0) + jnp.log1p(jnp.exp(-jnp.abs(x)))


def _gelu_tanh(x):
    c = math.sqrt(2.0 / math.pi)
    return 0.5 * x * (1.0 + jnp.tanh(c * (x + 0.044715 * (x * x * x))))


def _split16(a):
    hi = a.astype(bf16)
    return hi, (a - hi.astype(f32)).astype(bf16)


def _unit_lower_solve(nmats, rhss, eye, m_ref, n_levels):
    d32 = functools.partial(jnp.dot, preferred_element_type=f32)
    def d16(a, b):
        return d32(a, b).astype(bf16)

    n_hi, n_lo = zip(*[_split16(n) for n in nmats])
    invs = [(eye - n * m_ref[MASK_OFF]).astype(bf16) for n in nmats]
    for lvl in range(1, n_levels):
        mask16 = m_ref[MASK_OFF + lvl].astype(bf16)
        t1 = [d16(inv, nh * mask16) for inv, nh in zip(invs, n_hi)]
        t2 = [d16(t, inv) for t, inv in zip(t1, invs)]
        invs = [inv - t for inv, t in zip(invs, t2)]
    x0 = [d32(inv, r.astype(bf16)) for inv, r in zip(invs, rhss)]
    x_hi, x_lo = zip(*[_split16(x) for x in x0])
    nx = [d32(nh, xh) + (d32(nh, xl) + d32(nl, xh)) for nh, nl, xh, xl in zip(n_hi, n_lo, x_hi, x_lo)]
    resid = [(r - x - p).astype(bf16) for r, x, p in zip(rhss, x0, nx)]
    return [x + d32(inv, rs) for x, inv, rs in zip(x0, invs, resid)]


def _segment_cumsum(tri16, g):
    g1 = g.astype(bf16)
    r1 = g - g1.astype(f32)
    g2 = r1.astype(bf16)
    g3 = (r1 - g2.astype(f32)).astype(bf16)
    s = jnp.dot(tri16, jnp.concatenate([g1, g2, g3], axis=1), preferred_element_type=f32)
    n = g.shape[1]
    return s[:, :n] + (s[:, n:2 * n] + s[:, 2 * n:])


def _delta_body(x_ref, w_ref, nw_ref, cw_ref, alog_ref, dtb_ref, dnw_ref, m_ref, conv0_ref, s0_ref,
                odn_ref, us5_ref, zs5_ref, convout_ref, s_ref, cbuf, *, nb, tb):
    rows = nb * tb
    n_groups = rows // GROUP
    segs = GROUP // tb
    n_levels = tb.bit_length() - 1

    @pl.when(pl.program_id(1) == 0)
    def _init():
        cbuf[:, 0:SUBLANES, :] = conv0_ref[...]
        s_ref[...] = s0_ref[...]

    x = x_ref[...].reshape(rows, D_MODEL)
    ms = jnp.mean(x * x, axis=-1, keepdims=True)
    h = (x * lax.rsqrt(ms + EPS) * nw_ref[...]).astype(bf16)

    qkv_raw = jnp.dot(h, w_ref[:, OFF_QKV:OFF_QKV + CONV_CH], preferred_element_type=f32)
    qkv_parts = []
    tails = []
    for b in range(nb):
        cbuf[b, SUBLANES:SUBLANES + tb, :] = qkv_raw[b * tb:(b + 1) * tb, :]
        acc = cbuf[b, 5:5 + tb, :] * cw_ref[0:1, :]
        for j in range(1, CONV_K):
            acc = acc + cbuf[b, 5 + j:5 + j + tb, :] * cw_ref[j:j + 1, :]
        qkv_parts.append(_silu(acc))
        tail = cbuf[b, tb:tb + SUBLANES, :]
        cbuf[b, 0:SUBLANES, :] = tail
        tails.append(tail)
    convout_ref[...] = jnp.stack(tails)
    qkv = qkv_parts[0] if nb == 1 else jnp.concatenate(qkv_parts, axis=0)

    for j in range(S5_PLANES):
        lo = OFF_US5 + j * LANES
        us5_ref[:, j] = jnp.dot(h, w_ref[:, lo:lo + LANES], preferred_element_type=f32).reshape(nb, tb, LANES)
    zs5_ref[...] = jnp.dot(h, w_ref[:, OFF_ZS5:OFF_ZS5 + S5_WIDTH],
                           preferred_element_type=f32).reshape(nb, tb, S5_WIDTH)
    zdn = jnp.dot(h, w_ref[:, OFF_ZDN:OFF_ZDN + DN_WIDTH], preferred_element_type=f32)
    ab = jnp.dot(h, w_ref[:, OFF_AB:OFF_AB + LANES], preferred_element_type=f32)

    g_full = -jnp.exp(alog_ref[...]) * _softplus(ab + dtb_ref[...])
    beta_full = _sigmoid(ab)

    incl = m_ref[MASK_INCL]
    strict = m_ref[MASK_STRICT]
    eye = incl - strict
    tri16 = incl.astype(bf16)
    dnw = dnw_ref[...]

    chains = [(gi, hd) for gi in range(n_groups) for hd in range(DN_HEADS)]
    gc_blks = [_segment_cumsum(tri16, g_full[gi * GROUP:(gi + 1) * GROUP]) for gi in range(n_groups)]
    qs, ks, vbs, kbs, gccs = [], [], [], [], []
    for gi, hd in chains:
        r0, lo = gi * GROUP, hd * DN_D
        q = qkv[r0:r0 + GROUP, lo:lo + DN_D]
        k = qkv[r0:r0 + GROUP, DN_WIDTH + lo:DN_WIDTH + lo + DN_D]
        v = qkv[r0:r0 + GROUP, 2 * DN_WIDTH + lo:2 * DN_WIDTH + lo + DN_D]
        q = q * lax.rsqrt(jnp.sum(q * q, axis=-1, keepdims=True) + L2_EPS) * (DN_D ** -0.5)
        k = k * lax.rsqrt(jnp.sum(k * k, axis=-1, keepdims=True) + L2_EPS)
        beta = beta_full[r0:r0 + GROUP, DN_HEADS + hd:DN_HEADS + hd + 1]
        qs.append(q)
        ks.append(k)
        vbs.append(v * beta)
        kbs.append(k * beta)
        gccs.append(gc_blks[gi][:, hd:hd + 1])
    egs = [jnp.exp(g) for g in gccs]

    dmats = []
    for g in gccs:
        gcb = jnp.broadcast_to(g, (GROUP, GROUP))
        dmats.append(jnp.exp((gcb - gcb.T) * incl) * incl)
    k16s = [k.astype(bf16) for k in ks]
    nmats = [_bdot_g(kb, k16, _NT) * (d * strict) for kb, k16, d in zip(kbs, k16s, dmats)]
    attns = [(_bdot_g(q, k16, _NT) * d).astype(bf16) for q, k16, d in zip(qs, k16s, dmats)]
    rhss = [jnp.concatenate([vb, kb * eg], axis=1) for vb, kb, eg in zip(vbs, kbs, egs)]
    sols = _unit_lower_solve(nmats, rhss, eye, m_ref, n_levels)

    def stream_of(gi, sg):
        return (gi * GROUP) // tb + sg

    s_old = {(b, hd): s_ref[b, hd] for b in range(nb) for hd in range(DN_HEADS)}
    s16 = {key: val.astype(bf16) for key, val in s_old.items()}
    v_news, o_states = [], []
    for (gi, hd), sol, q, eg in zip(chains, sols, qs, egs):
        u = sol[:, :DN_D]
        w16 = sol[:, DN_D:].astype(bf16)
        qe16 = (q * eg).astype(bf16)
        vn, os_ = [], []
        for sg in range(segs):
            a0 = sg * tb
            st = s16[(stream_of(gi, sg), hd)]
            vn.append(u[a0:a0 + tb] - jnp.dot(w16[a0:a0 + tb], st, preferred_element_type=f32))
            os_.append(jnp.dot(qe16[a0:a0 + tb], st, preferred_element_type=f32))
        v_news.append(vn[0] if segs == 1 else jnp.concatenate(vn, axis=0))
        o_states.append(os_[0] if segs == 1 else jnp.concatenate(os_, axis=0))
    vn16s = [vn.astype(bf16) for vn in v_news]
    outs = [os_ + jnp.dot(a, vn, preferred_element_type=f32) for os_, a, vn in zip(o_states, attns, vn16s)]

    s_new = {}
    for (gi, hd), k, g, vn in zip(chains, ks, gccs, vn16s):
        for sg in range(segs):
            a0 = sg * tb
            key = (stream_of(gi, sg), hd)
            glast = g[a0 + tb - 1:a0 + tb, :]
            k_dec = (k[a0:a0 + tb] * jnp.exp(glast - g[a0:a0 + tb])).astype(bf16)
            s_new[key] = (s_old[key] * jnp.exp(glast)
                          + lax.dot_general(k_dec, vn[a0:a0 + tb], _TN, preferred_element_type=f32))

    o_rows = []
    for gi in range(n_groups):
        o_heads = []
        for hd in range(DN_HEADS):
            o = outs[gi * DN_HEADS + hd]
            on = o * lax.rsqrt(jnp.mean(o * o, axis=-1, keepdims=True) + EPS) * dnw
            o_heads.append(on * _silu(zdn[gi * GROUP:(gi + 1) * GROUP, hd * DN_D:(hd + 1) * DN_D]))
        o_rows.append(jnp.concatenate(o_heads, axis=1))
    o_all = o_rows[0] if n_groups == 1 else jnp.concatenate(o_rows, axis=0)
    odn_ref[...] = o_all.astype(bf16).reshape(nb, tb, DN_WIDTH)
    s_ref[...] = jnp.stack([jnp.stack([s_new[(b, hd)] for hd in range(DN_HEADS)]) for b in range(nb)])


def _delta_masks(tb):
    r = np.arange(GROUP)[:, None]
    c = np.arange(GROUP)[None, :]
    same = (r // tb) == (c // tb)
    ms = [same & (r >= c), same & (r > c)]
    for l in range(tb.bit_length() - 1):
        ms.append(((r >> (l + 1)) == (c >> (l + 1))) & (((r >> l) & 1) == 1) & (((c >> l) & 1) == 0))
    return jnp.asarray(np.stack(ms).astype(np.float32))


def _delta_call(x, w_in_r, norm_w, conv_w, alog, dtb, dnw, conv0, s0, *, nb, tb):
    B, T, _ = x.shape
    assert GROUP % tb == 0 and (nb * tb) % GROUP == 0 and B % nb == 0 and T % tb == 0
    grid = (B // nb, T // tb)
    const2 = lambda b, t: (0, 0)
    body = functools.partial(_delta_body, nb=nb, tb=tb)
    masks = _delta_masks(tb)
    one = pl.Buffered(1)
    return pl.pallas_call(
        body,
        grid=grid,
        in_specs=[
            pl.BlockSpec((nb, tb, D_MODEL), lambda b, t: (b, t, 0)),
            pl.BlockSpec((D_MODEL, IN_COLS_PAD), const2, pipeline_mode=one),
            pl.BlockSpec((1, D_MODEL), const2),
            pl.BlockSpec((CONV_K, CONV_CH), const2),
            pl.BlockSpec((1, LANES), const2),
            pl.BlockSpec((1, LANES), const2),
            pl.BlockSpec((1, DN_D), const2),
            pl.BlockSpec(masks.shape, lambda b, t: (0, 0, 0), pipeline_mode=one),
            pl.BlockSpec((nb, SUBLANES, CONV_CH), lambda b, t: (b, 0, 0)),
            pl.BlockSpec((nb, DN_HEADS, DN_D, DN_D), lambda b, t: (b, 0, 0, 0)),
        ],
        out_specs=[
            pl.BlockSpec((nb, tb, DN_WIDTH), lambda b, t: (b, t, 0)),
            pl.BlockSpec((nb, S5_PLANES, tb, LANES), lambda b, t: (b, 0, t, 0)),
            pl.BlockSpec((nb, tb, S5_WIDTH), lambda b, t: (b, t, 0)),
            pl.BlockSpec((nb, SUBLANES, CONV_CH), lambda b, t: (b, 0, 0)),
            pl.BlockSpec((nb, DN_HEADS, DN_D, DN_D), lambda b, t: (b, 0, 0, 0)),
        ],
        out_shape=[
            jax.ShapeDtypeStruct((B, T, DN_WIDTH), bf16),
            jax.ShapeDtypeStruct((B, S5_PLANES, T, LANES), f32),
            jax.ShapeDtypeStruct((B, T, S5_WIDTH), f32),
            jax.ShapeDtypeStruct((B, SUBLANES, CONV_CH), f32),
            jax.ShapeDtypeStruct((B, DN_HEADS, DN_D, DN_D), f32),
        ],
        scratch_shapes=[pltpu.VMEM((nb, tb + SUBLANES, CONV_CH), f32)],
        compiler_params=pltpu.CompilerParams(
            dimension_semantics=("arbitrary", "arbitrary"), vmem_limit_bytes=VMEM_LIMIT),
        name="delta",
    )(x, w_in_r, norm_w, conv_w, alog, dtb, dnw, masks, conv0, s0)


def _s5mat_body(cpr_ref, cpi_ref, bpr_ref, bpi_ref, t_ref):
    t = _xdot(cpr_ref[0], bpr_ref[0]) - _xdot(cpi_ref[0], bpi_ref[0])
    ri = lax.broadcasted_iota(jnp.int32, (S5_GL, S5_GL), 0) >> 4
    ci = lax.broadcasted_iota(jnp.int32, (S5_GL, S5_GL), 1) >> 4
    t_ref[0] = jnp.where(ri >= ci, t, 0.0).astype(bf16)


def _s5mat_call(cpr, cpi, bpr, bpi):
    return pl.pallas_call(
        _s5mat_body,
        grid=(S5_GROUPS,),
        in_specs=[
            pl.BlockSpec((1, S5_GL, S5_STATE), lambda g: (g, 0, 0)),
            pl.BlockSpec((1, S5_GL, S5_STATE), lambda g: (g, 0, 0)),
            pl.BlockSpec((1, S5_STATE, S5_GL), lambda g: (g, 0, 0)),
            pl.BlockSpec((1, S5_STATE, S5_GL), lambda g: (g, 0, 0)),
        ],
        out_specs=pl.BlockSpec((1, S5_GL, S5_GL), lambda g: (g, 0, 0)),
        out_shape=jax.ShapeDtypeStruct((S5_GROUPS, S5_GL, S5_GL), bf16),
        compiler_params=pltpu.CompilerParams(dimension_semantics=("arbitrary",)),
        name="s5mat",
    )(cpr, cpi, bpr, bpi)


def _s5_body(u_ref, z_ref, xr0_ref, xi0_ref, t_ref, wp_ref, vt_ref, ar_ref, ai_ref, d_ref,
             gw_ref, gb_ref,
             o_ref, xr_ref, xi_ref,
             at_scr, er_scr, ei_scr, xinr_scr, xini_scr, yt_scr, y_scr, *, tb, n_streams, n_chunks):
    R = tb // S5_L
    used = n_streams * n_chunks

    @pl.when(pl.program_id(1) == 0)
    def _init():
        xr_ref[...] = xr0_ref[...]
        xi_ref[...] = xi0_ref[...]

    for f in range(S5_L):
        for j in range(S5_PLANES):
            ut = u_ref[0, j, pl.ds(f, R, stride=S5_L), :]
            ut_t = ut.T.astype(bf16)
            for gl in range(LANES // S5_GROUP):
                g = j * (LANES // S5_GROUP) + gl
                at_scr[g, f * S5_GROUP:(f + 1) * S5_GROUP, :] = ut_t[gl * S5_GROUP:(gl + 1) * S5_GROUP, :]

    for p in range(S5_PAIRS):
        a_t = at_scr[2 * p:2 * p + 2].reshape(2 * S5_GL, R)
        e = lax.dot_general(a_t, wp_ref[p], _TN, preferred_element_type=f32)
        er_scr[:, p * LANES:(p + 1) * LANES] = e[:, :LANES]
        ei_scr[:, p * LANES:(p + 1) * LANES] = e[:, LANES:]

    if used < R:
        xinr_scr[...] = jnp.zeros_like(xinr_scr)
        xini_scr[...] = jnp.zeros_like(xini_scr)

    a_r = ar_ref[...]
    a_i = ai_ref[...]
    for s in range(n_streams):
        def step(c, carry):
            xr, xi = carry
            row = s * n_chunks + c
            xinr_scr[pl.ds(row, 1), :] = xr
            xini_scr[pl.ds(row, 1), :] = xi
            er = er_scr[pl.ds(row, 1), :]
            ei = ei_scr[pl.ds(row, 1), :]
            return a_r * xr - a_i * xi + er, a_r * xi + a_i * xr + ei

        xr, xi = lax.fori_loop(0, n_chunks, step, (xr_ref[0, s:s + 1, :], xi_ref[0, s:s + 1, :]))
        xr_ref[0, s:s + 1, :] = xr
        xi_ref[0, s:s + 1, :] = xi

    for g in range(S5_GROUPS):
        p = g // 2
        xin = jnp.concatenate([xinr_scr[:, p * LANES:(p + 1) * LANES],
                               xini_scr[:, p * LANES:(p + 1) * LANES]], axis=1).astype(bf16)
        yt_scr[g] = (jnp.dot(t_ref[g], at_scr[g], preferred_element_type=f32)
                     + lax.dot_general(vt_ref[g], xin, _NT, preferred_element_type=f32))

    for f in range(S5_L):
        gpp = LANES // S5_GROUP
        for j in range(S5_PLANES):
            yt = yt_scr[j * gpp:(j + 1) * gpp, f * S5_GROUP:(f + 1) * S5_GROUP, :].reshape(LANES, R)
            y_scr[j, pl.ds(f, R, stride=S5_L), :] = yt.T

    sub = 256
    for rb in range(tb // sub):
        sl = pl.ds(rb * sub, sub)
        y_intra = jnp.concatenate([y_scr[j, sl, :] for j in range(S5_PLANES)], axis=1)
        u_nat = jnp.concatenate([u_ref[0, j, sl, :] for j in range(S5_PLANES)], axis=1)
        y = y_intra + d_ref[...] * u_nat
        gy = _gelu_tanh(y)
        gate = _sigmoid(_bdot(gy, gw_ref[...]) + gb_ref[...])
        o_ref[0, sl, :] = (gy * gate * _silu(z_ref[0, sl, :])).astype(bf16)


def _s5_call(u, z, xr0, xi0, tmat, wp, vt, ar, ai, dvec, gw, gb, *, tb, n_streams, n_chunks):
    G, T, _ = z.shape
    grid = (G, T // tb)
    R = tb // S5_L
    nstate = S5_GROUPS * S5_STATE
    c2 = lambda g, t: (0, 0)
    c3 = lambda g, t: (0, 0, 0)
    body = functools.partial(_s5_body, tb=tb, n_streams=n_streams, n_chunks=n_chunks)
    one = pl.Buffered(1)
    return pl.pallas_call(
        body,
        grid=grid,
        in_specs=[
            pl.BlockSpec((1, S5_PLANES, tb, LANES), lambda g, t: (g, 0, t, 0)),
            pl.BlockSpec((1, tb, S5_WIDTH), lambda g, t: (g, t, 0)),
            pl.BlockSpec((1, n_streams, nstate), lambda g, t: (g, 0, 0)),
            pl.BlockSpec((1, n_streams, nstate), lambda g, t: (g, 0, 0)),
            pl.BlockSpec((S5_GROUPS, S5_GL, S5_GL), c3, pipeline_mode=one),
            pl.BlockSpec((S5_PAIRS, 2 * S5_GL, 2 * LANES), c3, pipeline_mode=one),
            pl.BlockSpec((S5_GROUPS, S5_GL, 2 * LANES), c3, pipeline_mode=one),
            pl.BlockSpec((1, nstate), c2),
            pl.BlockSpec((1, nstate), c2),
            pl.BlockSpec((1, S5_WIDTH), c2),
            pl.BlockSpec((S5_WIDTH, S5_WIDTH), c2),
            pl.BlockSpec((1, S5_WIDTH), c2),
        ],
        out_specs=[
            pl.BlockSpec((1, tb, S5_WIDTH), lambda g, t: (g, t, 0)),
            pl.BlockSpec((1, n_streams, nstate), lambda g, t: (g, 0, 0)),
            pl.BlockSpec((1, n_streams, nstate), lambda g, t: (g, 0, 0)),
        ],
        out_shape=[
            jax.ShapeDtypeStruct((G, T, S5_WIDTH), bf16),
            jax.ShapeDtypeStruct((G, n_streams, nstate), f32),
            jax.ShapeDtypeStruct((G, n_streams, nstate), f32),
        ],
        scratch_shapes=[
            pltpu.VMEM((S5_GROUPS, S5_GL, R), bf16),
            pltpu.VMEM((R, nstate), f32),
            pltpu.VMEM((R, nstate), f32),
            pltpu.VMEM((R, nstate), f32),
            pltpu.VMEM((R, nstate), f32),
            pltpu.VMEM((S5_GROUPS, S5_GL, R), f32),
            pltpu.VMEM((S5_PLANES, tb, LANES), f32),
        ],
        compiler_params=pltpu.CompilerParams(
            dimension_semantics=("arbitrary", "arbitrary"), vmem_limit_bytes=VMEM_LIMIT),
        name="s5",
    )(u, z, xr0, xi0, tmat, wp, vt, ar, ai, dvec, gw, gb)


def _out_body(x_ref, odn_ref, os5_ref, wo_ref, fw_ref, y_ref):
    acc = (x_ref[...]
           + jnp.dot(odn_ref[...], wo_ref[0:DN_WIDTH, :], preferred_element_type=f32)
           + jnp.dot(os5_ref[...], wo_ref[DN_WIDTH:, :], preferred_element_type=f32))
    ms = jnp.mean(acc * acc, axis=-1, keepdims=True)
    y_ref[...] = acc * lax.rsqrt(ms + EPS) * fw_ref[...]


def _out_call(x2, odn2, os52, w_out16, fw, *, rows):
    n = x2.shape[0]
    return pl.pallas_call(
        _out_body,
        grid=(n // rows,),
        in_specs=[
            pl.BlockSpec((rows, D_MODEL), lambda i: (i, 0)),
            pl.BlockSpec((rows, DN_WIDTH), lambda i: (i, 0)),
            pl.BlockSpec((rows, S5_WIDTH), lambda i: (i, 0)),
            pl.BlockSpec((D_MODEL, D_MODEL), lambda i: (0, 0)),
            pl.BlockSpec((1, D_MODEL), lambda i: (0, 0)),
        ],
        out_specs=pl.BlockSpec((rows, D_MODEL), lambda i: (i, 0)),
        out_shape=jax.ShapeDtypeStruct((n, D_MODEL), f32),
        compiler_params=pltpu.CompilerParams(
            dimension_semantics=("arbitrary",), vmem_limit_bytes=VMEM_LIMIT),
        name="outproj",
    )(x2, odn2, os52, w_out16, fw)


def _s5_operands(a_re, a_im, log_dt, b_re, b_im, c_re, c_im):
    lam_re = jnp.minimum(a_re, -1e-4)
    lam_im = a_im
    dt = jnp.exp(log_dt)[:, None]
    ldt_re, ldt_im = lam_re * dt, lam_im * dt

    def lpow(k):
        mag = jnp.exp(ldt_re * k)
        return mag * jnp.cos(ldt_im * k), mag * jnp.sin(ldt_im * k)

    lb_re, lb_im = lpow(1.0)
    den = lam_re * lam_re + lam_im * lam_im
    f_re = ((lb_re - 1.0) * lam_re + lb_im * lam_im) / den
    f_im = (lb_im * lam_re - (lb_re - 1.0) * lam_im) / den
    bb_re = f_re[..., None] * b_re - f_im[..., None] * b_im
    bb_im = f_re[..., None] * b_im + f_im[..., None] * b_re

    fr = jnp.arange(S5_L, dtype=f32)
    G, N = S5_GROUPS, S5_STATE

    def cmul(ar, ai, br, bi):
        return ar * br - ai * bi, ar * bi + ai * br

    pr, pi = lpow(fr[:, None, None])
    pr, pi = pr.transpose(1, 0, 2), pi.transpose(1, 0, 2)
    cpr, cpi = cmul(c_re[:, None], c_im[:, None], pr[:, :, None], pi[:, :, None])
    cpr, cpi = cpr.reshape(G, S5_GL, N), cpi.reshape(G, S5_GL, N)
    qr, qi = lpow(-fr[:, None, None])
    qr, qi = qr.transpose(1, 2, 0), qi.transpose(1, 2, 0)
    bpr, bpi = cmul(qr[..., None], qi[..., None], bb_re[:, :, None], bb_im[:, :, None])
    bpr, bpi = bpr.reshape(G, N, S5_GL), bpi.reshape(G, N, S5_GL)

    wr_, wi_ = lpow((S5_L - 1.0) - fr[:, None, None])
    wr_, wi_ = wr_.transpose(1, 0, 2), wi_.transpose(1, 0, 2)
    bbr_t, bbi_t = bb_re.transpose(0, 2, 1), bb_im.transpose(0, 2, 1)
    w_re, w_im = cmul(wr_[:, :, None], wi_[:, :, None], bbr_t[:, None], bbi_t[:, None])
    w_re, w_im = w_re.reshape(G, S5_GL, N), w_im.reshape(G, S5_GL, N)
    vr_, vi_ = lpow(fr[:, None, None] + 1.0)
    vr_, vi_ = vr_.transpose(1, 0, 2), vi_.transpose(1, 0, 2)
    v_re, v_im = cmul(c_re[:, None], c_im[:, None], vr_[:, :, None], vi_[:, :, None])
    v_re, v_im = v_re.reshape(G, S5_GL, N), v_im.reshape(G, S5_GL, N)

    zeros = jnp.zeros((S5_PAIRS, S5_GL, N), f32)

    def pair_cols(re, im):
        re, im = re.reshape(S5_PAIRS, 2, S5_GL, N), im.reshape(S5_PAIRS, 2, S5_GL, N)
        even = jnp.concatenate([re[:, 0], zeros, im[:, 0], zeros], axis=-1)
        odd = jnp.concatenate([zeros, re[:, 1], zeros, im[:, 1]], axis=-1)
        return jnp.stack([even, odd], axis=1)

    wp = pair_cols(w_re, w_im).reshape(S5_PAIRS, 2 * S5_GL, 2 * LANES).astype(bf16)
    vt = pair_cols(v_re, -v_im).reshape(S5_GROUPS, S5_GL, 2 * LANES).astype(bf16)
    a16r, a16i = lpow(float(S5_L))
    return cpr, cpi, bpr, bpi, wp, vt, a16r.reshape(1, G * N), a16i.reshape(1, G * N)


def _pad_lanes(v, start=0):
    out = jnp.zeros((1, LANES), f32)
    return out.at[0, start:start + v.shape[0]].set(v.astype(f32))


def _layer(x, conv0, s0, xr0, xi0, prm, *, delta_cfg, s5_cfg, out_rows):
    B, T, _ = x.shape
    conv0p = jnp.concatenate([jnp.zeros((B, SUBLANES - (CONV_K - 1), CONV_CH), f32), conv0], axis=1)
    odn, us5, zs5, convout, s_new = _delta_call(
        x, prm["w_in_r"], prm["norm_w"], prm["conv_w"], prm["alog"], prm["dtb"], prm["dnw"],
        conv0p, s0, **delta_cfg)

    nstate = S5_GROUPS * S5_STATE
    if s5_cfg["flatten"]:
        tb = s5_cfg["tb"]
        uf = us5.transpose(1, 0, 2, 3).reshape(1, S5_PLANES, B * T, LANES)
        zf = zs5.reshape(1, B * T, S5_WIDTH)
        pad = tb - B * T
        uf = jnp.pad(uf, ((0, 0), (0, 0), (0, pad), (0, 0)))
        zf = jnp.pad(zf, ((0, 0), (0, pad), (0, 0)))
        os5, xr, xi = _s5_call(uf, zf, xr0.reshape(1, B, nstate), xi0.reshape(1, B, nstate),
                               prm["tmat"], prm["wp"], prm["vt"], prm["ar"], prm["ai"], prm["dvec"],
                               prm["gw"], prm["gb"], tb=tb, n_streams=B, n_chunks=T // S5_L)
        os5 = os5[0, :B * T].reshape(B, T, S5_WIDTH)
    else:
        tb = s5_cfg["tb"]
        os5, xr, xi = _s5_call(us5, zs5, xr0.reshape(B, 1, nstate), xi0.reshape(B, 1, nstate),
                               prm["tmat"], prm["wp"], prm["vt"], prm["ar"], prm["ai"], prm["dvec"],
                               prm["gw"], prm["gb"], tb=tb, n_streams=1, n_chunks=tb // S5_L)

    y = _out_call(x.reshape(B * T, D_MODEL), odn.reshape(B * T, DN_WIDTH), os5.reshape(B * T, S5_WIDTH),
                  prm["w_out"], prm["fw"], rows=out_rows)
    return (y.reshape(B, T, D_MODEL), convout[:, SUBLANES - (CONV_K - 1):, :], s_new,
            xr.reshape(B, S5_GROUPS, S5_STATE), xi.reshape(B, S5_GROUPS, S5_STATE))


def kernel(x_prompt, x_sample, cache_conv, state_dn, state_s5_re, state_s5_im, norm_w, w_in, conv_w, dn_A_log, dn_dt_bias, dn_norm_w, s5_A_re, s5_A_im, s5_log_dt, s5_B_re, s5_B_im, s5_C_re, s5_C_im, s5_D, glu_w, glu_b, w_out, final_norm_w):
    depth = norm_w.shape[0]
    assert depth == 1
    l = 0
    wi = w_in[l]
    o0 = CONV_CH
    o2 = o0 + 2 * DN_HEADS
    w_in_r = jnp.concatenate(
        [wi[:, :o0], wi[:, o2:], wi[:, o0:o2], jnp.zeros((D_MODEL, LANES - 2 * DN_HEADS), f32)],
        axis=1).astype(bf16)
    cpr, cpi, bpr, bpi, wp, vt, ar, ai = _s5_operands(
        s5_A_re[l].astype(f32), s5_A_im[l].astype(f32), s5_log_dt[l].astype(f32),
        s5_B_re[l].astype(f32), s5_B_im[l].astype(f32), s5_C_re[l].astype(f32), s5_C_im[l].astype(f32))
    prm = dict(
        w_in_r=w_in_r,
        norm_w=norm_w[l].reshape(1, D_MODEL).astype(f32),
        conv_w=conv_w[l].astype(f32),
        alog=_pad_lanes(dn_A_log[l]),
        dtb=_pad_lanes(dn_dt_bias[l]),
        dnw=dn_norm_w[l].reshape(1, DN_D).astype(f32),
        tmat=_s5mat_call(cpr, cpi, bpr, bpi),
        wp=wp, vt=vt, ar=ar, ai=ai,
        dvec=s5_D[l].reshape(1, S5_WIDTH).astype(f32),
        gw=glu_w[l].astype(bf16),
        gb=glu_b[l].reshape(1, S5_WIDTH).astype(f32),
        w_out=w_out[l].astype(bf16),
        fw=final_norm_w.reshape(1, D_MODEL).astype(f32),
    )

    bp = x_prompt.shape[0]
    yp, c1, d1, r1, i1 = _layer(
        x_prompt,
        jnp.zeros((bp, CONV_K - 1, CONV_CH), f32),
        jnp.zeros((bp, DN_HEADS, DN_D, DN_D), f32),
        jnp.zeros((bp, S5_GROUPS, S5_STATE), f32),
        jnp.zeros((bp, S5_GROUPS, S5_STATE), f32),
        prm,
        delta_cfg=dict(nb=2, tb=256),
        s5_cfg=dict(flatten=False, tb=2048),
        out_rows=512)
    ys, c2, d2, r2, i2 = _layer(
        x_sample, cache_conv[l].astype(f32), state_dn[l].astype(f32),
        state_s5_re[l].astype(f32), state_s5_im[l].astype(f32),
        prm,
        delta_cfg=dict(nb=16, tb=32),
        s5_cfg=dict(flatten=True, tb=2048),
        out_rows=512)

    return (yp, ys, c1[None], d1[None], r1[None], i1[None], c2[None], d2[None], r2[None], i2[None])
```

```python
import functools
import math

import jax
import jax.numpy as jnp
import numpy as np
from jax import lax
from jax.experimental import pallas as pl
from jax.experimental.pallas import tpu as pltpu

bf16 = jnp.bfloat16
f32 = jnp.float32

LANES = 128
SUBLANES = 8

D_MODEL = 1024
DN_HEADS = 4
DN_D = 128
DN_WIDTH = DN_HEADS * DN_D
CONV_K = 4
CONV_CH = 3 * DN_WIDTH
S5_WIDTH = 512
S5_GROUP = 16
S5_GROUPS = 32
S5_STATE = 64
S5_PAIRS = S5_GROUPS // 2
S5_L = 16
S5_GL = S5_GROUP * S5_L
S5_PLANES = S5_WIDTH // LANES
EPS = 1e-6
L2_EPS = 1e-6

OFF_QKV = 0
OFF_ZDN = CONV_CH
OFF_US5 = OFF_ZDN + DN_WIDTH
OFF_ZS5 = OFF_US5 + S5_WIDTH
OFF_AB = OFF_ZS5 + S5_WIDTH
IN_COLS_PAD = OFF_AB + 128

GROUP = 256
MASK_INCL, MASK_STRICT, MASK_OFF = 0, 1, 2
VMEM_LIMIT =56 * 1024 * 1024

_NT = (((1,), (1,)), ((), ()))
_TN = (((0,), (0,)), ((), ()))


def _bdot(a, b):
    return jnp.dot(a.astype(bf16), b.astype(bf16), preferred_element_type=f32)


def _bdot_g(a, b, dims):
    return lax.dot_general(a.astype(bf16), b.astype(bf16), dims, preferred_element_type=f32)


def _sigmoid(x):
    return 1.0 / (1.0 + jnp.exp(-x))


def _silu(x):
    return x * _sigmoid(x)


def _softplus(x):
    return jnp.maximum(x, 0.0) + jnp.log1p(jnp.exp(-jnp.abs(x)))


def _gelu_tanh(x):
    c = math.sqrt(2.0 / math.pi)
    return 0.5 * x * (1.0 + jnp.tanh(c * (x + 0.044715 * (x * x * x))))


def _unit_lower_solve(nmats, rhss, eye, m_ref, n_levels):
    d32 = functools.partial(jnp.dot, preferred_element_type=f32)

    def d16(a, b):
        return d32(a, b).astype(bf16)

    n16 = [n.astype(bf16) for n in nmats]
    invs = [(eye - n * m_ref[MASK_OFF]).astype(bf16) for n in nmats]
    for lvl in range(1, n_levels):
        mask16 = m_ref[MASK_OFF + lvl].astype(bf16)
        t1 = [d16(inv, n * mask16) for inv, n in zip(invs, n16)]
        t2 = [d16(t, inv) for t, inv in zip(t1, invs)]
        invs = [inv - t for inv, t in zip(invs, t2)]
    return [d32(inv, r.astype(bf16)) for inv, r in zip(invs, rhss)]


def _segment_cumsum(tri16, g):
    g1 = g.astype(bf16)
    r1 = g - g1.astype(f32)
    g2 = r1.astype(bf16)
    g3 = (r1 - g2.astype(f32)).astype(bf16)
    s = jnp.dot(tri16, jnp.concatenate([g1, g2, g3], axis=1), preferred_element_type=f32)
    n = g.shape[1]
    return s[:, :n] + (s[:, n:2 * n] + s[:, 2 * n:])


def _delta_body(x_ref, w_ref, nw_ref, cw_ref, alog_ref, dtb_ref, dnw_ref, m_ref, conv0_ref, s0_ref,
                odn_ref, us5_ref, zs5_ref, convout_ref, s_ref, cbuf, *, nb, tb):
    rows = nb * tb
    n_groups = rows // GROUP
    segs = GROUP // tb
    n_levels = tb.bit_length() - 1

    @pl.when(pl.program_id(1) == 0)
    def _init():
        cbuf[:, 0:SUBLANES, :] = conv0_ref[...]
        s_ref[...] = s0_ref[...]

    x = x_ref[...].reshape(rows, D_MODEL)
    ms = jnp.mean(x * x, axis=-1, keepdims=True)
    h = (x * lax.rsqrt(ms + EPS) * nw_ref[...]).astype(bf16)

    qkv_raw = jnp.dot(h, w_ref[:, OFF_QKV:OFF_QKV + CONV_CH], preferred_element_type=f32)
    qkv_parts = []
    tails = []
    for b in range(nb):
        cbuf[b, SUBLANES:SUBLANES + tb, :] = qkv_raw[b * tb:(b + 1) * tb, :]
        acc = cbuf[b, 5:5 + tb, :] * cw_ref[0:1, :]
        for j in range(1, CONV_K):
            acc = acc + cbuf[b, 5 + j:5 + j + tb, :] * cw_ref[j:j + 1, :]
        qkv_parts.append(_silu(acc))
        tail = cbuf[b, tb:tb + SUBLANES, :]
        cbuf[b, 0:SUBLANES, :] = tail
        tails.append(tail)
    convout_ref[...] = jnp.stack(tails)
    qkv = qkv_parts[0] if nb == 1 else jnp.concatenate(qkv_parts, axis=0)

    for j in range(S5_PLANES):
        lo = OFF_US5 + j * LANES
        us5_ref[:, j] = jnp.dot(h, w_ref[:, lo:lo + LANES], preferred_element_type=f32).reshape(nb, tb, LANES)
    zs5_ref[...] = jnp.dot(h, w_ref[:, OFF_ZS5:OFF_ZS5 + S5_WIDTH],
                           preferred_element_type=f32).reshape(nb, tb, S5_WIDTH)
    zdn = jnp.dot(h, w_ref[:, OFF_ZDN:OFF_ZDN + DN_WIDTH], preferred_element_type=f32)
    ab = jnp.dot(h, w_ref[:, OFF_AB:OFF_AB + LANES], preferred_element_type=f32)

    g_full = -jnp.exp(alog_ref[...]) * _softplus(ab + dtb_ref[...])
    beta_full = _sigmoid(ab)

    incl = m_ref[MASK_INCL]
    strict = m_ref[MASK_STRICT]
    eye = incl - strict
    tri16 = incl.astype(bf16)
    dnw = dnw_ref[...]

    chains = [(gi, hd) for gi in range(n_groups) for hd in range(DN_HEADS)]
    gc_blks = [_segment_cumsum(tri16, g_full[gi * GROUP:(gi + 1) * GROUP]) for gi in range(n_groups)]
    qs, ks, vbs, kbs, gccs = [], [], [], [], []
    for gi, hd in chains:
        r0, lo = gi * GROUP, hd * DN_D
        q = qkv[r0:r0 + GROUP, lo:lo + DN_D]
        k = qkv[r0:r0 + GROUP, DN_WIDTH + lo:DN_WIDTH + lo + DN_D]
        v = qkv[r0:r0 + GROUP, 2 * DN_WIDTH + lo:2 * DN_WIDTH + lo + DN_D]
        q = q * lax.rsqrt(jnp.sum(q * q, axis=-1, keepdims=True) + L2_EPS) * (DN_D ** -0.5)
        k = k * lax.rsqrt(jnp.sum(k * k, axis=-1, keepdims=True) + L2_EPS)
        beta = beta_full[r0:r0 + GROUP, DN_HEADS + hd:DN_HEADS + hd + 1]
        qs.append(q)
        ks.append(k)
        vbs.append(v * beta)
        kbs.append(k * beta)
        gccs.append(gc_blks[gi][:, hd:hd + 1])
    egs = [jnp.exp(g) for g in gccs]

    dmats = []
    for g in gccs:
        gcb = jnp.broadcast_to(g, (GROUP, GROUP))
        dmats.append(jnp.exp((gcb - gcb.T) * incl) * incl)
    k16s = [k.astype(bf16) for k in ks]
    nmats = [_bdot_g(kb, k16, _NT) * (d * strict) for kb, k16, d in zip(kbs, k16s, dmats)]
    attns = [(_bdot_g(q, k16, _NT) * d).astype(bf16) for q, k16, d in zip(qs, k16s, dmats)]
    rhss = [jnp.concatenate([vb, kb * eg], axis=1) for vb, kb, eg in zip(vbs, kbs, egs)]
    sols = _unit_lower_solve(nmats, rhss, eye, m_ref, n_levels)

    def stream_of(gi, sg):
        return (gi * GROUP) // tb + sg

    s_old = {(b, hd): s_ref[b, hd] for b in range(nb) for hd in range(DN_HEADS)}
    s16 = {key: val.astype(bf16) for key, val in s_old.items()}
    v_news, o_states = [], []
    for (gi, hd), sol, q, eg in zip(chains, sols, qs, egs):
        u = sol[:, :DN_D]
        w16 = sol[:, DN_D:].astype(bf16)
        qe16 = (q * eg).astype(bf16)
        vn, os_ = [], []
        for sg in range(segs):
            a0 = sg * tb
            st = s16[(stream_of(gi, sg), hd)]
            vn.append(u[a0:a0 + tb] - jnp.dot(w16[a0:a0 + tb], st, preferred_element_type=f32))
            os_.append(jnp.dot(qe16[a0:a0 + tb], st, preferred_element_type=f32))
        v_news.append(vn[0] if segs == 1 else jnp.concatenate(vn, axis=0))
        o_states.append(os_[0] if segs == 1 else jnp.concatenate(os_, axis=0))
    vn16s = [vn.astype(bf16) for vn in v_news]
    outs = [os_ + jnp.dot(a, vn, preferred_element_type=f32) for os_, a, vn in zip(o_states, attns, vn16s)]

    s_new = {}
    for (gi, hd), k, g, vn in zip(chains, ks, gccs, vn16s):
        for sg in range(segs):
            a0 = sg * tb
            key = (stream_of(gi, sg), hd)
            glast = g[a0 + tb - 1:a0 + tb, :]
            k_dec = (k[a0:a0 + tb] * jnp.exp(glast - g[a0:a0 + tb])).astype(bf16)
            s_new[key] = (s_old[key] * jnp.exp(glast)
                          + lax.dot_general(k_dec, vn[a0:a0 + tb], _TN, preferred_element_type=f32))

    o_rows = []
    for gi in range(n_groups):
        o_heads = []
        for hd in range(DN_HEADS):
            o = outs[gi * DN_HEADS + hd]
            on = o * lax.rsqrt(jnp.mean(o * o, axis=-1, keepdims=True) + EPS) * dnw
            o_heads.append(on * _silu(zdn[gi * GROUP:(gi + 1) * GROUP, hd * DN_D:(hd + 1) * DN_D]))
        o_rows.append(jnp.concatenate(o_heads, axis=1))
    o_all = o_rows[0] if n_groups == 1 else jnp.concatenate(o_rows, axis=0)
    odn_ref[...] = o_all.astype(bf16).reshape(nb, tb, DN_WIDTH)
    s_ref[...] = jnp.stack([jnp.stack([s_new[(b, hd)] for hd in range(DN_HEADS)]) for b in range(nb)])


def _delta_masks(tb):
    r = np.arange(GROUP)[:, None]
    c = np.arange(GROUP)[None, :]
    same = (r // tb) == (c // tb)
    ms = [same & (r >= c), same & (r > c)]
    for l in range(tb.bit_length() - 1):
        ms.append(((r >> (l + 1)) == (c >> (l + 1))) & (((r >> l) & 1) == 1) & (((c >> l) & 1) == 0))
    return jnp.asarray(np.stack(ms).astype(np.float32))


def _delta_call(x, w_in_r, norm_w, conv_w, alog, dtb, dnw, conv0, s0, *, nb, tb):
    B, T, _ = x.shape
    assert GROUP % tb == 0 and (nb * tb) % GROUP == 0 and B % nb == 0 and T % tb == 0
    grid = (B // nb, T // tb)
    const2 = lambda b, t: (0, 0)
    body = functools.partial(_delta_body, nb=nb, tb=tb)
    masks = _delta_masks(tb)
    one = pl.Buffered(1)
    return pl.pallas_call(
        body,
        grid=grid,
        in_specs=[
            pl.BlockSpec((nb, tb, D_MODEL), lambda b, t: (b, t, 0)),
            pl.BlockSpec((D_MODEL, IN_COLS_PAD), const2, pipeline_mode=one),
            pl.BlockSpec((1, D_MODEL), const2),
            pl.BlockSpec((CONV_K, CONV_CH), const2),
            pl.BlockSpec((1, LANES), const2),
            pl.BlockSpec((1, LANES), const2),
            pl.BlockSpec((1, DN_D), const2),
            pl.BlockSpec(masks.shape, lambda b, t: (0, 0, 0), pipeline_mode=one),
            pl.BlockSpec((nb, SUBLANES, CONV_CH), lambda b, t: (b, 0, 0)),
            pl.BlockSpec((nb, DN_HEADS, DN_D, DN_D), lambda b, t: (b, 0, 0, 0)),
        ],
        out_specs=[
            pl.BlockSpec((nb, tb, DN_WIDTH), lambda b, t: (b, t, 0)),
            pl.BlockSpec((nb, S5_PLANES, tb, LANES), lambda b, t: (b, 0, t, 0)),
            pl.BlockSpec((nb, tb, S5_WIDTH), lambda b, t: (b, t, 0)),
            pl.BlockSpec((nb, SUBLANES, CONV_CH), lambda b, t: (b, 0, 0)),
            pl.BlockSpec((nb, DN_HEADS, DN_D, DN_D), lambda b, t: (b, 0, 0, 0)),
        ],
        out_shape=[
            jax.ShapeDtypeStruct((B, T, DN_WIDTH), bf16),
            jax.ShapeDtypeStruct((B, S5_PLANES, T, LANES), f32),
            jax.ShapeDtypeStruct((B, T, S5_WIDTH), f32),
            jax.ShapeDtypeStruct((B, SUBLANES, CONV_CH), f32),
            jax.ShapeDtypeStruct((B, DN_HEADS, DN_D, DN_D), f32),
        ],
        scratch_shapes=[pltpu.VMEM((nb, tb + SUBLANES, CONV_CH), f32)],
        compiler_params=pltpu.CompilerParams(
            dimension_semantics=("arbitrary", "arbitrary"), vmem_limit_bytes=VMEM_LIMIT),
        name="delta",
    )(x, w_in_r, norm_w, conv_w, alog, dtb, dnw, masks, conv0, s0)


S5MAT_GROUPS_PER_STEP = 8


def _split16(a):
    hi = a.astype(bf16)
    return hi, (a - hi.astype(f32)).astype(bf16)


def _s5mat_body(cp_ref, bp_ref, t_ref):
    ri = lax.broadcasted_iota(jnp.int32, (S5_GL, S5_GL), 0) >> 4
    ci = lax.broadcasted_iota(jnp.int32, (S5_GL, S5_GL), 1) >> 4
    causal = ri >= ci
    d = functools.partial(jnp.dot, preferred_element_type=f32)
    for g in range(S5MAT_GROUPS_PER_STEP):
        ch, cl = _split16(cp_ref[g])
        bh, bl = _split16(bp_ref[g])
        t = d(ch, bh) + (d(ch, bl) + d(cl, bh))
        t_ref[g] = jnp.where(causal, t, 0.0).astype(bf16)


def _s5mat_call(cp, bp):
    n = S5MAT_GROUPS_PER_STEP
    return pl.pallas_call(
        _s5mat_body,
        grid=(S5_GROUPS // n,),
        in_specs=[
            pl.BlockSpec((n, S5_GL, 2 * S5_STATE), lambda g: (g, 0, 0)),
            pl.BlockSpec((n, 2 * S5_STATE, S5_GL), lambda g: (g, 0, 0)),
        ],
        out_specs=pl.BlockSpec((n, S5_GL, S5_GL), lambda g: (g, 0, 0)),
        out_shape=jax.ShapeDtypeStruct((S5_GROUPS, S5_GL, S5_GL), bf16),
        compiler_params=pltpu.CompilerParams(dimension_semantics=("arbitrary",)),
        name="s5mat",
    )(cp, bp)


def _s5_body(u_ref, z_ref, xr0_ref, xi0_ref, t_ref, wp_ref, vt_ref, ar_ref, ai_ref, d_ref,
             gw_ref, gb_ref,
             o_ref, xr_ref, xi_ref,
             at_scr, er_scr, ei_scr, xinr_scr, xini_scr, yt_scr, y_scr, *, tb, n_streams, n_chunks):
    R = tb // S5_L
    used = n_streams * n_chunks

    @pl.when(pl.program_id(1) == 0)
    def _init():
        xr_ref[...] = xr0_ref[...]
        xi_ref[...] = xi0_ref[...]

    for f in range(S5_L):
        for j in range(S5_PLANES):
            ut = u_ref[0, j, pl.ds(f, R, stride=S5_L), :]
            ut_t = ut.T.astype(bf16)
            for gl in range(LANES // S5_GROUP):
                g = j * (LANES // S5_GROUP) + gl
                at_scr[g, f * S5_GROUP:(f + 1) * S5_GROUP, :] = ut_t[gl * S5_GROUP:(gl + 1) * S5_GROUP, :]

    for p in range(S5_PAIRS):
        a_t = at_scr[2 * p:2 * p + 2].reshape(2 * S5_GL, R)
        e = lax.dot_general(a_t, wp_ref[p], _TN, preferred_element_type=f32)
        er_scr[:, p * LANES:(p + 1) * LANES] = e[:, :LANES]
        ei_scr[:, p * LANES:(p + 1) * LANES] = e[:, LANES:]

    if used < R:
        xinr_scr[...] = jnp.zeros_like(xinr_scr)
        xini_scr[...] = jnp.zeros_like(xini_scr)

    a_r = ar_ref[...]
    a_i = ai_ref[...]
    for s in range(n_streams):
        def step(c, carry):
            xr, xi = carry
            row = s * n_chunks + c
            xinr_scr[pl.ds(row, 1), :] = xr
            xini_scr[pl.ds(row, 1), :] = xi
            er = er_scr[pl.ds(row, 1), :]
            ei = ei_scr[pl.ds(row, 1), :]
            return a_r * xr - a_i * xi + er, a_r * xi + a_i * xr + ei

        xr, xi = lax.fori_loop(0, n_chunks, step, (xr_ref[0, s:s + 1, :], xi_ref[0, s:s + 1, :]))
        xr_ref[0, s:s + 1, :] = xr
        xi_ref[0, s:s + 1, :] = xi

    for g in range(S5_GROUPS):
        p = g // 2
        xin = jnp.concatenate([xinr_scr[:, p * LANES:(p + 1) * LANES],
                               xini_scr[:, p * LANES:(p + 1) * LANES]], axis=1).astype(bf16)
        yt_scr[g] = (jnp.dot(t_ref[g], at_scr[g], preferred_element_type=f32)
                     + lax.dot_general(vt_ref[g], xin, _NT, preferred_element_type=f32))

    for f in range(S5_L):
        gpp = LANES // S5_GROUP
        for j in range(S5_PLANES):
            yt = yt_scr[j * gpp:(j + 1) * gpp, f * S5_GROUP:(f + 1) * S5_GROUP, :].reshape(LANES, R)
            y_scr[j, pl.ds(f, R, stride=S5_L), :] = yt.T

    sub = 256
    for rb in range(tb // sub):
        sl = pl.ds(rb * sub, sub)
        y_intra = jnp.concatenate([y_scr[j, sl, :] for j in range(S5_PLANES)], axis=1)
        u_nat = jnp.concatenate([u_ref[0, j, sl, :] for j in range(S5_PLANES)], axis=1)
        y = y_intra + d_ref[...] * u_nat
        gy = _gelu_tanh(y)
        gate = _sigmoid(_bdot(gy, gw_ref[...]) + gb_ref[...])
        o_ref[0, sl, :] = (gy * gate * _silu(z_ref[0, sl, :])).astype(bf16)


def _s5_call(u, z, xr0, xi0, tmat, wp, vt, ar, ai, dvec, gw, gb, *, tb, n_streams, n_chunks):
    G, T, _ = z.shape
    grid = (G, T // tb)
    R = tb // S5_L
    nstate = S5_GROUPS * S5_STATE
    c2 = lambda g, t: (0, 0)
    c3 = lambda g, t: (0, 0, 0)
    body = functools.partial(_s5_body, tb=tb, n_streams=n_streams, n_chunks=n_chunks)
    one = pl.Buffered(1)
    return pl.pallas_call(
        body,
        grid=grid,
        in_specs=[
            pl.BlockSpec((1, S5_PLANES, tb, LANES), lambda g, t: (g, 0, t, 0)),
            pl.BlockSpec((1, tb, S5_WIDTH), lambda g, t: (g, t, 0)),
            pl.BlockSpec((1, n_streams, nstate), lambda g, t: (g, 0, 0)),
            pl.BlockSpec((1, n_streams, nstate), lambda g, t: (g, 0, 0)),
            pl.BlockSpec((S5_GROUPS, S5_GL, S5_GL), c3, pipeline_mode=one),
            pl.BlockSpec((S5_PAIRS, 2 * S5_GL, 2 * LANES), c3, pipeline_mode=one),
            pl.BlockSpec((S5_GROUPS, S5_GL, 2 * LANES), c3, pipeline_mode=one),
            pl.BlockSpec((1, nstate), c2),
            pl.BlockSpec((1, nstate), c2),
            pl.BlockSpec((1, S5_WIDTH), c2),
            pl.BlockSpec((S5_WIDTH, S5_WIDTH), c2),
            pl.BlockSpec((1, S5_WIDTH), c2),
        ],
        out_specs=[
            pl.BlockSpec((1, tb, S5_WIDTH), lambda g, t: (g, t, 0)),
            pl.BlockSpec((1, n_streams, nstate), lambda g, t: (g, 0, 0)),
            pl.BlockSpec((1, n_streams, nstate), lambda g, t: (g, 0, 0)),
        ],
        out_shape=[
            jax.ShapeDtypeStruct((G, T, S5_WIDTH), bf16),
            jax.ShapeDtypeStruct((G, n_streams, nstate), f32),
            jax.ShapeDtypeStruct((G, n_streams, nstate), f32),
        ],
        scratch_shapes=[
            pltpu.VMEM((S5_GROUPS, S5_GL, R), bf16),
            pltpu.VMEM((R, nstate), f32),
            pltpu.VMEM((R, nstate), f32),
            pltpu.VMEM((R, nstate), f32),
            pltpu.VMEM((R, nstate), f32),
            pltpu.VMEM((S5_GROUPS, S5_GL, R), f32),
            pltpu.VMEM((S5_PLANES, tb, LANES), f32),
        ],
        compiler_params=pltpu.CompilerParams(
            dimension_semantics=("arbitrary", "arbitrary"), vmem_limit_bytes=VMEM_LIMIT),
        name="s5",
    )(u, z, xr0, xi0, tmat, wp, vt, ar, ai, dvec, gw, gb)


def _out_body(x_ref, odn_ref, os5_ref, wo_ref, fw_ref, y_ref):
    acc = (x_ref[...]
           + jnp.dot(odn_ref[...], wo_ref[0:DN_WIDTH, :], preferred_element_type=f32)
           + jnp.dot(os5_ref[...], wo_ref[DN_WIDTH:, :], preferred_element_type=f32))
    ms = jnp.mean(acc * acc, axis=-1, keepdims=True)
    y_ref[...] = acc * lax.rsqrt(ms + EPS) * fw_ref[...]


def _out_call(x2, odn2, os52, w_out16, fw, *, rows):
    n = x2.shape[0]
    return pl.pallas_call(
        _out_body,
        grid=(n // rows,),
        in_specs=[
            pl.BlockSpec((rows, D_MODEL), lambda i: (i, 0)),
            pl.BlockSpec((rows, DN_WIDTH), lambda i: (i, 0)),
            pl.BlockSpec((rows, S5_WIDTH), lambda i: (i, 0)),
            pl.BlockSpec((D_MODEL, D_MODEL), lambda i: (0, 0)),
            pl.BlockSpec((1, D_MODEL), lambda i: (0, 0)),
        ],
        out_specs=pl.BlockSpec((rows, D_MODEL), lambda i: (i, 0)),
        out_shape=jax.ShapeDtypeStruct((n, D_MODEL), f32),
        compiler_params=pltpu.CompilerParams(
            dimension_semantics=("arbitrary",), vmem_limit_bytes=VMEM_LIMIT),
        name="outproj",
    )(x2, odn2, os52, w_out16, fw)


def _s5_operands(a_re, a_im, log_dt, b_re, b_im, c_re, c_im):
    lam_re = jnp.minimum(a_re, -1e-4)
    lam_im = a_im
    dt = jnp.exp(log_dt)[:, None]
    ldt_re, ldt_im = lam_re * dt, lam_im * dt

    def lpow(k):
        mag = jnp.exp(ldt_re * k)
        return mag * jnp.cos(ldt_im * k), mag * jnp.sin(ldt_im * k)

    lb_re, lb_im = lpow(1.0)
    den = lam_re * lam_re + lam_im * lam_im
    f_re = ((lb_re - 1.0) * lam_re + lb_im * lam_im) / den
    f_im = (lb_im * lam_re - (lb_re - 1.0) * lam_im) / den
    bb_re = f_re[..., None] * b_re - f_im[..., None] * b_im
    bb_im = f_re[..., None] * b_im + f_im[..., None] * b_re

    fr = jnp.arange(S5_L, dtype=f32)
    G, N = S5_GROUPS, S5_STATE

    def cmul(ar, ai, br, bi):
        return ar * br - ai * bi, ar * bi + ai * br

    pr, pi = lpow(fr[:, None, None])
    pr, pi = pr.transpose(1, 0, 2), pi.transpose(1, 0, 2)
    cpr, cpi = cmul(c_re[:, None], c_im[:, None], pr[:, :, None], pi[:, :, None])
    cpr, cpi = cpr.reshape(G, S5_GL, N), cpi.reshape(G, S5_GL, N)
    qr, qi = lpow(-fr[:, None, None])
    qr, qi = qr.transpose(1, 2, 0), qi.transpose(1, 2, 0)
    bpr, bpi = cmul(qr[..., None], qi[..., None], bb_re[:, :, None], bb_im[:, :, None])
    bpr, bpi = bpr.reshape(G, N, S5_GL), bpi.reshape(G, N, S5_GL)

    wr_, wi_ = lpow((S5_L - 1.0) - fr[:, None, None])
    wr_, wi_ = wr_.transpose(1, 0, 2), wi_.transpose(1, 0, 2)
    bbr_t, bbi_t = bb_re.transpose(0, 2, 1), bb_im.transpose(0, 2, 1)
    w_re, w_im = cmul(wr_[:, :, None], wi_[:, :, None], bbr_t[:, None], bbi_t[:, None])
    w_re, w_im = w_re.reshape(G, S5_GL, N), w_im.reshape(G, S5_GL, N)
    vr_, vi_ = lpow(fr[:, None, None] + 1.0)
    vr_, vi_ = vr_.transpose(1, 0, 2), vi_.transpose(1, 0, 2)
    v_re, v_im = cmul(c_re[:, None], c_im[:, None], vr_[:, :, None], vi_[:, :, None])
    v_re, v_im = v_re.reshape(G, S5_GL, N), v_im.reshape(G, S5_GL, N)

    zeros = jnp.zeros((S5_PAIRS, S5_GL, N), f32)

    def pair_cols(re, im):
        re, im = re.reshape(S5_PAIRS, 2, S5_GL, N), im.reshape(S5_PAIRS, 2, S5_GL, N)
        even = jnp.concatenate([re[:, 0], zeros, im[:, 0], zeros], axis=-1)
        odd = jnp.concatenate([zeros, re[:, 1], zeros, im[:, 1]], axis=-1)
        return jnp.stack([even, odd], axis=1)

    wp = pair_cols(w_re, w_im).reshape(S5_PAIRS, 2 * S5_GL, 2 * LANES).astype(bf16)
    vt = pair_cols(v_re, -v_im).reshape(S5_GROUPS, S5_GL, 2 * LANES).astype(bf16)
    a16r, a16i = lpow(float(S5_L))
    cp = jnp.concatenate([cpr, cpi], axis=2)
    bp = jnp.concatenate([bpr, -bpi], axis=1)
    return cp, bp, wp, vt, a16r.reshape(1, G * N), a16i.reshape(1, G * N)


def _pad_lanes(v, start=0):
    out = jnp.zeros((1, LANES), f32)
    return out.at[0, start:start + v.shape[0]].set(v.astype(f32))


def _layer(x, conv0, s0, xr0, xi0, prm, *, delta_cfg, s5_cfg, out_rows):
    B, T, _ = x.shape
    conv0p = jnp.concatenate([jnp.zeros((B, SUBLANES - (CONV_K - 1), CONV_CH), f32), conv0], axis=1)
    odn, us5, zs5, convout, s_new = _delta_call(
        x, prm["w_in_r"], prm["norm_w"], prm["conv_w"], prm["alog"], prm["dtb"], prm["dnw"],
        conv0p, s0, **delta_cfg)

    nstate = S5_GROUPS * S5_STATE
    if s5_cfg["flatten"]:
        tb = s5_cfg["tb"]
        uf = us5.transpose(1, 0, 2, 3).reshape(1, S5_PLANES, B * T, LANES)
        zf = zs5.reshape(1, B * T, S5_WIDTH)
        pad = tb - B * T
        uf = jnp.pad(uf, ((0, 0), (0, 0), (0, pad), (0, 0)))
        zf = jnp.pad(zf, ((0, 0), (0, pad), (0, 0)))
        os5, xr, xi = _s5_call(uf, zf, xr0.reshape(1, B, nstate), xi0.reshape(1, B, nstate),
                               prm["tmat"], prm["wp"], prm["vt"], prm["ar"], prm["ai"], prm["dvec"],
                               prm["gw"], prm["gb"], tb=tb, n_streams=B, n_chunks=T // S5_L)
        os5 = os5[0, :B * T].reshape(B, T, S5_WIDTH)
    else:
        tb = s5_cfg["tb"]
        os5, xr, xi = _s5_call(us5, zs5, xr0.reshape(B, 1, nstate), xi0.reshape(B, 1, nstate),
                               prm["tmat"], prm["wp"], prm["vt"], prm["ar"], prm["ai"], prm["dvec"],
                               prm["gw"], prm["gb"], tb=tb, n_streams=1, n_chunks=tb // S5_L)

    y = _out_call(x.reshape(B * T, D_MODEL), odn.reshape(B * T, DN_WIDTH), os5.reshape(B * T, S5_WIDTH),
                  prm["w_out"], prm["fw"], rows=out_rows)
    return (y.reshape(B, T, D_MODEL), convout[:, SUBLANES - (CONV_K - 1):, :], s_new,
            xr.reshape(B, S5_GROUPS, S5_STATE), xi.reshape(B, S5_GROUPS, S5_STATE))


def kernel(x_prompt, x_sample, cache_conv, state_dn, state_s5_re, state_s5_im, norm_w, w_in, conv_w, dn_A_log, dn_dt_bias, dn_norm_w, s5_A_re, s5_A_im, s5_log_dt, s5_B_re, s5_B_im, s5_C_re, s5_C_im, s5_D, glu_w, glu_b, w_out, final_norm_w):
    depth = norm_w.shape[0]
    assert depth == 1
    l = 0
    wi = w_in[l]
    o0 = CONV_CH
    o2 = o0 + 2 * DN_HEADS
    w_in_r = jnp.concatenate(
        [wi[:, :o0], wi[:, o2:], wi[:, o0:o2], jnp.zeros((D_MODEL, LANES - 2 * DN_HEADS), f32)],
        axis=1).astype(bf16)
    cp, bp, wp, vt, ar, ai = _s5_operands(
        s5_A_re[l].astype(f32), s5_A_im[l].astype(f32), s5_log_dt[l].astype(f32),
        s5_B_re[l].astype(f32), s5_B_im[l].astype(f32), s5_C_re[l].astype(f32), s5_C_im[l].astype(f32))
    prm = dict(
        w_in_r=w_in_r,
        norm_w=norm_w[l].reshape(1, D_MODEL).astype(f32),
        conv_w=conv_w[l].astype(f32),
        alog=_pad_lanes(dn_A_log[l]),
        dtb=_pad_lanes(dn_dt_bias[l]),
        dnw=dn_norm_w[l].reshape(1, DN_D).astype(f32),
        tmat=_s5mat_call(cp, bp),
        wp=wp, vt=vt, ar=ar, ai=ai,
        dvec=s5_D[l].reshape(1, S5_WIDTH).astype(f32),
        gw=glu_w[l].astype(bf16),
        gb=glu_b[l].reshape(1, S5_WIDTH).astype(f32),
        w_out=w_out[l].astype(bf16),
        fw=final_norm_w.reshape(1, D_MODEL).astype(f32),
    )

    bp = x_prompt.shape[0]
    yp, c1, d1, r1, i1 = _layer(
        x_prompt,
        jnp.zeros((bp, CONV_K - 1, CONV_CH), f32),
        jnp.zeros((bp, DN_HEADS, DN_D, DN_D), f32),
        jnp.zeros((bp, S5_GROUPS, S5_STATE), f32),
        jnp.zeros((bp, S5_GROUPS, S5_STATE), f32),
        prm,
        delta_cfg=dict(nb=2, tb=256),
        s5_cfg=dict(flatten=False, tb=2048),
        out_rows=1024)
    ys, c2, d2, r2, i2 = _layer(
        x_sample, cache_conv[l].astype(f32), state_dn[l].astype(f32),
        state_s5_re[l].astype(f32), state_s5_im[l].astype(f32),
        prm,
        delta_cfg=dict(nb=16, tb=32),
        s5_cfg=dict(flatten=True, tb=2048),
        out_rows=512)

    return (yp, ys, c1[None], d1[None], r1[None], i1[None], c2[None], d2[None], r2[None], i2[None])
```

```python
import functools
import math

import jax
import jax.numpy as jnp
import numpy as np
from jax import lax
from jax.experimental import pallas as pl
from jax.experimental.pallas import tpu as pltpu

bf16 = jnp.bfloat16
f32 = jnp.float32

LANES = 128
SUBLANES = 8

D_MODEL = 1024
DN_HEADS = 4
DN_D = 128
DN_WIDTH = DN_HEADS * DN_D
CONV_K = 4
CONV_CH = 3 * DN_WIDTH
S5_WIDTH = 512
S5_GROUP = 16
S5_GROUPS = 32
S5_STATE = 64
S5_PAIRS = S5_GROUPS // 2
S5_L = 16
S5_GL = S5_GROUP * S5_L
S5_PLANES = S5_WIDTH // LANES
EPS = 1e-6
L2_EPS = 1e-6

OFF_QKV = 0
OFF_ZDN = CONV_CH
OFF_US5 = OFF_ZDN + DN_WIDTH
OFF_ZS5 = OFF_US5 + S5_WIDTH
OFF_AB = OFF_ZS5 + S5_WIDTH
IN_COLS_PAD = OFF_AB + 128

GROUP = 256
MASK_INCL, MASK_STRICT, MASK_OFF = 0, 1, 2
VMEM_LIMIT =56 * 1024 * 1024

_NT = (((1,), (1,)), ((), ()))
_TN = (((0,), (0,)), ((), ()))


def _bdot(a, b):
    return jnp.dot(a.astype(bf16), b.astype(bf16), preferred_element_type=f32)


def _bdot_g(a, b, dims):
    return lax.dot_general(a.astype(bf16), b.astype(bf16), dims, preferred_element_type=f32)


def _sigmoid(x):
    return 1.0 / (1.0 + jnp.exp(-x))


def _silu(x):
    return x * _sigmoid(x)


def _softplus(x):
    return jnp.maximum(x, 0.0) + jnp.log1p(jnp.exp(-jnp.abs(x)))


def _gelu_tanh(x):
    c = math.sqrt(2.0 / math.pi)
    return 0.5 * x * (1.0 + jnp.tanh(c * (x + 0.044715 * (x * x * x))))


def _zero_after(x):
    bits = pltpu.bitcast(x[0:SUBLANES, 0:LANES].astype(f32), jnp.uint32)
    half = jnp.uint32(16)
    bits = lax.shift_right_logical(lax.shift_right_logical(bits, half), half)
    return pltpu.bitcast(bits, f32)[0:1, :]


def _unit_lower_solve(nmats, rhss, eye, m_ref, n_levels, fillers):
    d32 = functools.partial(jnp.dot, preferred_element_type=f32)

    def d16(a, b):
        return d32(a, b).astype(bf16)

    def fill(after):
        if fillers:
            fillers.pop(0)(_zero_after(after[-1]))

    n16 = [n.astype(bf16) for n in nmats]
    invs = [(eye - n * m_ref[MASK_OFF]).astype(bf16) for n in nmats]
    for lvl in range(1, n_levels):
        mask16 = m_ref[MASK_OFF + lvl].astype(bf16)
        t1 = [d16(inv, n * mask16) for inv, n in zip(invs, n16)]
        fill(invs)
        t2 = [d16(t, inv) for t, inv in zip(t1, invs)]
        fill(t1)
        invs = [inv - t for inv, t in zip(invs, t2)]
    return [d32(inv, r.astype(bf16)) for inv, r in zip(invs, rhss)]


def _segment_cumsum(tri16, g):
    g1 = g.astype(bf16)
    r1 = g - g1.astype(f32)
    g2 = r1.astype(bf16)
    g3 = (r1 - g2.astype(f32)).astype(bf16)
    s = jnp.dot(tri16, jnp.concatenate([g1, g2, g3], axis=1), preferred_element_type=f32)
    n = g.shape[1]
    return s[:, :n] + (s[:, n:2 * n] + s[:, 2 * n:])


def _delta_body(x_ref, w_ref, nw_ref, cw_ref, alog_ref, dtb_ref, dnw_ref, m_ref, conv0_ref, s0_ref,
                odn_ref, us5_ref, zs5_ref, convout_ref, s_ref,
                cbuf, qn_scr, kn_scr, v_scr, gc_scr, beta_scr, zdn_scr, *, nb, tb, n_t, pipelined):
    rows = nb * tb
    n_groups = rows // GROUP
    segs = GROUP // tb
    n_levels = tb.bit_length() - 1
    step = pl.program_id(0)
    fresh = step < n_t

    @pl.when(step == 0)
    def _init():
        cbuf[:, 0:SUBLANES, :] = conv0_ref[...]
        s_ref[...] = s0_ref[...]
        if pipelined:
            for scr in (qn_scr, kn_scr, v_scr, gc_scr, beta_scr, zdn_scr):
                scr[...] = jnp.zeros_like(scr)

    front = {}

    def project(lo, width):
        return jnp.dot(front["h"], w_ref[:, lo:lo + width], preferred_element_type=f32)

    def a_norm_in():
        x = x_ref[...].reshape(rows, D_MODEL)
        ms = jnp.mean(x * x, axis=-1, keepdims=True)
        front["h"] = (x * lax.rsqrt(ms + EPS) * nw_ref[...]).astype(bf16)

    def a_project_qkv(s):
        front[("raw", s)] = project(OFF_QKV + s * DN_WIDTH, DN_WIDTH)

    def a_project_us5():
        u = project(OFF_US5, S5_WIDTH)
        for j in range(S5_PLANES):
            us5_ref[:, j] = u[:, j * LANES:(j + 1) * LANES].reshape(nb, tb, LANES)

    def a_project_zs5():
        zs5_ref[...] = project(OFF_ZS5, S5_WIDTH).reshape(nb, tb, S5_WIDTH)

    def a_project_zdn():
        front["zdn"] = project(OFF_ZDN, DN_WIDTH)

    def a_project_gates():
        ab = project(OFF_AB, LANES)
        front["g"] = -jnp.exp(alog_ref[...]) * _softplus(ab + dtb_ref[...])
        front["beta"] = _sigmoid(ab)

    projections = ([functools.partial(a_project_qkv, s) for s in range(3)]
                   + [a_project_us5, a_project_zs5, a_project_zdn, a_project_gates])

    def a_conv(b, s, zero):
        cols = slice(s * DN_WIDTH, (s + 1) * DN_WIDTH)
        cbuf[b, SUBLANES:SUBLANES + tb, cols] = front[("raw", s)][b * tb:(b + 1) * tb, :]
        taps = [cw_ref[j:j + 1, cols] for j in range(CONV_K)]
        if zero is not None:
            zero_w = jnp.concatenate([zero] * (DN_WIDTH // LANES), axis=1)
            taps = [t + zero_w for t in taps]
        acc = cbuf[b, 5:5 + tb, cols] * taps[0]
        for j in range(1, CONV_K):
            acc = acc + cbuf[b, 5 + j:5 + j + tb, cols] * taps[j]
        front[("act", b, s)] = _silu(acc)
        tail = cbuf[b, tb:tb + SUBLANES, cols]
        if pipelined:
            tail = jnp.where(fresh, tail, cbuf[b, 0:SUBLANES, cols])
        cbuf[b, 0:SUBLANES, cols] = tail
        convout_ref[b, :, cols] = tail

    def a_norm(b, hd, zero):
        lo = hd * DN_D
        q = front[("act", b, 0)][:, lo:lo + DN_D]
        k = front[("act", b, 1)][:, lo:lo + DN_D]
        if zero is not None:
            q, k = q + zero, k + zero
        front[("qn", b, hd)] = (q * lax.rsqrt(jnp.sum(q * q, axis=-1, keepdims=True) + L2_EPS)
                                * (DN_D ** -0.5))
        front[("kn", b, hd)] = k * lax.rsqrt(jnp.sum(k * k, axis=-1, keepdims=True) + L2_EPS)

    def a_gates(zero):
        tri16 = m_ref[MASK_INCL].astype(bf16)
        g = front["g"] if zero is None else front["g"] + zero
        front["gc"] = [_segment_cumsum(tri16, g[gi * GROUP:(gi + 1) * GROUP])
                       for gi in range(n_groups)]

    def a_store():
        for b in range(nb):
            r = slice(b * tb, (b + 1) * tb)
            for hd in range(DN_HEADS):
                c = slice(hd * DN_D, (hd + 1) * DN_D)
                qn_scr[r, c] = front[("qn", b, hd)]
                kn_scr[r, c] = front[("kn", b, hd)]
            v_scr[r, :] = front[("act", b, 2)]
        for gi in range(n_groups):
            gc_scr[gi * GROUP:(gi + 1) * GROUP, :] = front["gc"][gi]
        beta_scr[...] = front["beta"]
        zdn_scr[...] = front["zdn"]

    fillers = ([functools.partial(a_conv, b, s) for b in range(nb) for s in range(3)] + [a_gates]
               + [functools.partial(a_norm, b, hd) for b in range(nb) for hd in range(DN_HEADS)])

    incl = m_ref[MASK_INCL]
    strict = m_ref[MASK_STRICT]
    eye = incl - strict
    dnw = dnw_ref[...]
    chains = [(gi, hd) for gi in range(n_groups) for hd in range(DN_HEADS)]
    back = {}

    def stream_of(gi, sg):
        return (gi * GROUP) // tb + sg

    for name in ("q16", "qe16", "k16", "kb16", "rhs", "dmat", "kdec", "sdecay", "nmat", "attn"):
        back[name] = [None] * len(chains)

    def b_load(ci):
        gi, hd = chains[ci]
        r = slice(gi * GROUP, (gi + 1) * GROUP)
        c = slice(hd * DN_D, (hd + 1) * DN_D)
        q, k, v = qn_scr[r, c], kn_scr[r, c], v_scr[r, c]
        beta = beta_scr[r, DN_HEADS + hd:DN_HEADS + hd + 1]
        g = gc_scr[r, hd:hd + 1]
        eg = jnp.exp(g)
        kb = k * beta
        gcb = jnp.broadcast_to(g, (GROUP, GROUP))
        back["dmat"][ci] = jnp.exp((gcb - gcb.T) * incl) * incl
        back["q16"][ci] = q.astype(bf16)
        back["qe16"][ci] = (q * eg).astype(bf16)
        back["k16"][ci] = k.astype(bf16)
        back["kb16"][ci] = kb.astype(bf16)
        back["rhs"][ci] = jnp.concatenate([v * beta, kb * eg], axis=1)
        kd, sd = [], []
        for sg in range(segs):
            a0 = sg * tb
            glast = g[a0 + tb - 1:a0 + tb, :]
            kd.append((k[a0:a0 + tb] * jnp.exp(glast - g[a0:a0 + tb])).astype(bf16))
            sd.append(jnp.exp(glast))
        back["kdec"][ci] = kd
        back["sdecay"][ci] = sd

    def b_mats(ci):
        d = functools.partial(lax.dot_general, dimension_numbers=_NT, preferred_element_type=f32)
        dm = back["dmat"][ci]
        back["nmat"][ci] = d(back["kb16"][ci], back["k16"][ci]) * (dm * strict)
        back["attn"][ci] = (d(back["q16"][ci], back["k16"][ci]) * dm).astype(bf16)

    def b_state(sols):
        s_old = {(b, hd): s_ref[b, hd] for b in range(nb) for hd in range(DN_HEADS)}
        s16 = {key: val.astype(bf16) for key, val in s_old.items()}
        v_news, o_states = [], []
        for (gi, hd), sol, qe16 in zip(chains, sols, back["qe16"]):
            u = sol[:, :DN_D]
            w16 = sol[:, DN_D:].astype(bf16)
            vn, os_ = [], []
            for sg in range(segs):
                a0 = sg * tb
                st = s16[(stream_of(gi, sg), hd)]
                vn.append(u[a0:a0 + tb] - jnp.dot(w16[a0:a0 + tb], st, preferred_element_type=f32))
                os_.append(jnp.dot(qe16[a0:a0 + tb], st, preferred_element_type=f32))
            v_news.append(vn[0] if segs == 1 else jnp.concatenate(vn, axis=0))
            o_states.append(os_[0] if segs == 1 else jnp.concatenate(os_, axis=0))
        vn16s = [vn.astype(bf16) for vn in v_news]
        outs = [os_ + jnp.dot(a, vn, preferred_element_type=f32)
                for os_, a, vn in zip(o_states, back["attn"], vn16s)]

        s_new = {}
        for (gi, hd), kd, sd, vn in zip(chains, back["kdec"], back["sdecay"], vn16s):
            for sg in range(segs):
                a0 = sg * tb
                key = (stream_of(gi, sg), hd)
                s_new[key] = (s_old[key] * sd[sg]
                              + lax.dot_general(kd[sg], vn[a0:a0 + tb], _TN, preferred_element_type=f32))

        o_rows = []
        for gi in range(n_groups):
            o_heads = []
            for hd in range(DN_HEADS):
                o = outs[gi * DN_HEADS + hd]
                on = o * lax.rsqrt(jnp.mean(o * o, axis=-1, keepdims=True) + EPS) * dnw
                zd = zdn_scr[gi * GROUP:(gi + 1) * GROUP, hd * DN_D:(hd + 1) * DN_D]
                o_heads.append(on * _silu(zd))
            o_rows.append(jnp.concatenate(o_heads, axis=1))
        o_all = o_rows[0] if n_groups == 1 else jnp.concatenate(o_rows, axis=0)
        odn_ref[...] = o_all.astype(bf16).reshape(nb, tb, DN_WIDTH)
        s_ref[...] = jnp.stack([jnp.stack([s_new[(b, hd)] for hd in range(DN_HEADS)]) for b in range(nb)])

    n_ch = len(chains)
    if pipelined:
        a_norm_in()
        back_setup = ([functools.partial(b_load, ci) for ci in range(n_ch)]
                      + [functools.partial(b_mats, ci) for ci in range(n_ch)])
        per_piece = -(-len(back_setup) // len(projections))
        for piece in projections:
            piece()
            for thunk in back_setup[:per_piece]:
                thunk()
            del back_setup[:per_piece]
        sols = _unit_lower_solve(back["nmat"], back["rhs"], eye, m_ref, n_levels, fillers)
        for fill in fillers:
            fill(None)
        b_state(sols)
        a_store()
    else:
        a_norm_in()
        for piece in projections:
            piece()
        for fill in fillers:
            fill(None)
        a_store()
        for ci in range(n_ch):
            b_load(ci)
        for ci in range(n_ch):
            b_mats(ci)
        b_state(_unit_lower_solve(back["nmat"], back["rhs"], eye, m_ref, n_levels, []))


def _delta_masks(tb):
    r = np.arange(GROUP)[:, None]
    c = np.arange(GROUP)[None, :]
    same = (r // tb) == (c // tb)
    ms = [same & (r >= c), same & (r > c)]
    for l in range(tb.bit_length() - 1):
        ms.append(((r >> (l + 1)) == (c >> (l + 1))) & (((r >> l) & 1) == 1) & (((c >> l) & 1) == 0))
    return jnp.asarray(np.stack(ms).astype(np.float32))


def _delta_call(x, w_in_r, norm_w, conv_w, alog, dtb, dnw, conv0, s0, *, tb, pipelined):
    nb, T, _ = x.shape
    assert GROUP % tb == 0 and (nb * tb) % GROUP == 0 and T % tb == 0
    n_t = T // tb
    rows = nb * tb
    if pipelined:
        grid = (n_t + 1,)
        cur = lambda i: jnp.minimum(i, n_t - 1)
        prev = lambda i: jnp.maximum(i - 1, 0)
    else:
        grid = (n_t,)
        cur = prev = lambda i: i
    const2 = lambda i: (0, 0)
    const3 = lambda i: (0, 0, 0)
    const4 = lambda i: (0, 0, 0, 0)
    body = functools.partial(_delta_body, nb=nb, tb=tb, n_t=n_t, pipelined=pipelined)
    masks = _delta_masks(tb)
    one = pl.Buffered(1)
    return pl.pallas_call(
        body,
        grid=grid,
        in_specs=[
            pl.BlockSpec((nb, tb, D_MODEL), lambda i: (0, cur(i), 0)),
            pl.BlockSpec((D_MODEL, IN_COLS_PAD), const2, pipeline_mode=one),
            pl.BlockSpec((1, D_MODEL), const2),
            pl.BlockSpec((CONV_K, CONV_CH), const2),
            pl.BlockSpec((1, LANES), const2),
            pl.BlockSpec((1, LANES), const2),
            pl.BlockSpec((1, DN_D), const2),
            pl.BlockSpec(masks.shape, const3, pipeline_mode=one),
            pl.BlockSpec((nb, SUBLANES, CONV_CH), const3),
            pl.BlockSpec((nb, DN_HEADS, DN_D, DN_D), const4),
        ],
        out_specs=[
            pl.BlockSpec((nb, tb, DN_WIDTH), lambda i: (0, prev(i), 0)),
            pl.BlockSpec((nb, S5_PLANES, tb, LANES), lambda i: (0, 0, cur(i), 0)),
            pl.BlockSpec((nb, tb, S5_WIDTH), lambda i: (0, cur(i), 0)),
            pl.BlockSpec((nb, SUBLANES, CONV_CH), const3),
            pl.BlockSpec((nb, DN_HEADS, DN_D, DN_D), const4),
        ],
        out_shape=[
            jax.ShapeDtypeStruct((nb, T, DN_WIDTH), bf16),
            jax.ShapeDtypeStruct((nb, S5_PLANES, T, LANES), f32),
            jax.ShapeDtypeStruct((nb, T, S5_WIDTH), f32),
            jax.ShapeDtypeStruct((nb, SUBLANES, CONV_CH), f32),
            jax.ShapeDtypeStruct((nb, DN_HEADS, DN_D, DN_D), f32),
        ],
        scratch_shapes=[
            pltpu.VMEM((nb, tb + SUBLANES, CONV_CH), f32),
            pltpu.VMEM((rows, DN_WIDTH), f32),
            pltpu.VMEM((rows, DN_WIDTH), f32),
            pltpu.VMEM((rows, DN_WIDTH), f32),
            pltpu.VMEM((rows, LANES), f32),
            pltpu.VMEM((rows, LANES), f32),
            pltpu.VMEM((rows, DN_WIDTH), f32),
        ],
        compiler_params=pltpu.CompilerParams(
            dimension_semantics=("arbitrary",), vmem_limit_bytes=VMEM_LIMIT),
        name="delta",
    )(x, w_in_r, norm_w, conv_w, alog, dtb, dnw, masks, conv0, s0)


S5MAT_GROUPS_PER_STEP = 8


def _split16(a):
    hi = a.astype(bf16)
    return hi, (a - hi.astype(f32)).astype(bf16)


def _s5mat_body(cp_ref, bp_ref, t_ref):
    ri = lax.broadcasted_iota(jnp.int32, (S5_GL, S5_GL), 0) >> 4
    ci = lax.broadcasted_iota(jnp.int32, (S5_GL, S5_GL), 1) >> 4
    causal = ri >= ci
    d = functools.partial(jnp.dot, preferred_element_type=f32)
    for g in range(S5MAT_GROUPS_PER_STEP):
        ch, cl = _split16(cp_ref[g])
        bh, bl = _split16(bp_ref[g])
        t = d(ch, bh) + (d(ch, bl) + d(cl, bh))
        t_ref[g] = jnp.where(causal, t, 0.0).astype(bf16)


def _s5mat_call(cp, bp):
    n = S5MAT_GROUPS_PER_STEP
    return pl.pallas_call(
        _s5mat_body,
        grid=(S5_GROUPS // n,),
        in_specs=[
            pl.BlockSpec((n, S5_GL, 2 * S5_STATE), lambda g: (g, 0, 0)),
            pl.BlockSpec((n, 2 * S5_STATE, S5_GL), lambda g: (g, 0, 0)),
        ],
        out_specs=pl.BlockSpec((n, S5_GL, S5_GL), lambda g: (g, 0, 0)),
        out_shape=jax.ShapeDtypeStruct((S5_GROUPS, S5_GL, S5_GL), bf16),
        compiler_params=pltpu.CompilerParams(dimension_semantics=("arbitrary",)),
        name="s5mat",
    )(cp, bp)


def _s5_body(u_ref, z_ref, xr0_ref, xi0_ref, t_ref, wp_ref, vt_ref, ar_ref, ai_ref, d_ref,
             gw_ref, gb_ref,
             o_ref, xr_ref, xi_ref,
             at_scr, er_scr, ei_scr, xinr_scr, xini_scr, yt_scr, y_scr, *, tb, n_streams, n_chunks):
    R = tb // S5_L
    used = n_streams * n_chunks

    @pl.when(pl.program_id(1) == 0)
    def _init():
        xr_ref[...] = xr0_ref[...]
        xi_ref[...] = xi0_ref[...]

    for f in range(S5_L):
        for j in range(S5_PLANES):
            ut = u_ref[0, j, pl.ds(f, R, stride=S5_L), :]
            ut_t = ut.T.astype(bf16)
            for gl in range(LANES // S5_GROUP):
                g = j * (LANES // S5_GROUP) + gl
                at_scr[g, f * S5_GROUP:(f + 1) * S5_GROUP, :] = ut_t[gl * S5_GROUP:(gl + 1) * S5_GROUP, :]

    for p in range(S5_PAIRS):
        a_t = at_scr[2 * p:2 * p + 2].reshape(2 * S5_GL, R)
        e = lax.dot_general(a_t, wp_ref[p], _TN, preferred_element_type=f32)
        er_scr[:, p * LANES:(p + 1) * LANES] = e[:, :LANES]
        ei_scr[:, p * LANES:(p + 1) * LANES] = e[:, LANES:]

    if used < R:
        xinr_scr[...] = jnp.zeros_like(xinr_scr)
        xini_scr[...] = jnp.zeros_like(xini_scr)

    a_r = ar_ref[...]
    a_i = ai_ref[...]
    for s in range(n_streams):
        def step(c, carry):
            xr, xi = carry
            row = s * n_chunks + c
            xinr_scr[pl.ds(row, 1), :] = xr
            xini_scr[pl.ds(row, 1), :] = xi
            er = er_scr[pl.ds(row, 1), :]
            ei = ei_scr[pl.ds(row, 1), :]
            return a_r * xr - a_i * xi + er, a_r * xi + a_i * xr + ei

        xr, xi = lax.fori_loop(0, n_chunks, step, (xr_ref[0, s:s + 1, :], xi_ref[0, s:s + 1, :]))
        xr_ref[0, s:s + 1, :] = xr
        xi_ref[0, s:s + 1, :] = xi

    for g in range(S5_GROUPS):
        p = g // 2
        xin = jnp.concatenate([xinr_scr[:, p * LANES:(p + 1) * LANES],
                               xini_scr[:, p * LANES:(p + 1) * LANES]], axis=1).astype(bf16)
        yt_scr[g] = (jnp.dot(t_ref[g], at_scr[g], preferred_element_type=f32)
                     + lax.dot_general(vt_ref[g], xin, _NT, preferred_element_type=f32))

    for f in range(S5_L):
        gpp = LANES // S5_GROUP
        for j in range(S5_PLANES):
            yt = yt_scr[j * gpp:(j + 1) * gpp, f * S5_GROUP:(f + 1) * S5_GROUP, :].reshape(LANES, R)
            y_scr[j, pl.ds(f, R, stride=S5_L), :] = yt.T

    sub = 256
    for rb in range(tb // sub):
        sl = pl.ds(rb * sub, sub)
        y_intra = jnp.concatenate([y_scr[j, sl, :] for j in range(S5_PLANES)], axis=1)
        u_nat = jnp.concatenate([u_ref[0, j, sl, :] for j in range(S5_PLANES)], axis=1)
        y = y_intra + d_ref[...] * u_nat
        gy = _gelu_tanh(y)
        gate = _sigmoid(_bdot(gy, gw_ref[...]) + gb_ref[...])
        o_ref[0, sl, :] = (gy * gate * _silu(z_ref[0, sl, :])).astype(bf16)


def _s5_call(u, z, xr0, xi0, tmat, wp, vt, ar, ai, dvec, gw, gb, *, tb, n_streams, n_chunks):
    G, T, _ = z.shape
    grid = (G, T // tb)
    R = tb // S5_L
    nstate = S5_GROUPS * S5_STATE
    c2 = lambda g, t: (0, 0)
    c3 = lambda g, t: (0, 0, 0)
    body = functools.partial(_s5_body, tb=tb, n_streams=n_streams, n_chunks=n_chunks)
    one = pl.Buffered(1)
    return pl.pallas_call(
        body,
        grid=grid,
        in_specs=[
            pl.BlockSpec((1, S5_PLANES, tb, LANES), lambda g, t: (g, 0, t, 0)),
            pl.BlockSpec((1, tb, S5_WIDTH), lambda g, t: (g, t, 0)),
            pl.BlockSpec((1, n_streams, nstate), lambda g, t: (g, 0, 0)),
            pl.BlockSpec((1, n_streams, nstate), lambda g, t: (g, 0, 0)),
            pl.BlockSpec((S5_GROUPS, S5_GL, S5_GL), c3, pipeline_mode=one),
            pl.BlockSpec((S5_PAIRS, 2 * S5_GL, 2 * LANES), c3, pipeline_mode=one),
            pl.BlockSpec((S5_GROUPS, S5_GL, 2 * LANES), c3, pipeline_mode=one),
            pl.BlockSpec((1, nstate), c2),
            pl.BlockSpec((1, nstate), c2),
            pl.BlockSpec((1, S5_WIDTH), c2),
            pl.BlockSpec((S5_WIDTH, S5_WIDTH), c2),
            pl.BlockSpec((1, S5_WIDTH), c2),
        ],
        out_specs=[
            pl.BlockSpec((1, tb, S5_WIDTH), lambda g, t: (g, t, 0)),
            pl.BlockSpec((1, n_streams, nstate), lambda g, t: (g, 0, 0)),
            pl.BlockSpec((1, n_streams, nstate), lambda g, t: (g, 0, 0)),
        ],
        out_shape=[
            jax.ShapeDtypeStruct((G, T, S5_WIDTH), bf16),
            jax.ShapeDtypeStruct((G, n_streams, nstate), f32),
            jax.ShapeDtypeStruct((G, n_streams, nstate), f32),
        ],
        scratch_shapes=[
            pltpu.VMEM((S5_GROUPS, S5_GL, R), bf16),
            pltpu.VMEM((R, nstate), f32),
            pltpu.VMEM((R, nstate), f32),
            pltpu.VMEM((R, nstate), f32),
            pltpu.VMEM((R, nstate), f32),
            pltpu.VMEM((S5_GROUPS, S5_GL, R), f32),
            pltpu.VMEM((S5_PLANES, tb, LANES), f32),
        ],
        compiler_params=pltpu.CompilerParams(
            dimension_semantics=("arbitrary", "arbitrary"), vmem_limit_bytes=VMEM_LIMIT),
        name="s5",
    )(u, z, xr0, xi0, tmat, wp, vt, ar, ai, dvec, gw, gb)


def _out_body(x_ref, odn_ref, os5_ref, wo_ref, fw_ref, y_ref):
    acc = (x_ref[...]
           + jnp.dot(odn_ref[...], wo_ref[0:DN_WIDTH, :], preferred_element_type=f32)
           + jnp.dot(os5_ref[...], wo_ref[DN_WIDTH:, :], preferred_element_type=f32))
    ms = jnp.mean(acc * acc, axis=-1, keepdims=True)
    y_ref[...] = acc * lax.rsqrt(ms + EPS) * fw_ref[...]


def _out_call(x2, odn2, os52, w_out16, fw, *, rows):
    n = x2.shape[0]
    return pl.pallas_call(
        _out_body,
        grid=(n // rows,),
        in_specs=[
            pl.BlockSpec((rows, D_MODEL), lambda i: (i, 0)),
            pl.BlockSpec((rows, DN_WIDTH), lambda i: (i, 0)),
            pl.BlockSpec((rows, S5_WIDTH), lambda i: (i, 0)),
            pl.BlockSpec((D_MODEL, D_MODEL), lambda i: (0, 0)),
            pl.BlockSpec((1, D_MODEL), lambda i: (0, 0)),
        ],
        out_specs=pl.BlockSpec((rows, D_MODEL), lambda i: (i, 0)),
        out_shape=jax.ShapeDtypeStruct((n, D_MODEL), f32),
        compiler_params=pltpu.CompilerParams(
            dimension_semantics=("arbitrary",), vmem_limit_bytes=VMEM_LIMIT),
        name="outproj",
    )(x2, odn2, os52, w_out16, fw)


def _s5_operands(a_re, a_im, log_dt, b_re, b_im, c_re, c_im):
    lam_re = jnp.minimum(a_re, -1e-4)
    lam_im = a_im
    dt = jnp.exp(log_dt)[:, None]
    ldt_re, ldt_im = lam_re * dt, lam_im * dt

    def lpow(k):
        mag = jnp.exp(ldt_re * k)
        return mag * jnp.cos(ldt_im * k), mag * jnp.sin(ldt_im * k)

    lb_re, lb_im = lpow(1.0)
    den = lam_re * lam_re + lam_im * lam_im
    f_re = ((lb_re - 1.0) * lam_re + lb_im * lam_im) / den
    f_im = (lb_im * lam_re - (lb_re - 1.0) * lam_im) / den
    bb_re = f_re[..., None] * b_re - f_im[..., None] * b_im
    bb_im = f_re[..., None] * b_im + f_im[..., None] * b_re

    fr = jnp.arange(S5_L, dtype=f32)
    G, N = S5_GROUPS, S5_STATE

    def cmul(ar, ai, br, bi):
        return ar * br - ai * bi, ar * bi + ai * br

    pr, pi = lpow(fr[:, None, None])
    pr, pi = pr.transpose(1, 0, 2), pi.transpose(1, 0, 2)
    cpr, cpi = cmul(c_re[:, None], c_im[:, None], pr[:, :, None], pi[:, :, None])
    cpr, cpi = cpr.reshape(G, S5_GL, N), cpi.reshape(G, S5_GL, N)
    qr, qi = lpow(-fr[:, None, None])
    qr, qi = qr.transpose(1, 2, 0), qi.transpose(1, 2, 0)
    bpr, bpi = cmul(qr[..., None], qi[..., None], bb_re[:, :, None], bb_im[:, :, None])
    bpr, bpi = bpr.reshape(G, N, S5_GL), bpi.reshape(G, N, S5_GL)

    wr_, wi_ = lpow((S5_L - 1.0) - fr[:, None, None])
    wr_, wi_ = wr_.transpose(1, 0, 2), wi_.transpose(1, 0, 2)
    bbr_t, bbi_t = bb_re.transpose(0, 2, 1), bb_im.transpose(0, 2, 1)
    w_re, w_im = cmul(wr_[:, :, None], wi_[:, :, None], bbr_t[:, None], bbi_t[:, None])
    w_re, w_im = w_re.reshape(G, S5_GL, N), w_im.reshape(G, S5_GL, N)
    vr_, vi_ = lpow(fr[:, None, None] + 1.0)
    vr_, vi_ = vr_.transpose(1, 0, 2), vi_.transpose(1, 0, 2)
    v_re, v_im = cmul(c_re[:, None], c_im[:, None], vr_[:, :, None], vi_[:, :, None])
    v_re, v_im = v_re.reshape(G, S5_GL, N), v_im.reshape(G, S5_GL, N)

    zeros = jnp.zeros((S5_PAIRS, S5_GL, N), f32)

    def pair_cols(re, im):
        re, im = re.reshape(S5_PAIRS, 2, S5_GL, N), im.reshape(S5_PAIRS, 2, S5_GL, N)
        even = jnp.concatenate([re[:, 0], zeros, im[:, 0], zeros], axis=-1)
        odd = jnp.concatenate([zeros, re[:, 1], zeros, im[:, 1]], axis=-1)
        return jnp.stack([even, odd], axis=1)

    wp = pair_cols(w_re, w_im).reshape(S5_PAIRS, 2 * S5_GL, 2 * LANES).astype(bf16)
    vt = pair_cols(v_re, -v_im).reshape(S5_GROUPS, S5_GL, 2 * LANES).astype(bf16)
    a16r, a16i = lpow(float(S5_L))
    cp = jnp.concatenate([cpr, cpi], axis=2)
    bp = jnp.concatenate([bpr, -bpi], axis=1)
    return cp, bp, wp, vt, a16r.reshape(1, G * N), a16i.reshape(1, G * N)


def _pad_lanes(v, start=0):
    out = jnp.zeros((1, LANES), f32)
    return out.at[0, start:start + v.shape[0]].set(v.astype(f32))


def _layer(x, conv0, s0, xr0, xi0, prm, *, delta_cfg, s5_cfg, out_rows):
    B, T, _ = x.shape
    conv0p = jnp.concatenate([jnp.zeros((B, SUBLANES - (CONV_K - 1), CONV_CH), f32), conv0], axis=1)
    odn, us5, zs5, convout, s_new = _delta_call(
        x, prm["w_in_r"], prm["norm_w"], prm["conv_w"], prm["alog"], prm["dtb"], prm["dnw"],
        conv0p, s0, **delta_cfg)

    nstate = S5_GROUPS * S5_STATE
    if s5_cfg["flatten"]:
        tb = s5_cfg["tb"]
        uf = us5.transpose(1, 0, 2, 3).reshape(1, S5_PLANES, B * T, LANES)
        zf = zs5.reshape(1, B * T, S5_WIDTH)
        pad = tb - B * T
        uf = jnp.pad(uf, ((0, 0), (0, 0), (0, pad), (0, 0)))
        zf = jnp.pad(zf, ((0, 0), (0, pad), (0, 0)))
        os5, xr, xi = _s5_call(uf, zf, xr0.reshape(1, B, nstate), xi0.reshape(1, B, nstate),
                               prm["tmat"], prm["wp"], prm["vt"], prm["ar"], prm["ai"], prm["dvec"],
                               prm["gw"], prm["gb"], tb=tb, n_streams=B, n_chunks=T // S5_L)
        os5 = os5[0, :B * T].reshape(B, T, S5_WIDTH)
    else:
        tb = s5_cfg["tb"]
        os5, xr, xi = _s5_call(us5, zs5, xr0.reshape(B, 1, nstate), xi0.reshape(B, 1, nstate),
                               prm["tmat"], prm["wp"], prm["vt"], prm["ar"], prm["ai"], prm["dvec"],
                               prm["gw"], prm["gb"], tb=tb, n_streams=1, n_chunks=tb // S5_L)

    y = _out_call(x.reshape(B * T, D_MODEL), odn.reshape(B * T, DN_WIDTH), os5.reshape(B * T, S5_WIDTH),
                  prm["w_out"], prm["fw"], rows=out_rows)
    return (y.reshape(B, T, D_MODEL), convout[:, SUBLANES - (CONV_K - 1):, :], s_new,
            xr.reshape(B, S5_GROUPS, S5_STATE), xi.reshape(B, S5_GROUPS, S5_STATE))


def kernel(x_prompt, x_sample, cache_conv, state_dn, state_s5_re, state_s5_im, norm_w, w_in, conv_w, dn_A_log, dn_dt_bias, dn_norm_w, s5_A_re, s5_A_im, s5_log_dt, s5_B_re, s5_B_im, s5_C_re, s5_C_im, s5_D, glu_w, glu_b, w_out, final_norm_w):
    depth = norm_w.shape[0]
    assert depth == 1
    l = 0
    wi = w_in[l]
    o0 = CONV_CH
    o2 = o0 + 2 * DN_HEADS
    w_in_r = jnp.concatenate(
        [wi[:, :o0], wi[:, o2:], wi[:, o0:o2], jnp.zeros((D_MODEL, LANES - 2 * DN_HEADS), f32)],
        axis=1).astype(bf16)
    cp, bp, wp, vt, ar, ai = _s5_operands(
        s5_A_re[l].astype(f32), s5_A_im[l].astype(f32), s5_log_dt[l].astype(f32),
        s5_B_re[l].astype(f32), s5_B_im[l].astype(f32), s5_C_re[l].astype(f32), s5_C_im[l].astype(f32))
    prm = dict(
        w_in_r=w_in_r,
        norm_w=norm_w[l].reshape(1, D_MODEL).astype(f32),
        conv_w=conv_w[l].astype(f32),
        alog=_pad_lanes(dn_A_log[l]),
        dtb=_pad_lanes(dn_dt_bias[l]),
        dnw=dn_norm_w[l].reshape(1, DN_D).astype(f32),
        tmat=_s5mat_call(cp, bp),
        wp=wp, vt=vt, ar=ar, ai=ai,
        dvec=s5_D[l].reshape(1, S5_WIDTH).astype(f32),
        gw=glu_w[l].astype(bf16),
        gb=glu_b[l].reshape(1, S5_WIDTH).astype(f32),
        w_out=w_out[l].astype(bf16),
        fw=final_norm_w.reshape(1, D_MODEL).astype(f32),
    )

    bp = x_prompt.shape[0]
    yp, c1, d1, r1, i1 = _layer(
        x_prompt,
        jnp.zeros((bp, CONV_K - 1, CONV_CH), f32),
        jnp.zeros((bp, DN_HEADS, DN_D, DN_D), f32),
        jnp.zeros((bp, S5_GROUPS, S5_STATE), f32),
        jnp.zeros((bp, S5_GROUPS, S5_STATE), f32),
        prm,
        delta_cfg=dict(tb=256, pipelined=True),
        s5_cfg=dict(flatten=False, tb=2048),
        out_rows=1024)
    ys, c2, d2, r2, i2 = _layer(
        x_sample, cache_conv[l].astype(f32), state_dn[l].astype(f32),
        state_s5_re[l].astype(f32), state_s5_im[l].astype(f32),
        prm,
        delta_cfg=dict(tb=32, pipelined=False),
        s5_cfg=dict(flatten=True, tb=2048),
        out_rows=512)

    return (yp, ys, c1[None], d1[None], r1[None], i1[None], c2[None], d2[None], r2[None], i2[None])
```

```python
import functools
import math

import jax
import jax.numpy as jnp
import numpy as np
from jax import lax
from jax.experimental import pallas as pl
from jax.experimental.pallas import tpu as pltpu

bf16 = jnp.bfloat16
f32 = jnp.float32

LANES = 128
SUBLANES = 8

D_MODEL = 1024
DN_HEADS = 4
DN_D = 128
DN_WIDTH = DN_HEADS * DN_D
CONV_K = 4
CONV_CH = 3 * DN_WIDTH
S5_WIDTH = 512
S5_GROUP = 16
S5_GROUPS = 32
S5_STATE = 64
S5_PAIRS = S5_GROUPS // 2
S5_L = 16
S5_GL = S5_GROUP * S5_L
S5_PLANES = S5_WIDTH // LANES
EPS = 1e-6
L2_EPS = 1e-6

IN_COLS = CONV_CH + 2 * DN_HEADS + DN_WIDTH + 2 * S5_WIDTH
W2_COLS = LANES + DN_WIDTH + 2 * S5_WIDTH
AB_LANE = LANES - 2 * DN_HEADS
W2_ZDN = LANES
W2_US5 = W2_ZDN + DN_WIDTH
W2_ZS5 = W2_US5 + S5_WIDTH

GROUP = 256
MASK_INCL, MASK_STRICT, MASK_OFF = 0, 1, 2
VMEM_LIMIT =56 * 1024 * 1024

_NT = (((1,), (1,)), ((), ()))
_TN = (((0,), (0,)), ((), ()))


def _bdot(a, b):
    return jnp.dot(a.astype(bf16), b.astype(bf16), preferred_element_type=f32)


def _bdot_g(a, b, dims):
    return lax.dot_general(a.astype(bf16), b.astype(bf16), dims, preferred_element_type=f32)


def _sigmoid(x):
    return 1.0 / (1.0 + jnp.exp(-x))


def _silu(x):
    return x * _sigmoid(x)


def _softplus(x):
    return jnp.maximum(x, 0.0) + jnp.log1p(jnp.exp(-jnp.abs(x)))


def _gelu_tanh(x):
    c = math.sqrt(2.0 / math.pi)
    return 0.5 * x * (1.0 + jnp.tanh(c * (x + 0.044715 * (x * x * x))))


def _zero_after(x):
    bits = pltpu.bitcast(x[0:SUBLANES, 0:LANES].astype(f32), jnp.uint32)
    half = jnp.uint32(16)
    bits = lax.shift_right_logical(lax.shift_right_logical(bits, half), half)
    return pltpu.bitcast(bits, f32)[0:1, :]


def _unit_lower_solve(nmats, rhss, eye, m_ref, n_levels, fillers):
    d32 = functools.partial(jnp.dot, preferred_element_type=f32)

    def d16(a, b):
        return d32(a, b).astype(bf16)

    def fill(after):
        if fillers:
            fillers.pop(0)(_zero_after(after[-1]))

    n16 = [n.astype(bf16) for n in nmats]
    invs = [(eye - n * m_ref[MASK_OFF]).astype(bf16) for n in nmats]
    for lvl in range(1, n_levels):
        mask16 = m_ref[MASK_OFF + lvl].astype(bf16)
        t1 = [d16(inv, n * mask16) for inv, n in zip(invs, n16)]
        fill(invs)
        t2 = [d16(t, inv) for t, inv in zip(t1, invs)]
        fill(t1)
        invs = [inv - t for inv, t in zip(invs, t2)]
    return [d32(inv, r.astype(bf16)) for inv, r in zip(invs, rhss)]


def _segment_cumsum(tri16, g):
    g1 = g.astype(bf16)
    r1 = g - g1.astype(f32)
    g2 = r1.astype(bf16)
    g3 = (r1 - g2.astype(f32)).astype(bf16)
    s = jnp.dot(tri16, jnp.concatenate([g1, g2, g3], axis=1), preferred_element_type=f32)
    n = g.shape[1]
    return s[:, :n] + (s[:, n:2 * n] + s[:, 2 * n:])


def _delta_body(x_ref, w1_ref, w2_ref, nw_ref, cw_ref, alog_ref, dtb_ref, dnw_ref, m_ref, conv0_ref, s0_ref,
                odn_ref, us5_ref, zs5_ref, convout_ref, s_ref,
                cbuf, qn_scr, kn_scr, v_scr, gc_scr, beta_scr, zdn_scr, *, nb, tb, n_t, pipelined):
    rows = nb * tb
    n_groups = rows // GROUP
    segs = GROUP // tb
    n_levels = tb.bit_length() - 1
    step = pl.program_id(0)
    fresh = step < n_t

    @pl.when(step == 0)
    def _init():
        cbuf[:, 0:SUBLANES, :] = conv0_ref[...]
        s_ref[...] = s0_ref[...]
        if pipelined:
            for scr in (qn_scr, kn_scr, v_scr, gc_scr, beta_scr, zdn_scr):
                scr[...] = jnp.zeros_like(scr)

    front = {}

    def project(w_ref, lo, width):
        return jnp.dot(front["h"], w_ref[:, lo:lo + width], preferred_element_type=f32)

    def a_norm_in():
        x = x_ref[...].reshape(rows, D_MODEL)
        ms = jnp.mean(x * x, axis=-1, keepdims=True)
        front["h"] = (x * lax.rsqrt(ms + EPS) * nw_ref[...]).astype(bf16)

    def a_project_qkv(s):
        front[("raw", s)] = project(w1_ref, s * DN_WIDTH, DN_WIDTH)

    def a_project_us5():
        u = project(w2_ref, W2_US5, S5_WIDTH)
        for j in range(S5_PLANES):
            us5_ref[:, j] = u[:, j * LANES:(j + 1) * LANES].reshape(nb, tb, LANES)

    def a_project_zs5():
        zs5_ref[...] = project(w2_ref, W2_ZS5, S5_WIDTH).reshape(nb, tb, S5_WIDTH)

    def a_project_zdn():
        front["zdn"] = project(w2_ref, W2_ZDN, DN_WIDTH)

    def a_project_gates():
        ab = project(w2_ref, 0, LANES)
        front["g"] = -jnp.exp(alog_ref[...]) * _softplus(ab + dtb_ref[...])
        front["beta"] = _sigmoid(ab)

    projections = ([functools.partial(a_project_qkv, s) for s in range(3)]
                   + [a_project_us5, a_project_zs5, a_project_zdn, a_project_gates])

    def a_conv(b, s, zero):
        cols = slice(s * DN_WIDTH, (s + 1) * DN_WIDTH)
        cbuf[b, SUBLANES:SUBLANES + tb, cols] = front[("raw", s)][b * tb:(b + 1) * tb, :]
        taps = [cw_ref[j:j + 1, cols] for j in range(CONV_K)]
        if zero is not None:
            zero_w = jnp.concatenate([zero] * (DN_WIDTH // LANES), axis=1)
            taps = [t + zero_w for t in taps]
        acc = cbuf[b, 5:5 + tb, cols] * taps[0]
        for j in range(1, CONV_K):
            acc = acc + cbuf[b, 5 + j:5 + j + tb, cols] * taps[j]
        front[("act", b, s)] = _silu(acc)
        tail = cbuf[b, tb:tb + SUBLANES, cols]
        if pipelined:
            tail = jnp.where(fresh, tail, cbuf[b, 0:SUBLANES, cols])
        cbuf[b, 0:SUBLANES, cols] = tail
        convout_ref[b, :, cols] = tail

    def a_norm(b, hd, zero):
        lo = hd * DN_D
        q = front[("act", b, 0)][:, lo:lo + DN_D]
        k = front[("act", b, 1)][:, lo:lo + DN_D]
        if zero is not None:
            q, k = q + zero, k + zero
        front[("qn", b, hd)] = (q * lax.rsqrt(jnp.sum(q * q, axis=-1, keepdims=True) + L2_EPS)
                                * (DN_D ** -0.5))
        front[("kn", b, hd)] = k * lax.rsqrt(jnp.sum(k * k, axis=-1, keepdims=True) + L2_EPS)

    def a_gates(zero):
        tri16 = m_ref[MASK_INCL].astype(bf16)
        g = front["g"] if zero is None else front["g"] + zero
        front["gc"] = [_segment_cumsum(tri16, g[gi * GROUP:(gi + 1) * GROUP])
                       for gi in range(n_groups)]

    def a_store():
        for b in range(nb):
            r = slice(b * tb, (b + 1) * tb)
            for hd in range(DN_HEADS):
                c = slice(hd * DN_D, (hd + 1) * DN_D)
                qn_scr[r, c] = front[("qn", b, hd)]
                kn_scr[r, c] = front[("kn", b, hd)]
            v_scr[r, :] = front[("act", b, 2)]
        for gi in range(n_groups):
            gc_scr[gi * GROUP:(gi + 1) * GROUP, :] = front["gc"][gi]
        beta_scr[...] = front["beta"]
        zdn_scr[...] = front["zdn"]

    fillers = ([functools.partial(a_conv, b, s) for b in range(nb) for s in range(3)] + [a_gates]
               + [functools.partial(a_norm, b, hd) for b in range(nb) for hd in range(DN_HEADS)])

    incl = m_ref[MASK_INCL]
    strict = m_ref[MASK_STRICT]
    eye = incl - strict
    dnw = dnw_ref[...]
    chains = [(gi, hd) for gi in range(n_groups) for hd in range(DN_HEADS)]
    back = {}

    def stream_of(gi, sg):
        return (gi * GROUP) // tb + sg

    for name in ("q16", "qe16", "k16", "kb16", "rhs", "dmat", "kdec", "sdecay", "nmat", "attn"):
        back[name] = [None] * len(chains)

    def b_load(ci):
        gi, hd = chains[ci]
        r = slice(gi * GROUP, (gi + 1) * GROUP)
        c = slice(hd * DN_D, (hd + 1) * DN_D)
        q, k, v = qn_scr[r, c], kn_scr[r, c], v_scr[r, c]
        beta = beta_scr[r, AB_LANE + DN_HEADS + hd:AB_LANE + DN_HEADS + hd + 1]
        g = gc_scr[r, AB_LANE + hd:AB_LANE + hd + 1]
        eg = jnp.exp(g)
        kb = k * beta
        gcb = jnp.broadcast_to(g, (GROUP, GROUP))
        back["dmat"][ci] = jnp.exp((gcb - gcb.T) * incl) * incl
        back["q16"][ci] = q.astype(bf16)
        back["qe16"][ci] = (q * eg).astype(bf16)
        back["k16"][ci] = k.astype(bf16)
        back["kb16"][ci] = kb.astype(bf16)
        back["rhs"][ci] = jnp.concatenate([v * beta, kb * eg], axis=1)
        kd, sd = [], []
        for sg in range(segs):
            a0 = sg * tb
            glast = g[a0 + tb - 1:a0 + tb, :]
            kd.append((k[a0:a0 + tb] * jnp.exp(glast - g[a0:a0 + tb])).astype(bf16))
            sd.append(jnp.exp(glast))
        back["kdec"][ci] = kd
        back["sdecay"][ci] = sd

    def b_mats(ci):
        d = functools.partial(lax.dot_general, dimension_numbers=_NT, preferred_element_type=f32)
        dm = back["dmat"][ci]
        back["nmat"][ci] = d(back["kb16"][ci], back["k16"][ci]) * (dm * strict)
        back["attn"][ci] = (d(back["q16"][ci], back["k16"][ci]) * dm).astype(bf16)

    def b_state(sols):
        s_old = {(b, hd): s_ref[b, hd] for b in range(nb) for hd in range(DN_HEADS)}
        s16 = {key: val.astype(bf16) for key, val in s_old.items()}
        v_news, o_states = [], []
        for (gi, hd), sol, qe16 in zip(chains, sols, back["qe16"]):
            u = sol[:, :DN_D]
            w16 = sol[:, DN_D:].astype(bf16)
            vn, os_ = [], []
            for sg in range(segs):
                a0 = sg * tb
                st = s16[(stream_of(gi, sg), hd)]
                vn.append(u[a0:a0 + tb] - jnp.dot(w16[a0:a0 + tb], st, preferred_element_type=f32))
                os_.append(jnp.dot(qe16[a0:a0 + tb], st, preferred_element_type=f32))
            v_news.append(vn[0] if segs == 1 else jnp.concatenate(vn, axis=0))
            o_states.append(os_[0] if segs == 1 else jnp.concatenate(os_, axis=0))
        vn16s = [vn.astype(bf16) for vn in v_news]
        outs = [os_ + jnp.dot(a, vn, preferred_element_type=f32)
                for os_, a, vn in zip(o_states, back["attn"], vn16s)]

        s_new = {}
        for (gi, hd), kd, sd, vn in zip(chains, back["kdec"], back["sdecay"], vn16s):
            for sg in range(segs):
                a0 = sg * tb
                key = (stream_of(gi, sg), hd)
                s_new[key] = (s_old[key] * sd[sg]
                              + lax.dot_general(kd[sg], vn[a0:a0 + tb], _TN, preferred_element_type=f32))

        o_rows = []
        for gi in range(n_groups):
            o_heads = []
            for hd in range(DN_HEADS):
                o = outs[gi * DN_HEADS + hd]
                on = o * lax.rsqrt(jnp.mean(o * o, axis=-1, keepdims=True) + EPS) * dnw
                zd = zdn_scr[gi * GROUP:(gi + 1) * GROUP, hd * DN_D:(hd + 1) * DN_D]
                o_heads.append(on * _silu(zd))
            o_rows.append(jnp.concatenate(o_heads, axis=1))
        o_all = o_rows[0] if n_groups == 1 else jnp.concatenate(o_rows, axis=0)
        odn_ref[...] = o_all.astype(bf16).reshape(nb, tb, DN_WIDTH)
        s_ref[...] = jnp.stack([jnp.stack([s_new[(b, hd)] for hd in range(DN_HEADS)]) for b in range(nb)])

    n_ch = len(chains)
    if pipelined:
        a_norm_in()
        back_setup = ([functools.partial(b_load, ci) for ci in range(n_ch)]
                      + [functools.partial(b_mats, ci) for ci in range(n_ch)])
        per_piece = -(-len(back_setup) // len(projections))
        for piece in projections:
            piece()
            for thunk in back_setup[:per_piece]:
                thunk()
            del back_setup[:per_piece]
        sols = _unit_lower_solve(back["nmat"], back["rhs"], eye, m_ref, n_levels, fillers)
        for fill in fillers:
            fill(None)
        b_state(sols)
        a_store()
    else:
        a_norm_in()
        for piece in projections:
            piece()
        for fill in fillers:
            fill(None)
        a_store()
        for ci in range(n_ch):
            b_load(ci)
        for ci in range(n_ch):
            b_mats(ci)
        b_state(_unit_lower_solve(back["nmat"], back["rhs"], eye, m_ref, n_levels, []))


def _delta_masks(tb):
    r = np.arange(GROUP)[:, None]
    c = np.arange(GROUP)[None, :]
    same = (r // tb) == (c // tb)
    ms = [same & (r >= c), same & (r > c)]
    for l in range(tb.bit_length() - 1):
        ms.append(((r >> (l + 1)) == (c >> (l + 1))) & (((r >> l) & 1) == 1) & (((c >> l) & 1) == 0))
    return jnp.asarray(np.stack(ms).astype(np.float32))


def _delta_call(x, w1, w2, norm_w, conv_w, alog, dtb, dnw, conv0, s0, *, tb, pipelined):
    nb, T, _ = x.shape
    assert GROUP % tb == 0 and (nb * tb) % GROUP == 0 and T % tb == 0
    n_t = T // tb
    rows = nb * tb
    if pipelined:
        grid = (n_t + 1,)
        cur = lambda i: jnp.minimum(i, n_t - 1)
        prev = lambda i: jnp.maximum(i - 1, 0)
    else:
        grid = (n_t,)
        cur = prev = lambda i: i
    const2 = lambda i: (0, 0)
    const3 = lambda i: (0, 0, 0)
    const4 = lambda i: (0, 0, 0, 0)
    body = functools.partial(_delta_body, nb=nb, tb=tb, n_t=n_t, pipelined=pipelined)
    masks = _delta_masks(tb)
    one = pl.Buffered(1)
    return pl.pallas_call(
        body,
        grid=grid,
        in_specs=[
            pl.BlockSpec((nb, tb, D_MODEL), lambda i: (0, cur(i), 0)),
            pl.BlockSpec((D_MODEL, CONV_CH), const2, pipeline_mode=one),
            pl.BlockSpec((D_MODEL, W2_COLS), const2, pipeline_mode=one),
            pl.BlockSpec((1, D_MODEL), const2),
            pl.BlockSpec((CONV_K, CONV_CH), const2),
            pl.BlockSpec((1, LANES), const2),
            pl.BlockSpec((1, LANES), const2),
            pl.BlockSpec((1, DN_D), const2),
            pl.BlockSpec(masks.shape, const3, pipeline_mode=one),
            pl.BlockSpec((nb, SUBLANES, CONV_CH), const3),
            pl.BlockSpec((nb, DN_HEADS, DN_D, DN_D), const4),
        ],
        out_specs=[
            pl.BlockSpec((nb, tb, DN_WIDTH), lambda i: (0, prev(i), 0)),
            pl.BlockSpec((nb, S5_PLANES, tb, LANES), lambda i: (0, 0, cur(i), 0)),
            pl.BlockSpec((nb, tb, S5_WIDTH), lambda i: (0, cur(i), 0)),
            pl.BlockSpec((nb, SUBLANES, CONV_CH), const3),
            pl.BlockSpec((nb, DN_HEADS, DN_D, DN_D), const4),
        ],
        out_shape=[
            jax.ShapeDtypeStruct((nb, T, DN_WIDTH), bf16),
            jax.ShapeDtypeStruct((nb, S5_PLANES, T, LANES), f32),
            jax.ShapeDtypeStruct((nb, T, S5_WIDTH), f32),
            jax.ShapeDtypeStruct((nb, SUBLANES, CONV_CH), f32),
            jax.ShapeDtypeStruct((nb, DN_HEADS, DN_D, DN_D), f32),
        ],
        scratch_shapes=[
            pltpu.VMEM((nb, tb + SUBLANES, CONV_CH), f32),
            pltpu.VMEM((rows, DN_WIDTH), f32),
            pltpu.VMEM((rows, DN_WIDTH), f32),
            pltpu.VMEM((rows, DN_WIDTH), f32),
            pltpu.VMEM((rows, LANES), f32),
            pltpu.VMEM((rows, LANES), f32),
            pltpu.VMEM((rows, DN_WIDTH), f32),
        ],
        compiler_params=pltpu.CompilerParams(
            dimension_semantics=("arbitrary",), vmem_limit_bytes=VMEM_LIMIT),
        name="delta",
    )(x, w1, w2, norm_w, conv_w, alog, dtb, dnw, masks, conv0, s0)


S5MAT_GROUPS_PER_STEP = 8


def _split16(a):
    hi = a.astype(bf16)
    return hi, (a - hi.astype(f32)).astype(bf16)


def _s5mat_body(cp_ref, bp_ref, t_ref):
    ri = lax.broadcasted_iota(jnp.int32, (S5_GL, S5_GL), 0) >> 4
    ci = lax.broadcasted_iota(jnp.int32, (S5_GL, S5_GL), 1) >> 4
    causal = ri >= ci
    d = functools.partial(lax.dot_general, dimension_numbers=_NT, preferred_element_type=f32)
    for g in range(S5MAT_GROUPS_PER_STEP):
        ch, cl = _split16(cp_ref[g])
        bh, bl = _split16(bp_ref[g])
        t = d(ch, bh) + (d(ch, bl) + d(cl, bh))
        t_ref[g] = jnp.where(causal, t, 0.0).astype(bf16)


def _s5mat_call(cp, bp):
    n = S5MAT_GROUPS_PER_STEP
    return pl.pallas_call(
        _s5mat_body,
        grid=(S5_GROUPS // n,),
        in_specs=[
            pl.BlockSpec((n, S5_GL, 2 * S5_STATE), lambda g: (g, 0, 0)),
            pl.BlockSpec((n, S5_GL, 2 * S5_STATE), lambda g: (g, 0, 0)),
        ],
        out_specs=pl.BlockSpec((n, S5_GL, S5_GL), lambda g: (g, 0, 0)),
        out_shape=jax.ShapeDtypeStruct((S5_GROUPS, S5_GL, S5_GL), bf16),
        compiler_params=pltpu.CompilerParams(dimension_semantics=("arbitrary",)),
        name="s5mat",
    )(cp, bp)


def _s5_body(u_ref, z_ref, xr0_ref, xi0_ref, t_ref, wp_ref, vt_ref, ar_ref, ai_ref, d_ref,
             gw_ref, gb_ref,
             o_ref, xr_ref, xi_ref,
             at_scr, er_scr, ei_scr, xinr_scr, xini_scr, yt_scr, y_scr, *, tb, n_streams, n_chunks):
    R = tb // S5_L
    used = n_streams * n_chunks

    @pl.when(pl.program_id(1) == 0)
    def _init():
        xr_ref[...] = xr0_ref[...]
        xi_ref[...] = xi0_ref[...]

    for f in range(S5_L):
        for j in range(S5_PLANES):
            ut = u_ref[0, j, pl.ds(f, R, stride=S5_L), :]
            ut_t = ut.T.astype(bf16)
            for gl in range(LANES // S5_GROUP):
                g = j * (LANES // S5_GROUP) + gl
                at_scr[g, f * S5_GROUP:(f + 1) * S5_GROUP, :] = ut_t[gl * S5_GROUP:(gl + 1) * S5_GROUP, :]

    for p in range(S5_PAIRS):
        a_t = at_scr[2 * p:2 * p + 2].reshape(2 * S5_GL, R)
        e = lax.dot_general(a_t, wp_ref[p], _TN, preferred_element_type=f32)
        er_scr[:, p * LANES:(p + 1) * LANES] = e[:, :LANES]
        ei_scr[:, p * LANES:(p + 1) * LANES] = e[:, LANES:]

    if used < R:
        xinr_scr[...] = jnp.zeros_like(xinr_scr)
        xini_scr[...] = jnp.zeros_like(xini_scr)

    a_r = ar_ref[...]
    a_i = ai_ref[...]
    for s in range(n_streams):
        def step(c, carry):
            xr, xi = carry
            row = s * n_chunks + c
            xinr_scr[pl.ds(row, 1), :] = xr
            xini_scr[pl.ds(row, 1), :] = xi
            er = er_scr[pl.ds(row, 1), :]
            ei = ei_scr[pl.ds(row, 1), :]
            return a_r * xr - a_i * xi + er, a_r * xi + a_i * xr + ei

        xr, xi = lax.fori_loop(0, n_chunks, step, (xr_ref[0, s:s + 1, :], xi_ref[0, s:s + 1, :]))
        xr_ref[0, s:s + 1, :] = xr
        xi_ref[0, s:s + 1, :] = xi

    for g in range(S5_GROUPS):
        p = g // 2
        xin = jnp.concatenate([xinr_scr[:, p * LANES:(p + 1) * LANES],
                               xini_scr[:, p * LANES:(p + 1) * LANES]], axis=1).astype(bf16)
        yt_scr[g] = (jnp.dot(t_ref[g], at_scr[g], preferred_element_type=f32)
                     + lax.dot_general(vt_ref[g], xin, _NT, preferred_element_type=f32))

    for f in range(S5_L):
        gpp = LANES // S5_GROUP
        for j in range(S5_PLANES):
            yt = yt_scr[j * gpp:(j + 1) * gpp, f * S5_GROUP:(f + 1) * S5_GROUP, :].reshape(LANES, R)
            y_scr[j, pl.ds(f, R, stride=S5_L), :] = yt.T

    sub = 256
    for rb in range(tb // sub):
        sl = pl.ds(rb * sub, sub)
        y_intra = jnp.concatenate([y_scr[j, sl, :] for j in range(S5_PLANES)], axis=1)
        u_nat = jnp.concatenate([u_ref[0, j, sl, :] for j in range(S5_PLANES)], axis=1)
        y = y_intra + d_ref[...] * u_nat
        gy = _gelu_tanh(y)
        gate = _sigmoid(_bdot(gy, gw_ref[...]) + gb_ref[...])
        o_ref[0, sl, :] = (gy * gate * _silu(z_ref[0, sl, :])).astype(bf16)


def _s5_call(u, z, xr0, xi0, tmat, wp, vt, ar, ai, dvec, gw, gb, *, tb, n_streams, n_chunks):
    G, T, _ = z.shape
    grid = (G, T // tb)
    R = tb // S5_L
    nstate = S5_GROUPS * S5_STATE
    c2 = lambda g, t: (0, 0)
    c3 = lambda g, t: (0, 0, 0)
    body = functools.partial(_s5_body, tb=tb, n_streams=n_streams, n_chunks=n_chunks)
    one = pl.Buffered(1)
    return pl.pallas_call(
        body,
        grid=grid,
        in_specs=[
            pl.BlockSpec((1, S5_PLANES, tb, LANES), lambda g, t: (g, 0, t, 0)),
            pl.BlockSpec((1, tb, S5_WIDTH), lambda g, t: (g, t, 0)),
            pl.BlockSpec((1, n_streams, nstate), lambda g, t: (g, 0, 0)),
            pl.BlockSpec((1, n_streams, nstate), lambda g, t: (g, 0, 0)),
            pl.BlockSpec((S5_GROUPS, S5_GL, S5_GL), c3, pipeline_mode=one),
            pl.BlockSpec((S5_PAIRS, 2 * S5_GL, 2 * LANES), c3, pipeline_mode=one),
            pl.BlockSpec((S5_GROUPS, S5_GL, 2 * LANES), c3, pipeline_mode=one),
            pl.BlockSpec((1, nstate), c2),
            pl.BlockSpec((1, nstate), c2),
            pl.BlockSpec((1, S5_WIDTH), c2),
            pl.BlockSpec((S5_WIDTH, S5_WIDTH), c2),
            pl.BlockSpec((1, S5_WIDTH), c2),
        ],
        out_specs=[
            pl.BlockSpec((1, tb, S5_WIDTH), lambda g, t: (g, t, 0)),
            pl.BlockSpec((1, n_streams, nstate), lambda g, t: (g, 0, 0)),
            pl.BlockSpec((1, n_streams, nstate), lambda g, t: (g, 0, 0)),
        ],
        out_shape=[
            jax.ShapeDtypeStruct((G, T, S5_WIDTH), bf16),
            jax.ShapeDtypeStruct((G, n_streams, nstate), f32),
            jax.ShapeDtypeStruct((G, n_streams, nstate), f32),
        ],
        scratch_shapes=[
            pltpu.VMEM((S5_GROUPS, S5_GL, R), bf16),
            pltpu.VMEM((R, nstate), f32),
            pltpu.VMEM((R, nstate), f32),
            pltpu.VMEM((R, nstate), f32),
            pltpu.VMEM((R, nstate), f32),
            pltpu.VMEM((S5_GROUPS, S5_GL, R), f32),
            pltpu.VMEM((S5_PLANES, tb, LANES), f32),
        ],
        compiler_params=pltpu.CompilerParams(
            dimension_semantics=("arbitrary", "arbitrary"), vmem_limit_bytes=VMEM_LIMIT),
        name="s5",
    )(u, z, xr0, xi0, tmat, wp, vt, ar, ai, dvec, gw, gb)


def _out_body(x_ref, odn_ref, os5_ref, wo_ref, fw_ref, y_ref):
    acc = (x_ref[...]
           + jnp.dot(odn_ref[...], wo_ref[0:DN_WIDTH, :], preferred_element_type=f32)
           + jnp.dot(os5_ref[...], wo_ref[DN_WIDTH:, :], preferred_element_type=f32))
    ms = jnp.mean(acc * acc, axis=-1, keepdims=True)
    y_ref[...] = acc * lax.rsqrt(ms + EPS) * fw_ref[...]


def _out_call(x2, odn2, os52, w_out16, fw, *, rows):
    n = x2.shape[0]
    return pl.pallas_call(
        _out_body,
        grid=(n // rows,),
        in_specs=[
            pl.BlockSpec((rows, D_MODEL), lambda i: (i, 0)),
            pl.BlockSpec((rows, DN_WIDTH), lambda i: (i, 0)),
            pl.BlockSpec((rows, S5_WIDTH), lambda i: (i, 0)),
            pl.BlockSpec((D_MODEL, D_MODEL), lambda i: (0, 0)),
            pl.BlockSpec((1, D_MODEL), lambda i: (0, 0)),
        ],
        out_specs=pl.BlockSpec((rows, D_MODEL), lambda i: (i, 0)),
        out_shape=jax.ShapeDtypeStruct((n, D_MODEL), f32),
        compiler_params=pltpu.CompilerParams(
            dimension_semantics=("arbitrary",), vmem_limit_bytes=VMEM_LIMIT),
        name="outproj",
    )(x2, odn2, os52, w_out16, fw)


def _s5_operands(a_re, a_im, log_dt, b_re, b_im, c_re, c_im):
    lam_re = jnp.minimum(a_re, -1e-4)
    lam_im = a_im
    dt = jnp.exp(log_dt)[:, None]
    ldt_re, ldt_im = lam_re * dt, lam_im * dt

    def lpow(k):
        mag = jnp.exp(ldt_re * k)
        return mag * jnp.cos(ldt_im * k), mag * jnp.sin(ldt_im * k)

    lb_re, lb_im = lpow(1.0)
    den = lam_re * lam_re + lam_im * lam_im
    f_re = ((lb_re - 1.0) * lam_re + lb_im * lam_im) / den
    f_im = (lb_im * lam_re - (lb_re - 1.0) * lam_im) / den
    bb_re = f_re[..., None] * b_re - f_im[..., None] * b_im
    bb_im = f_re[..., None] * b_im + f_im[..., None] * b_re

    fr = jnp.arange(S5_L, dtype=f32)
    G, N = S5_GROUPS, S5_STATE

    def cmul(ar, ai, br, bi):
        return ar * br - ai * bi, ar * bi + ai * br

    pr, pi = lpow(fr[:, None, None])
    pr, pi = pr.transpose(1, 0, 2), pi.transpose(1, 0, 2)
    cpr, cpi = cmul(c_re[:, None], c_im[:, None], pr[:, :, None], pi[:, :, None])
    cpr, cpi = cpr.reshape(G, S5_GL, N), cpi.reshape(G, S5_GL, N)
    bbr_t, bbi_t = bb_re.transpose(0, 2, 1), bb_im.transpose(0, 2, 1)
    qr, qi = lpow(-fr[:, None, None])
    qr, qi = qr.transpose(1, 0, 2), qi.transpose(1, 0, 2)
    bpr, bpi = cmul(qr[:, :, None], qi[:, :, None], bbr_t[:, None], bbi_t[:, None])
    bpr, bpi = bpr.reshape(G, S5_GL, N), bpi.reshape(G, S5_GL, N)

    wr_, wi_ = lpow((S5_L - 1.0) - fr[:, None, None])
    wr_, wi_ = wr_.transpose(1, 0, 2), wi_.transpose(1, 0, 2)
    w_re, w_im = cmul(wr_[:, :, None], wi_[:, :, None], bbr_t[:, None], bbi_t[:, None])
    w_re, w_im = w_re.reshape(G, S5_GL, N), w_im.reshape(G, S5_GL, N)
    vr_, vi_ = lpow(fr[:, None, None] + 1.0)
    vr_, vi_ = vr_.transpose(1, 0, 2), vi_.transpose(1, 0, 2)
    v_re, v_im = cmul(c_re[:, None], c_im[:, None], vr_[:, :, None], vi_[:, :, None])
    v_re, v_im = v_re.reshape(G, S5_GL, N), v_im.reshape(G, S5_GL, N)

    zeros = jnp.zeros((S5_PAIRS, S5_GL, N), f32)

    def pair_cols(re, im):
        re, im = re.reshape(S5_PAIRS, 2, S5_GL, N), im.reshape(S5_PAIRS, 2, S5_GL, N)
        even = jnp.concatenate([re[:, 0], zeros, im[:, 0], zeros], axis=-1)
        odd = jnp.concatenate([zeros, re[:, 1], zeros, im[:, 1]], axis=-1)
        return jnp.stack([even, odd], axis=1)

    wp = pair_cols(w_re, w_im).reshape(S5_PAIRS, 2 * S5_GL, 2 * LANES).astype(bf16)
    vt = pair_cols(v_re, -v_im).reshape(S5_GROUPS, S5_GL, 2 * LANES).astype(bf16)
    a16r, a16i = lpow(float(S5_L))
    cp = jnp.concatenate([cpr, cpi], axis=2)
    bp = jnp.concatenate([bpr, -bpi], axis=2)
    return cp, bp, wp, vt, a16r.reshape(1, G * N), a16i.reshape(1, G * N)


def _pad_lanes(v, start=0):
    out = jnp.zeros((1, LANES), f32)
    return out.at[0, start:start + v.shape[0]].set(v.astype(f32))


def _layer(x, conv0, s0, xr0, xi0, prm, *, delta_cfg, s5_cfg, out_rows):
    B, T, _ = x.shape
    conv0p = jnp.concatenate([jnp.zeros((B, SUBLANES - (CONV_K - 1), CONV_CH), f32), conv0], axis=1)
    odn, us5, zs5, convout, s_new = _delta_call(
        x, prm["w1"], prm["w2"], prm["norm_w"], prm["conv_w"], prm["alog"], prm["dtb"], prm["dnw"],
        conv0p, s0, **delta_cfg)

    nstate = S5_GROUPS * S5_STATE
    if s5_cfg["flatten"]:
        tb = s5_cfg["tb"]
        uf = us5.transpose(1, 0, 2, 3).reshape(1, S5_PLANES, B * T, LANES)
        zf = zs5.reshape(1, B * T, S5_WIDTH)
        pad = tb - B * T
        uf = jnp.pad(uf, ((0, 0), (0, 0), (0, pad), (0, 0)))
        zf = jnp.pad(zf, ((0, 0), (0, pad), (0, 0)))
        os5, xr, xi = _s5_call(uf, zf, xr0.reshape(1, B, nstate), xi0.reshape(1, B, nstate),
                               prm["tmat"], prm["wp"], prm["vt"], prm["ar"], prm["ai"], prm["dvec"],
                               prm["gw"], prm["gb"], tb=tb, n_streams=B, n_chunks=T // S5_L)
        os5 = os5[0, :B * T].reshape(B, T, S5_WIDTH)
    else:
        tb = s5_cfg["tb"]
        os5, xr, xi = _s5_call(us5, zs5, xr0.reshape(B, 1, nstate), xi0.reshape(B, 1, nstate),
                               prm["tmat"], prm["wp"], prm["vt"], prm["ar"], prm["ai"], prm["dvec"],
                               prm["gw"], prm["gb"], tb=tb, n_streams=1, n_chunks=tb // S5_L)

    y = _out_call(x.reshape(B * T, D_MODEL), odn.reshape(B * T, DN_WIDTH), os5.reshape(B * T, S5_WIDTH),
                  prm["w_out"], prm["fw"], rows=out_rows)
    return (y.reshape(B, T, D_MODEL), convout[:, SUBLANES - (CONV_K - 1):, :], s_new,
            xr.reshape(B, S5_GROUPS, S5_STATE), xi.reshape(B, S5_GROUPS, S5_STATE))


def kernel(x_prompt, x_sample, cache_conv, state_dn, state_s5_re, state_s5_im, norm_w, w_in, conv_w, dn_A_log, dn_dt_bias, dn_norm_w, s5_A_re, s5_A_im, s5_log_dt, s5_B_re, s5_B_im, s5_C_re, s5_C_im, s5_D, glu_w, glu_b, w_out, final_norm_w):
    depth = norm_w.shape[0]
    assert depth == 1
    l = 0
    assert w_in.shape[-1] == IN_COLS
    w1 = w_in[l, :, :CONV_CH].astype(bf16)
    w2 = w_in[l, :, IN_COLS - W2_COLS:].astype(bf16)
    cp, bp, wp, vt, ar, ai = _s5_operands(
        s5_A_re[l].astype(f32), s5_A_im[l].astype(f32), s5_log_dt[l].astype(f32),
        s5_B_re[l].astype(f32), s5_B_im[l].astype(f32), s5_C_re[l].astype(f32), s5_C_im[l].astype(f32))
    prm = dict(
        w1=w1, w2=w2,
        norm_w=norm_w[l].reshape(1, D_MODEL).astype(f32),
        conv_w=conv_w[l].astype(f32),
        alog=_pad_lanes(dn_A_log[l], AB_LANE),
        dtb=_pad_lanes(dn_dt_bias[l], AB_LANE),
        dnw=dn_norm_w[l].reshape(1, DN_D).astype(f32),
        tmat=_s5mat_call(cp, bp),
        wp=wp, vt=vt, ar=ar, ai=ai,
        dvec=s5_D[l].reshape(1, S5_WIDTH).astype(f32),
        gw=glu_w[l].astype(bf16),
        gb=glu_b[l].reshape(1, S5_WIDTH).astype(f32),
        w_out=w_out[l].astype(bf16),
        fw=final_norm_w.reshape(1, D_MODEL).astype(f32),
    )

    bp = x_prompt.shape[0]
    yp, c1, d1, r1, i1 = _layer(
        x_prompt,
        jnp.zeros((bp, CONV_K - 1, CONV_CH), f32),
        jnp.zeros((bp, DN_HEADS, DN_D, DN_D), f32),
        jnp.zeros((bp, S5_GROUPS, S5_STATE), f32),
        jnp.zeros((bp, S5_GROUPS, S5_STATE), f32),
        prm,
        delta_cfg=dict(tb=256, pipelined=True),
        s5_cfg=dict(flatten=False, tb=2048),
        out_rows=2048)
    ys, c2, d2, r2, i2 = _layer(
        x_sample, cache_conv[l].astype(f32), state_dn[l].astype(f32),
        state_s5_re[l].astype(f32), state_s5_im[l].astype(f32),
        prm,
        delta_cfg=dict(tb=32, pipelined=False),
        s5_cfg=dict(flatten=True, tb=2048),
        out_rows=512)

    return (yp, ys, c1[None], d1[None], r1[None], i1[None], c2[None], d2[None], r2[None], i2[None])
```

```python
import functools
import math

import jax
import jax.numpy as jnp
import numpy as np
from jax import lax
from jax.experimental import pallas as pl
from jax.experimental.pallas import tpu as pltpu

bf16 = jnp.bfloat16
f32 = jnp.float32

LANES = 128
SUBLANES = 8

D_MODEL = 1024
DN_HEADS = 4
DN_D = 128
DN_WIDTH = DN_HEADS * DN_D
CONV_K = 4
CONV_CH = 3 * DN_WIDTH
S5_WIDTH = 512
S5_GROUP = 16
S5_GROUPS = 32
S5_STATE = 64
S5_PAIRS = S5_GROUPS // 2
S5_L = 16
S5_GL = S5_GROUP * S5_L
OUT_SUB = 256
S5_PLANES = S5_WIDTH // LANES
EPS = 1e-6
L2_EPS = 1e-6

IN_COLS = CONV_CH + 2 * DN_HEADS + DN_WIDTH + 2 * S5_WIDTH
W2_COLS = LANES + DN_WIDTH + 2 * S5_WIDTH
AB_LANE = LANES - 2 * DN_HEADS
W2_ZDN = LANES
W2_US5 = W2_ZDN + DN_WIDTH
W2_ZS5 = W2_US5 + S5_WIDTH

GROUP = 256
MASK_INCL, MASK_STRICT, MASK_OFF = 0, 1, 2
VMEM_LIMIT =56 * 1024 * 1024

_NT = (((1,), (1,)), ((), ()))
_TN = (((0,), (0,)), ((), ()))


def _bdot(a, b):
    return jnp.dot(a.astype(bf16), b.astype(bf16), preferred_element_type=f32)


def _bdot_g(a, b, dims):
    return lax.dot_general(a.astype(bf16), b.astype(bf16), dims, preferred_element_type=f32)


def _sigmoid(x):
    return 1.0 / (1.0 + jnp.exp(-x))


def _silu(x):
    return x * _sigmoid(x)


def _softplus(x):
    return jnp.maximum(x, 0.0) + jnp.log1p(jnp.exp(-jnp.abs(x)))


def _gelu_tanh(x):
    c = math.sqrt(2.0 / math.pi)
    return 0.5 * x * (1.0 + jnp.tanh(c * (x + 0.044715 * (x * x * x))))


def _zero_after(x):
    bits = pltpu.bitcast(x[0:SUBLANES, 0:LANES].astype(f32), jnp.uint32)
    half = jnp.uint32(16)
    bits = lax.shift_right_logical(lax.shift_right_logical(bits, half), half)
    return pltpu.bitcast(bits, f32)[0:1, :]


def _unit_lower_solve(nmats, rhss, eye, m_ref, n_levels, fillers):
    d32 = functools.partial(jnp.dot, preferred_element_type=f32)

    def d16(a, b):
        return d32(a, b).astype(bf16)

    def fill(after):
        if fillers:
            fillers.pop(0)(_zero_after(after[-1]))

    n16 = [n.astype(bf16) for n in nmats]
    invs = [(eye - n * m_ref[MASK_OFF]).astype(bf16) for n in nmats]
    for lvl in range(1, n_levels):
        mask16 = m_ref[MASK_OFF + lvl].astype(bf16)
        t1 = [d16(inv, n * mask16) for inv, n in zip(invs, n16)]
        fill(invs)
        t2 = [d16(t, inv) for t, inv in zip(t1, invs)]
        fill(t1)
        invs = [inv - t for inv, t in zip(invs, t2)]
    return [d32(inv, r.astype(bf16)) for inv, r in zip(invs, rhss)]


def _segment_cumsum(tri16, g):
    g1 = g.astype(bf16)
    r1 = g - g1.astype(f32)
    g2 = r1.astype(bf16)
    g3 = (r1 - g2.astype(f32)).astype(bf16)
    s = jnp.dot(tri16, jnp.concatenate([g1, g2, g3], axis=1), preferred_element_type=f32)
    n = g.shape[1]
    return s[:, :n] + (s[:, n:2 * n] + s[:, 2 * n:])


def _delta_body(x_ref, w1_ref, w2_ref, nw_ref, cw_ref, alog_ref, dtb_ref, dnw_ref, m_ref, conv0_ref, s0_ref,
                odn_ref, us5_ref, zs5_ref, convout_ref, s_ref,
                cbuf, qn_scr, kn_scr, v_scr, gc_scr, beta_scr, zdn_scr, *, nb, tb, n_t, pipelined):
    rows = nb * tb
    n_groups = rows // GROUP
    segs = GROUP // tb
    n_levels = tb.bit_length() - 1
    step = pl.program_id(0)
    fresh = step < n_t

    @pl.when(step == 0)
    def _init():
        cbuf[:, 0:SUBLANES, :] = conv0_ref[...]
        s_ref[...] = s0_ref[...]
        if pipelined:
            for scr in (qn_scr, kn_scr, v_scr, gc_scr, beta_scr, zdn_scr):
                scr[...] = jnp.zeros_like(scr)

    front = {}

    def project(w_ref, lo, width):
        return jnp.dot(front["h"], w_ref[:, lo:lo + width], preferred_element_type=f32)

    def a_norm_in():
        x = x_ref[...].reshape(rows, D_MODEL)
        ms = jnp.mean(x * x, axis=-1, keepdims=True)
        front["h"] = (x * lax.rsqrt(ms + EPS) * nw_ref[...]).astype(bf16)

    def a_project_qkv(s):
        front[("raw", s)] = project(w1_ref, s * DN_WIDTH, DN_WIDTH)

    def a_project_us5():
        u = project(w2_ref, W2_US5, S5_WIDTH)
        for j in range(S5_PLANES):
            us5_ref[:, j] = u[:, j * LANES:(j + 1) * LANES].reshape(nb, tb, LANES)

    def a_project_zs5():
        zs5_ref[...] = project(w2_ref, W2_ZS5, S5_WIDTH).reshape(nb, tb, S5_WIDTH)

    def a_project_zdn():
        front["zdn"] = project(w2_ref, W2_ZDN, DN_WIDTH)

    def a_project_gates():
        ab = project(w2_ref, 0, LANES)
        front["g"] = -jnp.exp(alog_ref[...]) * _softplus(ab + dtb_ref[...])
        front["beta"] = _sigmoid(ab)

    projections = ([functools.partial(a_project_qkv, s) for s in range(3)]
                   + [a_project_us5, a_project_zs5, a_project_zdn, a_project_gates])

    def a_conv(b, s, zero):
        cols = slice(s * DN_WIDTH, (s + 1) * DN_WIDTH)
        cbuf[b, SUBLANES:SUBLANES + tb, cols] = front[("raw", s)][b * tb:(b + 1) * tb, :]
        taps = [cw_ref[j:j + 1, cols] for j in range(CONV_K)]
        if zero is not None:
            zero_w = jnp.concatenate([zero] * (DN_WIDTH // LANES), axis=1)
            taps = [t + zero_w for t in taps]
        acc = cbuf[b, 5:5 + tb, cols] * taps[0]
        for j in range(1, CONV_K):
            acc = acc + cbuf[b, 5 + j:5 + j + tb, cols] * taps[j]
        front[("act", b, s)] = _silu(acc)
        tail = cbuf[b, tb:tb + SUBLANES, cols]
        if pipelined:
            tail = jnp.where(fresh, tail, cbuf[b, 0:SUBLANES, cols])
        cbuf[b, 0:SUBLANES, cols] = tail
        convout_ref[b, :, cols] = tail

    def a_norm(b, hd, zero):
        lo = hd * DN_D
        q = front[("act", b, 0)][:, lo:lo + DN_D]
        k = front[("act", b, 1)][:, lo:lo + DN_D]
        if zero is not None:
            q, k = q + zero, k + zero
        front[("qn", b, hd)] = (q * lax.rsqrt(jnp.sum(q * q, axis=-1, keepdims=True) + L2_EPS)
                                * (DN_D ** -0.5))
        front[("kn", b, hd)] = k * lax.rsqrt(jnp.sum(k * k, axis=-1, keepdims=True) + L2_EPS)

    def a_gates(zero):
        tri16 = m_ref[MASK_INCL].astype(bf16)
        g = front["g"] if zero is None else front["g"] + zero
        front["gc"] = [_segment_cumsum(tri16, g[gi * GROUP:(gi + 1) * GROUP])
                       for gi in range(n_groups)]

    def a_store():
        for b in range(nb):
            r = slice(b * tb, (b + 1) * tb)
            for hd in range(DN_HEADS):
                c = slice(hd * DN_D, (hd + 1) * DN_D)
                qn_scr[r, c] = front[("qn", b, hd)]
                kn_scr[r, c] = front[("kn", b, hd)]
            v_scr[r, :] = front[("act", b, 2)]
        for gi in range(n_groups):
            gc_scr[gi * GROUP:(gi + 1) * GROUP, :] = front["gc"][gi]
        beta_scr[...] = front["beta"]
        zdn_scr[...] = front["zdn"]

    fillers = ([functools.partial(a_conv, b, s) for b in range(nb) for s in range(3)] + [a_gates]
               + [functools.partial(a_norm, b, hd) for b in range(nb) for hd in range(DN_HEADS)])

    incl = m_ref[MASK_INCL]
    strict = m_ref[MASK_STRICT]
    eye = incl - strict
    dnw = dnw_ref[...]
    chains = [(gi, hd) for gi in range(n_groups) for hd in range(DN_HEADS)]
    back = {}

    def stream_of(gi, sg):
        return (gi * GROUP) // tb + sg

    for name in ("q16", "qe16", "k16", "kb16", "rhs", "dmat", "kdec", "sdecay", "nmat", "attn"):
        back[name] = [None] * len(chains)

    def b_load(ci):
        gi, hd = chains[ci]
        r = slice(gi * GROUP, (gi + 1) * GROUP)
        c = slice(hd * DN_D, (hd + 1) * DN_D)
        q, k, v = qn_scr[r, c], kn_scr[r, c], v_scr[r, c]
        beta = beta_scr[r, AB_LANE + DN_HEADS + hd:AB_LANE + DN_HEADS + hd + 1]
        g = gc_scr[r, AB_LANE + hd:AB_LANE + hd + 1]
        eg = jnp.exp(g)
        kb = k * beta
        gcb = jnp.broadcast_to(g, (GROUP, GROUP))
        back["dmat"][ci] = jnp.exp((gcb - gcb.T) * incl) * incl
        back["q16"][ci] = q.astype(bf16)
        back["qe16"][ci] = (q * eg).astype(bf16)
        back["k16"][ci] = k.astype(bf16)
        back["kb16"][ci] = kb.astype(bf16)
        back["rhs"][ci] = jnp.concatenate([v * beta, kb * eg], axis=1)
        kd, sd = [], []
        for sg in range(segs):
            a0 = sg * tb
            glast = g[a0 + tb - 1:a0 + tb, :]
            kd.append((k[a0:a0 + tb] * jnp.exp(glast - g[a0:a0 + tb])).astype(bf16))
            sd.append(jnp.exp(glast))
        back["kdec"][ci] = kd
        back["sdecay"][ci] = sd

    def b_mats(ci):
        d = functools.partial(lax.dot_general, dimension_numbers=_NT, preferred_element_type=f32)
        dm = back["dmat"][ci]
        back["nmat"][ci] = d(back["kb16"][ci], back["k16"][ci]) * (dm * strict)
        back["attn"][ci] = (d(back["q16"][ci], back["k16"][ci]) * dm).astype(bf16)

    def b_state(sols):
        s_old = {(b, hd): s_ref[b, hd] for b in range(nb) for hd in range(DN_HEADS)}
        s16 = {key: val.astype(bf16) for key, val in s_old.items()}
        v_news, o_states = [], []
        for (gi, hd), sol, qe16 in zip(chains, sols, back["qe16"]):
            u = sol[:, :DN_D]
            w16 = sol[:, DN_D:].astype(bf16)
            vn, os_ = [], []
            for sg in range(segs):
                a0 = sg * tb
                st = s16[(stream_of(gi, sg), hd)]
                vn.append(u[a0:a0 + tb] - jnp.dot(w16[a0:a0 + tb], st, preferred_element_type=f32))
                os_.append(jnp.dot(qe16[a0:a0 + tb], st, preferred_element_type=f32))
            v_news.append(vn[0] if segs == 1 else jnp.concatenate(vn, axis=0))
            o_states.append(os_[0] if segs == 1 else jnp.concatenate(os_, axis=0))
        vn16s = [vn.astype(bf16) for vn in v_news]
        outs = [os_ + jnp.dot(a, vn, preferred_element_type=f32)
                for os_, a, vn in zip(o_states, back["attn"], vn16s)]

        s_new = {}
        for (gi, hd), kd, sd, vn in zip(chains, back["kdec"], back["sdecay"], vn16s):
            for sg in range(segs):
                a0 = sg * tb
                key = (stream_of(gi, sg), hd)
                s_new[key] = (s_old[key] * sd[sg]
                              + lax.dot_general(kd[sg], vn[a0:a0 + tb], _TN, preferred_element_type=f32))

        o_rows = []
        for gi in range(n_groups):
            o_heads = []
            for hd in range(DN_HEADS):
                o = outs[gi * DN_HEADS + hd]
                on = o * lax.rsqrt(jnp.mean(o * o, axis=-1, keepdims=True) + EPS) * dnw
                zd = zdn_scr[gi * GROUP:(gi + 1) * GROUP, hd * DN_D:(hd + 1) * DN_D]
                o_heads.append(on * _silu(zd))
            o_rows.append(jnp.concatenate(o_heads, axis=1))
        o_all = o_rows[0] if n_groups == 1 else jnp.concatenate(o_rows, axis=0)
        odn_ref[...] = o_all.astype(bf16).reshape(nb, tb, DN_WIDTH)
        s_ref[...] = jnp.stack([jnp.stack([s_new[(b, hd)] for hd in range(DN_HEADS)]) for b in range(nb)])

    n_ch = len(chains)
    if pipelined:
        a_norm_in()
        back_setup = ([functools.partial(b_load, ci) for ci in range(n_ch)]
                      + [functools.partial(b_mats, ci) for ci in range(n_ch)])
        per_piece = -(-len(back_setup) // len(projections))
        for piece in projections:
            piece()
            for thunk in back_setup[:per_piece]:
                thunk()
            del back_setup[:per_piece]
        sols = _unit_lower_solve(back["nmat"], back["rhs"], eye, m_ref, n_levels, fillers)
        for fill in fillers:
            fill(None)
        b_state(sols)
        a_store()
    else:
        a_norm_in()
        for piece in projections:
            piece()
        for fill in fillers:
            fill(None)
        a_store()
        for ci in range(n_ch):
            b_load(ci)
        for ci in range(n_ch):
            b_mats(ci)
        b_state(_unit_lower_solve(back["nmat"], back["rhs"], eye, m_ref, n_levels, []))


def _delta_masks(tb):
    r = np.arange(GROUP)[:, None]
    c = np.arange(GROUP)[None, :]
    same = (r // tb) == (c // tb)
    ms = [same & (r >= c), same & (r > c)]
    for l in range(tb.bit_length() - 1):
        ms.append(((r >> (l + 1)) == (c >> (l + 1))) & (((r >> l) & 1) == 1) & (((c >> l) & 1) == 0))
    return jnp.asarray(np.stack(ms).astype(np.float32))


def _delta_call(x, w1, w2, norm_w, conv_w, alog, dtb, dnw, conv0, s0, *, tb, pipelined):
    nb, T, _ = x.shape
    assert GROUP % tb == 0 and (nb * tb) % GROUP == 0 and T % tb == 0
    n_t = T // tb
    rows = nb * tb
    if pipelined:
        grid = (n_t + 1,)
        cur = lambda i: jnp.minimum(i, n_t - 1)
        prev = lambda i: jnp.maximum(i - 1, 0)
    else:
        grid = (n_t,)
        cur = prev = lambda i: i
    const2 = lambda i: (0, 0)
    const3 = lambda i: (0, 0, 0)
    const4 = lambda i: (0, 0, 0, 0)
    body = functools.partial(_delta_body, nb=nb, tb=tb, n_t=n_t, pipelined=pipelined)
    masks = _delta_masks(tb)
    one = pl.Buffered(1)
    return pl.pallas_call(
        body,
        grid=grid,
        in_specs=[
            pl.BlockSpec((nb, tb, D_MODEL), lambda i: (0, cur(i), 0)),
            pl.BlockSpec((D_MODEL, CONV_CH), const2, pipeline_mode=one),
            pl.BlockSpec((D_MODEL, W2_COLS), const2, pipeline_mode=one),
            pl.BlockSpec((1, D_MODEL), const2),
            pl.BlockSpec((CONV_K, CONV_CH), const2),
            pl.BlockSpec((1, LANES), const2),
            pl.BlockSpec((1, LANES), const2),
            pl.BlockSpec((1, DN_D), const2),
            pl.BlockSpec(masks.shape, const3, pipeline_mode=one),
            pl.BlockSpec((nb, SUBLANES, CONV_CH), const3),
            pl.BlockSpec((nb, DN_HEADS, DN_D, DN_D), const4),
        ],
        out_specs=[
            pl.BlockSpec((nb, tb, DN_WIDTH), lambda i: (0, prev(i), 0)),
            pl.BlockSpec((nb, S5_PLANES, tb, LANES), lambda i: (0, 0, cur(i), 0)),
            pl.BlockSpec((nb, tb, S5_WIDTH), lambda i: (0, cur(i), 0)),
            pl.BlockSpec((nb, SUBLANES, CONV_CH), const3),
            pl.BlockSpec((nb, DN_HEADS, DN_D, DN_D), const4),
        ],
        out_shape=[
            jax.ShapeDtypeStruct((nb, T, DN_WIDTH), bf16),
            jax.ShapeDtypeStruct((nb, S5_PLANES, T, LANES), f32),
            jax.ShapeDtypeStruct((nb, T, S5_WIDTH), f32),
            jax.ShapeDtypeStruct((nb, SUBLANES, CONV_CH), f32),
            jax.ShapeDtypeStruct((nb, DN_HEADS, DN_D, DN_D), f32),
        ],
        scratch_shapes=[
            pltpu.VMEM((nb, tb + SUBLANES, CONV_CH), f32),
            pltpu.VMEM((rows, DN_WIDTH), f32),
            pltpu.VMEM((rows, DN_WIDTH), f32),
            pltpu.VMEM((rows, DN_WIDTH), f32),
            pltpu.VMEM((rows, LANES), f32),
            pltpu.VMEM((rows, LANES), f32),
            pltpu.VMEM((rows, DN_WIDTH), f32),
        ],
        compiler_params=pltpu.CompilerParams(
            dimension_semantics=("arbitrary",), vmem_limit_bytes=VMEM_LIMIT),
        name="delta",
    )(x, w1, w2, norm_w, conv_w, alog, dtb, dnw, masks, conv0, s0)


S5MAT_GROUPS_PER_STEP = 8


def _split16(a):
    hi = a.astype(bf16)
    return hi, (a - hi.astype(f32)).astype(bf16)


def _s5mat_body(cp_ref, bp_ref, t_ref):
    ri = lax.broadcasted_iota(jnp.int32, (S5_GL, S5_GL), 0) >> 4
    ci = lax.broadcasted_iota(jnp.int32, (S5_GL, S5_GL), 1) >> 4
    causal = ri >= ci
    d = functools.partial(lax.dot_general, dimension_numbers=_NT, preferred_element_type=f32)
    for g in range(S5MAT_GROUPS_PER_STEP):
        ch, cl = _split16(cp_ref[g])
        bh, bl = _split16(bp_ref[g])
        t = d(ch, bh) + (d(ch, bl) + d(cl, bh))
        t_ref[g] = jnp.where(causal, t, 0.0).astype(bf16)


def _s5mat_call(cp, bp):
    n = S5MAT_GROUPS_PER_STEP
    return pl.pallas_call(
        _s5mat_body,
        grid=(S5_GROUPS // n,),
        in_specs=[
            pl.BlockSpec((n, S5_GL, 2 * S5_STATE), lambda g: (g, 0, 0)),
            pl.BlockSpec((n, S5_GL, 2 * S5_STATE), lambda g: (g, 0, 0)),
        ],
        out_specs=pl.BlockSpec((n, S5_GL, S5_GL), lambda g: (g, 0, 0)),
        out_shape=jax.ShapeDtypeStruct((S5_GROUPS, S5_GL, S5_GL), bf16),
        compiler_params=pltpu.CompilerParams(dimension_semantics=("arbitrary",)),
        name="s5mat",
    )(cp, bp)


def _s5_body(u_ref, z_ref, x_hbm, odn_hbm, xr0_ref, xi0_ref, t_ref, wp_ref, vt_ref, ar_ref, ai_ref,
             d_ref, gw_ref, gb_ref, wo_ref, fw_ref,
             y_hbm, xr_ref, xi_ref,
             at_scr, er_scr, ei_scr, xinr_scr, xini_scr, yt_scr, y_scr,
             xbuf, obuf, ybuf, sem_x, sem_o, sem_y, *, tb, n_streams, n_chunks, n_sub):
    R = tb // S5_L
    used = n_streams * n_chunks
    g_idx = pl.program_id(0)
    row_base = pl.program_id(1) * tb

    def in_copies(rb, slot):
        rows = pl.ds(row_base + rb * OUT_SUB, OUT_SUB)
        return (pltpu.make_async_copy(x_hbm.at[g_idx, rows, :], xbuf.at[slot], sem_x.at[slot]),
                pltpu.make_async_copy(odn_hbm.at[g_idx, rows, :], obuf.at[slot], sem_o.at[slot]))

    def out_copy(rb, slot):
        rows = pl.ds(row_base + rb * OUT_SUB, OUT_SUB)
        return pltpu.make_async_copy(ybuf.at[slot], y_hbm.at[g_idx, rows, :], sem_y.at[slot])

    for cp in in_copies(0, 0):
        cp.start()

    @pl.when(pl.program_id(1) == 0)
    def _init():
        xr_ref[...] = xr0_ref[...]
        xi_ref[...] = xi0_ref[...]

    for f in range(S5_L):
        for j in range(S5_PLANES):
            ut = u_ref[0, j, pl.ds(f, R, stride=S5_L), :]
            ut_t = ut.T.astype(bf16)
            for gl in range(LANES // S5_GROUP):
                g = j * (LANES // S5_GROUP) + gl
                at_scr[g, f * S5_GROUP:(f + 1) * S5_GROUP, :] = ut_t[gl * S5_GROUP:(gl + 1) * S5_GROUP, :]

    for p in range(S5_PAIRS):
        a_t = at_scr[2 * p:2 * p + 2].reshape(2 * S5_GL, R)
        e = lax.dot_general(a_t, wp_ref[p], _TN, preferred_element_type=f32)
        er_scr[:, p * LANES:(p + 1) * LANES] = e[:, :LANES]
        ei_scr[:, p * LANES:(p + 1) * LANES] = e[:, LANES:]

    if used < R:
        xinr_scr[...] = jnp.zeros_like(xinr_scr)
        xini_scr[...] = jnp.zeros_like(xini_scr)

    a_r = ar_ref[...]
    a_i = ai_ref[...]
    for s in range(n_streams):
        def step(c, carry):
            xr, xi = carry
            row = s * n_chunks + c
            xinr_scr[pl.ds(row, 1), :] = xr
            xini_scr[pl.ds(row, 1), :] = xi
            er = er_scr[pl.ds(row, 1), :]
            ei = ei_scr[pl.ds(row, 1), :]
            return a_r * xr - a_i * xi + er, a_r * xi + a_i * xr + ei

        xr, xi = lax.fori_loop(0, n_chunks, step, (xr_ref[0, s:s + 1, :], xi_ref[0, s:s + 1, :]))
        xr_ref[0, s:s + 1, :] = xr
        xi_ref[0, s:s + 1, :] = xi

    for g in range(S5_GROUPS):
        p = g // 2
        xin = jnp.concatenate([xinr_scr[:, p * LANES:(p + 1) * LANES],
                               xini_scr[:, p * LANES:(p + 1) * LANES]], axis=1).astype(bf16)
        yt_scr[g] = (jnp.dot(t_ref[g], at_scr[g], preferred_element_type=f32)
                     + lax.dot_general(vt_ref[g], xin, _NT, preferred_element_type=f32))

    for f in range(S5_L):
        gpp = LANES // S5_GROUP
        for j in range(S5_PLANES):
            yt = yt_scr[j * gpp:(j + 1) * gpp, f * S5_GROUP:(f + 1) * S5_GROUP, :].reshape(LANES, R)
            y_scr[j, pl.ds(f, R, stride=S5_L), :] = yt.T

    wo_dn = wo_ref[0:DN_WIDTH, :]
    wo_s5 = wo_ref[DN_WIDTH:, :]
    for rb in range(n_sub):
        slot = rb % 2
        if rb + 1 < n_sub:
            for cp in in_copies(rb + 1, 1 - slot):
                cp.start()
        for cp in in_copies(rb, slot):
            cp.wait()
        sl = pl.ds(rb * OUT_SUB, OUT_SUB)
        y_intra = jnp.concatenate([y_scr[j, sl, :] for j in range(S5_PLANES)], axis=1)
        u_nat = jnp.concatenate([u_ref[0, j, sl, :] for j in range(S5_PLANES)], axis=1)
        y = y_intra + d_ref[...] * u_nat
        gy = _gelu_tanh(y)
        gate = _sigmoid(_bdot(gy, gw_ref[...]) + gb_ref[...])
        o_s5 = (gy * gate * _silu(z_ref[0, sl, :])).astype(bf16)
        acc = (xbuf[slot]
               + jnp.dot(obuf[slot], wo_dn, preferred_element_type=f32)
               + jnp.dot(o_s5, wo_s5, preferred_element_type=f32))
        ms = jnp.mean(acc * acc, axis=-1, keepdims=True)
        if rb >= 2:
            out_copy(rb - 2, slot).wait()
        ybuf[slot] = acc * lax.rsqrt(ms + EPS) * fw_ref[...]
        out_copy(rb, slot).start()
    for rb in range(max(0, n_sub - 2), n_sub):
        out_copy(rb, rb % 2).wait()


def _s5_call(u, z, x, odn, xr0, xi0, tmat, wp, vt, ar, ai, dvec, gw, gb, w_out, fw,
             *, tb, n_streams, n_chunks):
    G, T, _ = z.shape
    t_valid = x.shape[1]
    assert x.shape[0] == G and (t_valid % tb == 0 or (t_valid < tb and T == tb))
    n_sub = min(tb, t_valid) // OUT_SUB
    grid = (G, T // tb)
    R = tb // S5_L
    nstate = S5_GROUPS * S5_STATE
    c2 = lambda g, t: (0, 0)
    c3 = lambda g, t: (0, 0, 0)
    body = functools.partial(_s5_body, tb=tb, n_streams=n_streams, n_chunks=n_chunks, n_sub=n_sub)
    one = pl.Buffered(1)
    hbm = pl.BlockSpec(memory_space=pl.ANY)
    return pl.pallas_call(
        body,
        grid=grid,
        in_specs=[
            pl.BlockSpec((1, S5_PLANES, tb, LANES), lambda g, t: (g, 0, t, 0)),
            pl.BlockSpec((1, tb, S5_WIDTH), lambda g, t: (g, t, 0)),
            hbm,
            hbm,
            pl.BlockSpec((1, n_streams, nstate), lambda g, t: (g, 0, 0)),
            pl.BlockSpec((1, n_streams, nstate), lambda g, t: (g, 0, 0)),
            pl.BlockSpec((S5_GROUPS, S5_GL, S5_GL), c3, pipeline_mode=one),
            pl.BlockSpec((S5_PAIRS, 2 * S5_GL, 2 * LANES), c3, pipeline_mode=one),
            pl.BlockSpec((S5_GROUPS, S5_GL, 2 * LANES), c3, pipeline_mode=one),
            pl.BlockSpec((1, nstate), c2),
            pl.BlockSpec((1, nstate), c2),
            pl.BlockSpec((1, S5_WIDTH), c2),
            pl.BlockSpec((S5_WIDTH, S5_WIDTH), c2),
            pl.BlockSpec((1, S5_WIDTH), c2),
            pl.BlockSpec((D_MODEL, D_MODEL), c2, pipeline_mode=one),
            pl.BlockSpec((1, D_MODEL), c2),
        ],
        out_specs=[
            hbm,
            pl.BlockSpec((1, n_streams, nstate), lambda g, t: (g, 0, 0)),
            pl.BlockSpec((1, n_streams, nstate), lambda g, t: (g, 0, 0)),
        ],
        out_shape=[
            jax.ShapeDtypeStruct((G, t_valid, D_MODEL), f32),
            jax.ShapeDtypeStruct((G, n_streams, nstate), f32),
            jax.ShapeDtypeStruct((G, n_streams, nstate), f32),
        ],
        scratch_shapes=[
            pltpu.VMEM((S5_GROUPS, S5_GL, R), bf16),
            pltpu.VMEM((R, nstate), f32),
            pltpu.VMEM((R, nstate), f32),
            pltpu.VMEM((R, nstate), f32),
            pltpu.VMEM((R, nstate), f32),
            pltpu.VMEM((S5_GROUPS, S5_GL, R), f32),
            pltpu.VMEM((S5_PLANES, tb, LANES), f32),
            pltpu.VMEM((2, OUT_SUB, D_MODEL), f32),
            pltpu.VMEM((2, OUT_SUB, DN_WIDTH), bf16),
            pltpu.VMEM((2, OUT_SUB, D_MODEL), f32),
            pltpu.SemaphoreType.DMA((2,)),
            pltpu.SemaphoreType.DMA((2,)),
            pltpu.SemaphoreType.DMA((2,)),
        ],
        compiler_params=pltpu.CompilerParams(
            dimension_semantics=("arbitrary", "arbitrary"), vmem_limit_bytes=VMEM_LIMIT),
        name="s5",
    )(u, z, x, odn, xr0, xi0, tmat, wp, vt, ar, ai, dvec, gw, gb, w_out, fw)


def _s5_operands(a_re, a_im, log_dt, b_re, b_im, c_re, c_im):
    lam_re = jnp.minimum(a_re, -1e-4)
    lam_im = a_im
    dt = jnp.exp(log_dt)[:, None]
    ldt_re, ldt_im = lam_re * dt, lam_im * dt

    def lpow(k):
        mag = jnp.exp(ldt_re * k)
        return mag * jnp.cos(ldt_im * k), mag * jnp.sin(ldt_im * k)

    lb_re, lb_im = lpow(1.0)
    den = lam_re * lam_re + lam_im * lam_im
    f_re = ((lb_re - 1.0) * lam_re + lb_im * lam_im) / den
    f_im = (lb_im * lam_re - (lb_re - 1.0) * lam_im) / den
    bb_re = f_re[..., None] * b_re - f_im[..., None] * b_im
    bb_im = f_re[..., None] * b_im + f_im[..., None] * b_re

    fr = jnp.arange(S5_L, dtype=f32)
    G, N = S5_GROUPS, S5_STATE

    def cmul(ar, ai, br, bi):
        return ar * br - ai * bi, ar * bi + ai * br

    pr, pi = lpow(fr[:, None, None])
    pr, pi = pr.transpose(1, 0, 2), pi.transpose(1, 0, 2)
    cpr, cpi = cmul(c_re[:, None], c_im[:, None], pr[:, :, None], pi[:, :, None])
    cpr, cpi = cpr.reshape(G, S5_GL, N), cpi.reshape(G, S5_GL, N)
    bbr_t, bbi_t = bb_re.transpose(0, 2, 1), bb_im.transpose(0, 2, 1)
    qr, qi = lpow(-fr[:, None, None])
    qr, qi = qr.transpose(1, 0, 2), qi.transpose(1, 0, 2)
    bpr, bpi = cmul(qr[:, :, None], qi[:, :, None], bbr_t[:, None], bbi_t[:, None])
    bpr, bpi = bpr.reshape(G, S5_GL, N), bpi.reshape(G, S5_GL, N)

    wr_, wi_ = lpow((S5_L - 1.0) - fr[:, None, None])
    wr_, wi_ = wr_.transpose(1, 0, 2), wi_.transpose(1, 0, 2)
    w_re, w_im = cmul(wr_[:, :, None], wi_[:, :, None], bbr_t[:, None], bbi_t[:, None])
    w_re, w_im = w_re.reshape(G, S5_GL, N), w_im.reshape(G, S5_GL, N)
    vr_, vi_ = lpow(fr[:, None, None] + 1.0)
    vr_, vi_ = vr_.transpose(1, 0, 2), vi_.transpose(1, 0, 2)
    v_re, v_im = cmul(c_re[:, None], c_im[:, None], vr_[:, :, None], vi_[:, :, None])
    v_re, v_im = v_re.reshape(G, S5_GL, N), v_im.reshape(G, S5_GL, N)

    zeros = jnp.zeros((S5_PAIRS, S5_GL, N), f32)

    def pair_cols(re, im):
        re, im = re.reshape(S5_PAIRS, 2, S5_GL, N), im.reshape(S5_PAIRS, 2, S5_GL, N)
        even = jnp.concatenate([re[:, 0], zeros, im[:, 0], zeros], axis=-1)
        odd = jnp.concatenate([zeros, re[:, 1], zeros, im[:, 1]], axis=-1)
        return jnp.stack([even, odd], axis=1)

    wp = pair_cols(w_re, w_im).reshape(S5_PAIRS, 2 * S5_GL, 2 * LANES).astype(bf16)
    vt = pair_cols(v_re, -v_im).reshape(S5_GROUPS, S5_GL, 2 * LANES).astype(bf16)
    a16r, a16i = lpow(float(S5_L))
    cp = jnp.concatenate([cpr, cpi], axis=2)
    bp = jnp.concatenate([bpr, -bpi], axis=2)
    return cp, bp, wp, vt, a16r.reshape(1, G * N), a16i.reshape(1, G * N)


def _pad_lanes(v, start=0):
    out = jnp.zeros((1, LANES), f32)
    return out.at[0, start:start + v.shape[0]].set(v.astype(f32))


def _layer(x, conv0, s0, xr0, xi0, prm, *, delta_cfg, s5_cfg):
    B, T, _ = x.shape
    conv0p = jnp.concatenate([jnp.zeros((B, SUBLANES - (CONV_K - 1), CONV_CH), f32), conv0], axis=1)
    odn, us5, zs5, convout, s_new = _delta_call(
        x, prm["w1"], prm["w2"], prm["norm_w"], prm["conv_w"], prm["alog"], prm["dtb"], prm["dnw"],
        conv0p, s0, **delta_cfg)

    nstate = S5_GROUPS * S5_STATE
    if s5_cfg["flatten"]:
        tb = s5_cfg["tb"]
        uf = us5.transpose(1, 0, 2, 3).reshape(1, S5_PLANES, B * T, LANES)
        zf = zs5.reshape(1, B * T, S5_WIDTH)
        pad = tb - B * T
        uf = jnp.pad(uf, ((0, 0), (0, 0), (0, pad), (0, 0)))
        zf = jnp.pad(zf, ((0, 0), (0, pad), (0, 0)))
        y, xr, xi = _s5_call(uf, zf, x.reshape(1, B * T, D_MODEL), odn.reshape(1, B * T, DN_WIDTH),
                             xr0.reshape(1, B, nstate), xi0.reshape(1, B, nstate),
                             prm["tmat"], prm["wp"], prm["vt"], prm["ar"], prm["ai"], prm["dvec"],
                             prm["gw"], prm["gb"], prm["w_out"], prm["fw"],
                             tb=tb, n_streams=B, n_chunks=T // S5_L)
    else:
        tb = s5_cfg["tb"]
        y, xr, xi = _s5_call(us5, zs5, x, odn, xr0.reshape(B, 1, nstate), xi0.reshape(B, 1, nstate),
                             prm["tmat"], prm["wp"], prm["vt"], prm["ar"], prm["ai"], prm["dvec"],
                             prm["gw"], prm["gb"], prm["w_out"], prm["fw"],
                             tb=tb, n_streams=1, n_chunks=tb // S5_L)

    return (y.reshape(B, T, D_MODEL), convout[:, SUBLANES - (CONV_K - 1):, :], s_new,
            xr.reshape(B, S5_GROUPS, S5_STATE), xi.reshape(B, S5_GROUPS, S5_STATE))


def kernel(x_prompt, x_sample, cache_conv, state_dn, state_s5_re, state_s5_im, norm_w, w_in, conv_w, dn_A_log, dn_dt_bias, dn_norm_w, s5_A_re, s5_A_im, s5_log_dt, s5_B_re, s5_B_im, s5_C_re, s5_C_im, s5_D, glu_w, glu_b, w_out, final_norm_w):
    depth = norm_w.shape[0]
    assert depth == 1
    l = 0
    assert w_in.shape[-1] == IN_COLS
    w1 = w_in[l, :, :CONV_CH].astype(bf16)
    w2 = w_in[l, :, IN_COLS - W2_COLS:].astype(bf16)
    cp, bp, wp, vt, ar, ai = _s5_operands(
        s5_A_re[l].astype(f32), s5_A_im[l].astype(f32), s5_log_dt[l].astype(f32),
        s5_B_re[l].astype(f32), s5_B_im[l].astype(f32), s5_C_re[l].astype(f32), s5_C_im[l].astype(f32))
    prm = dict(
        w1=w1, w2=w2,
        norm_w=norm_w[l].reshape(1, D_MODEL).astype(f32),
        conv_w=conv_w[l].astype(f32),
        alog=_pad_lanes(dn_A_log[l], AB_LANE),
        dtb=_pad_lanes(dn_dt_bias[l], AB_LANE),
        dnw=dn_norm_w[l].reshape(1, DN_D).astype(f32),
        tmat=_s5mat_call(cp, bp),
        wp=wp, vt=vt, ar=ar, ai=ai,
        dvec=s5_D[l].reshape(1, S5_WIDTH).astype(f32),
        gw=glu_w[l].astype(bf16),
        gb=glu_b[l].reshape(1, S5_WIDTH).astype(f32),
        w_out=w_out[l].astype(bf16),
        fw=final_norm_w.reshape(1, D_MODEL).astype(f32),
    )

    bp = x_prompt.shape[0]
    yp, c1, d1, r1, i1 = _layer(
        x_prompt,
        jnp.zeros((bp, CONV_K - 1, CONV_CH), f32),
        jnp.zeros((bp, DN_HEADS, DN_D, DN_D), f32),
        jnp.zeros((bp, S5_GROUPS, S5_STATE), f32),
        jnp.zeros((bp, S5_GROUPS, S5_STATE), f32),
        prm,
        delta_cfg=dict(tb=256, pipelined=True),
        s5_cfg=dict(flatten=False, tb=2048))
    ys, c2, d2, r2, i2 = _layer(
        x_sample, cache_conv[l].astype(f32), state_dn[l].astype(f32),
        state_s5_re[l].astype(f32), state_s5_im[l].astype(f32),
        prm,
        delta_cfg=dict(tb=32, pipelined=False),
        s5_cfg=dict(flatten=True, tb=2048))

    return (yp, ys, c1[None], d1[None], r1[None], i1[None], c2[None], d2[None], r2[None], i2[None])
```

```python
import functools
import math

import jax
import jax.numpy as jnp
import numpy as np
from jax import lax
from jax.experimental import pallas as pl
from jax.experimental.pallas import tpu as pltpu

bf16 = jnp.bfloat16
f32 = jnp.float32

LANES = 128
SUBLANES = 8

D_MODEL = 1024
DN_HEADS = 4
DN_D = 128
DN_WIDTH = DN_HEADS * DN_D
CONV_K = 4
CONV_CH = 3 * DN_WIDTH
S5_WIDTH = 512
S5_GROUP = 16
S5_GROUPS = 32
S5_STATE = 64
S5_PAIRS = S5_GROUPS // 2
S5_L = 16
S5_GL = S5_GROUP * S5_L
OUT_SUB = 256
S5_PLANES = S5_WIDTH // LANES
EPS = 1e-6
L2_EPS = 1e-6

IN_COLS = CONV_CH + 2 * DN_HEADS + DN_WIDTH + 2 * S5_WIDTH
W2_COLS = LANES + DN_WIDTH + 2 * S5_WIDTH
AB_LANE = LANES - 2 * DN_HEADS
W2_ZDN = LANES
W2_US5 = W2_ZDN + DN_WIDTH
W2_ZS5 = W2_US5 + S5_WIDTH

GROUP = 256
MASK_INCL, MASK_STRICT, MASK_OFF = 0, 1, 2
VMEM_LIMIT =56 * 1024 * 1024

_NT = (((1,), (1,)), ((), ()))
_TN = (((0,), (0,)), ((), ()))


def _bdot(a, b):
    return jnp.dot(a.astype(bf16), b.astype(bf16), preferred_element_type=f32)


def _bdot_g(a, b, dims):
    return lax.dot_general(a.astype(bf16), b.astype(bf16), dims, preferred_element_type=f32)


def _sigmoid(x):
    return 1.0 / (1.0 + jnp.exp(-x))


def _silu(x):
    return x * _sigmoid(x)


def _softplus(x):
    return jnp.maximum(x, 0.0) + jnp.log1p(jnp.exp(-jnp.abs(x)))


def _gelu_tanh(x):
    c = math.sqrt(2.0 / math.pi)
    return 0.5 * x * (1.0 + jnp.tanh(c * (x + 0.044715 * (x * x * x))))


def _zero_after(x):
    bits = pltpu.bitcast(x[0:SUBLANES, 0:LANES].astype(f32), jnp.uint32)
    half = jnp.uint32(16)
    bits = lax.shift_right_logical(lax.shift_right_logical(bits, half), half)
    return pltpu.bitcast(bits, f32)[0:1, :]


def _unit_lower_solve(nmats, rhss, eye, m_ref, n_levels, fillers):
    d32 = functools.partial(jnp.dot, preferred_element_type=f32)

    def d16(a, b):
        return d32(a, b).astype(bf16)

    def fill(after):
        if fillers:
            fillers.pop(0)(_zero_after(after[-1]))

    n16 = [n.astype(bf16) for n in nmats]
    invs = [(eye - n * m_ref[MASK_OFF]).astype(bf16) for n in nmats]
    for lvl in range(1, n_levels):
        mask16 = m_ref[MASK_OFF + lvl].astype(bf16)
        t1 = [d16(inv, n * mask16) for inv, n in zip(invs, n16)]
        fill(invs)
        t2 = [d16(t, inv) for t, inv in zip(t1, invs)]
        fill(t1)
        invs = [inv - t for inv, t in zip(invs, t2)]
    return [d32(inv, r.astype(bf16)) for inv, r in zip(invs, rhss)]


def _segment_cumsum(tri16, g):
    g1 = g.astype(bf16)
    r1 = g - g1.astype(f32)
    g2 = r1.astype(bf16)
    g3 = (r1 - g2.astype(f32)).astype(bf16)
    s = jnp.dot(tri16, jnp.concatenate([g1, g2, g3], axis=1), preferred_element_type=f32)
    n = g.shape[1]
    return s[:, :n] + (s[:, n:2 * n] + s[:, 2 * n:])


def _delta_body(x_ref, w1_ref, w2_ref, nw_ref, cw_ref, alog_ref, dtb_ref, dnw_ref, m_ref, conv0_ref, s0_ref,
                odn_ref, us5_ref, zs5_ref, convout_ref, s_ref,
                cbuf, qn_scr, kn_scr, v_scr, gc_scr, beta_scr, zdn_scr, *, nb, tb, n_t, pipelined):
    rows = nb * tb
    n_groups = rows // GROUP
    segs = GROUP // tb
    n_levels = tb.bit_length() - 1
    step = pl.program_id(0)
    fresh = step < n_t

    @pl.when(step == 0)
    def _init():
        cbuf[:, 0:SUBLANES, :] = conv0_ref[...]
        s_ref[...] = s0_ref[...]
        if pipelined:
            for scr in (qn_scr, kn_scr, v_scr, gc_scr, beta_scr, zdn_scr):
                scr[...] = jnp.zeros_like(scr)

    front = {}

    def project(w_ref, lo, width):
        return jnp.dot(front["h"], w_ref[:, lo:lo + width], preferred_element_type=f32)

    def a_norm_in():
        x = x_ref[...].reshape(rows, D_MODEL)
        ms = jnp.mean(x * x, axis=-1, keepdims=True)
        front["h"] = (x * lax.rsqrt(ms + EPS) * nw_ref[...]).astype(bf16)

    def a_project_qkv(s):
        front[("raw", s)] = project(w1_ref, s * DN_WIDTH, DN_WIDTH)

    def a_project_us5():
        u = project(w2_ref, W2_US5, S5_WIDTH)
        for j in range(S5_PLANES):
            us5_ref[:, j] = u[:, j * LANES:(j + 1) * LANES].reshape(nb, tb, LANES)

    def a_project_zs5():
        zs5_ref[...] = project(w2_ref, W2_ZS5, S5_WIDTH).reshape(nb, tb, S5_WIDTH)

    def a_project_zdn():
        front["zdn"] = project(w2_ref, W2_ZDN, DN_WIDTH)

    def a_project_gates():
        ab = project(w2_ref, 0, LANES)
        front["g"] = -jnp.exp(alog_ref[...]) * _softplus(ab + dtb_ref[...])
        front["beta"] = _sigmoid(ab)

    projections = ([functools.partial(a_project_qkv, s) for s in range(3)]
                   + [a_project_us5, a_project_zs5, a_project_zdn, a_project_gates])

    def a_conv(b, s, zero):
        cols = slice(s * DN_WIDTH, (s + 1) * DN_WIDTH)
        cbuf[b, SUBLANES:SUBLANES + tb, cols] = front[("raw", s)][b * tb:(b + 1) * tb, :]
        taps = [cw_ref[j:j + 1, cols] for j in range(CONV_K)]
        if zero is not None:
            zero_w = jnp.concatenate([zero] * (DN_WIDTH // LANES), axis=1)
            taps = [t + zero_w for t in taps]
        acc = cbuf[b, 5:5 + tb, cols] * taps[0]
        for j in range(1, CONV_K):
            acc = acc + cbuf[b, 5 + j:5 + j + tb, cols] * taps[j]
        front[("act", b, s)] = _silu(acc)
        tail = cbuf[b, tb:tb + SUBLANES, cols]
        if pipelined:
            tail = jnp.where(fresh, tail, cbuf[b, 0:SUBLANES, cols])
        cbuf[b, 0:SUBLANES, cols] = tail
        convout_ref[b, :, cols] = tail

    def a_norm(b, hd, zero):
        lo = hd * DN_D
        q = front[("act", b, 0)][:, lo:lo + DN_D]
        k = front[("act", b, 1)][:, lo:lo + DN_D]
        if zero is not None:
            q, k = q + zero, k + zero
        front[("qn", b, hd)] = (q * lax.rsqrt(jnp.sum(q * q, axis=-1, keepdims=True) + L2_EPS)
                                * (DN_D ** -0.5))
        front[("kn", b, hd)] = k * lax.rsqrt(jnp.sum(k * k, axis=-1, keepdims=True) + L2_EPS)

    def a_gates(zero):
        tri16 = m_ref[MASK_INCL].astype(bf16)
        g = front["g"] if zero is None else front["g"] + zero
        front["gc"] = [_segment_cumsum(tri16, g[gi * GROUP:(gi + 1) * GROUP])
                       for gi in range(n_groups)]

    def a_store():
        for b in range(nb):
            r = slice(b * tb, (b + 1) * tb)
            for hd in range(DN_HEADS):
                c = slice(hd * DN_D, (hd + 1) * DN_D)
                qn_scr[r, c] = front[("qn", b, hd)]
                kn_scr[r, c] = front[("kn", b, hd)]
            v_scr[r, :] = front[("act", b, 2)]
        for gi in range(n_groups):
            gc_scr[gi * GROUP:(gi + 1) * GROUP, :] = front["gc"][gi]
        beta_scr[...] = front["beta"]
        zdn_scr[...] = front["zdn"]

    fillers = ([functools.partial(a_conv, b, s) for b in range(nb) for s in range(3)] + [a_gates]
               + [functools.partial(a_norm, b, hd) for b in range(nb) for hd in range(DN_HEADS)])

    incl = m_ref[MASK_INCL]
    strict = m_ref[MASK_STRICT]
    eye = incl - strict
    dnw = dnw_ref[...]
    chains = [(gi, hd) for gi in range(n_groups) for hd in range(DN_HEADS)]
    back = {}

    def stream_of(gi, sg):
        return (gi * GROUP) // tb + sg

    for name in ("q16", "qe16", "k16", "kb16", "rhs", "dmat", "kdec", "sdecay", "nmat", "attn"):
        back[name] = [None] * len(chains)

    def b_load(ci):
        gi, hd = chains[ci]
        r = slice(gi * GROUP, (gi + 1) * GROUP)
        c = slice(hd * DN_D, (hd + 1) * DN_D)
        q, k, v = qn_scr[r, c], kn_scr[r, c], v_scr[r, c]
        beta = beta_scr[r, AB_LANE + DN_HEADS + hd:AB_LANE + DN_HEADS + hd + 1]
        g = gc_scr[r, AB_LANE + hd:AB_LANE + hd + 1]
        eg = jnp.exp(g)
        kb = k * beta
        gcb = jnp.broadcast_to(g, (GROUP, GROUP))
        back["dmat"][ci] = jnp.exp((gcb - gcb.T) * incl) * incl
        back["q16"][ci] = q.astype(bf16)
        back["qe16"][ci] = (q * eg).astype(bf16)
        back["k16"][ci] = k.astype(bf16)
        back["kb16"][ci] = kb.astype(bf16)
        back["rhs"][ci] = jnp.concatenate([v * beta, kb * eg], axis=1)
        kd, sd = [], []
        for sg in range(segs):
            a0 = sg * tb
            glast = g[a0 + tb - 1:a0 + tb, :]
            kd.append((k[a0:a0 + tb] * jnp.exp(glast - g[a0:a0 + tb])).astype(bf16))
            sd.append(jnp.exp(glast))
        back["kdec"][ci] = kd
        back["sdecay"][ci] = sd

    def b_mats(ci):
        d = functools.partial(lax.dot_general, dimension_numbers=_NT, preferred_element_type=f32)
        dm = back["dmat"][ci]
        back["nmat"][ci] = d(back["kb16"][ci], back["k16"][ci]) * (dm * strict)
        back["attn"][ci] = (d(back["q16"][ci], back["k16"][ci]) * dm).astype(bf16)

    def b_state(sols):
        s_old = {(b, hd): s_ref[b, hd] for b in range(nb) for hd in range(DN_HEADS)}
        s16 = {key: val.astype(bf16) for key, val in s_old.items()}
        v_news, o_states = [], []
        for (gi, hd), sol, qe16 in zip(chains, sols, back["qe16"]):
            u = sol[:, :DN_D]
            w16 = sol[:, DN_D:].astype(bf16)
            vn, os_ = [], []
            for sg in range(segs):
                a0 = sg * tb
                st = s16[(stream_of(gi, sg), hd)]
                vn.append(u[a0:a0 + tb] - jnp.dot(w16[a0:a0 + tb], st, preferred_element_type=f32))
                os_.append(jnp.dot(qe16[a0:a0 + tb], st, preferred_element_type=f32))
            v_news.append(vn[0] if segs == 1 else jnp.concatenate(vn, axis=0))
            o_states.append(os_[0] if segs == 1 else jnp.concatenate(os_, axis=0))
        vn16s = [vn.astype(bf16) for vn in v_news]
        outs = [os_ + jnp.dot(a, vn, preferred_element_type=f32)
                for os_, a, vn in zip(o_states, back["attn"], vn16s)]

        s_new = {}
        for (gi, hd), kd, sd, vn in zip(chains, back["kdec"], back["sdecay"], vn16s):
            for sg in range(segs):
                a0 = sg * tb
                key = (stream_of(gi, sg), hd)
                s_new[key] = (s_old[key] * sd[sg]
                              + lax.dot_general(kd[sg], vn[a0:a0 + tb], _TN, preferred_element_type=f32))

        o_rows = []
        for gi in range(n_groups):
            o_heads = []
            for hd in range(DN_HEADS):
                o = outs[gi * DN_HEADS + hd]
                on = o * lax.rsqrt(jnp.mean(o * o, axis=-1, keepdims=True) + EPS) * dnw
                zd = zdn_scr[gi * GROUP:(gi + 1) * GROUP, hd * DN_D:(hd + 1) * DN_D]
                o_heads.append(on * _silu(zd))
            o_rows.append(jnp.concatenate(o_heads, axis=1))
        o_all = o_rows[0] if n_groups == 1 else jnp.concatenate(o_rows, axis=0)
        odn_ref[...] = o_all.astype(bf16).reshape(nb, tb, DN_WIDTH)
        s_ref[...] = jnp.stack([jnp.stack([s_new[(b, hd)] for hd in range(DN_HEADS)]) for b in range(nb)])

    n_ch = len(chains)
    if pipelined:
        a_norm_in()
        back_setup = ([functools.partial(b_load, ci) for ci in range(n_ch)]
                      + [functools.partial(b_mats, ci) for ci in range(n_ch)])
        per_piece = -(-len(back_setup) // len(projections))
        for piece in projections:
            piece()
            for thunk in back_setup[:per_piece]:
                thunk()
            del back_setup[:per_piece]
        sols = _unit_lower_solve(back["nmat"], back["rhs"], eye, m_ref, n_levels, fillers)
        for fill in fillers:
            fill(None)
        b_state(sols)
        a_store()
    else:
        a_norm_in()
        for piece in projections:
            piece()
        for fill in fillers:
            fill(None)
        a_store()
        for ci in range(n_ch):
            b_load(ci)
        for ci in range(n_ch):
            b_mats(ci)
        b_state(_unit_lower_solve(back["nmat"], back["rhs"], eye, m_ref, n_levels, []))


def _delta_masks(tb):
    r = np.arange(GROUP)[:, None]
    c = np.arange(GROUP)[None, :]
    same = (r // tb) == (c // tb)
    ms = [same & (r >= c), same & (r > c)]
    for l in range(tb.bit_length() - 1):
        ms.append(((r >> (l + 1)) == (c >> (l + 1))) & (((r >> l) & 1) == 1) & (((c >> l) & 1) == 0))
    return jnp.asarray(np.stack(ms).astype(np.float32))


def _delta_call(x, w1, w2, norm_w, conv_w, alog, dtb, dnw, conv0, s0, *, tb, pipelined):
    nb, T, _ = x.shape
    assert GROUP % tb == 0 and (nb * tb) % GROUP == 0 and T % tb == 0
    n_t = T // tb
    rows = nb * tb
    if pipelined:
        grid = (n_t + 1,)
        cur = lambda i: jnp.minimum(i, n_t - 1)
        prev = lambda i: jnp.maximum(i - 1, 0)
    else:
        grid = (n_t,)
        cur = prev = lambda i: i
    const2 = lambda i: (0, 0)
    const3 = lambda i: (0, 0, 0)
    const4 = lambda i: (0, 0, 0, 0)
    body = functools.partial(_delta_body, nb=nb, tb=tb, n_t=n_t, pipelined=pipelined)
    masks = _delta_masks(tb)
    one = pl.Buffered(1)
    return pl.pallas_call(
        body,
        grid=grid,
        in_specs=[
            pl.BlockSpec((nb, tb, D_MODEL), lambda i: (0, cur(i), 0)),
            pl.BlockSpec((D_MODEL, CONV_CH), const2, pipeline_mode=one),
            pl.BlockSpec((D_MODEL, W2_COLS), const2, pipeline_mode=one),
            pl.BlockSpec((1, D_MODEL), const2),
            pl.BlockSpec((CONV_K, CONV_CH), const2),
            pl.BlockSpec((1, LANES), const2),
            pl.BlockSpec((1, LANES), const2),
            pl.BlockSpec((1, DN_D), const2),
            pl.BlockSpec(masks.shape, const3, pipeline_mode=one),
            pl.BlockSpec((nb, SUBLANES, CONV_CH), const3),
            pl.BlockSpec((nb, DN_HEADS, DN_D, DN_D), const4),
        ],
        out_specs=[
            pl.BlockSpec((nb, tb, DN_WIDTH), lambda i: (0, prev(i), 0)),
            pl.BlockSpec((nb, S5_PLANES, tb, LANES), lambda i: (0, 0, cur(i), 0)),
            pl.BlockSpec((nb, tb, S5_WIDTH), lambda i: (0, cur(i), 0)),
            pl.BlockSpec((nb, SUBLANES, CONV_CH), const3),
            pl.BlockSpec((nb, DN_HEADS, DN_D, DN_D), const4),
        ],
        out_shape=[
            jax.ShapeDtypeStruct((nb, T, DN_WIDTH), bf16),
            jax.ShapeDtypeStruct((nb, S5_PLANES, T, LANES), f32),
            jax.ShapeDtypeStruct((nb, T, S5_WIDTH), f32),
            jax.ShapeDtypeStruct((nb, SUBLANES, CONV_CH), f32),
            jax.ShapeDtypeStruct((nb, DN_HEADS, DN_D, DN_D), f32),
        ],
        scratch_shapes=[
            pltpu.VMEM((nb, tb + SUBLANES, CONV_CH), f32),
            pltpu.VMEM((rows, DN_WIDTH), f32),
            pltpu.VMEM((rows, DN_WIDTH), f32),
            pltpu.VMEM((rows, DN_WIDTH), f32),
            pltpu.VMEM((rows, LANES), f32),
            pltpu.VMEM((rows, LANES), f32),
            pltpu.VMEM((rows, DN_WIDTH), f32),
        ],
        compiler_params=pltpu.CompilerParams(
            dimension_semantics=("arbitrary",), vmem_limit_bytes=VMEM_LIMIT),
        name="delta",
    )(x, w1, w2, norm_w, conv_w, alog, dtb, dnw, masks, conv0, s0)


S5MAT_GROUPS_PER_STEP = 8


def _split16(a):
    hi = a.astype(bf16)
    return hi, (a - hi.astype(f32)).astype(bf16)


def _s5mat_body(cp_ref, bp_ref, t_ref):
    ri = lax.broadcasted_iota(jnp.int32, (S5_GL, S5_GL), 0) >> 4
    ci = lax.broadcasted_iota(jnp.int32, (S5_GL, S5_GL), 1) >> 4
    causal = ri >= ci
    d = functools.partial(lax.dot_general, dimension_numbers=_NT, preferred_element_type=f32)
    for g in range(S5MAT_GROUPS_PER_STEP):
        ch, cl = _split16(cp_ref[g])
        bh, bl = _split16(bp_ref[g])
        t = d(ch, bh) + (d(ch, bl) + d(cl, bh))
        t_ref[g] = jnp.where(causal, t, 0.0).astype(bf16)


def _s5mat_call(cp, bp):
    n = S5MAT_GROUPS_PER_STEP
    return pl.pallas_call(
        _s5mat_body,
        grid=(S5_GROUPS // n,),
        in_specs=[
            pl.BlockSpec((n, S5_GL, 2 * S5_STATE), lambda g: (g, 0, 0)),
            pl.BlockSpec((n, S5_GL, 2 * S5_STATE), lambda g: (g, 0, 0)),
        ],
        out_specs=pl.BlockSpec((n, S5_GL, S5_GL), lambda g: (g, 0, 0)),
        out_shape=jax.ShapeDtypeStruct((S5_GROUPS, S5_GL, S5_GL), bf16),
        compiler_params=pltpu.CompilerParams(dimension_semantics=("arbitrary",)),
        name="s5mat",
    )(cp, bp)


def _s5_body(u_ref, z_ref, x_hbm, odn_hbm, xr0_ref, xi0_ref, t_ref, wp_ref, vt_ref, ar_ref, ai_ref,
             d_ref, gw_ref, gb_ref, wo_ref, fw_ref,
             y_hbm, xr_ref, xi_ref,
             at_scr, er_scr, ei_scr, xinr_scr, xini_scr, yt_scr, y_scr,
             xbuf, obuf, ybuf, sem_x, sem_o, sem_y, *, tb, n_streams, n_chunks, n_sub):
    R = tb // S5_L
    used = n_streams * n_chunks
    g_idx = pl.program_id(0)
    row_base = pl.program_id(1) * tb

    def in_copies(rb, slot):
        rows = pl.ds(row_base + rb * OUT_SUB, OUT_SUB)
        return (pltpu.make_async_copy(x_hbm.at[g_idx, rows, :], xbuf.at[slot], sem_x.at[slot]),
                pltpu.make_async_copy(odn_hbm.at[g_idx, rows, :], obuf.at[slot], sem_o.at[slot]))

    def out_copy(rb, slot):
        rows = pl.ds(row_base + rb * OUT_SUB, OUT_SUB)
        return pltpu.make_async_copy(ybuf.at[slot], y_hbm.at[g_idx, rows, :], sem_y.at[slot])

    for cp in in_copies(0, 0):
        cp.start()

    @pl.when(pl.program_id(1) == 0)
    def _init():
        xr_ref[...] = xr0_ref[...]
        xi_ref[...] = xi0_ref[...]

    for f in range(S5_L):
        for j in range(S5_PLANES):
            ut = u_ref[0, j, pl.ds(f, R, stride=S5_L), :]
            ut_t = ut.T.astype(bf16)
            for gl in range(LANES // S5_GROUP):
                g = j * (LANES // S5_GROUP) + gl
                at_scr[g, f * S5_GROUP:(f + 1) * S5_GROUP, :] = ut_t[gl * S5_GROUP:(gl + 1) * S5_GROUP, :]

    for p in range(S5_PAIRS):
        a_t = at_scr[2 * p:2 * p + 2].reshape(2 * S5_GL, R)
        e = lax.dot_general(a_t, wp_ref[p], _TN, preferred_element_type=f32)
        er_scr[:, p * LANES:(p + 1) * LANES] = e[:, :LANES]
        ei_scr[:, p * LANES:(p + 1) * LANES] = e[:, LANES:]

    if used < R:
        xinr_scr[...] = jnp.zeros_like(xinr_scr)
        xini_scr[...] = jnp.zeros_like(xini_scr)

    a_r = ar_ref[...]
    a_i = ai_ref[...]
    for s in range(n_streams):
        def step(c, carry):
            xr, xi = carry
            row = s * n_chunks + c
            xinr_scr[pl.ds(row, 1), :] = xr
            xini_scr[pl.ds(row, 1), :] = xi
            er = er_scr[pl.ds(row, 1), :]
            ei = ei_scr[pl.ds(row, 1), :]
            return a_r * xr - a_i * xi + er, a_r * xi + a_i * xr + ei

        xr, xi = lax.fori_loop(0, n_chunks, step, (xr_ref[0, s:s + 1, :], xi_ref[0, s:s + 1, :]))
        xr_ref[0, s:s + 1, :] = xr
        xi_ref[0, s:s + 1, :] = xi

    for g in range(S5_GROUPS):
        p = g // 2
        xin = jnp.concatenate([xinr_scr[:, p * LANES:(p + 1) * LANES],
                               xini_scr[:, p * LANES:(p + 1) * LANES]], axis=1).astype(bf16)
        yt_scr[g] = (jnp.dot(t_ref[g], at_scr[g], preferred_element_type=f32)
                     + lax.dot_general(vt_ref[g], xin, _NT, preferred_element_type=f32))

    for f in range(S5_L):
        gpp = LANES // S5_GROUP
        for j in range(S5_PLANES):
            yt = yt_scr[j * gpp:(j + 1) * gpp, f * S5_GROUP:(f + 1) * S5_GROUP, :].reshape(LANES, R)
            y_scr[j, pl.ds(f, R, stride=S5_L), :] = yt.T

    n_col = D_MODEL // GROUP
    assert n_col == S5_PLANES

    def region(gate_rb, proj_rb, o_prev):
        slot = None if proj_rb is None else proj_rb % 2
        sl = None if gate_rb is None else pl.ds(gate_rb * OUT_SUB, OUT_SUB)
        gys, accs, outs = [], [], []

        def proj_piece(c, lhs, w_lo, prev):
            cols = slice(c * GROUP, (c + 1) * GROUP)
            return prev + jnp.dot(lhs, wo_ref[w_lo:w_lo + lhs.shape[1], cols], preferred_element_type=f32)

        for c in range(n_col):
            if sl is not None:
                lanes = slice(c * LANES, (c + 1) * LANES)
                y = y_scr[c, sl, :] + d_ref[:, lanes] * u_ref[0, c, sl, :]
                gys.append(_gelu_tanh(y))
            if slot is not None:
                accs.append(proj_piece(c, obuf[slot], 0, xbuf[slot, :, c * GROUP:(c + 1) * GROUP]))
        if sl is not None:
            gy = jnp.concatenate(gys, axis=1)
            pre = _bdot(gy, gw_ref[...]) + gb_ref[...]
        for c in range(n_col):
            if sl is not None:
                lanes = slice(c * LANES, (c + 1) * LANES)
                gate = _sigmoid(pre[:, lanes])
                outs.append((gys[c] * gate * _silu(z_ref[0, sl, lanes])).astype(bf16))
            if slot is not None:
                accs[c] = proj_piece(c, o_prev, DN_WIDTH, accs[c])
        if slot is not None:
            acc = jnp.concatenate(accs, axis=1)
            ms = jnp.mean(acc * acc, axis=-1, keepdims=True)
            ybuf[slot] = acc * lax.rsqrt(ms + EPS) * fw_ref[...]
        return None if sl is None else jnp.concatenate(outs, axis=1)

    o_prev = None
    for j in range(n_sub + 1):
        if j >= 2:
            out_copy(j - 2, j % 2).start()
        if j >= 1:
            for cp in in_copies(j - 1, (j - 1) % 2):
                cp.wait()
        if j >= 3:
            out_copy(j - 3, (j - 3) % 2).wait()
        if 1 <= j < n_sub:
            for cp in in_copies(j, j % 2):
                cp.start()
        o_prev = region(j if j < n_sub else None, j - 1 if j >= 1 else None, o_prev)
    out_copy(n_sub - 1, (n_sub - 1) % 2).start()
    if n_sub >= 2:
        out_copy(n_sub - 2, (n_sub - 2) % 2).wait()
    out_copy(n_sub - 1, (n_sub - 1) % 2).wait()


def _s5_call(u, z, x, odn, xr0, xi0, tmat, wp, vt, ar, ai, dvec, gw, gb, w_out, fw,
             *, tb, n_streams, n_chunks):
    G, T, _ = z.shape
    t_valid = x.shape[1]
    assert x.shape[0] == G and (t_valid % tb == 0 or (t_valid < tb and T == tb))
    n_sub = min(tb, t_valid) // OUT_SUB
    grid = (G, T // tb)
    R = tb // S5_L
    nstate = S5_GROUPS * S5_STATE
    c2 = lambda g, t: (0, 0)
    c3 = lambda g, t: (0, 0, 0)
    body = functools.partial(_s5_body, tb=tb, n_streams=n_streams, n_chunks=n_chunks, n_sub=n_sub)
    one = pl.Buffered(1)
    hbm = pl.BlockSpec(memory_space=pl.ANY)
    return pl.pallas_call(
        body,
        grid=grid,
        in_specs=[
            pl.BlockSpec((1, S5_PLANES, tb, LANES), lambda g, t: (g, 0, t, 0)),
            pl.BlockSpec((1, tb, S5_WIDTH), lambda g, t: (g, t, 0)),
            hbm,
            hbm,
            pl.BlockSpec((1, n_streams, nstate), lambda g, t: (g, 0, 0)),
            pl.BlockSpec((1, n_streams, nstate), lambda g, t: (g, 0, 0)),
            pl.BlockSpec((S5_GROUPS, S5_GL, S5_GL), c3, pipeline_mode=one),
            pl.BlockSpec((S5_PAIRS, 2 * S5_GL, 2 * LANES), c3, pipeline_mode=one),
            pl.BlockSpec((S5_GROUPS, S5_GL, 2 * LANES), c3, pipeline_mode=one),
            pl.BlockSpec((1, nstate), c2),
            pl.BlockSpec((1, nstate), c2),
            pl.BlockSpec((1, S5_WIDTH), c2),
            pl.BlockSpec((S5_WIDTH, S5_WIDTH), c2),
            pl.BlockSpec((1, S5_WIDTH), c2),
            pl.BlockSpec((D_MODEL, D_MODEL), c2, pipeline_mode=one),
            pl.BlockSpec((1, D_MODEL), c2),
        ],
        out_specs=[
            hbm,
            pl.BlockSpec((1, n_streams, nstate), lambda g, t: (g, 0, 0)),
            pl.BlockSpec((1, n_streams, nstate), lambda g, t: (g, 0, 0)),
        ],
        out_shape=[
            jax.ShapeDtypeStruct((G, t_valid, D_MODEL), f32),
            jax.ShapeDtypeStruct((G, n_streams, nstate), f32),
            jax.ShapeDtypeStruct((G, n_streams, nstate), f32),
        ],
        scratch_shapes=[
            pltpu.VMEM((S5_GROUPS, S5_GL, R), bf16),
            pltpu.VMEM((R, nstate), f32),
            pltpu.VMEM((R, nstate), f32),
            pltpu.VMEM((R, nstate), f32),
            pltpu.VMEM((R, nstate), f32),
            pltpu.VMEM((S5_GROUPS, S5_GL, R), f32),
            pltpu.VMEM((S5_PLANES, tb, LANES), f32),
            pltpu.VMEM((2, OUT_SUB, D_MODEL), f32),
            pltpu.VMEM((2, OUT_SUB, DN_WIDTH), bf16),
            pltpu.VMEM((2, OUT_SUB, D_MODEL), f32),
            pltpu.SemaphoreType.DMA((2,)),
            pltpu.SemaphoreType.DMA((2,)),
            pltpu.SemaphoreType.DMA((2,)),
        ],
        compiler_params=pltpu.CompilerParams(
            dimension_semantics=("arbitrary", "arbitrary"), vmem_limit_bytes=VMEM_LIMIT),
        name="s5",
    )(u, z, x, odn, xr0, xi0, tmat, wp, vt, ar, ai, dvec, gw, gb, w_out, fw)


def _s5_operands(a_re, a_im, log_dt, b_re, b_im, c_re, c_im):
    lam_re = jnp.minimum(a_re, -1e-4)
    lam_im = a_im
    dt = jnp.exp(log_dt)[:, None]
    ldt_re, ldt_im = lam_re * dt, lam_im * dt

    def lpow(k):
        mag = jnp.exp(ldt_re * k)
        return mag * jnp.cos(ldt_im * k), mag * jnp.sin(ldt_im * k)

    lb_re, lb_im = lpow(1.0)
    den = lam_re * lam_re + lam_im * lam_im
    f_re = ((lb_re - 1.0) * lam_re + lb_im * lam_im) / den
    f_im = (lb_im * lam_re - (lb_re - 1.0) * lam_im) / den
    bb_re = f_re[..., None] * b_re - f_im[..., None] * b_im
    bb_im = f_re[..., None] * b_im + f_im[..., None] * b_re

    fr = jnp.arange(S5_L, dtype=f32)
    G, N = S5_GROUPS, S5_STATE

    def cmul(ar, ai, br, bi):
        return ar * br - ai * bi, ar * bi + ai * br

    pr, pi = lpow(fr[:, None, None])
    pr, pi = pr.transpose(1, 0, 2), pi.transpose(1, 0, 2)
    cpr, cpi = cmul(c_re[:, None], c_im[:, None], pr[:, :, None], pi[:, :, None])
    cpr, cpi = cpr.reshape(G, S5_GL, N), cpi.reshape(G, S5_GL, N)
    bbr_t, bbi_t = bb_re.transpose(0, 2, 1), bb_im.transpose(0, 2, 1)
    qr, qi = lpow(-fr[:, None, None])
    qr, qi = qr.transpose(1, 0, 2), qi.transpose(1, 0, 2)
    bpr, bpi = cmul(qr[:, :, None], qi[:, :, None], bbr_t[:, None], bbi_t[:, None])
    bpr, bpi = bpr.reshape(G, S5_GL, N), bpi.reshape(G, S5_GL, N)

    wr_, wi_ = lpow((S5_L - 1.0) - fr[:, None, None])
    wr_, wi_ = wr_.transpose(1, 0, 2), wi_.transpose(1, 0, 2)
    w_re, w_im = cmul(wr_[:, :, None], wi_[:, :, None], bbr_t[:, None], bbi_t[:, None])
    w_re, w_im = w_re.reshape(G, S5_GL, N), w_im.reshape(G, S5_GL, N)
    vr_, vi_ = lpow(fr[:, None, None] + 1.0)
    vr_, vi_ = vr_.transpose(1, 0, 2), vi_.transpose(1, 0, 2)
    v_re, v_im = cmul(c_re[:, None], c_im[:, None], vr_[:, :, None], vi_[:, :, None])
    v_re, v_im = v_re.reshape(G, S5_GL, N), v_im.reshape(G, S5_GL, N)

    zeros = jnp.zeros((S5_PAIRS, S5_GL, N), f32)

    def pair_cols(re, im):
        re, im = re.reshape(S5_PAIRS, 2, S5_GL, N), im.reshape(S5_PAIRS, 2, S5_GL, N)
        even = jnp.concatenate([re[:, 0], zeros, im[:, 0], zeros], axis=-1)
        odd = jnp.concatenate([zeros, re[:, 1], zeros, im[:, 1]], axis=-1)
        return jnp.stack([even, odd], axis=1)

    wp = pair_cols(w_re, w_im).reshape(S5_PAIRS, 2 * S5_GL, 2 * LANES).astype(bf16)
    vt = pair_cols(v_re, -v_im).reshape(S5_GROUPS, S5_GL, 2 * LANES).astype(bf16)
    a16r, a16i = lpow(float(S5_L))
    cp = jnp.concatenate([cpr, cpi], axis=2)
    bp = jnp.concatenate([bpr, -bpi], axis=2)
    return cp, bp, wp, vt, a16r.reshape(1, G * N), a16i.reshape(1, G * N)


def _pad_lanes(v, start=0):
    out = jnp.zeros((1, LANES), f32)
    return out.at[0, start:start + v.shape[0]].set(v.astype(f32))


def _layer(x, conv0, s0, xr0, xi0, prm, *, delta_cfg, s5_cfg):
    B, T, _ = x.shape
    conv0p = jnp.concatenate([jnp.zeros((B, SUBLANES - (CONV_K - 1), CONV_CH), f32), conv0], axis=1)
    odn, us5, zs5, convout, s_new = _delta_call(
        x, prm["w1"], prm["w2"], prm["norm_w"], prm["conv_w"], prm["alog"], prm["dtb"], prm["dnw"],
        conv0p, s0, **delta_cfg)

    nstate = S5_GROUPS * S5_STATE
    if s5_cfg["flatten"]:
        tb = s5_cfg["tb"]
        uf = us5.transpose(1, 0, 2, 3).reshape(1, S5_PLANES, B * T, LANES)
        zf = zs5.reshape(1, B * T, S5_WIDTH)
        pad = tb - B * T
        uf = jnp.pad(uf, ((0, 0), (0, 0), (0, pad), (0, 0)))
        zf = jnp.pad(zf, ((0, 0), (0, pad), (0, 0)))
        y, xr, xi = _s5_call(uf, zf, x.reshape(1, B * T, D_MODEL), odn.reshape(1, B * T, DN_WIDTH),
                             xr0.reshape(1, B, nstate), xi0.reshape(1, B, nstate),
                             prm["tmat"], prm["wp"], prm["vt"], prm["ar"], prm["ai"], prm["dvec"],
                             prm["gw"], prm["gb"], prm["w_out"], prm["fw"],
                             tb=tb, n_streams=B, n_chunks=T // S5_L)
    else:
        tb = s5_cfg["tb"]
        y, xr, xi = _s5_call(us5, zs5, x, odn, xr0.reshape(B, 1, nstate), xi0.reshape(B, 1, nstate),
                             prm["tmat"], prm["wp"], prm["vt"], prm["ar"], prm["ai"], prm["dvec"],
                             prm["gw"], prm["gb"], prm["w_out"], prm["fw"],
                             tb=tb, n_streams=1, n_chunks=tb // S5_L)

    return (y.reshape(B, T, D_MODEL), convout[:, SUBLANES - (CONV_K - 1):, :], s_new,
            xr.reshape(B, S5_GROUPS, S5_STATE), xi.reshape(B, S5_GROUPS, S5_STATE))


def kernel(x_prompt, x_sample, cache_conv, state_dn, state_s5_re, state_s5_im, norm_w, w_in, conv_w, dn_A_log, dn_dt_bias, dn_norm_w, s5_A_re, s5_A_im, s5_log_dt, s5_B_re, s5_B_im, s5_C_re, s5_C_im, s5_D, glu_w, glu_b, w_out, final_norm_w):
    depth = norm_w.shape[0]
    assert depth == 1
    l = 0
    assert w_in.shape[-1] == IN_COLS
    w1 = w_in[l, :, :CONV_CH].astype(bf16)
    w2 = w_in[l, :, IN_COLS - W2_COLS:].astype(bf16)
    cp, bp, wp, vt, ar, ai = _s5_operands(
        s5_A_re[l].astype(f32), s5_A_im[l].astype(f32), s5_log_dt[l].astype(f32),
        s5_B_re[l].astype(f32), s5_B_im[l].astype(f32), s5_C_re[l].astype(f32), s5_C_im[l].astype(f32))
    prm = dict(
        w1=w1, w2=w2,
        norm_w=norm_w[l].reshape(1, D_MODEL).astype(f32),
        conv_w=conv_w[l].astype(f32),
        alog=_pad_lanes(dn_A_log[l], AB_LANE),
        dtb=_pad_lanes(dn_dt_bias[l], AB_LANE),
        dnw=dn_norm_w[l].reshape(1, DN_D).astype(f32),
        tmat=_s5mat_call(cp, bp),
        wp=wp, vt=vt, ar=ar, ai=ai,
        dvec=s5_D[l].reshape(1, S5_WIDTH).astype(f32),
        gw=glu_w[l].astype(bf16),
        gb=glu_b[l].reshape(1, S5_WIDTH).astype(f32),
        w_out=w_out[l].astype(bf16),
        fw=final_norm_w.reshape(1, D_MODEL).astype(f32),
    )

    bp = x_prompt.shape[0]
    yp, c1, d1, r1, i1 = _layer(
        x_prompt,
        jnp.zeros((bp, CONV_K - 1, CONV_CH), f32),
        jnp.zeros((bp, DN_HEADS, DN_D, DN_D), f32),
        jnp.zeros((bp, S5_GROUPS, S5_STATE), f32),
        jnp.zeros((bp, S5_GROUPS, S5_STATE), f32),
        prm,
        delta_cfg=dict(tb=256, pipelined=True),
        s5_cfg=dict(flatten=False, tb=2048))
    ys, c2, d2, r2, i2 = _layer(
        x_sample, cache_conv[l].astype(f32), state_dn[l].astype(f32),
        state_s5_re[l].astype(f32), state_s5_im[l].astype(f32),
        prm,
        delta_cfg=dict(tb=32, pipelined=False),
        s5_cfg=dict(flatten=True, tb=2048))

    return (yp, ys, c1[None], d1[None], r1[None], i1[None], c2[None], d2[None], r2[None], i2[None])
```

```python
import functools
import math

import jax
import jax.numpy as jnp
import numpy as np
from jax import lax
from jax.experimental import pallas as pl
from jax.experimental.pallas import tpu as pltpu

bf16 = jnp.bfloat16
f32 = jnp.float32

LANES = 128
SUBLANES = 8

D_MODEL = 1024
DN_HEADS = 4
DN_D = 128
DN_WIDTH = DN_HEADS * DN_D
CONV_K = 4
CONV_CH = 3 * DN_WIDTH
S5_WIDTH = 512
S5_GROUP = 16
S5_GROUPS = 32
S5_STATE = 64
S5_PAIRS = S5_GROUPS // 2
S5_L = 16
S5_GL = S5_GROUP * S5_L
OUT_SUB = 256
OUT_ROWS = 2048
S5_PLANES = S5_WIDTH // LANES
EPS = 1e-6
L2_EPS = 1e-6

IN_COLS = CONV_CH + 2 * DN_HEADS + DN_WIDTH + 2 * S5_WIDTH
W2_COLS = LANES + DN_WIDTH + 2 * S5_WIDTH
AB_LANE = LANES - 2 * DN_HEADS
W2_ZDN = LANES
W2_US5 = W2_ZDN + DN_WIDTH
W2_ZS5 = W2_US5 + S5_WIDTH

GROUP = 256
MASK_INCL, MASK_STRICT, MASK_OFF = 0, 1, 2
FOLD_MIN_ROWS = 64
VMEM_LIMIT =56 * 1024 * 1024

_NT = (((1,), (1,)), ((), ()))
_TN = (((0,), (0,)), ((), ()))


def _bdot(a, b):
    return jnp.dot(a.astype(bf16), b.astype(bf16), preferred_element_type=f32)


def _bdot_g(a, b, dims):
    return lax.dot_general(a.astype(bf16), b.astype(bf16), dims, preferred_element_type=f32)


def _sigmoid(x):
    return 1.0 / (1.0 + jnp.exp(-x))


def _silu(x):
    return x * _sigmoid(x)


def _softplus(x):
    return jnp.maximum(x, 0.0) + jnp.log1p(jnp.exp(-jnp.abs(x)))


def _gelu_tanh(x):
    c = math.sqrt(2.0 / math.pi)
    return 0.5 * x * (1.0 + jnp.tanh(c * (x + 0.044715 * (x * x * x))))


def _zero_after(x):
    bits = pltpu.bitcast(x[0:SUBLANES, 0:LANES].astype(f32), jnp.uint32)
    half = jnp.uint32(16)
    bits = lax.shift_right_logical(lax.shift_right_logical(bits, half), half)
    return pltpu.bitcast(bits, f32)[0:1, :]


def _unit_lower_solve(nmats, rhss, eye, m_ref, n_levels, fillers):
    d32 = functools.partial(jnp.dot, preferred_element_type=f32)

    def d16(a, b):
        return d32(a, b).astype(bf16)

    def fill(after):
        if fillers:
            fillers.pop(0)(_zero_after(after[-1]))

    def fold(x, rows):
        out = x[0:rows]
        for k in range(1, GROUP // rows):
            out = out + x[k * rows:(k + 1) * rows]
        return out

    def unfold(xf, rows):
        lane_blk = lax.broadcasted_iota(jnp.int32, (rows, GROUP), 1) >> (rows.bit_length() - 1)
        return jnp.concatenate([jnp.where(lane_blk == k, xf, jnp.zeros_like(xf))
                                for k in range(GROUP // rows)], axis=0)

    n16 = [n.astype(bf16) for n in nmats]
    invs = [(eye - n * m_ref[MASK_OFF]).astype(bf16) for n in nmats]
    for lvl in range(1, n_levels):
        rows = min(GROUP, max(FOLD_MIN_ROWS, 2 ** (lvl + 1)))
        mask16 = m_ref[MASK_OFF + lvl].astype(bf16)
        lhs = invs if rows == GROUP else [fold(inv, rows) for inv in invs]
        t1 = [d16(l, n * mask16) for l, n in zip(lhs, n16)]
        fill(invs)
        t2 = [d16(t, inv) for t, inv in zip(t1, invs)]
        fill(t1)
        if rows != GROUP:
            t2 = [unfold(t, rows) for t in t2]
        invs = [inv - t for inv, t in zip(invs, t2)]
    return [d32(inv, r.astype(bf16)) for inv, r in zip(invs, rhss)]


def _segment_cumsum(tri16, g):
    g1 = g.astype(bf16)
    r1 = g - g1.astype(f32)
    g2 = r1.astype(bf16)
    g3 = (r1 - g2.astype(f32)).astype(bf16)
    s = jnp.dot(tri16, jnp.concatenate([g1, g2, g3], axis=1), preferred_element_type=f32)
    n = g.shape[1]
    return s[:, :n] + (s[:, n:2 * n] + s[:, 2 * n:])


def _delta_body(x_ref, w1_ref, w2_ref, nw_ref, cw_ref, alog_ref, dtb_ref, dnw_ref, m_ref, conv0_ref, s0_ref,
                odn_ref, us5_ref, zs5_ref, convout_ref, s_ref,
                cbuf, qn_scr, kn_scr, v_scr, gc_scr, beta_scr, zdn_scr, *, nb, tb, n_t, pipelined):
    rows = nb * tb
    n_groups = rows // GROUP
    segs = GROUP // tb
    n_levels = tb.bit_length() - 1
    step = pl.program_id(0)
    fresh = step < n_t

    @pl.when(step == 0)
    def _init():
        cbuf[:, 0:SUBLANES, :] = conv0_ref[...]
        s_ref[...] = s0_ref[...]
        if pipelined:
            for scr in (qn_scr, kn_scr, v_scr, gc_scr, beta_scr, zdn_scr):
                scr[...] = jnp.zeros_like(scr)

    front = {}

    def project(w_ref, lo, width):
        return jnp.dot(front["h"], w_ref[:, lo:lo + width], preferred_element_type=f32)

    def a_norm_in():
        x = x_ref[...].reshape(rows, D_MODEL)
        ms = jnp.mean(x * x, axis=-1, keepdims=True)
        front["h"] = (x * lax.rsqrt(ms + EPS) * nw_ref[...]).astype(bf16)

    def a_project_qkv(s):
        front[("raw", s)] = project(w1_ref, s * DN_WIDTH, DN_WIDTH)

    def a_project_us5():
        u = project(w2_ref, W2_US5, S5_WIDTH)
        for j in range(S5_PLANES):
            us5_ref[:, j] = u[:, j * LANES:(j + 1) * LANES].reshape(nb, tb, LANES)

    def a_project_zs5():
        zs5_ref[...] = project(w2_ref, W2_ZS5, S5_WIDTH).reshape(nb, tb, S5_WIDTH)

    def a_project_zdn():
        front["zdn"] = project(w2_ref, W2_ZDN, DN_WIDTH)

    def a_project_gates():
        ab = project(w2_ref, 0, LANES)
        front["g"] = -jnp.exp(alog_ref[...]) * _softplus(ab + dtb_ref[...])
        front["beta"] = _sigmoid(ab)

    projections = ([functools.partial(a_project_qkv, s) for s in range(3)]
                   + [a_project_us5, a_project_zs5, a_project_zdn, a_project_gates])

    def a_conv(b, s, zero):
        cols = slice(s * DN_WIDTH, (s + 1) * DN_WIDTH)
        cbuf[b, SUBLANES:SUBLANES + tb, cols] = front[("raw", s)][b * tb:(b + 1) * tb, :]
        taps = [cw_ref[j:j + 1, cols] for j in range(CONV_K)]
        if zero is not None:
            zero_w = jnp.concatenate([zero] * (DN_WIDTH // LANES), axis=1)
            taps = [t + zero_w for t in taps]
        acc = cbuf[b, 5:5 + tb, cols] * taps[0]
        for j in range(1, CONV_K):
            acc = acc + cbuf[b, 5 + j:5 + j + tb, cols] * taps[j]
        front[("act", b, s)] = _silu(acc)
        tail = cbuf[b, tb:tb + SUBLANES, cols]
        if pipelined:
            tail = jnp.where(fresh, tail, cbuf[b, 0:SUBLANES, cols])
        cbuf[b, 0:SUBLANES, cols] = tail
        convout_ref[b, :, cols] = tail

    def a_norm(b, hd, zero):
        lo = hd * DN_D
        q = front[("act", b, 0)][:, lo:lo + DN_D]
        k = front[("act", b, 1)][:, lo:lo + DN_D]
        if zero is not None:
            q, k = q + zero, k + zero
        front[("qn", b, hd)] = (q * lax.rsqrt(jnp.sum(q * q, axis=-1, keepdims=True) + L2_EPS)
                                * (DN_D ** -0.5))
        front[("kn", b, hd)] = k * lax.rsqrt(jnp.sum(k * k, axis=-1, keepdims=True) + L2_EPS)

    def a_gates(zero):
        tri16 = m_ref[MASK_INCL].astype(bf16)
        g = front["g"] if zero is None else front["g"] + zero
        front["gc"] = [_segment_cumsum(tri16, g[gi * GROUP:(gi + 1) * GROUP])
                       for gi in range(n_groups)]

    def a_store():
        for b in range(nb):
            r = slice(b * tb, (b + 1) * tb)
            for hd in range(DN_HEADS):
                c = slice(hd * DN_D, (hd + 1) * DN_D)
                qn_scr[r, c] = front[("qn", b, hd)]
                kn_scr[r, c] = front[("kn", b, hd)]
            v_scr[r, :] = front[("act", b, 2)]
        for gi in range(n_groups):
            gc_scr[gi * GROUP:(gi + 1) * GROUP, :] = front["gc"][gi]
        beta_scr[...] = front["beta"]
        zdn_scr[...] = front["zdn"]

    fillers = ([functools.partial(a_conv, b, s) for b in range(nb) for s in range(3)] + [a_gates]
               + [functools.partial(a_norm, b, hd) for b in range(nb) for hd in range(DN_HEADS)])

    incl = m_ref[MASK_INCL]
    strict = m_ref[MASK_STRICT]
    eye = incl - strict
    dnw = dnw_ref[...]
    chains = [(gi, hd) for gi in range(n_groups) for hd in range(DN_HEADS)]
    back = {}

    def stream_of(gi, sg):
        return (gi * GROUP) // tb + sg

    for name in ("q16", "qe16", "k16", "kb16", "rhs", "dmat", "kdec", "sdecay", "nmat", "attn"):
        back[name] = [None] * len(chains)

    def b_load(ci):
        gi, hd = chains[ci]
        r = slice(gi * GROUP, (gi + 1) * GROUP)
        c = slice(hd * DN_D, (hd + 1) * DN_D)
        q, k, v = qn_scr[r, c], kn_scr[r, c], v_scr[r, c]
        beta = beta_scr[r, AB_LANE + DN_HEADS + hd:AB_LANE + DN_HEADS + hd + 1]
        g = gc_scr[r, AB_LANE + hd:AB_LANE + hd + 1]
        eg = jnp.exp(g)
        kb = k * beta
        gcb = jnp.broadcast_to(g, (GROUP, GROUP))
        back["dmat"][ci] = jnp.exp((gcb - gcb.T) * incl) * incl
        back["q16"][ci] = q.astype(bf16)
        back["qe16"][ci] = (q * eg).astype(bf16)
        back["k16"][ci] = k.astype(bf16)
        back["kb16"][ci] = kb.astype(bf16)
        back["rhs"][ci] = jnp.concatenate([v * beta, kb * eg], axis=1)
        kd, sd = [], []
        for sg in range(segs):
            a0 = sg * tb
            glast = g[a0 + tb - 1:a0 + tb, :]
            kd.append((k[a0:a0 + tb] * jnp.exp(glast - g[a0:a0 + tb])).astype(bf16))
            sd.append(jnp.exp(glast))
        back["kdec"][ci] = kd
        back["sdecay"][ci] = sd

    def b_mats(ci):
        d = functools.partial(lax.dot_general, dimension_numbers=_NT, preferred_element_type=f32)
        dm = back["dmat"][ci]
        back["nmat"][ci] = d(back["kb16"][ci], back["k16"][ci]) * (dm * strict)
        back["attn"][ci] = (d(back["q16"][ci], back["k16"][ci]) * dm).astype(bf16)

    def b_state(sols):
        s_old = {(b, hd): s_ref[b, hd] for b in range(nb) for hd in range(DN_HEADS)}
        s16 = {key: val.astype(bf16) for key, val in s_old.items()}
        v_news, o_states = [], []
        for (gi, hd), sol, qe16 in zip(chains, sols, back["qe16"]):
            u = sol[:, :DN_D]
            w16 = sol[:, DN_D:].astype(bf16)
            vn, os_ = [], []
            for sg in range(segs):
                a0 = sg * tb
                st = s16[(stream_of(gi, sg), hd)]
                vn.append(u[a0:a0 + tb] - jnp.dot(w16[a0:a0 + tb], st, preferred_element_type=f32))
                os_.append(jnp.dot(qe16[a0:a0 + tb], st, preferred_element_type=f32))
            v_news.append(vn[0] if segs == 1 else jnp.concatenate(vn, axis=0))
            o_states.append(os_[0] if segs == 1 else jnp.concatenate(os_, axis=0))
        vn16s = [vn.astype(bf16) for vn in v_news]
        outs = [os_ + jnp.dot(a, vn, preferred_element_type=f32)
                for os_, a, vn in zip(o_states, back["attn"], vn16s)]

        s_new = {}
        for (gi, hd), kd, sd, vn in zip(chains, back["kdec"], back["sdecay"], vn16s):
            for sg in range(segs):
                a0 = sg * tb
                key = (stream_of(gi, sg), hd)
                s_new[key] = (s_old[key] * sd[sg]
                              + lax.dot_general(kd[sg], vn[a0:a0 + tb], _TN, preferred_element_type=f32))

        o_rows = []
        for gi in range(n_groups):
            o_heads = []
            for hd in range(DN_HEADS):
                o = outs[gi * DN_HEADS + hd]
                on = o * lax.rsqrt(jnp.mean(o * o, axis=-1, keepdims=True) + EPS) * dnw
                zd = zdn_scr[gi * GROUP:(gi + 1) * GROUP, hd * DN_D:(hd + 1) * DN_D]
                o_heads.append(on * _silu(zd))
            o_rows.append(jnp.concatenate(o_heads, axis=1))
        o_all = o_rows[0] if n_groups == 1 else jnp.concatenate(o_rows, axis=0)
        odn_ref[...] = o_all.astype(bf16).reshape(nb, tb, DN_WIDTH)
        s_ref[...] = jnp.stack([jnp.stack([s_new[(b, hd)] for hd in range(DN_HEADS)]) for b in range(nb)])

    n_ch = len(chains)
    if pipelined:
        a_norm_in()
        back_setup = ([functools.partial(b_load, ci) for ci in range(n_ch)]
                      + [functools.partial(b_mats, ci) for ci in range(n_ch)])
        per_piece = -(-len(back_setup) // len(projections))
        for piece in projections:
            piece()
            for thunk in back_setup[:per_piece]:
                thunk()
            del back_setup[:per_piece]
        sols = _unit_lower_solve(back["nmat"], back["rhs"], eye, m_ref, n_levels, fillers)
        for fill in fillers:
            fill(None)
        b_state(sols)
        a_store()
    else:
        a_norm_in()
        for piece in projections:
            piece()
        for fill in fillers:
            fill(None)
        a_store()
        for ci in range(n_ch):
            b_load(ci)
        for ci in range(n_ch):
            b_mats(ci)
        b_state(_unit_lower_solve(back["nmat"], back["rhs"], eye, m_ref, n_levels, []))


def _delta_masks(tb):
    r = np.arange(GROUP)[:, None]
    c = np.arange(GROUP)[None, :]
    same = (r // tb) == (c // tb)
    ms = [same & (r >= c), same & (r > c)]
    for l in range(tb.bit_length() - 1):
        ms.append(((r >> (l + 1)) == (c >> (l + 1))) & (((r >> l) & 1) == 1) & (((c >> l) & 1) == 0))
    return jnp.asarray(np.stack(ms).astype(np.float32))


def _delta_call(x, w1, w2, norm_w, conv_w, alog, dtb, dnw, conv0, s0, *, tb, pipelined):
    nb, T, _ = x.shape
    assert GROUP % tb == 0 and (nb * tb) % GROUP == 0 and T % tb == 0
    n_t = T // tb
    rows = nb * tb
    if pipelined:
        grid = (n_t + 1,)
        cur = lambda i: jnp.minimum(i, n_t - 1)
        prev = lambda i: jnp.maximum(i - 1, 0)
    else:
        grid = (n_t,)
        cur = prev = lambda i: i
    const2 = lambda i: (0, 0)
    const3 = lambda i: (0, 0, 0)
    const4 = lambda i: (0, 0, 0, 0)
    body = functools.partial(_delta_body, nb=nb, tb=tb, n_t=n_t, pipelined=pipelined)
    masks = _delta_masks(tb)
    one = pl.Buffered(1)
    return pl.pallas_call(
        body,
        grid=grid,
        in_specs=[
            pl.BlockSpec((nb, tb, D_MODEL), lambda i: (0, cur(i), 0)),
            pl.BlockSpec((D_MODEL, CONV_CH), const2, pipeline_mode=one),
            pl.BlockSpec((D_MODEL, W2_COLS), const2, pipeline_mode=one),
            pl.BlockSpec((1, D_MODEL), const2),
            pl.BlockSpec((CONV_K, CONV_CH), const2),
            pl.BlockSpec((1, LANES), const2),
            pl.BlockSpec((1, LANES), const2),
            pl.BlockSpec((1, DN_D), const2),
            pl.BlockSpec(masks.shape, const3, pipeline_mode=one),
            pl.BlockSpec((nb, SUBLANES, CONV_CH), const3),
            pl.BlockSpec((nb, DN_HEADS, DN_D, DN_D), const4),
        ],
        out_specs=[
            pl.BlockSpec((nb, tb, DN_WIDTH), lambda i: (0, prev(i), 0)),
            pl.BlockSpec((nb, S5_PLANES, tb, LANES), lambda i: (0, 0, cur(i), 0)),
            pl.BlockSpec((nb, tb, S5_WIDTH), lambda i: (0, cur(i), 0)),
            pl.BlockSpec((nb, SUBLANES, CONV_CH), const3),
            pl.BlockSpec((nb, DN_HEADS, DN_D, DN_D), const4),
        ],
        out_shape=[
            jax.ShapeDtypeStruct((nb, T, DN_WIDTH), bf16),
            jax.ShapeDtypeStruct((nb, S5_PLANES, T, LANES), f32),
            jax.ShapeDtypeStruct((nb, T, S5_WIDTH), f32),
            jax.ShapeDtypeStruct((nb, SUBLANES, CONV_CH), f32),
            jax.ShapeDtypeStruct((nb, DN_HEADS, DN_D, DN_D), f32),
        ],
        scratch_shapes=[
            pltpu.VMEM((nb, tb + SUBLANES, CONV_CH), f32),
            pltpu.VMEM((rows, DN_WIDTH), f32),
            pltpu.VMEM((rows, DN_WIDTH), f32),
            pltpu.VMEM((rows, DN_WIDTH), f32),
            pltpu.VMEM((rows, LANES), f32),
            pltpu.VMEM((rows, LANES), f32),
            pltpu.VMEM((rows, DN_WIDTH), f32),
        ],
        compiler_params=pltpu.CompilerParams(
            dimension_semantics=("arbitrary",), vmem_limit_bytes=VMEM_LIMIT),
        name="delta",
    )(x, w1, w2, norm_w, conv_w, alog, dtb, dnw, masks, conv0, s0)


S5MAT_GROUPS_PER_STEP = 8


def _split16(a):
    hi = a.astype(bf16)
    return hi, (a - hi.astype(f32)).astype(bf16)


def _s5mat_body(cp_ref, bp_ref, t_ref):
    ri = lax.broadcasted_iota(jnp.int32, (S5_GL, S5_GL), 0) >> 4
    ci = lax.broadcasted_iota(jnp.int32, (S5_GL, S5_GL), 1) >> 4
    causal = ri >= ci
    d = functools.partial(lax.dot_general, dimension_numbers=_NT, preferred_element_type=f32)
    for g in range(S5MAT_GROUPS_PER_STEP):
        ch, cl = _split16(cp_ref[g])
        bh, bl = _split16(bp_ref[g])
        t = d(ch, bh) + (d(ch, bl) + d(cl, bh))
        t_ref[g] = jnp.where(causal, t, 0.0).astype(bf16)


def _s5mat_call(cp, bp):
    n = S5MAT_GROUPS_PER_STEP
    return pl.pallas_call(
        _s5mat_body,
        grid=(S5_GROUPS // n,),
        in_specs=[
            pl.BlockSpec((n, S5_GL, 2 * S5_STATE), lambda g: (g, 0, 0)),
            pl.BlockSpec((n, S5_GL, 2 * S5_STATE), lambda g: (g, 0, 0)),
        ],
        out_specs=pl.BlockSpec((n, S5_GL, S5_GL), lambda g: (g, 0, 0)),
        out_shape=jax.ShapeDtypeStruct((S5_GROUPS, S5_GL, S5_GL), bf16),
        compiler_params=pltpu.CompilerParams(dimension_semantics=("arbitrary",)),
        name="s5mat",
    )(cp, bp)


def _s5_body(u_ref, z_ref, xr0_ref, xi0_ref, t_ref, wp_ref, vt_ref, ar_ref, ai_ref, d_ref,
             gw_ref, gb_ref,
             o_ref, xr_ref, xi_ref,
             at_scr, er_scr, ei_scr, xinr_scr, xini_scr, yt_scr, y_scr, *, tb, n_streams, n_chunks):
    R = tb // S5_L
    used = n_streams * n_chunks

    @pl.when(pl.program_id(1) == 0)
    def _init():
        xr_ref[...] = xr0_ref[...]
        xi_ref[...] = xi0_ref[...]

    for f in range(S5_L):
        for j in range(S5_PLANES):
            ut = u_ref[0, j, pl.ds(f, R, stride=S5_L), :]
            ut_t = ut.astype(bf16).T
            for gl in range(LANES // S5_GROUP):
                g = j * (LANES // S5_GROUP) + gl
                at_scr[g, f * S5_GROUP:(f + 1) * S5_GROUP, :] = ut_t[gl * S5_GROUP:(gl + 1) * S5_GROUP, :]

    for p in range(S5_PAIRS):
        a_t = at_scr[2 * p:2 * p + 2].reshape(2 * S5_GL, R)
        e = lax.dot_general(a_t, wp_ref[p], _TN, preferred_element_type=f32)
        er_scr[:, p * LANES:(p + 1) * LANES] = e[:, :LANES]
        ei_scr[:, p * LANES:(p + 1) * LANES] = e[:, LANES:]

    if used < R:
        xinr_scr[...] = jnp.zeros_like(xinr_scr)
        xini_scr[...] = jnp.zeros_like(xini_scr)

    a_r = ar_ref[...]
    a_i = ai_ref[...]
    for s in range(n_streams):
        def step(c, carry):
            xr, xi = carry
            row = s * n_chunks + c
            xinr_scr[pl.ds(row, 1), :] = xr
            xini_scr[pl.ds(row, 1), :] = xi
            er = er_scr[pl.ds(row, 1), :]
            ei = ei_scr[pl.ds(row, 1), :]
            return a_r * xr - a_i * xi + er, a_r * xi + a_i * xr + ei

        xr, xi = lax.fori_loop(0, n_chunks, step, (xr_ref[0, s:s + 1, :], xi_ref[0, s:s + 1, :]))
        xr_ref[0, s:s + 1, :] = xr
        xi_ref[0, s:s + 1, :] = xi

    for g in range(S5_GROUPS):
        p = g // 2
        xin = jnp.concatenate([xinr_scr[:, p * LANES:(p + 1) * LANES],
                               xini_scr[:, p * LANES:(p + 1) * LANES]], axis=1).astype(bf16)
        yt_scr[g] = (jnp.dot(t_ref[g], at_scr[g], preferred_element_type=f32)
                     + lax.dot_general(vt_ref[g], xin, _NT, preferred_element_type=f32))

    for f in range(S5_L):
        gpp = LANES // S5_GROUP
        for j in range(S5_PLANES):
            yt = yt_scr[j * gpp:(j + 1) * gpp, f * S5_GROUP:(f + 1) * S5_GROUP, :].reshape(LANES, R)
            y_scr[j, pl.ds(f, R, stride=S5_L), :] = yt.T

    for rb in range(tb // OUT_SUB):
        sl = pl.ds(rb * OUT_SUB, OUT_SUB)
        y_intra = jnp.concatenate([y_scr[j, sl, :] for j in range(S5_PLANES)], axis=1)
        u_nat = jnp.concatenate([u_ref[0, j, sl, :] for j in range(S5_PLANES)], axis=1)
        y = y_intra + d_ref[...] * u_nat
        gy = _gelu_tanh(y)
        gate = _sigmoid(_bdot(gy, gw_ref[...]) + gb_ref[...])
        o_ref[0, sl, :] = (gy * gate * _silu(z_ref[0, sl, :])).astype(bf16)


def _s5_call(u, z, xr0, xi0, tmat, wp, vt, ar, ai, dvec, gw, gb, *, tb, n_streams, n_chunks):
    G, T, _ = z.shape
    grid = (G, T // tb)
    R = tb // S5_L
    nstate = S5_GROUPS * S5_STATE
    c2 = lambda g, t: (0, 0)
    c3 = lambda g, t: (0, 0, 0)
    body = functools.partial(_s5_body, tb=tb, n_streams=n_streams, n_chunks=n_chunks)
    one = pl.Buffered(1)
    return pl.pallas_call(
        body,
        grid=grid,
        in_specs=[
            pl.BlockSpec((1, S5_PLANES, tb, LANES), lambda g, t: (g, 0, t, 0)),
            pl.BlockSpec((1, tb, S5_WIDTH), lambda g, t: (g, t, 0)),
            pl.BlockSpec((1, n_streams, nstate), lambda g, t: (g, 0, 0)),
            pl.BlockSpec((1, n_streams, nstate), lambda g, t: (g, 0, 0)),
            pl.BlockSpec((S5_GROUPS, S5_GL, S5_GL), c3, pipeline_mode=one),
            pl.BlockSpec((S5_PAIRS, 2 * S5_GL, 2 * LANES), c3, pipeline_mode=one),
            pl.BlockSpec((S5_GROUPS, S5_GL, 2 * LANES), c3, pipeline_mode=one),
            pl.BlockSpec((1, nstate), c2),
            pl.BlockSpec((1, nstate), c2),
            pl.BlockSpec((1, S5_WIDTH), c2),
            pl.BlockSpec((S5_WIDTH, S5_WIDTH), c2),
            pl.BlockSpec((1, S5_WIDTH), c2),
        ],
        out_specs=[
            pl.BlockSpec((1, tb, S5_WIDTH), lambda g, t: (g, t, 0)),
            pl.BlockSpec((1, n_streams, nstate), lambda g, t: (g, 0, 0)),
            pl.BlockSpec((1, n_streams, nstate), lambda g, t: (g, 0, 0)),
        ],
        out_shape=[
            jax.ShapeDtypeStruct((G, T, S5_WIDTH), bf16),
            jax.ShapeDtypeStruct((G, n_streams, nstate), f32),
            jax.ShapeDtypeStruct((G, n_streams, nstate), f32),
        ],
        scratch_shapes=[
            pltpu.VMEM((S5_GROUPS, S5_GL, R), bf16),
            pltpu.VMEM((R, nstate), f32),
            pltpu.VMEM((R, nstate), f32),
            pltpu.VMEM((R, nstate), f32),
            pltpu.VMEM((R, nstate), f32),
            pltpu.VMEM((S5_GROUPS, S5_GL, R), f32),
            pltpu.VMEM((S5_PLANES, tb, LANES), f32),
        ],
        compiler_params=pltpu.CompilerParams(
            dimension_semantics=("arbitrary", "arbitrary"), vmem_limit_bytes=VMEM_LIMIT),
        name="s5",
    )(u, z, xr0, xi0, tmat, wp, vt, ar, ai, dvec, gw, gb)


def _out_body(x_ref, odn_ref, os5_ref, wo_ref, fw_ref, y_ref):
    acc = (x_ref[...]
           + jnp.dot(odn_ref[...], wo_ref[0:DN_WIDTH, :], preferred_element_type=f32)
           + jnp.dot(os5_ref[...], wo_ref[DN_WIDTH:, :], preferred_element_type=f32))
    ms = jnp.mean(acc * acc, axis=-1, keepdims=True)
    y_ref[...] = acc * lax.rsqrt(ms + EPS) * fw_ref[...]


def _out_call(x2, odn2, os52, w_out16, fw, *, rows):
    n = x2.shape[0]
    return pl.pallas_call(
        _out_body,
        grid=(n // rows,),
        in_specs=[
            pl.BlockSpec((rows, D_MODEL), lambda i: (i, 0)),
            pl.BlockSpec((rows, DN_WIDTH), lambda i: (i, 0)),
            pl.BlockSpec((rows, S5_WIDTH), lambda i: (i, 0)),
            pl.BlockSpec((D_MODEL, D_MODEL), lambda i: (0, 0)),
            pl.BlockSpec((1, D_MODEL), lambda i: (0, 0)),
        ],
        out_specs=pl.BlockSpec((rows, D_MODEL), lambda i: (i, 0)),
        out_shape=jax.ShapeDtypeStruct((n, D_MODEL), f32),
        compiler_params=pltpu.CompilerParams(
            dimension_semantics=("arbitrary",), vmem_limit_bytes=VMEM_LIMIT),
        name="outproj",
    )(x2, odn2, os52, w_out16, fw)


def _s5_operands(a_re, a_im, log_dt, b_re, b_im, c_re, c_im):
    lam_re = jnp.minimum(a_re, -1e-4)
    lam_im = a_im
    dt = jnp.exp(log_dt)[:, None]
    ldt_re, ldt_im = lam_re * dt, lam_im * dt

    def lpow(k):
        mag = jnp.exp(ldt_re * k)
        return mag * jnp.cos(ldt_im * k), mag * jnp.sin(ldt_im * k)

    lb_re, lb_im = lpow(1.0)
    den = lam_re * lam_re + lam_im * lam_im
    f_re = ((lb_re - 1.0) * lam_re + lb_im * lam_im) / den
    f_im = (lb_im * lam_re - (lb_re - 1.0) * lam_im) / den
    bb_re = f_re[..., None] * b_re - f_im[..., None] * b_im
    bb_im = f_re[..., None] * b_im + f_im[..., None] * b_re

    fr = jnp.arange(S5_L, dtype=f32)
    G, N = S5_GROUPS, S5_STATE

    def cmul(ar, ai, br, bi):
        return ar * br - ai * bi, ar * bi + ai * br

    pr, pi = lpow(fr[:, None, None])
    pr, pi = pr.transpose(1, 0, 2), pi.transpose(1, 0, 2)
    cpr, cpi = cmul(c_re[:, None], c_im[:, None], pr[:, :, None], pi[:, :, None])
    cpr, cpi = cpr.reshape(G, S5_GL, N), cpi.reshape(G, S5_GL, N)
    bbr_t, bbi_t = bb_re.transpose(0, 2, 1), bb_im.transpose(0, 2, 1)
    qr, qi = lpow(-fr[:, None, None])
    qr, qi = qr.transpose(1, 0, 2), qi.transpose(1, 0, 2)
    bpr, bpi = cmul(qr[:, :, None], qi[:, :, None], bbr_t[:, None], bbi_t[:, None])
    bpr, bpi = bpr.reshape(G, S5_GL, N), bpi.reshape(G, S5_GL, N)

    wr_, wi_ = lpow((S5_L - 1.0) - fr[:, None, None])
    wr_, wi_ = wr_.transpose(1, 0, 2), wi_.transpose(1, 0, 2)
    w_re, w_im = cmul(wr_[:, :, None], wi_[:, :, None], bbr_t[:, None], bbi_t[:, None])
    w_re, w_im = w_re.reshape(G, S5_GL, N), w_im.reshape(G, S5_GL, N)
    vr_, vi_ = lpow(fr[:, None, None] + 1.0)
    vr_, vi_ = vr_.transpose(1, 0, 2), vi_.transpose(1, 0, 2)
    v_re, v_im = cmul(c_re[:, None], c_im[:, None], vr_[:, :, None], vi_[:, :, None])
    v_re, v_im = v_re.reshape(G, S5_GL, N), v_im.reshape(G, S5_GL, N)

    zeros = jnp.zeros((S5_PAIRS, S5_GL, N), f32)

    def pair_cols(re, im):
        re, im = re.reshape(S5_PAIRS, 2, S5_GL, N), im.reshape(S5_PAIRS, 2, S5_GL, N)
        even = jnp.concatenate([re[:, 0], zeros, im[:, 0], zeros], axis=-1)
        odd = jnp.concatenate([zeros, re[:, 1], zeros, im[:, 1]], axis=-1)
        return jnp.stack([even, odd], axis=1)

    wp = pair_cols(w_re, w_im).reshape(S5_PAIRS, 2 * S5_GL, 2 * LANES).astype(bf16)
    vt = pair_cols(v_re, -v_im).reshape(S5_GROUPS, S5_GL, 2 * LANES).astype(bf16)
    a16r, a16i = lpow(float(S5_L))
    cp = jnp.concatenate([cpr, cpi], axis=2)
    bp = jnp.concatenate([bpr, -bpi], axis=2)
    return cp, bp, wp, vt, a16r.reshape(1, G * N), a16i.reshape(1, G * N)


def _pad_lanes(v, start=0):
    out = jnp.zeros((1, LANES), f32)
    return out.at[0, start:start + v.shape[0]].set(v.astype(f32))


def _layer(x, conv0, s0, xr0, xi0, prm, *, delta_cfg, s5_cfg):
    B, T, _ = x.shape
    conv0p = jnp.concatenate([jnp.zeros((B, SUBLANES - (CONV_K - 1), CONV_CH), f32), conv0], axis=1)
    odn, us5, zs5, convout, s_new = _delta_call(
        x, prm["w1"], prm["w2"], prm["norm_w"], prm["conv_w"], prm["alog"], prm["dtb"], prm["dnw"],
        conv0p, s0, **delta_cfg)

    nstate = S5_GROUPS * S5_STATE
    if s5_cfg["flatten"]:
        tb = s5_cfg["tb"]
        uf = us5.transpose(1, 0, 2, 3).reshape(1, S5_PLANES, B * T, LANES)
        zf = zs5.reshape(1, B * T, S5_WIDTH)
        pad = tb - B * T
        uf = jnp.pad(uf, ((0, 0), (0, 0), (0, pad), (0, 0)))
        zf = jnp.pad(zf, ((0, 0), (0, pad), (0, 0)))
        os5, xr, xi = _s5_call(uf, zf, xr0.reshape(1, B, nstate), xi0.reshape(1, B, nstate),
                               prm["tmat"], prm["wp"], prm["vt"], prm["ar"], prm["ai"], prm["dvec"],
                               prm["gw"], prm["gb"], tb=tb, n_streams=B, n_chunks=T // S5_L)
        os5 = os5[0, :B * T].reshape(B, T, S5_WIDTH)
    else:
        tb = s5_cfg["tb"]
        os5, xr, xi = _s5_call(us5, zs5, xr0.reshape(B, 1, nstate), xi0.reshape(B, 1, nstate),
                               prm["tmat"], prm["wp"], prm["vt"], prm["ar"], prm["ai"], prm["dvec"],
                               prm["gw"], prm["gb"], tb=tb, n_streams=1, n_chunks=tb // S5_L)

    y = _out_call(x.reshape(B * T, D_MODEL), odn.reshape(B * T, DN_WIDTH), os5.reshape(B * T, S5_WIDTH),
                  prm["w_out"], prm["fw"], rows=min(OUT_ROWS, B * T))
    return (y.reshape(B, T, D_MODEL), convout[:, SUBLANES - (CONV_K - 1):, :], s_new,
            xr.reshape(B, S5_GROUPS, S5_STATE), xi.reshape(B, S5_GROUPS, S5_STATE))


def kernel(x_prompt, x_sample, cache_conv, state_dn, state_s5_re, state_s5_im, norm_w, w_in, conv_w, dn_A_log, dn_dt_bias, dn_norm_w, s5_A_re, s5_A_im, s5_log_dt, s5_B_re, s5_B_im, s5_C_re, s5_C_im, s5_D, glu_w, glu_b, w_out, final_norm_w):
    depth = norm_w.shape[0]
    assert depth == 1
    l = 0
    assert w_in.shape[-1] == IN_COLS
    w1 = w_in[l, :, :CONV_CH].astype(bf16)
    w2 = w_in[l, :, IN_COLS - W2_COLS:].astype(bf16)
    cp, bp, wp, vt, ar, ai = _s5_operands(
        s5_A_re[l].astype(f32), s5_A_im[l].astype(f32), s5_log_dt[l].astype(f32),
        s5_B_re[l].astype(f32), s5_B_im[l].astype(f32), s5_C_re[l].astype(f32), s5_C_im[l].astype(f32))
    prm = dict(
        w1=w1, w2=w2,
        norm_w=norm_w[l].reshape(1, D_MODEL).astype(f32),
        conv_w=conv_w[l].astype(f32),
        alog=_pad_lanes(dn_A_log[l], AB_LANE),
        dtb=_pad_lanes(dn_dt_bias[l], AB_LANE),
        dnw=dn_norm_w[l].reshape(1, DN_D).astype(f32),
        tmat=_s5mat_call(cp, bp),
        wp=wp, vt=vt, ar=ar, ai=ai,
        dvec=s5_D[l].reshape(1, S5_WIDTH).astype(f32),
        gw=glu_w[l].astype(bf16),
        gb=glu_b[l].reshape(1, S5_WIDTH).astype(f32),
        w_out=w_out[l].astype(bf16),
        fw=final_norm_w.reshape(1, D_MODEL).astype(f32),
    )

    bp = x_prompt.shape[0]
    yp, c1, d1, r1, i1 = _layer(
        x_prompt,
        jnp.zeros((bp, CONV_K - 1, CONV_CH), f32),
        jnp.zeros((bp, DN_HEADS, DN_D, DN_D), f32),
        jnp.zeros((bp, S5_GROUPS, S5_STATE), f32),
        jnp.zeros((bp, S5_GROUPS, S5_STATE), f32),
        prm,
        delta_cfg=dict(tb=256, pipelined=True),
        s5_cfg=dict(flatten=False, tb=2048))
    ys, c2, d2, r2, i2 = _layer(
        x_sample, cache_conv[l].astype(f32), state_dn[l].astype(f32),
        state_s5_re[l].astype(f32), state_s5_im[l].astype(f32),
        prm,
        delta_cfg=dict(tb=32, pipelined=False),
        s5_cfg=dict(flatten=True, tb=2048))

    return (yp, ys, c1[None], d1[None], r1[None], i1[None], c2[None], d2[None], r2[None], i2[None])
```

```python
import functools
import math

import jax
import jax.numpy as jnp
import numpy as np
from jax import lax
from jax.experimental import pallas as pl
from jax.experimental.pallas import tpu as pltpu

bf16 = jnp.bfloat16
f32 = jnp.float32

LANES = 128
SUBLANES = 8

D_MODEL = 1024
DN_HEADS = 4
DN_D = 128
DN_WIDTH = DN_HEADS * DN_D
CONV_K = 4
CONV_CH = 3 * DN_WIDTH
S5_WIDTH = 512
S5_GROUP = 16
S5_GROUPS = 32
S5_STATE = 64
S5_PAIRS = S5_GROUPS // 2
S5_L = 16
S5_GL = S5_GROUP * S5_L
OUT_SUB = 256
OUT_ROWS = 2048
S5_PLANES = S5_WIDTH // LANES
EPS = 1e-6
L2_EPS = 1e-6

IN_COLS = CONV_CH + 2 * DN_HEADS + DN_WIDTH + 2 * S5_WIDTH
W2_COLS = LANES + DN_WIDTH + 2 * S5_WIDTH
AB_LANE = LANES - 2 * DN_HEADS
W2_ZDN = LANES
W2_US5 = W2_ZDN + DN_WIDTH
W2_ZS5 = W2_US5 + S5_WIDTH

GROUP = 256
MASK_INCL, MASK_STRICT, MASK_OFF = 0, 1, 2
FOLD_MIN_ROWS = 64
VMEM_LIMIT =56 * 1024 * 1024

_NT = (((1,), (1,)), ((), ()))
_TN = (((0,), (0,)), ((), ()))


def _bdot(a, b):
    return jnp.dot(a.astype(bf16), b.astype(bf16), preferred_element_type=f32)


def _bdot_g(a, b, dims):
    return lax.dot_general(a.astype(bf16), b.astype(bf16), dims, preferred_element_type=f32)


def _sigmoid(x):
    return 1.0 / (1.0 + jnp.exp2(x * (-math.log2(math.e))))


def _silu(x):
    return x * _sigmoid(x)


def _softplus(x):
    return jnp.maximum(x, 0.0) + jnp.log1p(jnp.exp(-jnp.abs(x)))


def _gelu_tanh(x):
    c = math.sqrt(2.0 / math.pi)
    return x * (0.5 + 0.5 * jnp.tanh(x * (c + (c * 0.044715) * (x * x))))


def _zero_after(x):
    bits = pltpu.bitcast(x[0:SUBLANES, 0:LANES].astype(f32), jnp.uint32)
    half = jnp.uint32(16)
    bits = lax.shift_right_logical(lax.shift_right_logical(bits, half), half)
    return pltpu.bitcast(bits, f32)[0:1, :]


def _unit_lower_solve(nmats, rhss, eye, m_ref, n_levels, fillers):
    d32 = functools.partial(jnp.dot, preferred_element_type=f32)

    def d16(a, b):
        return d32(a, b).astype(bf16)

    def fill(after):
        if fillers:
            fillers.pop(0)(_zero_after(after[-1]))

    def fold(x, rows):
        out = x[0:rows]
        for k in range(1, GROUP // rows):
            out = out + x[k * rows:(k + 1) * rows]
        return out

    def unfold(xf, rows):
        lane_blk = lax.broadcasted_iota(jnp.int32, (rows, GROUP), 1) >> (rows.bit_length() - 1)
        return jnp.concatenate([jnp.where(lane_blk == k, xf, jnp.zeros_like(xf))
                                for k in range(GROUP // rows)], axis=0)

    n16 = [n.astype(bf16) for n in nmats]
    invs = [(eye - n * m_ref[MASK_OFF]).astype(bf16) for n in nmats]
    for lvl in range(1, n_levels):
        rows = min(GROUP, max(FOLD_MIN_ROWS, 2 ** (lvl + 1)))
        mask16 = m_ref[MASK_OFF + lvl].astype(bf16)
        lhs = invs if rows == GROUP else [fold(inv, rows) for inv in invs]
        t1 = [d16(l, n * mask16) for l, n in zip(lhs, n16)]
        fill(invs)
        t2 = [d16(t, inv) for t, inv in zip(t1, invs)]
        fill(t1)
        if rows != GROUP:
            t2 = [unfold(t, rows) for t in t2]
        invs = [inv - t for inv, t in zip(invs, t2)]
    return [d32(inv, r.astype(bf16)) for inv, r in zip(invs, rhss)]


def _segment_cumsum(tri16, g):
    g1 = g.astype(bf16)
    r1 = g - g1.astype(f32)
    g2 = r1.astype(bf16)
    g3 = (r1 - g2.astype(f32)).astype(bf16)
    s = jnp.dot(tri16, jnp.concatenate([g1, g2, g3], axis=1), preferred_element_type=f32)
    n = g.shape[1]
    return s[:, :n] + (s[:, n:2 * n] + s[:, 2 * n:])


def _delta_body(x_ref, w1_ref, w2_ref, nw_ref, cw_ref, alog_ref, dtb_ref, dnw_ref, m_ref, conv0_ref, s0_ref,
                odn_ref, us5_ref, zs5_ref, convout_ref, s_ref,
                cbuf, qn_scr, kn_scr, v_scr, gc_scr, beta_scr, zdn_scr, *, nb, tb, n_t, pipelined):
    rows = nb * tb
    n_groups = rows // GROUP
    segs = GROUP // tb
    n_levels = tb.bit_length() - 1
    step = pl.program_id(0)
    fresh = step < n_t

    @pl.when(step == 0)
    def _init():
        cbuf[:, 0:SUBLANES, :] = conv0_ref[...]
        s_ref[...] = s0_ref[...]
        if pipelined:
            for scr in (qn_scr, kn_scr, v_scr, gc_scr, beta_scr, zdn_scr):
                scr[...] = jnp.zeros_like(scr)

    front = {}

    def project(w_ref, lo, width):
        return jnp.dot(front["h"], w_ref[:, lo:lo + width], preferred_element_type=f32)

    def a_norm_in():
        x = x_ref[...].reshape(rows, D_MODEL)
        ms = jnp.mean(x * x, axis=-1, keepdims=True)
        front["h"] = (x * lax.rsqrt(ms + EPS) * nw_ref[...]).astype(bf16)

    def a_project_qkv(s):
        front[("raw", s)] = project(w1_ref, s * DN_WIDTH, DN_WIDTH)

    def a_project_us5():
        u = project(w2_ref, W2_US5, S5_WIDTH)
        for j in range(S5_PLANES):
            us5_ref[:, j] = u[:, j * LANES:(j + 1) * LANES].reshape(nb, tb, LANES)

    def a_project_zs5():
        zs5_ref[...] = project(w2_ref, W2_ZS5, S5_WIDTH).reshape(nb, tb, S5_WIDTH)

    def a_project_zdn():
        front["zdn"] = project(w2_ref, W2_ZDN, DN_WIDTH)

    def a_project_gates():
        ab = project(w2_ref, 0, LANES)
        front["g"] = -jnp.exp(alog_ref[...]) * _softplus(ab + dtb_ref[...])
        front["beta"] = _sigmoid(ab)

    projections = ([functools.partial(a_project_qkv, s) for s in range(3)]
                   + [a_project_us5, a_project_zs5, a_project_zdn, a_project_gates])

    def a_conv(b, s, zero):
        cols = slice(s * DN_WIDTH, (s + 1) * DN_WIDTH)
        cbuf[b, SUBLANES:SUBLANES + tb, cols] = front[("raw", s)][b * tb:(b + 1) * tb, :]
        taps = [cw_ref[j:j + 1, cols] for j in range(CONV_K)]
        if zero is not None:
            zero_w = jnp.concatenate([zero] * (DN_WIDTH // LANES), axis=1)
            taps = [t + zero_w for t in taps]
        acc = cbuf[b, 5:5 + tb, cols] * taps[0]
        for j in range(1, CONV_K):
            acc = acc + cbuf[b, 5 + j:5 + j + tb, cols] * taps[j]
        front[("act", b, s)] = _silu(acc)
        tail = cbuf[b, tb:tb + SUBLANES, cols]
        if pipelined:
            tail = jnp.where(fresh, tail, cbuf[b, 0:SUBLANES, cols])
        cbuf[b, 0:SUBLANES, cols] = tail
        convout_ref[b, :, cols] = tail

    def a_norm(b, hd, zero):
        lo = hd * DN_D
        q = front[("act", b, 0)][:, lo:lo + DN_D]
        k = front[("act", b, 1)][:, lo:lo + DN_D]
        if zero is not None:
            q, k = q + zero, k + zero
        front[("qn", b, hd)] = (q * lax.rsqrt(jnp.sum(q * q, axis=-1, keepdims=True) + L2_EPS)
                                * (DN_D ** -0.5))
        front[("kn", b, hd)] = k * lax.rsqrt(jnp.sum(k * k, axis=-1, keepdims=True) + L2_EPS)

    def a_gates(zero):
        tri16 = m_ref[MASK_INCL].astype(bf16)
        g = front["g"] if zero is None else front["g"] + zero
        front["gc"] = [_segment_cumsum(tri16, g[gi * GROUP:(gi + 1) * GROUP])
                       for gi in range(n_groups)]

    def a_store():
        for b in range(nb):
            r = slice(b * tb, (b + 1) * tb)
            for hd in range(DN_HEADS):
                c = slice(hd * DN_D, (hd + 1) * DN_D)
                qn_scr[r, c] = front[("qn", b, hd)]
                kn_scr[r, c] = front[("kn", b, hd)]
            v_scr[r, :] = front[("act", b, 2)]
        for gi in range(n_groups):
            gc_scr[gi * GROUP:(gi + 1) * GROUP, :] = front["gc"][gi]
        beta_scr[...] = front["beta"]
        zdn_scr[...] = front["zdn"]

    fillers = ([functools.partial(a_conv, b, s) for b in range(nb) for s in range(3)] + [a_gates]
               + [functools.partial(a_norm, b, hd) for b in range(nb) for hd in range(DN_HEADS)])

    incl = m_ref[MASK_INCL]
    strict = m_ref[MASK_STRICT]
    eye = incl - strict
    dnw = dnw_ref[...]
    chains = [(gi, hd) for gi in range(n_groups) for hd in range(DN_HEADS)]
    back = {}

    def stream_of(gi, sg):
        return (gi * GROUP) // tb + sg

    for name in ("q16", "qe16", "k16", "kb16", "rhs", "dmat", "kdec", "sdecay", "nmat", "attn"):
        back[name] = [None] * len(chains)

    def b_load(ci):
        gi, hd = chains[ci]
        r = slice(gi * GROUP, (gi + 1) * GROUP)
        c = slice(hd * DN_D, (hd + 1) * DN_D)
        q, k, v = qn_scr[r, c], kn_scr[r, c], v_scr[r, c]
        beta = beta_scr[r, AB_LANE + DN_HEADS + hd:AB_LANE + DN_HEADS + hd + 1]
        g = gc_scr[r, AB_LANE + hd:AB_LANE + hd + 1]
        eg = jnp.exp(g)
        kb = k * beta
        gcb = jnp.broadcast_to(g, (GROUP, GROUP))
        back["dmat"][ci] = jnp.exp((gcb - gcb.T) * incl) * incl
        back["q16"][ci] = q.astype(bf16)
        back["qe16"][ci] = (q * eg).astype(bf16)
        back["k16"][ci] = k.astype(bf16)
        back["kb16"][ci] = kb.astype(bf16)
        back["rhs"][ci] = jnp.concatenate([v * beta, kb * eg], axis=1)
        kd, sd = [], []
        for sg in range(segs):
            a0 = sg * tb
            glast = g[a0 + tb - 1:a0 + tb, :]
            kd.append((k[a0:a0 + tb] * jnp.exp(glast - g[a0:a0 + tb])).astype(bf16))
            sd.append(jnp.exp(glast))
        back["kdec"][ci] = kd
        back["sdecay"][ci] = sd

    def b_mats(ci):
        d = functools.partial(lax.dot_general, dimension_numbers=_NT, preferred_element_type=f32)
        dm = back["dmat"][ci]
        back["nmat"][ci] = d(back["kb16"][ci], back["k16"][ci]) * (dm * strict)
        back["attn"][ci] = (d(back["q16"][ci], back["k16"][ci]) * dm).astype(bf16)

    def b_state(sols):
        s_old = {(b, hd): s_ref[b, hd] for b in range(nb) for hd in range(DN_HEADS)}
        s16 = {key: val.astype(bf16) for key, val in s_old.items()}
        v_news, o_states = [], []
        for (gi, hd), sol, qe16 in zip(chains, sols, back["qe16"]):
            u = sol[:, :DN_D]
            w16 = sol[:, DN_D:].astype(bf16)
            vn, os_ = [], []
            for sg in range(segs):
                a0 = sg * tb
                st = s16[(stream_of(gi, sg), hd)]
                vn.append(u[a0:a0 + tb] - jnp.dot(w16[a0:a0 + tb], st, preferred_element_type=f32))
                os_.append(jnp.dot(qe16[a0:a0 + tb], st, preferred_element_type=f32))
            v_news.append(vn[0] if segs == 1 else jnp.concatenate(vn, axis=0))
            o_states.append(os_[0] if segs == 1 else jnp.concatenate(os_, axis=0))
        vn16s = [vn.astype(bf16) for vn in v_news]
        outs = [os_ + jnp.dot(a, vn, preferred_element_type=f32)
                for os_, a, vn in zip(o_states, back["attn"], vn16s)]

        s_new = {}
        for (gi, hd), kd, sd, vn in zip(chains, back["kdec"], back["sdecay"], vn16s):
            for sg in range(segs):
                a0 = sg * tb
                key = (stream_of(gi, sg), hd)
                s_new[key] = (s_old[key] * sd[sg]
                              + lax.dot_general(kd[sg], vn[a0:a0 + tb], _TN, preferred_element_type=f32))

        o_rows = []
        for gi in range(n_groups):
            o_heads = []
            for hd in range(DN_HEADS):
                o = outs[gi * DN_HEADS + hd]
                on = o * lax.rsqrt(jnp.mean(o * o, axis=-1, keepdims=True) + EPS) * dnw
                zd = zdn_scr[gi * GROUP:(gi + 1) * GROUP, hd * DN_D:(hd + 1) * DN_D]
                o_heads.append(on * _silu(zd))
            o_rows.append(jnp.concatenate(o_heads, axis=1))
        o_all = o_rows[0] if n_groups == 1 else jnp.concatenate(o_rows, axis=0)
        odn_ref[...] = o_all.astype(bf16).reshape(nb, tb, DN_WIDTH)
        s_ref[...] = jnp.stack([jnp.stack([s_new[(b, hd)] for hd in range(DN_HEADS)]) for b in range(nb)])

    n_ch = len(chains)
    if pipelined:
        a_norm_in()
        back_setup = ([functools.partial(b_load, ci) for ci in range(n_ch)]
                      + [functools.partial(b_mats, ci) for ci in range(n_ch)])
        convs = [fillers.pop(0) for _ in range(nb * 3)]
        per_piece = len(back_setup) // len(projections)
        for idx, piece in enumerate(projections):
            piece()
            for thunk in back_setup[:per_piece]:
                thunk()
            del back_setup[:per_piece]
            if idx >= 3 and convs:
                convs.pop(0)(None)
        while back_setup or convs:
            if back_setup:
                back_setup.pop(0)()
            if convs:
                convs.pop(0)(None)
        sols = _unit_lower_solve(back["nmat"], back["rhs"], eye, m_ref, n_levels, fillers)
        for fill in fillers:
            fill(None)
        b_state(sols)
        a_store()
    else:
        a_norm_in()
        for piece in projections:
            piece()
        for fill in fillers:
            fill(None)
        a_store()
        for ci in range(n_ch):
            b_load(ci)
        for ci in range(n_ch):
            b_mats(ci)
        b_state(_unit_lower_solve(back["nmat"], back["rhs"], eye, m_ref, n_levels, []))


def _delta_masks(tb):
    r = np.arange(GROUP)[:, None]
    c = np.arange(GROUP)[None, :]
    same = (r // tb) == (c // tb)
    ms = [same & (r >= c), same & (r > c)]
    for l in range(tb.bit_length() - 1):
        ms.append(((r >> (l + 1)) == (c >> (l + 1))) & (((r >> l) & 1) == 1) & (((c >> l) & 1) == 0))
    return jnp.asarray(np.stack(ms).astype(np.float32))


def _delta_call(x, w1, w2, norm_w, conv_w, alog, dtb, dnw, conv0, s0, *, tb, pipelined):
    nb, T, _ = x.shape
    assert GROUP % tb == 0 and (nb * tb) % GROUP == 0 and T % tb == 0
    n_t = T // tb
    rows = nb * tb
    if pipelined:
        grid = (n_t + 1,)
        cur = lambda i: jnp.minimum(i, n_t - 1)
        prev = lambda i: jnp.maximum(i - 1, 0)
    else:
        grid = (n_t,)
        cur = prev = lambda i: i
    const2 = lambda i: (0, 0)
    const3 = lambda i: (0, 0, 0)
    const4 = lambda i: (0, 0, 0, 0)
    body = functools.partial(_delta_body, nb=nb, tb=tb, n_t=n_t, pipelined=pipelined)
    masks = _delta_masks(tb)
    one = pl.Buffered(1)
    return pl.pallas_call(
        body,
        grid=grid,
        in_specs=[
            pl.BlockSpec((nb, tb, D_MODEL), lambda i: (0, cur(i), 0)),
            pl.BlockSpec((D_MODEL, CONV_CH), const2, pipeline_mode=one),
            pl.BlockSpec((D_MODEL, W2_COLS), const2, pipeline_mode=one),
            pl.BlockSpec((1, D_MODEL), const2),
            pl.BlockSpec((CONV_K, CONV_CH), const2),
            pl.BlockSpec((1, LANES), const2),
            pl.BlockSpec((1, LANES), const2),
            pl.BlockSpec((1, DN_D), const2),
            pl.BlockSpec(masks.shape, const3, pipeline_mode=one),
            pl.BlockSpec((nb, SUBLANES, CONV_CH), const3),
            pl.BlockSpec((nb, DN_HEADS, DN_D, DN_D), const4),
        ],
        out_specs=[
            pl.BlockSpec((nb, tb, DN_WIDTH), lambda i: (0, prev(i), 0)),
            pl.BlockSpec((nb, S5_PLANES, tb, LANES), lambda i: (0, 0, cur(i), 0)),
            pl.BlockSpec((nb, tb, S5_WIDTH), lambda i: (0, cur(i), 0)),
            pl.BlockSpec((nb, SUBLANES, CONV_CH), const3),
            pl.BlockSpec((nb, DN_HEADS, DN_D, DN_D), const4),
        ],
        out_shape=[
            jax.ShapeDtypeStruct((nb, T, DN_WIDTH), bf16),
            jax.ShapeDtypeStruct((nb, S5_PLANES, T, LANES), f32),
            jax.ShapeDtypeStruct((nb, T, S5_WIDTH), f32),
            jax.ShapeDtypeStruct((nb, SUBLANES, CONV_CH), f32),
            jax.ShapeDtypeStruct((nb, DN_HEADS, DN_D, DN_D), f32),
        ],
        scratch_shapes=[
            pltpu.VMEM((nb, tb + SUBLANES, CONV_CH), f32),
            pltpu.VMEM((rows, DN_WIDTH), f32),
            pltpu.VMEM((rows, DN_WIDTH), f32),
            pltpu.VMEM((rows, DN_WIDTH), f32),
            pltpu.VMEM((rows, LANES), f32),
            pltpu.VMEM((rows, LANES), f32),
            pltpu.VMEM((rows, DN_WIDTH), f32),
        ],
        compiler_params=pltpu.CompilerParams(
            dimension_semantics=("arbitrary",), vmem_limit_bytes=VMEM_LIMIT),
        name="delta",
    )(x, w1, w2, norm_w, conv_w, alog, dtb, dnw, masks, conv0, s0)


S5MAT_GROUPS_PER_STEP = 8


def _split16(a):
    hi = a.astype(bf16)
    return hi, (a - hi.astype(f32)).astype(bf16)


def _s5mat_body(cp_ref, bp_ref, t_ref):
    ri = lax.broadcasted_iota(jnp.int32, (S5_GL, S5_GL), 0) >> 4
    ci = lax.broadcasted_iota(jnp.int32, (S5_GL, S5_GL), 1) >> 4
    causal = ri >= ci
    d = functools.partial(lax.dot_general, dimension_numbers=_NT, preferred_element_type=f32)
    for g in range(S5MAT_GROUPS_PER_STEP):
        ch, cl = _split16(cp_ref[g])
        bh, bl = _split16(bp_ref[g])
        t = d(ch, bh) + (d(ch, bl) + d(cl, bh))
        t_ref[g] = jnp.where(causal, t, 0.0).astype(bf16)


def _s5mat_call(cp, bp):
    n = S5MAT_GROUPS_PER_STEP
    return pl.pallas_call(
        _s5mat_body,
        grid=(S5_GROUPS // n,),
        in_specs=[
            pl.BlockSpec((n, S5_GL, 2 * S5_STATE), lambda g: (g, 0, 0)),
            pl.BlockSpec((n, S5_GL, 2 * S5_STATE), lambda g: (g, 0, 0)),
        ],
        out_specs=pl.BlockSpec((n, S5_GL, S5_GL), lambda g: (g, 0, 0)),
        out_shape=jax.ShapeDtypeStruct((S5_GROUPS, S5_GL, S5_GL), bf16),
        compiler_params=pltpu.CompilerParams(dimension_semantics=("arbitrary",)),
        name="s5mat",
    )(cp, bp)


def _s5_body(u_ref, z_ref, xr0_ref, xi0_ref, t_ref, wp_ref, vt_ref, ar_ref, ai_ref, d_ref,
             gw_ref, gb_ref,
             o_ref, xr_ref, xi_ref,
             at_scr, er_scr, ei_scr, xinr_scr, xini_scr, yt_scr, y_scr, *, tb, n_streams, n_chunks):
    R = tb // S5_L
    used = n_streams * n_chunks

    @pl.when(pl.program_id(1) == 0)
    def _init():
        xr_ref[...] = xr0_ref[...]
        xi_ref[...] = xi0_ref[...]

    for f in range(S5_L):
        for j in range(S5_PLANES):
            ut = u_ref[0, j, pl.ds(f, R, stride=S5_L), :]
            ut_t = ut.astype(bf16).T
            for gl in range(LANES // S5_GROUP):
                g = j * (LANES // S5_GROUP) + gl
                at_scr[g, f * S5_GROUP:(f + 1) * S5_GROUP, :] = ut_t[gl * S5_GROUP:(gl + 1) * S5_GROUP, :]

    for p in range(S5_PAIRS):
        a_t = at_scr[2 * p:2 * p + 2].reshape(2 * S5_GL, R)
        e = lax.dot_general(a_t, wp_ref[p], _TN, preferred_element_type=f32)
        er_scr[:, p * LANES:(p + 1) * LANES] = e[:, :LANES]
        ei_scr[:, p * LANES:(p + 1) * LANES] = e[:, LANES:]

    if used < R:
        xinr_scr[...] = jnp.zeros_like(xinr_scr)
        xini_scr[...] = jnp.zeros_like(xini_scr)

    a_r = ar_ref[...]
    a_i = ai_ref[...]
    for s in range(n_streams):
        def step(c, carry):
            xr, xi = carry
            row = s * n_chunks + c
            xinr_scr[pl.ds(row, 1), :] = xr
            xini_scr[pl.ds(row, 1), :] = xi
            er = er_scr[pl.ds(row, 1), :]
            ei = ei_scr[pl.ds(row, 1), :]
            return a_r * xr - a_i * xi + er, a_r * xi + a_i * xr + ei

        xr, xi = lax.fori_loop(0, n_chunks, step, (xr_ref[0, s:s + 1, :], xi_ref[0, s:s + 1, :]))
        xr_ref[0, s:s + 1, :] = xr
        xi_ref[0, s:s + 1, :] = xi

    for g in range(S5_GROUPS):
        p = g // 2
        xin = jnp.concatenate([xinr_scr[:, p * LANES:(p + 1) * LANES],
                               xini_scr[:, p * LANES:(p + 1) * LANES]], axis=1).astype(bf16)
        yt_scr[g] = (jnp.dot(t_ref[g], at_scr[g], preferred_element_type=f32)
                     + lax.dot_general(vt_ref[g], xin, _NT, preferred_element_type=f32))

    for f in range(S5_L):
        gpp = LANES // S5_GROUP
        for j in range(S5_PLANES):
            yt = yt_scr[j * gpp:(j + 1) * gpp, f * S5_GROUP:(f + 1) * S5_GROUP, :].reshape(LANES, R)
            y_scr[j, pl.ds(f, R, stride=S5_L), :] = yt.T

    for rb in range(tb // OUT_SUB):
        sl = pl.ds(rb * OUT_SUB, OUT_SUB)
        y_intra = jnp.concatenate([y_scr[j, sl, :] for j in range(S5_PLANES)], axis=1)
        u_nat = jnp.concatenate([u_ref[0, j, sl, :] for j in range(S5_PLANES)], axis=1)
        y = y_intra + d_ref[...] * u_nat
        gy = _gelu_tanh(y)
        gate = _sigmoid(_bdot(gy, gw_ref[...]) + gb_ref[...])
        o_ref[0, sl, :] = (gy * gate * _silu(z_ref[0, sl, :])).astype(bf16)


def _s5_call(u, z, xr0, xi0, tmat, wp, vt, ar, ai, dvec, gw, gb, *, tb, n_streams, n_chunks):
    G, T, _ = z.shape
    grid = (G, T // tb)
    R = tb // S5_L
    nstate = S5_GROUPS * S5_STATE
    c2 = lambda g, t: (0, 0)
    c3 = lambda g, t: (0, 0, 0)
    body = functools.partial(_s5_body, tb=tb, n_streams=n_streams, n_chunks=n_chunks)
    one = pl.Buffered(1)
    return pl.pallas_call(
        body,
        grid=grid,
        in_specs=[
            pl.BlockSpec((1, S5_PLANES, tb, LANES), lambda g, t: (g, 0, t, 0)),
            pl.BlockSpec((1, tb, S5_WIDTH), lambda g, t: (g, t, 0)),
            pl.BlockSpec((1, n_streams, nstate), lambda g, t: (g, 0, 0)),
            pl.BlockSpec((1, n_streams, nstate), lambda g, t: (g, 0, 0)),
            pl.BlockSpec((S5_GROUPS, S5_GL, S5_GL), c3, pipeline_mode=one),
            pl.BlockSpec((S5_PAIRS, 2 * S5_GL, 2 * LANES), c3, pipeline_mode=one),
            pl.BlockSpec((S5_GROUPS, S5_GL, 2 * LANES), c3, pipeline_mode=one),
            pl.BlockSpec((1, nstate), c2),
            pl.BlockSpec((1, nstate), c2),
            pl.BlockSpec((1, S5_WIDTH), c2),
            pl.BlockSpec((S5_WIDTH, S5_WIDTH), c2),
            pl.BlockSpec((1, S5_WIDTH), c2),
        ],
        out_specs=[
            pl.BlockSpec((1, tb, S5_WIDTH), lambda g, t: (g, t, 0)),
            pl.BlockSpec((1, n_streams, nstate), lambda g, t: (g, 0, 0)),
            pl.BlockSpec((1, n_streams, nstate), lambda g, t: (g, 0, 0)),
        ],
        out_shape=[
            jax.ShapeDtypeStruct((G, T, S5_WIDTH), bf16),
            jax.ShapeDtypeStruct((G, n_streams, nstate), f32),
            jax.ShapeDtypeStruct((G, n_streams, nstate), f32),
        ],
        scratch_shapes=[
            pltpu.VMEM((S5_GROUPS, S5_GL, R), bf16),
            pltpu.VMEM((R, nstate), f32),
            pltpu.VMEM((R, nstate), f32),
            pltpu.VMEM((R, nstate), f32),
            pltpu.VMEM((R, nstate), f32),
            pltpu.VMEM((S5_GROUPS, S5_GL, R), f32),
            pltpu.VMEM((S5_PLANES, tb, LANES), f32),
        ],
        compiler_params=pltpu.CompilerParams(
            dimension_semantics=("arbitrary", "arbitrary"), vmem_limit_bytes=VMEM_LIMIT),
        name="s5",
    )(u, z, xr0, xi0, tmat, wp, vt, ar, ai, dvec, gw, gb)


def _out_body(x_ref, odn_ref, os5_ref, wo_ref, fw_ref, y_ref):
    acc = (x_ref[...]
           + jnp.dot(odn_ref[...], wo_ref[0:DN_WIDTH, :], preferred_element_type=f32)
           + jnp.dot(os5_ref[...], wo_ref[DN_WIDTH:, :], preferred_element_type=f32))
    ms = jnp.mean(acc * acc, axis=-1, keepdims=True)
    y_ref[...] = acc * lax.rsqrt(ms + EPS) * fw_ref[...]


def _out_call(x2, odn2, os52, w_out16, fw, *, rows):
    n = x2.shape[0]
    return pl.pallas_call(
        _out_body,
        grid=(n // rows,),
        in_specs=[
            pl.BlockSpec((rows, D_MODEL), lambda i: (i, 0)),
            pl.BlockSpec((rows, DN_WIDTH), lambda i: (i, 0)),
            pl.BlockSpec((rows, S5_WIDTH), lambda i: (i, 0)),
            pl.BlockSpec((D_MODEL, D_MODEL), lambda i: (0, 0)),
            pl.BlockSpec((1, D_MODEL), lambda i: (0, 0)),
        ],
        out_specs=pl.BlockSpec((rows, D_MODEL), lambda i: (i, 0)),
        out_shape=jax.ShapeDtypeStruct((n, D_MODEL), f32),
        compiler_params=pltpu.CompilerParams(
            dimension_semantics=("arbitrary",), vmem_limit_bytes=VMEM_LIMIT),
        name="outproj",
    )(x2, odn2, os52, w_out16, fw)


def _s5_operands(a_re, a_im, log_dt, b_re, b_im, c_re, c_im):
    lam_re = jnp.minimum(a_re, -1e-4)
    lam_im = a_im
    dt = jnp.exp(log_dt)[:, None]
    ldt_re, ldt_im = lam_re * dt, lam_im * dt

    def lpow(k):
        mag = jnp.exp(ldt_re * k)
        return mag * jnp.cos(ldt_im * k), mag * jnp.sin(ldt_im * k)

    lb_re, lb_im = lpow(1.0)
    den = lam_re * lam_re + lam_im * lam_im
    f_re = ((lb_re - 1.0) * lam_re + lb_im * lam_im) / den
    f_im = (lb_im * lam_re - (lb_re - 1.0) * lam_im) / den
    bb_re = f_re[..., None] * b_re - f_im[..., None] * b_im
    bb_im = f_re[..., None] * b_im + f_im[..., None] * b_re

    fr = jnp.arange(S5_L, dtype=f32)
    G, N = S5_GROUPS, S5_STATE

    def cmul(ar, ai, br, bi):
        return ar * br - ai * bi, ar * bi + ai * br

    pr, pi = lpow(fr[:, None, None])
    pr, pi = pr.transpose(1, 0, 2), pi.transpose(1, 0, 2)
    cpr, cpi = cmul(c_re[:, None], c_im[:, None], pr[:, :, None], pi[:, :, None])
    cpr, cpi = cpr.reshape(G, S5_GL, N), cpi.reshape(G, S5_GL, N)
    bbr_t, bbi_t = bb_re.transpose(0, 2, 1), bb_im.transpose(0, 2, 1)
    qr, qi = lpow(-fr[:, None, None])
    qr, qi = qr.transpose(1, 0, 2), qi.transpose(1, 0, 2)
    bpr, bpi = cmul(qr[:, :, None], qi[:, :, None], bbr_t[:, None], bbi_t[:, None])
    bpr, bpi = bpr.reshape(G, S5_GL, N), bpi.reshape(G, S5_GL, N)

    wr_, wi_ = lpow((S5_L - 1.0) - fr[:, None, None])
    wr_, wi_ = wr_.transpose(1, 0, 2), wi_.transpose(1, 0, 2)
    w_re, w_im = cmul(wr_[:, :, None], wi_[:, :, None], bbr_t[:, None], bbi_t[:, None])
    w_re, w_im = w_re.reshape(G, S5_GL, N), w_im.reshape(G, S5_GL, N)
    vr_, vi_ = lpow(fr[:, None, None] + 1.0)
    vr_, vi_ = vr_.transpose(1, 0, 2), vi_.transpose(1, 0, 2)
    v_re, v_im = cmul(c_re[:, None], c_im[:, None], vr_[:, :, None], vi_[:, :, None])
    v_re, v_im = v_re.reshape(G, S5_GL, N), v_im.reshape(G, S5_GL, N)

    zeros = jnp.zeros((S5_PAIRS, S5_GL, N), f32)

    def pair_cols(re, im):
        re, im = re.reshape(S5_PAIRS, 2, S5_GL, N), im.reshape(S5_PAIRS, 2, S5_GL, N)
        even = jnp.concatenate([re[:, 0], zeros, im[:, 0], zeros], axis=-1)
        odd = jnp.concatenate([zeros, re[:, 1], zeros, im[:, 1]], axis=-1)
        return jnp.stack([even, odd], axis=1)

    wp = pair_cols(w_re, w_im).reshape(S5_PAIRS, 2 * S5_GL, 2 * LANES).astype(bf16)
    vt = pair_cols(v_re, -v_im).reshape(S5_GROUPS, S5_GL, 2 * LANES).astype(bf16)
    a16r, a16i = lpow(float(S5_L))
    cp = jnp.concatenate([cpr, cpi], axis=2)
    bp = jnp.concatenate([bpr, -bpi], axis=2)
    return cp, bp, wp, vt, a16r.reshape(1, G * N), a16i.reshape(1, G * N)


def _pad_lanes(v, start=0):
    out = jnp.zeros((1, LANES), f32)
    return out.at[0, start:start + v.shape[0]].set(v.astype(f32))


def _layer(x, conv0, s0, xr0, xi0, prm, *, delta_cfg, s5_cfg):
    B, T, _ = x.shape
    conv0p = jnp.concatenate([jnp.zeros((B, SUBLANES - (CONV_K - 1), CONV_CH), f32), conv0], axis=1)
    odn, us5, zs5, convout, s_new = _delta_call(
        x, prm["w1"], prm["w2"], prm["norm_w"], prm["conv_w"], prm["alog"], prm["dtb"], prm["dnw"],
        conv0p, s0, **delta_cfg)

    nstate = S5_GROUPS * S5_STATE
    if s5_cfg["flatten"]:
        tb = s5_cfg["tb"]
        uf = us5.transpose(1, 0, 2, 3).reshape(1, S5_PLANES, B * T, LANES)
        zf = zs5.reshape(1, B * T, S5_WIDTH)
        pad = tb - B * T
        uf = jnp.pad(uf, ((0, 0), (0, 0), (0, pad), (0, 0)))
        zf = jnp.pad(zf, ((0, 0), (0, pad), (0, 0)))
        os5, xr, xi = _s5_call(uf, zf, xr0.reshape(1, B, nstate), xi0.reshape(1, B, nstate),
                               prm["tmat"], prm["wp"], prm["vt"], prm["ar"], prm["ai"], prm["dvec"],
                               prm["gw"], prm["gb"], tb=tb, n_streams=B, n_chunks=T // S5_L)
        os5 = os5[0, :B * T].reshape(B, T, S5_WIDTH)
    else:
        tb = s5_cfg["tb"]
        os5, xr, xi = _s5_call(us5, zs5, xr0.reshape(B, 1, nstate), xi0.reshape(B, 1, nstate),
                               prm["tmat"], prm["wp"], prm["vt"], prm["ar"], prm["ai"], prm["dvec"],
                               prm["gw"], prm["gb"], tb=tb, n_streams=1, n_chunks=tb // S5_L)

    y = _out_call(x.reshape(B * T, D_MODEL), odn.reshape(B * T, DN_WIDTH), os5.reshape(B * T, S5_WIDTH),
                  prm["w_out"], prm["fw"], rows=min(OUT_ROWS, B * T))
    return (y.reshape(B, T, D_MODEL), convout[:, SUBLANES - (CONV_K - 1):, :], s_new,
            xr.reshape(B, S5_GROUPS, S5_STATE), xi.reshape(B, S5_GROUPS, S5_STATE))


def kernel(x_prompt, x_sample, cache_conv, state_dn, state_s5_re, state_s5_im, norm_w, w_in, conv_w, dn_A_log, dn_dt_bias, dn_norm_w, s5_A_re, s5_A_im, s5_log_dt, s5_B_re, s5_B_im, s5_C_re, s5_C_im, s5_D, glu_w, glu_b, w_out, final_norm_w):
    depth = norm_w.shape[0]
    assert depth == 1
    l = 0
    assert w_in.shape[-1] == IN_COLS
    w1 = w_in[l, :, :CONV_CH].astype(bf16)
    w2 = w_in[l, :, IN_COLS - W2_COLS:].astype(bf16)
    cp, bp, wp, vt, ar, ai = _s5_operands(
        s5_A_re[l].astype(f32), s5_A_im[l].astype(f32), s5_log_dt[l].astype(f32),
        s5_B_re[l].astype(f32), s5_B_im[l].astype(f32), s5_C_re[l].astype(f32), s5_C_im[l].astype(f32))
    prm = dict(
        w1=w1, w2=w2,
        norm_w=norm_w[l].reshape(1, D_MODEL).astype(f32),
        conv_w=conv_w[l].astype(f32),
        alog=_pad_lanes(dn_A_log[l], AB_LANE),
        dtb=_pad_lanes(dn_dt_bias[l], AB_LANE),
        dnw=dn_norm_w[l].reshape(1, DN_D).astype(f32),
        tmat=_s5mat_call(cp, bp),
        wp=wp, vt=vt, ar=ar, ai=ai,
        dvec=s5_D[l].reshape(1, S5_WIDTH).astype(f32),
        gw=glu_w[l].astype(bf16),
        gb=glu_b[l].reshape(1, S5_WIDTH).astype(f32),
        w_out=w_out[l].astype(bf16),
        fw=final_norm_w.reshape(1, D_MODEL).astype(f32),
    )

    bp = x_prompt.shape[0]
    yp, c1, d1, r1, i1 = _layer(
        x_prompt,
        jnp.zeros((bp, CONV_K - 1, CONV_CH), f32),
        jnp.zeros((bp, DN_HEADS, DN_D, DN_D), f32),
        jnp.zeros((bp, S5_GROUPS, S5_STATE), f32),
        jnp.zeros((bp, S5_GROUPS, S5_STATE), f32),
        prm,
        delta_cfg=dict(tb=256, pipelined=True),
        s5_cfg=dict(flatten=False, tb=2048))
    ys, c2, d2, r2, i2 = _layer(
        x_sample, cache_conv[l].astype(f32), state_dn[l].astype(f32),
        state_s5_re[l].astype(f32), state_s5_im[l].astype(f32),
        prm,
        delta_cfg=dict(tb=32, pipelined=False),
        s5_cfg=dict(flatten=True, tb=2048))

    return (yp, ys, c1[None], d1[None], r1[None], i1[None], c2[None], d2[None], r2[None], i2[None])
```

```python
import functools
import math

import jax
import jax.numpy as jnp
import numpy as np
from jax import lax
from jax.experimental import pallas as pl
from jax.experimental.pallas import tpu as pltpu

bf16 = jnp.bfloat16
f32 = jnp.float32

LANES = 128
SUBLANES = 8

D_MODEL = 1024
DN_HEADS = 4
DN_D = 128
DN_WIDTH = DN_HEADS * DN_D
CONV_K = 4
CONV_CH = 3 * DN_WIDTH
S5_WIDTH = 512
S5_GROUP = 16
S5_GROUPS = 32
S5_STATE = 64
S5_PAIRS = S5_GROUPS // 2
S5_L = 16
S5_GL = S5_GROUP * S5_L
OUT_SUB = 256
OUT_ROWS = 2048
S5_PLANES = S5_WIDTH // LANES
EPS = 1e-6
L2_EPS = 1e-6

IN_COLS = CONV_CH + 2 * DN_HEADS + DN_WIDTH + 2 * S5_WIDTH
W2_COLS = LANES + DN_WIDTH + 2 * S5_WIDTH
AB_LANE = LANES - 2 * DN_HEADS
W2_ZDN = LANES
W2_US5 = W2_ZDN + DN_WIDTH
W2_ZS5 = W2_US5 + S5_WIDTH

GROUP = 256
MASK_INCL, MASK_STRICT, MASK_OFF = 0, 1, 2
FOLD_MIN_ROWS = 64
VMEM_LIMIT =56 * 1024 * 1024

_NT = (((1,), (1,)), ((), ()))
_TN = (((0,), (0,)), ((), ()))


def _bdot(a, b):
    return jnp.dot(a.astype(bf16), b.astype(bf16), preferred_element_type=f32)


def _bdot_g(a, b, dims):
    return lax.dot_general(a.astype(bf16), b.astype(bf16), dims, preferred_element_type=f32)


def _sigmoid(x):
    return 1.0 / (1.0 + jnp.exp2(x * (-math.log2(math.e))))


def _silu(x):
    return x * _sigmoid(x)


def _softplus(x):
    return jnp.maximum(x, 0.0) + jnp.log1p(jnp.exp(-jnp.abs(x)))


def _gelu_tanh(x):
    c = math.sqrt(2.0 / math.pi)
    return x * (0.5 + 0.5 * jnp.tanh(x * (c + (c * 0.044715) * (x * x))))


def _zero_after(x):
    bits = pltpu.bitcast(x[0:SUBLANES, 0:LANES].astype(f32), jnp.uint32)
    half = jnp.uint32(16)
    bits = lax.shift_right_logical(lax.shift_right_logical(bits, half), half)
    return pltpu.bitcast(bits, f32)[0:1, :]


def _unit_lower_solve(nmats, rhss, eye, m_ref, n_levels, fillers):
    d32 = functools.partial(jnp.dot, preferred_element_type=f32)

    def d16(a, b):
        return d32(a, b).astype(bf16)

    def fill(after):
        if fillers:
            fillers.pop(0)(_zero_after(after[-1]))

    def fold(x, rows):
        out = x[0:rows]
        for k in range(1, GROUP // rows):
            out = out + x[k * rows:(k + 1) * rows]
        return out

    def unfold(xf, rows):
        lane_blk = lax.broadcasted_iota(jnp.int32, (rows, GROUP), 1) >> (rows.bit_length() - 1)
        return jnp.concatenate([jnp.where(lane_blk == k, xf, jnp.zeros_like(xf))
                                for k in range(GROUP // rows)], axis=0)

    n16 = [n.astype(bf16) for n in nmats]
    invs = [(eye - n * m_ref[MASK_OFF]).astype(bf16) for n in nmats]
    for lvl in range(1, n_levels):
        rows = min(GROUP, max(FOLD_MIN_ROWS, 2 ** (lvl + 1)))
        mask16 = m_ref[MASK_OFF + lvl].astype(bf16)
        lhs = invs if rows == GROUP else [fold(inv, rows) for inv in invs]
        t1 = [d16(l, n * mask16) for l, n in zip(lhs, n16)]
        fill(invs)
        t2 = [d16(t, inv) for t, inv in zip(t1, invs)]
        fill(t1)
        if rows != GROUP:
            t2 = [unfold(t, rows) for t in t2]
        invs = [inv - t for inv, t in zip(invs, t2)]
    return [d32(inv, r.astype(bf16)) for inv, r in zip(invs, rhss)]


def _segment_cumsum(tri16, g):
    g1 = g.astype(bf16)
    r1 = g - g1.astype(f32)
    g2 = r1.astype(bf16)
    g3 = (r1 - g2.astype(f32)).astype(bf16)
    s = jnp.dot(tri16, jnp.concatenate([g1, g2, g3], axis=1), preferred_element_type=f32)
    n = g.shape[1]
    return s[:, :n] + (s[:, n:2 * n] + s[:, 2 * n:])


def _delta_body(x_ref, w1_ref, w2_ref, nw_ref, cw_ref, alog_ref, dtb_ref, dnw_ref, m_ref, conv0_ref, s0_ref,
                odn_ref, us5_ref, zs5_ref, convout_ref, s_ref,
                cbuf, qn_scr, kn_scr, v_scr, gc_scr, beta_scr, zdn_scr, *, nb, tb, n_t, pipelined):
    rows = nb * tb
    n_groups = rows // GROUP
    segs = GROUP // tb
    n_levels = tb.bit_length() - 1
    step = pl.program_id(0)
    fresh = step < n_t

    @pl.when(step == 0)
    def _init():
        cbuf[:, 0:SUBLANES, :] = conv0_ref[...]
        s_ref[...] = s0_ref[...]
        if pipelined:
            for scr in (qn_scr, kn_scr, v_scr, gc_scr, beta_scr, zdn_scr):
                scr[...] = jnp.zeros_like(scr)

    front = {}

    def project(w_ref, lo, width):
        return jnp.dot(front["h"], w_ref[:, lo:lo + width], preferred_element_type=f32)

    def a_norm_in():
        x = x_ref[...].reshape(rows, D_MODEL)
        ms = jnp.mean(x * x, axis=-1, keepdims=True)
        front["h"] = (x * lax.rsqrt(ms + EPS) * nw_ref[...]).astype(bf16)

    def a_project_qkv(s):
        front[("raw", s)] = project(w1_ref, s * DN_WIDTH, DN_WIDTH)

    def a_project_us5():
        u = project(w2_ref, W2_US5, S5_WIDTH)
        for j in range(S5_PLANES):
            us5_ref[:, j] = u[:, j * LANES:(j + 1) * LANES].reshape(nb, tb, LANES)

    def a_project_zs5():
        zs5_ref[...] = project(w2_ref, W2_ZS5, S5_WIDTH).reshape(nb, tb, S5_WIDTH)

    def a_project_zdn():
        front["zdn"] = project(w2_ref, W2_ZDN, DN_WIDTH)

    def a_project_gates():
        ab = project(w2_ref, 0, LANES)
        front["g"] = -jnp.exp(alog_ref[...]) * _softplus(ab + dtb_ref[...])
        front["beta"] = _sigmoid(ab)

    projections = ([functools.partial(a_project_qkv, s) for s in range(3)]
                   + [a_project_us5, a_project_zs5, a_project_zdn, a_project_gates])

    def a_conv(b, s, zero):
        cols = slice(s * DN_WIDTH, (s + 1) * DN_WIDTH)
        cbuf[b, SUBLANES:SUBLANES + tb, cols] = front[("raw", s)][b * tb:(b + 1) * tb, :]
        taps = [cw_ref[j:j + 1, cols] for j in range(CONV_K)]
        if zero is not None:
            zero_w = jnp.concatenate([zero] * (DN_WIDTH // LANES), axis=1)
            taps = [t + zero_w for t in taps]
        acc = cbuf[b, 5:5 + tb, cols] * taps[0]
        for j in range(1, CONV_K):
            acc = acc + cbuf[b, 5 + j:5 + j + tb, cols] * taps[j]
        front[("act", b, s)] = _silu(acc)
        tail = cbuf[b, tb:tb + SUBLANES, cols]
        if pipelined:
            tail = jnp.where(fresh, tail, cbuf[b, 0:SUBLANES, cols])
        cbuf[b, 0:SUBLANES, cols] = tail
        convout_ref[b, :, cols] = tail

    def a_norm(b, hd, zero):
        lo = hd * DN_D
        q = front[("act", b, 0)][:, lo:lo + DN_D]
        k = front[("act", b, 1)][:, lo:lo + DN_D]
        if zero is not None:
            q, k = q + zero, k + zero
        front[("qn", b, hd)] = (q * lax.rsqrt(jnp.sum(q * q, axis=-1, keepdims=True) + L2_EPS)
                                * (DN_D ** -0.5))
        front[("kn", b, hd)] = k * lax.rsqrt(jnp.sum(k * k, axis=-1, keepdims=True) + L2_EPS)

    def a_gates(zero):
        tri16 = m_ref[MASK_INCL].astype(bf16)
        g = front["g"] if zero is None else front["g"] + zero
        front["gc"] = [_segment_cumsum(tri16, g[gi * GROUP:(gi + 1) * GROUP])
                       for gi in range(n_groups)]

    def a_store():
        for b in range(nb):
            r = slice(b * tb, (b + 1) * tb)
            for hd in range(DN_HEADS):
                c = slice(hd * DN_D, (hd + 1) * DN_D)
                qn_scr[r, c] = front[("qn", b, hd)]
                kn_scr[r, c] = front[("kn", b, hd)]
            v_scr[r, :] = front[("act", b, 2)]
        for gi in range(n_groups):
            gc_scr[gi * GROUP:(gi + 1) * GROUP, :] = front["gc"][gi]
        beta_scr[...] = front["beta"]
        zdn_scr[...] = front["zdn"]

    fillers = ([functools.partial(a_conv, b, s) for b in range(nb) for s in range(3)] + [a_gates]
               + [functools.partial(a_norm, b, hd) for b in range(nb) for hd in range(DN_HEADS)])

    incl = m_ref[MASK_INCL]
    strict = m_ref[MASK_STRICT]
    eye = incl - strict
    dnw = dnw_ref[...]
    chains = [(gi, hd) for gi in range(n_groups) for hd in range(DN_HEADS)]
    back = {}

    def stream_of(gi, sg):
        return (gi * GROUP) // tb + sg

    for name in ("q16", "qe16", "k16", "kb16", "rhs", "dmat", "kdec", "sdecay", "nmat", "attn"):
        back[name] = [None] * len(chains)

    def b_load(ci):
        gi, hd = chains[ci]
        r = slice(gi * GROUP, (gi + 1) * GROUP)
        c = slice(hd * DN_D, (hd + 1) * DN_D)
        q, k, v = qn_scr[r, c], kn_scr[r, c], v_scr[r, c]
        beta = beta_scr[r, AB_LANE + DN_HEADS + hd:AB_LANE + DN_HEADS + hd + 1]
        g = gc_scr[r, AB_LANE + hd:AB_LANE + hd + 1]
        eg = jnp.exp(g)
        kb = k * beta
        gcb = jnp.broadcast_to(g * math.log2(math.e), (GROUP, GROUP))
        back["dmat"][ci] = jnp.exp2(jnp.minimum(gcb - gcb.T, 0.0)) * incl
        back["q16"][ci] = q.astype(bf16)
        back["qe16"][ci] = (q * eg).astype(bf16)
        back["k16"][ci] = k.astype(bf16)
        back["kb16"][ci] = kb.astype(bf16)
        back["rhs"][ci] = jnp.concatenate([v * beta, kb * eg], axis=1)
        kd, sd = [], []
        for sg in range(segs):
            a0 = sg * tb
            glast = g[a0 + tb - 1:a0 + tb, :]
            kd.append((k[a0:a0 + tb] * jnp.exp(glast - g[a0:a0 + tb])).astype(bf16))
            sd.append(jnp.exp(glast))
        back["kdec"][ci] = kd
        back["sdecay"][ci] = sd

    def b_mats(ci):
        d = functools.partial(lax.dot_general, dimension_numbers=_NT, preferred_element_type=f32)
        dm = back["dmat"][ci]
        back["nmat"][ci] = d(back["kb16"][ci], back["k16"][ci]) * (dm * strict)
        back["attn"][ci] = (d(back["q16"][ci], back["k16"][ci]) * dm).astype(bf16)

    def b_state(sols):
        s_old = {(b, hd): s_ref[b, hd] for b in range(nb) for hd in range(DN_HEADS)}
        s16 = {key: val.astype(bf16) for key, val in s_old.items()}
        v_news, o_states = [], []
        for (gi, hd), sol, qe16 in zip(chains, sols, back["qe16"]):
            u = sol[:, :DN_D]
            w16 = sol[:, DN_D:].astype(bf16)
            vn, os_ = [], []
            for sg in range(segs):
                a0 = sg * tb
                st = s16[(stream_of(gi, sg), hd)]
                vn.append(u[a0:a0 + tb] - jnp.dot(w16[a0:a0 + tb], st, preferred_element_type=f32))
                os_.append(jnp.dot(qe16[a0:a0 + tb], st, preferred_element_type=f32))
            v_news.append(vn[0] if segs == 1 else jnp.concatenate(vn, axis=0))
            o_states.append(os_[0] if segs == 1 else jnp.concatenate(os_, axis=0))
        vn16s = [vn.astype(bf16) for vn in v_news]
        outs = [os_ + jnp.dot(a, vn, preferred_element_type=f32)
                for os_, a, vn in zip(o_states, back["attn"], vn16s)]

        s_new = {}
        for (gi, hd), kd, sd, vn in zip(chains, back["kdec"], back["sdecay"], vn16s):
            for sg in range(segs):
                a0 = sg * tb
                key = (stream_of(gi, sg), hd)
                s_new[key] = (s_old[key] * sd[sg]
                              + lax.dot_general(kd[sg], vn[a0:a0 + tb], _TN, preferred_element_type=f32))

        o_rows = []
        for gi in range(n_groups):
            o_heads = []
            for hd in range(DN_HEADS):
                o = outs[gi * DN_HEADS + hd]
                on = o * lax.rsqrt(jnp.mean(o * o, axis=-1, keepdims=True) + EPS) * dnw
                zd = zdn_scr[gi * GROUP:(gi + 1) * GROUP, hd * DN_D:(hd + 1) * DN_D]
                o_heads.append(on * _silu(zd))
            o_rows.append(jnp.concatenate(o_heads, axis=1))
        o_all = o_rows[0] if n_groups == 1 else jnp.concatenate(o_rows, axis=0)
        odn_ref[...] = o_all.astype(bf16).reshape(nb, tb, DN_WIDTH)
        s_ref[...] = jnp.stack([jnp.stack([s_new[(b, hd)] for hd in range(DN_HEADS)]) for b in range(nb)])

    n_ch = len(chains)
    if pipelined:
        a_norm_in()
        for ci in range(n_ch):
            b_load(ci)
        back_setup = [functools.partial(b_mats, ci) for ci in range(n_ch)]
        convs = [fillers.pop(0) for _ in range(nb * 3)]
        per_piece = -(-len(back_setup) // len(projections))
        for idx, piece in enumerate(projections):
            piece()
            for thunk in back_setup[:per_piece]:
                thunk()
            del back_setup[:per_piece]
            if idx >= 3 and convs:
                convs.pop(0)(None)
        while back_setup or convs:
            if back_setup:
                back_setup.pop(0)()
            if convs:
                convs.pop(0)(None)
        sols = _unit_lower_solve(back["nmat"], back["rhs"], eye, m_ref, n_levels, fillers)
        for fill in fillers:
            fill(None)
        b_state(sols)
        a_store()
    else:
        a_norm_in()
        for piece in projections:
            piece()
        for fill in fillers:
            fill(None)
        a_store()
        for ci in range(n_ch):
            b_load(ci)
        for ci in range(n_ch):
            b_mats(ci)
        b_state(_unit_lower_solve(back["nmat"], back["rhs"], eye, m_ref, n_levels, []))


def _delta_masks(tb):
    r = np.arange(GROUP)[:, None]
    c = np.arange(GROUP)[None, :]
    same = (r // tb) == (c // tb)
    ms = [same & (r >= c), same & (r > c)]
    for l in range(tb.bit_length() - 1):
        ms.append(((r >> (l + 1)) == (c >> (l + 1))) & (((r >> l) & 1) == 1) & (((c >> l) & 1) == 0))
    return jnp.asarray(np.stack(ms).astype(np.float32))


def _delta_call(x, w1, w2, norm_w, conv_w, alog, dtb, dnw, conv0, s0, *, tb, pipelined):
    nb, T, _ = x.shape
    assert GROUP % tb == 0 and (nb * tb) % GROUP == 0 and T % tb == 0
    n_t = T // tb
    rows = nb * tb
    if pipelined:
        grid = (n_t + 1,)
        cur = lambda i: jnp.minimum(i, n_t - 1)
        prev = lambda i: jnp.maximum(i - 1, 0)
    else:
        grid = (n_t,)
        cur = prev = lambda i: i
    const2 = lambda i: (0, 0)
    const3 = lambda i: (0, 0, 0)
    const4 = lambda i: (0, 0, 0, 0)
    body = functools.partial(_delta_body, nb=nb, tb=tb, n_t=n_t, pipelined=pipelined)
    masks = _delta_masks(tb)
    one = pl.Buffered(1)
    return pl.pallas_call(
        body,
        grid=grid,
        in_specs=[
            pl.BlockSpec((nb, tb, D_MODEL), lambda i: (0, cur(i), 0)),
            pl.BlockSpec((D_MODEL, CONV_CH), const2, pipeline_mode=one),
            pl.BlockSpec((D_MODEL, W2_COLS), const2, pipeline_mode=one),
            pl.BlockSpec((1, D_MODEL), const2),
            pl.BlockSpec((CONV_K, CONV_CH), const2),
            pl.BlockSpec((1, LANES), const2),
            pl.BlockSpec((1, LANES), const2),
            pl.BlockSpec((1, DN_D), const2),
            pl.BlockSpec(masks.shape, const3, pipeline_mode=one),
            pl.BlockSpec((nb, SUBLANES, CONV_CH), const3),
            pl.BlockSpec((nb, DN_HEADS, DN_D, DN_D), const4),
        ],
        out_specs=[
            pl.BlockSpec((nb, tb, DN_WIDTH), lambda i: (0, prev(i), 0)),
            pl.BlockSpec((nb, S5_PLANES, tb, LANES), lambda i: (0, 0, cur(i), 0)),
            pl.BlockSpec((nb, tb, S5_WIDTH), lambda i: (0, cur(i), 0)),
            pl.BlockSpec((nb, SUBLANES, CONV_CH), const3),
            pl.BlockSpec((nb, DN_HEADS, DN_D, DN_D), const4),
        ],
        out_shape=[
            jax.ShapeDtypeStruct((nb, T, DN_WIDTH), bf16),
            jax.ShapeDtypeStruct((nb, S5_PLANES, T, LANES), f32),
            jax.ShapeDtypeStruct((nb, T, S5_WIDTH), f32),
            jax.ShapeDtypeStruct((nb, SUBLANES, CONV_CH), f32),
            jax.ShapeDtypeStruct((nb, DN_HEADS, DN_D, DN_D), f32),
        ],
        scratch_shapes=[
            pltpu.VMEM((nb, tb + SUBLANES, CONV_CH), f32),
            pltpu.VMEM((rows, DN_WIDTH), f32),
            pltpu.VMEM((rows, DN_WIDTH), f32),
            pltpu.VMEM((rows, DN_WIDTH), f32),
            pltpu.VMEM((rows, LANES), f32),
            pltpu.VMEM((rows, LANES), f32),
            pltpu.VMEM((rows, DN_WIDTH), f32),
        ],
        compiler_params=pltpu.CompilerParams(
            dimension_semantics=("arbitrary",), vmem_limit_bytes=VMEM_LIMIT),
        name="delta",
    )(x, w1, w2, norm_w, conv_w, alog, dtb, dnw, masks, conv0, s0)


S5MAT_GROUPS_PER_STEP = 8


def _split16(a):
    hi = a.astype(bf16)
    return hi, (a - hi.astype(f32)).astype(bf16)


def _s5mat_body(cp_ref, bp_ref, t_ref):
    ri = lax.broadcasted_iota(jnp.int32, (S5_GL, S5_GL), 0) >> 4
    ci = lax.broadcasted_iota(jnp.int32, (S5_GL, S5_GL), 1) >> 4
    causal = ri >= ci
    d = functools.partial(lax.dot_general, dimension_numbers=_NT, preferred_element_type=f32)
    for g in range(S5MAT_GROUPS_PER_STEP):
        ch, cl = _split16(cp_ref[g])
        bh, bl = _split16(bp_ref[g])
        t = d(ch, bh) + (d(ch, bl) + d(cl, bh))
        t_ref[g] = jnp.where(causal, t, 0.0).astype(bf16)


def _s5mat_call(cp, bp):
    n = S5MAT_GROUPS_PER_STEP
    return pl.pallas_call(
        _s5mat_body,
        grid=(S5_GROUPS // n,),
        in_specs=[
            pl.BlockSpec((n, S5_GL, 2 * S5_STATE), lambda g: (g, 0, 0)),
            pl.BlockSpec((n, S5_GL, 2 * S5_STATE), lambda g: (g, 0, 0)),
        ],
        out_specs=pl.BlockSpec((n, S5_GL, S5_GL), lambda g: (g, 0, 0)),
        out_shape=jax.ShapeDtypeStruct((S5_GROUPS, S5_GL, S5_GL), bf16),
        compiler_params=pltpu.CompilerParams(dimension_semantics=("arbitrary",)),
        name="s5mat",
    )(cp, bp)


def _s5_body(u_ref, z_ref, xr0_ref, xi0_ref, t_ref, wp_ref, vt_ref, ar_ref, ai_ref, d_ref,
             gw_ref, gb_ref,
             o_ref, xr_ref, xi_ref,
             at_scr, er_scr, ei_scr, xinr_scr, xini_scr, yt_scr, y_scr, *, tb, n_streams, n_chunks):
    R = tb // S5_L
    used = n_streams * n_chunks

    @pl.when(pl.program_id(1) == 0)
    def _init():
        xr_ref[...] = xr0_ref[...]
        xi_ref[...] = xi0_ref[...]

    for f in range(S5_L):
        for j in range(S5_PLANES):
            ut = u_ref[0, j, pl.ds(f, R, stride=S5_L), :]
            ut_t = ut.astype(bf16).T
            for gl in range(LANES // S5_GROUP):
                g = j * (LANES // S5_GROUP) + gl
                at_scr[g, f * S5_GROUP:(f + 1) * S5_GROUP, :] = ut_t[gl * S5_GROUP:(gl + 1) * S5_GROUP, :]

    for p in range(S5_PAIRS):
        a_t = at_scr[2 * p:2 * p + 2].reshape(2 * S5_GL, R)
        e = lax.dot_general(a_t, wp_ref[p], _TN, preferred_element_type=f32)
        er_scr[:, p * LANES:(p + 1) * LANES] = e[:, :LANES]
        ei_scr[:, p * LANES:(p + 1) * LANES] = e[:, LANES:]

    if used < R:
        xinr_scr[...] = jnp.zeros_like(xinr_scr)
        xini_scr[...] = jnp.zeros_like(xini_scr)

    a_r = ar_ref[...]
    a_i = ai_ref[...]
    for s in range(n_streams):
        def step(c, carry):
            xr, xi = carry
            row = s * n_chunks + c
            xinr_scr[pl.ds(row, 1), :] = xr
            xini_scr[pl.ds(row, 1), :] = xi
            er = er_scr[pl.ds(row, 1), :]
            ei = ei_scr[pl.ds(row, 1), :]
            return a_r * xr - a_i * xi + er, a_r * xi + a_i * xr + ei

        xr, xi = lax.fori_loop(0, n_chunks, step, (xr_ref[0, s:s + 1, :], xi_ref[0, s:s + 1, :]),
                               unroll=min(n_chunks, 4))
        xr_ref[0, s:s + 1, :] = xr
        xi_ref[0, s:s + 1, :] = xi

    for g in range(S5_GROUPS):
        p = g // 2
        xin = jnp.concatenate([xinr_scr[:, p * LANES:(p + 1) * LANES],
                               xini_scr[:, p * LANES:(p + 1) * LANES]], axis=1).astype(bf16)
        yt_scr[g] = (jnp.dot(t_ref[g], at_scr[g], preferred_element_type=f32)
                     + lax.dot_general(vt_ref[g], xin, _NT, preferred_element_type=f32))

    for f in range(S5_L):
        gpp = LANES // S5_GROUP
        for j in range(S5_PLANES):
            yt = yt_scr[j * gpp:(j + 1) * gpp, f * S5_GROUP:(f + 1) * S5_GROUP, :].reshape(LANES, R)
            y_scr[j, pl.ds(f, R, stride=S5_L), :] = yt.T

    for rb in range(tb // OUT_SUB):
        sl = pl.ds(rb * OUT_SUB, OUT_SUB)
        y_intra = jnp.concatenate([y_scr[j, sl, :] for j in range(S5_PLANES)], axis=1)
        u_nat = jnp.concatenate([u_ref[0, j, sl, :] for j in range(S5_PLANES)], axis=1)
        y = y_intra + d_ref[...] * u_nat
        gy = _gelu_tanh(y)
        gate = _sigmoid(_bdot(gy, gw_ref[...]) + gb_ref[...])
        o_ref[0, sl, :] = (gy * gate * _silu(z_ref[0, sl, :])).astype(bf16)


def _s5_call(u, z, xr0, xi0, tmat, wp, vt, ar, ai, dvec, gw, gb, *, tb, n_streams, n_chunks):
    G, T, _ = z.shape
    grid = (G, T // tb)
    R = tb // S5_L
    nstate = S5_GROUPS * S5_STATE
    c2 = lambda g, t: (0, 0)
    c3 = lambda g, t: (0, 0, 0)
    body = functools.partial(_s5_body, tb=tb, n_streams=n_streams, n_chunks=n_chunks)
    one = pl.Buffered(1)
    return pl.pallas_call(
        body,
        grid=grid,
        in_specs=[
            pl.BlockSpec((1, S5_PLANES, tb, LANES), lambda g, t: (g, 0, t, 0)),
            pl.BlockSpec((1, tb, S5_WIDTH), lambda g, t: (g, t, 0)),
            pl.BlockSpec((1, n_streams, nstate), lambda g, t: (g, 0, 0)),
            pl.BlockSpec((1, n_streams, nstate), lambda g, t: (g, 0, 0)),
            pl.BlockSpec((S5_GROUPS, S5_GL, S5_GL), c3, pipeline_mode=one),
            pl.BlockSpec((S5_PAIRS, 2 * S5_GL, 2 * LANES), c3, pipeline_mode=one),
            pl.BlockSpec((S5_GROUPS, S5_GL, 2 * LANES), c3, pipeline_mode=one),
            pl.BlockSpec((1, nstate), c2),
            pl.BlockSpec((1, nstate), c2),
            pl.BlockSpec((1, S5_WIDTH), c2),
            pl.BlockSpec((S5_WIDTH, S5_WIDTH), c2),
            pl.BlockSpec((1, S5_WIDTH), c2),
        ],
        out_specs=[
            pl.BlockSpec((1, tb, S5_WIDTH), lambda g, t: (g, t, 0)),
            pl.BlockSpec((1, n_streams, nstate), lambda g, t: (g, 0, 0)),
            pl.BlockSpec((1, n_streams, nstate), lambda g, t: (g, 0, 0)),
        ],
        out_shape=[
            jax.ShapeDtypeStruct((G, T, S5_WIDTH), bf16),
            jax.ShapeDtypeStruct((G, n_streams, nstate), f32),
            jax.ShapeDtypeStruct((G, n_streams, nstate), f32),
        ],
        scratch_shapes=[
            pltpu.VMEM((S5_GROUPS, S5_GL, R), bf16),
            pltpu.VMEM((R, nstate), f32),
            pltpu.VMEM((R, nstate), f32),
            pltpu.VMEM((R, nstate), f32),
            pltpu.VMEM((R, nstate), f32),
            pltpu.VMEM((S5_GROUPS, S5_GL, R), f32),
            pltpu.VMEM((S5_PLANES, tb, LANES), f32),
        ],
        compiler_params=pltpu.CompilerParams(
            dimension_semantics=("arbitrary", "arbitrary"), vmem_limit_bytes=VMEM_LIMIT),
        name="s5",
    )(u, z, xr0, xi0, tmat, wp, vt, ar, ai, dvec, gw, gb)


def _out_body(x_ref, odn_ref, os5_ref, wo_ref, fw_ref, y_ref):
    acc = (x_ref[...]
           + jnp.dot(odn_ref[...], wo_ref[0:DN_WIDTH, :], preferred_element_type=f32)
           + jnp.dot(os5_ref[...], wo_ref[DN_WIDTH:, :], preferred_element_type=f32))
    ms = jnp.mean(acc * acc, axis=-1, keepdims=True)
    y_ref[...] = acc * lax.rsqrt(ms + EPS) * fw_ref[...]


def _out_call(x2, odn2, os52, w_out16, fw, *, rows):
    n = x2.shape[0]
    return pl.pallas_call(
        _out_body,
        grid=(n // rows,),
        in_specs=[
            pl.BlockSpec((rows, D_MODEL), lambda i: (i, 0)),
            pl.BlockSpec((rows, DN_WIDTH), lambda i: (i, 0)),
            pl.BlockSpec((rows, S5_WIDTH), lambda i: (i, 0)),
            pl.BlockSpec((D_MODEL, D_MODEL), lambda i: (0, 0)),
            pl.BlockSpec((1, D_MODEL), lambda i: (0, 0)),
        ],
        out_specs=pl.BlockSpec((rows, D_MODEL), lambda i: (i, 0)),
        out_shape=jax.ShapeDtypeStruct((n, D_MODEL), f32),
        compiler_params=pltpu.CompilerParams(
            dimension_semantics=("arbitrary",), vmem_limit_bytes=VMEM_LIMIT),
        name="outproj",
    )(x2, odn2, os52, w_out16, fw)


def _s5_operands(a_re, a_im, log_dt, b_re, b_im, c_re, c_im):
    lam_re = jnp.minimum(a_re, -1e-4)
    lam_im = a_im
    dt = jnp.exp(log_dt)[:, None]
    ldt_re, ldt_im = lam_re * dt, lam_im * dt

    def lpow(k):
        mag = jnp.exp(ldt_re * k)
        return mag * jnp.cos(ldt_im * k), mag * jnp.sin(ldt_im * k)

    lb_re, lb_im = lpow(1.0)
    den = lam_re * lam_re + lam_im * lam_im
    f_re = ((lb_re - 1.0) * lam_re + lb_im * lam_im) / den
    f_im = (lb_im * lam_re - (lb_re - 1.0) * lam_im) / den
    bb_re = f_re[..., None] * b_re - f_im[..., None] * b_im
    bb_im = f_re[..., None] * b_im + f_im[..., None] * b_re

    fr = jnp.arange(S5_L, dtype=f32)
    G, N = S5_GROUPS, S5_STATE

    def cmul(ar, ai, br, bi):
        return ar * br - ai * bi, ar * bi + ai * br

    pr, pi = lpow(fr[:, None, None])
    pr, pi = pr.transpose(1, 0, 2), pi.transpose(1, 0, 2)
    cpr, cpi = cmul(c_re[:, None], c_im[:, None], pr[:, :, None], pi[:, :, None])
    cpr, cpi = cpr.reshape(G, S5_GL, N), cpi.reshape(G, S5_GL, N)
    bbr_t, bbi_t = bb_re.transpose(0, 2, 1), bb_im.transpose(0, 2, 1)
    qr, qi = lpow(-fr[:, None, None])
    qr, qi = qr.transpose(1, 0, 2), qi.transpose(1, 0, 2)
    bpr, bpi = cmul(qr[:, :, None], qi[:, :, None], bbr_t[:, None], bbi_t[:, None])
    bpr, bpi = bpr.reshape(G, S5_GL, N), bpi.reshape(G, S5_GL, N)

    wr_, wi_ = lpow((S5_L - 1.0) - fr[:, None, None])
    wr_, wi_ = wr_.transpose(1, 0, 2), wi_.transpose(1, 0, 2)
    w_re, w_im = cmul(wr_[:, :, None], wi_[:, :, None], bbr_t[:, None], bbi_t[:, None])
    w_re, w_im = w_re.reshape(G, S5_GL, N), w_im.reshape(G, S5_GL, N)
    vr_, vi_ = lpow(fr[:, None, None] + 1.0)
    vr_, vi_ = vr_.transpose(1, 0, 2), vi_.transpose(1, 0, 2)
    v_re, v_im = cmul(c_re[:, None], c_im[:, None], vr_[:, :, None], vi_[:, :, None])
    v_re, v_im = v_re.reshape(G, S5_GL, N), v_im.reshape(G, S5_GL, N)

    zeros = jnp.zeros((S5_PAIRS, S5_GL, N), f32)

    def pair_cols(re, im):
        re, im = re.reshape(S5_PAIRS, 2, S5_GL, N), im.reshape(S5_PAIRS, 2, S5_GL, N)
        even = jnp.concatenate([re[:, 0], zeros, im[:, 0], zeros], axis=-1)
        odd = jnp.concatenate([zeros, re[:, 1], zeros, im[:, 1]], axis=-1)
        return jnp.stack([even, odd], axis=1)

    wp = pair_cols(w_re, w_im).reshape(S5_PAIRS, 2 * S5_GL, 2 * LANES).astype(bf16)
    vt = pair_cols(v_re, -v_im).reshape(S5_GROUPS, S5_GL, 2 * LANES).astype(bf16)
    a16r, a16i = lpow(float(S5_L))
    cp = jnp.concatenate([cpr, cpi], axis=2)
    bp = jnp.concatenate([bpr, -bpi], axis=2)
    return cp, bp, wp, vt, a16r.reshape(1, G * N), a16i.reshape(1, G * N)


def _pad_lanes(v, start=0):
    out = jnp.zeros((1, LANES), f32)
    return out.at[0, start:start + v.shape[0]].set(v.astype(f32))


def _layer(x, conv0, s0, xr0, xi0, prm, *, delta_cfg, s5_cfg):
    B, T, _ = x.shape
    conv0p = jnp.concatenate([jnp.zeros((B, SUBLANES - (CONV_K - 1), CONV_CH), f32), conv0], axis=1)
    odn, us5, zs5, convout, s_new = _delta_call(
        x, prm["w1"], prm["w2"], prm["norm_w"], prm["conv_w"], prm["alog"], prm["dtb"], prm["dnw"],
        conv0p, s0, **delta_cfg)

    nstate = S5_GROUPS * S5_STATE
    if s5_cfg["flatten"]:
        tb = s5_cfg["tb"]
        uf = us5.transpose(1, 0, 2, 3).reshape(1, S5_PLANES, B * T, LANES)
        zf = zs5.reshape(1, B * T, S5_WIDTH)
        pad = tb - B * T
        uf = jnp.pad(uf, ((0, 0), (0, 0), (0, pad), (0, 0)))
        zf = jnp.pad(zf, ((0, 0), (0, pad), (0, 0)))
        os5, xr, xi = _s5_call(uf, zf, xr0.reshape(1, B, nstate), xi0.reshape(1, B, nstate),
                               prm["tmat"], prm["wp"], prm["vt"], prm["ar"], prm["ai"], prm["dvec"],
                               prm["gw"], prm["gb"], tb=tb, n_streams=B, n_chunks=T // S5_L)
        os5 = os5[0, :B * T].reshape(B, T, S5_WIDTH)
    else:
        tb = s5_cfg["tb"]
        os5, xr, xi = _s5_call(us5, zs5, xr0.reshape(B, 1, nstate), xi0.reshape(B, 1, nstate),
                               prm["tmat"], prm["wp"], prm["vt"], prm["ar"], prm["ai"], prm["dvec"],
                               prm["gw"], prm["gb"], tb=tb, n_streams=1, n_chunks=tb // S5_L)

    y = _out_call(x.reshape(B * T, D_MODEL), odn.reshape(B * T, DN_WIDTH), os5.reshape(B * T, S5_WIDTH),
                  prm["w_out"], prm["fw"], rows=min(OUT_ROWS, B * T))
    return (y.reshape(B, T, D_MODEL), convout[:, SUBLANES - (CONV_K - 1):, :], s_new,
            xr.reshape(B, S5_GROUPS, S5_STATE), xi.reshape(B, S5_GROUPS, S5_STATE))


def kernel(x_prompt, x_sample, cache_conv, state_dn, state_s5_re, state_s5_im, norm_w, w_in, conv_w, dn_A_log, dn_dt_bias, dn_norm_w, s5_A_re, s5_A_im, s5_log_dt, s5_B_re, s5_B_im, s5_C_re, s5_C_im, s5_D, glu_w, glu_b, w_out, final_norm_w):
    depth = norm_w.shape[0]
    assert depth == 1
    l = 0
    assert w_in.shape[-1] == IN_COLS
    w1 = w_in[l, :, :CONV_CH].astype(bf16)
    w2 = w_in[l, :, IN_COLS - W2_COLS:].astype(bf16)
    cp, bp, wp, vt, ar, ai = _s5_operands(
        s5_A_re[l].astype(f32), s5_A_im[l].astype(f32), s5_log_dt[l].astype(f32),
        s5_B_re[l].astype(f32), s5_B_im[l].astype(f32), s5_C_re[l].astype(f32), s5_C_im[l].astype(f32))
    prm = dict(
        w1=w1, w2=w2,
        norm_w=norm_w[l].reshape(1, D_MODEL).astype(f32),
        conv_w=conv_w[l].astype(f32),
        alog=_pad_lanes(dn_A_log[l], AB_LANE),
        dtb=_pad_lanes(dn_dt_bias[l], AB_LANE),
        dnw=dn_norm_w[l].reshape(1, DN_D).astype(f32),
        tmat=_s5mat_call(cp, bp),
        wp=wp, vt=vt, ar=ar, ai=ai,
        dvec=s5_D[l].reshape(1, S5_WIDTH).astype(f32),
        gw=glu_w[l].astype(bf16),
        gb=glu_b[l].reshape(1, S5_WIDTH).astype(f32),
        w_out=w_out[l].astype(bf16),
        fw=final_norm_w.reshape(1, D_MODEL).astype(f32),
    )

    bp = x_prompt.shape[0]
    yp, c1, d1, r1, i1 = _layer(
        x_prompt,
        jnp.zeros((bp, CONV_K - 1, CONV_CH), f32),
        jnp.zeros((bp, DN_HEADS, DN_D, DN_D), f32),
        jnp.zeros((bp, S5_GROUPS, S5_STATE), f32),
        jnp.zeros((bp, S5_GROUPS, S5_STATE), f32),
        prm,
        delta_cfg=dict(tb=256, pipelined=True),
        s5_cfg=dict(flatten=False, tb=2048))
    ys, c2, d2, r2, i2 = _layer(
        x_sample, cache_conv[l].astype(f32), state_dn[l].astype(f32),
        state_s5_re[l].astype(f32), state_s5_im[l].astype(f32),
        prm,
        delta_cfg=dict(tb=32, pipelined=False),
        s5_cfg=dict(flatten=True, tb=2048))

    return (yp, ys, c1[None], d1[None], r1[None], i1[None], c2[None], d2[None], r2[None], i2[None])
```

```python
import functools
import math

import jax
import jax.numpy as jnp
import numpy as np
from jax import lax
from jax.experimental import pallas as pl
from jax.experimental.pallas import tpu as pltpu

bf16 = jnp.bfloat16
f32 = jnp.float32

LANES = 128
SUBLANES = 8

D_MODEL = 1024
DN_HEADS = 4
DN_D = 128
DN_WIDTH = DN_HEADS * DN_D
CONV_K = 4
CONV_CH = 3 * DN_WIDTH
S5_WIDTH = 512
S5_GROUP = 16
S5_GROUPS = 32
S5_STATE = 64
S5_PAIRS = S5_GROUPS // 2
S5_L = 16
S5_GL = S5_GROUP * S5_L
OUT_SUB = 256
OUT_ROWS = 2048
S5_PLANES = S5_WIDTH // LANES
EPS = 1e-6
L2_EPS = 1e-6

IN_COLS = CONV_CH + 2 * DN_HEADS + DN_WIDTH + 2 * S5_WIDTH
W2_COLS = LANES + DN_WIDTH + 2 * S5_WIDTH
AB_LANE = LANES - 2 * DN_HEADS
W2_ZDN = LANES
W2_US5 = W2_ZDN + DN_WIDTH
W2_ZS5 = W2_US5 + S5_WIDTH

GROUP = 256
MASK_INCL, MASK_STRICT, MASK_OFF = 0, 1, 2
FOLD_MIN_ROWS = 64
VMEM_LIMIT =56 * 1024 * 1024

_NT = (((1,), (1,)), ((), ()))
_TN = (((0,), (0,)), ((), ()))


def _bdot(a, b):
    return jnp.dot(a.astype(bf16), b.astype(bf16), preferred_element_type=f32)


def _bdot_g(a, b, dims):
    return lax.dot_general(a.astype(bf16), b.astype(bf16), dims, preferred_element_type=f32)


def _sigmoid(x):
    return 1.0 / (1.0 + jnp.exp2(x * (-math.log2(math.e))))


def _silu(x):
    return x * _sigmoid(x)


def _softplus(x):
    return jnp.maximum(x, 0.0) + jnp.log1p(jnp.exp(-jnp.abs(x)))


def _gelu_tanh(x):
    c = math.sqrt(2.0 / math.pi)
    return x * (0.5 + 0.5 * jnp.tanh(x * (c + (c * 0.044715) * (x * x))))


def _zero_after(x):
    bits = pltpu.bitcast(x[0:SUBLANES, 0:LANES].astype(f32), jnp.uint32)
    half = jnp.uint32(16)
    bits = lax.shift_right_logical(lax.shift_right_logical(bits, half), half)
    return pltpu.bitcast(bits, f32)[0:1, :]


def _unit_lower_solve(nmats, rhss, eye, m_ref, n_levels, fillers):
    d32 = functools.partial(jnp.dot, preferred_element_type=f32)

    def d16(a, b):
        return d32(a, b).astype(bf16)

    def fill(after):
        if fillers:
            fillers.pop(0)(_zero_after(after[-1]))

    def fold(x, rows):
        out = x[0:rows]
        for k in range(1, GROUP // rows):
            out = out + x[k * rows:(k + 1) * rows]
        return out

    def unfold(xf, rows):
        lane_blk = lax.broadcasted_iota(jnp.int32, (rows, GROUP), 1) >> (rows.bit_length() - 1)
        return jnp.concatenate([jnp.where(lane_blk == k, xf, jnp.zeros_like(xf))
                                for k in range(GROUP // rows)], axis=0)

    n16 = [n.astype(bf16) for n in nmats]
    invs = [(eye - n * m_ref[MASK_OFF]).astype(bf16) for n in nmats]
    for lvl in range(1, n_levels):
        rows = min(GROUP, max(FOLD_MIN_ROWS, 2 ** (lvl + 1)))
        mask16 = m_ref[MASK_OFF + lvl].astype(bf16)
        lhs = invs if rows == GROUP else [fold(inv, rows) for inv in invs]
        t1 = [d16(l, n * mask16) for l, n in zip(lhs, n16)]
        fill(invs)
        t2 = [d16(t, inv) for t, inv in zip(t1, invs)]
        fill(t1)
        if rows != GROUP:
            t2 = [unfold(t, rows) for t in t2]
        invs = [inv - t for inv, t in zip(invs, t2)]
    return [d32(inv, r.astype(bf16)) for inv, r in zip(invs, rhss)]


def _segment_cumsum(tri16, g):
    g1 = g.astype(bf16)
    r1 = g - g1.astype(f32)
    g2 = r1.astype(bf16)
    g3 = (r1 - g2.astype(f32)).astype(bf16)
    s = jnp.dot(tri16, jnp.concatenate([g1, g2, g3], axis=1), preferred_element_type=f32)
    n = g.shape[1]
    return s[:, :n] + (s[:, n:2 * n] + s[:, 2 * n:])


WIN_ROWS = 128


def _win_body(w_ref, w1_ref, w2_ref):
    w1_ref[...] = w_ref[:, 0:CONV_CH].astype(bf16)
    w2_ref[...] = w_ref[:, IN_COLS - W2_COLS:IN_COLS].astype(bf16)


def _win_call(w_in):
    return pl.pallas_call(
        _win_body,
        grid=(D_MODEL // WIN_ROWS,),
        in_specs=[pl.BlockSpec((WIN_ROWS, IN_COLS), lambda i: (i, 0))],
        out_specs=[pl.BlockSpec((WIN_ROWS, CONV_CH), lambda i: (i, 0)),
                   pl.BlockSpec((WIN_ROWS, W2_COLS), lambda i: (i, 0))],
        out_shape=[jax.ShapeDtypeStruct((D_MODEL, CONV_CH), bf16),
                   jax.ShapeDtypeStruct((D_MODEL, W2_COLS), bf16)],
        compiler_params=pltpu.CompilerParams(dimension_semantics=("arbitrary",)),
        name="winprep",
    )(w_in)


def _delta_body(x_ref, w1_ref, w2_ref, nw_ref, cw_ref, alog_ref, dtb_ref, dnw_ref, m_ref, conv0_ref, s0_ref,
                odn_ref, us5_ref, zs5_ref, convout_ref, s_ref,
                cbuf, qn_scr, kn_scr, v_scr, gc_scr, beta_scr, zdn_scr, *, nb, tb, n_t, pipelined):
    rows = nb * tb
    n_groups = rows // GROUP
    segs = GROUP // tb
    n_levels = tb.bit_length() - 1
    step = pl.program_id(0)
    fresh = step < n_t

    @pl.when(step == 0)
    def _init():
        cbuf[:, 0:SUBLANES, :] = conv0_ref[...]
        s_ref[...] = s0_ref[...]
        if pipelined:
            for scr in (qn_scr, kn_scr, v_scr, gc_scr, beta_scr, zdn_scr):
                scr[...] = jnp.zeros_like(scr)

    front = {}

    def project(w_ref, lo, width):
        return jnp.dot(front["h"], w_ref[:, lo:lo + width], preferred_element_type=f32)

    def a_norm_in():
        x = x_ref[...].reshape(rows, D_MODEL)
        ms = jnp.mean(x * x, axis=-1, keepdims=True)
        front["h"] = (x * lax.rsqrt(ms + EPS) * nw_ref[...]).astype(bf16)

    def a_project_qkv(s):
        front[("raw", s)] = project(w1_ref, s * DN_WIDTH, DN_WIDTH)

    def a_project_us5():
        u = project(w2_ref, W2_US5, S5_WIDTH)
        for j in range(S5_PLANES):
            us5_ref[:, j] = u[:, j * LANES:(j + 1) * LANES].reshape(nb, tb, LANES)

    def a_project_zs5():
        zs5_ref[...] = project(w2_ref, W2_ZS5, S5_WIDTH).reshape(nb, tb, S5_WIDTH)

    def a_project_zdn():
        front["zdn"] = project(w2_ref, W2_ZDN, DN_WIDTH)

    def a_project_gates():
        ab = project(w2_ref, 0, LANES)
        front["g"] = -jnp.exp(alog_ref[...]) * _softplus(ab + dtb_ref[...])
        front["beta"] = _sigmoid(ab)

    projections = ([functools.partial(a_project_qkv, s) for s in range(3)]
                   + [a_project_us5, a_project_zs5, a_project_zdn, a_project_gates])

    def a_conv(b, s, zero):
        cols = slice(s * DN_WIDTH, (s + 1) * DN_WIDTH)
        cbuf[b, SUBLANES:SUBLANES + tb, cols] = front[("raw", s)][b * tb:(b + 1) * tb, :]
        taps = [cw_ref[j:j + 1, cols] for j in range(CONV_K)]
        if zero is not None:
            zero_w = jnp.concatenate([zero] * (DN_WIDTH // LANES), axis=1)
            taps = [t + zero_w for t in taps]
        acc = cbuf[b, 5:5 + tb, cols] * taps[0]
        for j in range(1, CONV_K):
            acc = acc + cbuf[b, 5 + j:5 + j + tb, cols] * taps[j]
        front[("act", b, s)] = _silu(acc)
        tail = cbuf[b, tb:tb + SUBLANES, cols]
        if pipelined:
            tail = jnp.where(fresh, tail, cbuf[b, 0:SUBLANES, cols])
        cbuf[b, 0:SUBLANES, cols] = tail
        convout_ref[b, :, cols] = tail

    def a_norm(b, hd, zero):
        lo = hd * DN_D
        q = front[("act", b, 0)][:, lo:lo + DN_D]
        k = front[("act", b, 1)][:, lo:lo + DN_D]
        if zero is not None:
            q, k = q + zero, k + zero
        front[("qn", b, hd)] = (q * lax.rsqrt(jnp.sum(q * q, axis=-1, keepdims=True) + L2_EPS)
                                * (DN_D ** -0.5))
        front[("kn", b, hd)] = k * lax.rsqrt(jnp.sum(k * k, axis=-1, keepdims=True) + L2_EPS)

    def a_gates(zero):
        tri16 = m_ref[MASK_INCL].astype(bf16)
        g = front["g"] if zero is None else front["g"] + zero
        front["gc"] = [_segment_cumsum(tri16, g[gi * GROUP:(gi + 1) * GROUP])
                       for gi in range(n_groups)]

    def a_store():
        for b in range(nb):
            r = slice(b * tb, (b + 1) * tb)
            for hd in range(DN_HEADS):
                c = slice(hd * DN_D, (hd + 1) * DN_D)
                qn_scr[r, c] = front[("qn", b, hd)]
                kn_scr[r, c] = front[("kn", b, hd)]
            v_scr[r, :] = front[("act", b, 2)]
        for gi in range(n_groups):
            gc_scr[gi * GROUP:(gi + 1) * GROUP, :] = front["gc"][gi]
        beta_scr[...] = front["beta"]
        zdn_scr[...] = front["zdn"]

    fillers = ([functools.partial(a_conv, b, s) for b in range(nb) for s in range(3)] + [a_gates]
               + [functools.partial(a_norm, b, hd) for b in range(nb) for hd in range(DN_HEADS)])

    incl = m_ref[MASK_INCL]
    strict = m_ref[MASK_STRICT]
    eye = incl - strict
    dnw = dnw_ref[...]
    chains = [(gi, hd) for gi in range(n_groups) for hd in range(DN_HEADS)]
    back = {}

    def stream_of(gi, sg):
        return (gi * GROUP) // tb + sg

    for name in ("q16", "qe16", "k16", "kb16", "rhs", "dmat", "kdec", "sdecay", "nmat", "attn"):
        back[name] = [None] * len(chains)

    def b_load(ci):
        gi, hd = chains[ci]
        r = slice(gi * GROUP, (gi + 1) * GROUP)
        c = slice(hd * DN_D, (hd + 1) * DN_D)
        q, k, v = qn_scr[r, c], kn_scr[r, c], v_scr[r, c]
        beta = beta_scr[r, AB_LANE + DN_HEADS + hd:AB_LANE + DN_HEADS + hd + 1]
        g = gc_scr[r, AB_LANE + hd:AB_LANE + hd + 1]
        eg = jnp.exp(g)
        kb = k * beta
        gcb = jnp.broadcast_to(g * math.log2(math.e), (GROUP, GROUP))
        back["dmat"][ci] = jnp.exp2(jnp.minimum(gcb - gcb.T, 0.0)) * incl
        back["q16"][ci] = q.astype(bf16)
        back["qe16"][ci] = (q * eg).astype(bf16)
        back["k16"][ci] = k.astype(bf16)
        back["kb16"][ci] = kb.astype(bf16)
        back["rhs"][ci] = jnp.concatenate([v * beta, kb * eg], axis=1)
        kd, sd = [], []
        for sg in range(segs):
            a0 = sg * tb
            glast = g[a0 + tb - 1:a0 + tb, :]
            kd.append((k[a0:a0 + tb] * jnp.exp(glast - g[a0:a0 + tb])).astype(bf16))
            sd.append(jnp.exp(glast))
        back["kdec"][ci] = kd
        back["sdecay"][ci] = sd

    def b_mats(ci):
        d = functools.partial(lax.dot_general, dimension_numbers=_NT, preferred_element_type=f32)
        dm = back["dmat"][ci]
        back["nmat"][ci] = d(back["kb16"][ci], back["k16"][ci]) * (dm * strict)
        back["attn"][ci] = (d(back["q16"][ci], back["k16"][ci]) * dm).astype(bf16)

    def b_state(sols):
        s_old = {(b, hd): s_ref[b, hd] for b in range(nb) for hd in range(DN_HEADS)}
        s16 = {key: val.astype(bf16) for key, val in s_old.items()}
        v_news, o_states = [], []
        for (gi, hd), sol, qe16 in zip(chains, sols, back["qe16"]):
            u = sol[:, :DN_D]
            w16 = sol[:, DN_D:].astype(bf16)
            vn, os_ = [], []
            for sg in range(segs):
                a0 = sg * tb
                st = s16[(stream_of(gi, sg), hd)]
                vn.append(u[a0:a0 + tb] - jnp.dot(w16[a0:a0 + tb], st, preferred_element_type=f32))
                os_.append(jnp.dot(qe16[a0:a0 + tb], st, preferred_element_type=f32))
            v_news.append(vn[0] if segs == 1 else jnp.concatenate(vn, axis=0))
            o_states.append(os_[0] if segs == 1 else jnp.concatenate(os_, axis=0))
        vn16s = [vn.astype(bf16) for vn in v_news]
        outs = [os_ + jnp.dot(a, vn, preferred_element_type=f32)
                for os_, a, vn in zip(o_states, back["attn"], vn16s)]

        s_new = {}
        for (gi, hd), kd, sd, vn in zip(chains, back["kdec"], back["sdecay"], vn16s):
            for sg in range(segs):
                a0 = sg * tb
                key = (stream_of(gi, sg), hd)
                s_new[key] = (s_old[key] * sd[sg]
                              + lax.dot_general(kd[sg], vn[a0:a0 + tb], _TN, preferred_element_type=f32))

        o_rows = []
        for gi in range(n_groups):
            o_heads = []
            for hd in range(DN_HEADS):
                o = outs[gi * DN_HEADS + hd]
                on = o * lax.rsqrt(jnp.mean(o * o, axis=-1, keepdims=True) + EPS) * dnw
                zd = zdn_scr[gi * GROUP:(gi + 1) * GROUP, hd * DN_D:(hd + 1) * DN_D]
                o_heads.append(on * _silu(zd))
            o_rows.append(jnp.concatenate(o_heads, axis=1))
        o_all = o_rows[0] if n_groups == 1 else jnp.concatenate(o_rows, axis=0)
        odn_ref[...] = o_all.astype(bf16).reshape(nb, tb, DN_WIDTH)
        s_ref[...] = jnp.stack([jnp.stack([s_new[(b, hd)] for hd in range(DN_HEADS)]) for b in range(nb)])

    n_ch = len(chains)
    if pipelined:
        a_norm_in()
        for ci in range(n_ch):
            b_load(ci)
        back_setup = [functools.partial(b_mats, ci) for ci in range(n_ch)]
        convs = [fillers.pop(0) for _ in range(nb * 3)]
        per_piece = -(-len(back_setup) // len(projections))
        for idx, piece in enumerate(projections):
            piece()
            for thunk in back_setup[:per_piece]:
                thunk()
            del back_setup[:per_piece]
            if idx >= 3 and convs:
                convs.pop(0)(None)
        while back_setup or convs:
            if back_setup:
                back_setup.pop(0)()
            if convs:
                convs.pop(0)(None)
        sols = _unit_lower_solve(back["nmat"], back["rhs"], eye, m_ref, n_levels, fillers)
        for fill in fillers:
            fill(None)
        b_state(sols)
        a_store()
    else:
        a_norm_in()
        for piece in projections:
            piece()
        for fill in fillers:
            fill(None)
        a_store()
        for ci in range(n_ch):
            b_load(ci)
        for ci in range(n_ch):
            b_mats(ci)
        b_state(_unit_lower_solve(back["nmat"], back["rhs"], eye, m_ref, n_levels, []))


def _delta_masks(tb):
    r = np.arange(GROUP)[:, None]
    c = np.arange(GROUP)[None, :]
    same = (r // tb) == (c // tb)
    ms = [same & (r >= c), same & (r > c)]
    for l in range(tb.bit_length() - 1):
        ms.append(((r >> (l + 1)) == (c >> (l + 1))) & (((r >> l) & 1) == 1) & (((c >> l) & 1) == 0))
    return jnp.asarray(np.stack(ms).astype(np.float32))


def _delta_call(x, w1, w2, norm_w, conv_w, alog, dtb, dnw, conv0, s0, *, tb, pipelined):
    nb, T, _ = x.shape
    assert GROUP % tb == 0 and (nb * tb) % GROUP == 0 and T % tb == 0
    n_t = T // tb
    rows = nb * tb
    if pipelined:
        grid = (n_t + 1,)
        cur = lambda i: jnp.minimum(i, n_t - 1)
        prev = lambda i: jnp.maximum(i - 1, 0)
    else:
        grid = (n_t,)
        cur = prev = lambda i: i
    const2 = lambda i: (0, 0)
    const3 = lambda i: (0, 0, 0)
    const4 = lambda i: (0, 0, 0, 0)
    body = functools.partial(_delta_body, nb=nb, tb=tb, n_t=n_t, pipelined=pipelined)
    masks = _delta_masks(tb)
    one = pl.Buffered(1)
    return pl.pallas_call(
        body,
        grid=grid,
        in_specs=[
            pl.BlockSpec((nb, tb, D_MODEL), lambda i: (0, cur(i), 0)),
            pl.BlockSpec((D_MODEL, CONV_CH), const2, pipeline_mode=one),
            pl.BlockSpec((D_MODEL, W2_COLS), const2, pipeline_mode=one),
            pl.BlockSpec((1, D_MODEL), const2),
            pl.BlockSpec((CONV_K, CONV_CH), const2),
            pl.BlockSpec((1, LANES), const2),
            pl.BlockSpec((1, LANES), const2),
            pl.BlockSpec((1, DN_D), const2),
            pl.BlockSpec(masks.shape, const3, pipeline_mode=one),
            pl.BlockSpec((nb, SUBLANES, CONV_CH), const3),
            pl.BlockSpec((nb, DN_HEADS, DN_D, DN_D), const4),
        ],
        out_specs=[
            pl.BlockSpec((nb, tb, DN_WIDTH), lambda i: (0, prev(i), 0)),
            pl.BlockSpec((nb, S5_PLANES, tb, LANES), lambda i: (0, 0, cur(i), 0)),
            pl.BlockSpec((nb, tb, S5_WIDTH), lambda i: (0, cur(i), 0)),
            pl.BlockSpec((nb, SUBLANES, CONV_CH), const3),
            pl.BlockSpec((nb, DN_HEADS, DN_D, DN_D), const4),
        ],
        out_shape=[
            jax.ShapeDtypeStruct((nb, T, DN_WIDTH), bf16),
            jax.ShapeDtypeStruct((nb, S5_PLANES, T, LANES), f32),
            jax.ShapeDtypeStruct((nb, T, S5_WIDTH), f32),
            jax.ShapeDtypeStruct((nb, SUBLANES, CONV_CH), f32),
            jax.ShapeDtypeStruct((nb, DN_HEADS, DN_D, DN_D), f32),
        ],
        scratch_shapes=[
            pltpu.VMEM((nb, tb + SUBLANES, CONV_CH), f32),
            pltpu.VMEM((rows, DN_WIDTH), f32),
            pltpu.VMEM((rows, DN_WIDTH), f32),
            pltpu.VMEM((rows, DN_WIDTH), f32),
            pltpu.VMEM((rows, LANES), f32),
            pltpu.VMEM((rows, LANES), f32),
            pltpu.VMEM((rows, DN_WIDTH), f32),
        ],
        compiler_params=pltpu.CompilerParams(
            dimension_semantics=("arbitrary",), vmem_limit_bytes=VMEM_LIMIT),
        name="delta",
    )(x, w1, w2, norm_w, conv_w, alog, dtb, dnw, masks, conv0, s0)


S5MAT_GROUPS_PER_STEP = 8


def _split16(a):
    hi = a.astype(bf16)
    return hi, (a - hi.astype(f32)).astype(bf16)


def _s5mat_body(cp_ref, bp_ref, t_ref):
    ri = lax.broadcasted_iota(jnp.int32, (S5_GL, S5_GL), 0) >> 4
    ci = lax.broadcasted_iota(jnp.int32, (S5_GL, S5_GL), 1) >> 4
    causal = ri >= ci
    d = functools.partial(lax.dot_general, dimension_numbers=_NT, preferred_element_type=f32)
    for g in range(S5MAT_GROUPS_PER_STEP):
        ch, cl = _split16(cp_ref[g])
        bh, bl = _split16(bp_ref[g])
        t = d(ch, bh) + (d(ch, bl) + d(cl, bh))
        t_ref[g] = jnp.where(causal, t, 0.0).astype(bf16)


def _s5mat_call(cp, bp):
    n = S5MAT_GROUPS_PER_STEP
    return pl.pallas_call(
        _s5mat_body,
        grid=(S5_GROUPS // n,),
        in_specs=[
            pl.BlockSpec((n, S5_GL, 2 * S5_STATE), lambda g: (g, 0, 0)),
            pl.BlockSpec((n, S5_GL, 2 * S5_STATE), lambda g: (g, 0, 0)),
        ],
        out_specs=pl.BlockSpec((n, S5_GL, S5_GL), lambda g: (g, 0, 0)),
        out_shape=jax.ShapeDtypeStruct((S5_GROUPS, S5_GL, S5_GL), bf16),
        compiler_params=pltpu.CompilerParams(dimension_semantics=("arbitrary",)),
        name="s5mat",
    )(cp, bp)


def _s5_body(u_ref, z_ref, xr0_ref, xi0_ref, t_ref, wp_ref, vt_ref, ar_ref, ai_ref, d_ref,
             gw_ref, gb_ref,
             o_ref, xr_ref, xi_ref,
             at_scr, er_scr, ei_scr, xinr_scr, xini_scr, yt_scr, y_scr, *, tb, n_streams, n_chunks):
    R = tb // S5_L
    used = n_streams * n_chunks

    @pl.when(pl.program_id(1) == 0)
    def _init():
        xr_ref[...] = xr0_ref[...]
        xi_ref[...] = xi0_ref[...]

    for f in range(S5_L):
        for j in range(S5_PLANES):
            ut = u_ref[0, j, pl.ds(f, R, stride=S5_L), :]
            ut_t = ut.astype(bf16).T
            for gl in range(LANES // S5_GROUP):
                g = j * (LANES // S5_GROUP) + gl
                at_scr[g, f * S5_GROUP:(f + 1) * S5_GROUP, :] = ut_t[gl * S5_GROUP:(gl + 1) * S5_GROUP, :]

    for p in range(S5_PAIRS):
        a_t = at_scr[2 * p:2 * p + 2].reshape(2 * S5_GL, R)
        e = lax.dot_general(a_t, wp_ref[p], _TN, preferred_element_type=f32)
        er_scr[:, p * LANES:(p + 1) * LANES] = e[:, :LANES]
        ei_scr[:, p * LANES:(p + 1) * LANES] = e[:, LANES:]

    if used < R:
        xinr_scr[...] = jnp.zeros_like(xinr_scr)
        xini_scr[...] = jnp.zeros_like(xini_scr)

    a_r = ar_ref[...]
    a_i = ai_ref[...]
    for s in range(n_streams):
        def step(c, carry):
            xr, xi = carry
            row = s * n_chunks + c
            xinr_scr[pl.ds(row, 1), :] = xr
            xini_scr[pl.ds(row, 1), :] = xi
            er = er_scr[pl.ds(row, 1), :]
            ei = ei_scr[pl.ds(row, 1), :]
            return a_r * xr - a_i * xi + er, a_r * xi + a_i * xr + ei

        xr, xi = lax.fori_loop(0, n_chunks, step, (xr_ref[0, s:s + 1, :], xi_ref[0, s:s + 1, :]),
                               unroll=min(n_chunks, 4))
        xr_ref[0, s:s + 1, :] = xr
        xi_ref[0, s:s + 1, :] = xi

    for g in range(S5_GROUPS):
        p = g // 2
        xin = jnp.concatenate([xinr_scr[:, p * LANES:(p + 1) * LANES],
                               xini_scr[:, p * LANES:(p + 1) * LANES]], axis=1).astype(bf16)
        yt_scr[g] = (jnp.dot(t_ref[g], at_scr[g], preferred_element_type=f32)
                     + lax.dot_general(vt_ref[g], xin, _NT, preferred_element_type=f32))

    for f in range(S5_L):
        gpp = LANES // S5_GROUP
        for j in range(S5_PLANES):
            yt = yt_scr[j * gpp:(j + 1) * gpp, f * S5_GROUP:(f + 1) * S5_GROUP, :].reshape(LANES, R)
            y_scr[j, pl.ds(f, R, stride=S5_L), :] = yt.T

    for rb in range(tb // OUT_SUB):
        sl = pl.ds(rb * OUT_SUB, OUT_SUB)
        y_intra = jnp.concatenate([y_scr[j, sl, :] for j in range(S5_PLANES)], axis=1)
        u_nat = jnp.concatenate([u_ref[0, j, sl, :] for j in range(S5_PLANES)], axis=1)
        y = y_intra + d_ref[...] * u_nat
        gy = _gelu_tanh(y)
        gate = _sigmoid(_bdot(gy, gw_ref[...]) + gb_ref[...])
        o_ref[0, sl, :] = (gy * gate * _silu(z_ref[0, sl, :])).astype(bf16)


def _s5_call(u, z, xr0, xi0, tmat, wp, vt, ar, ai, dvec, gw, gb, *, tb, n_streams, n_chunks):
    G, T, _ = z.shape
    grid = (G, T // tb)
    R = tb // S5_L
    nstate = S5_GROUPS * S5_STATE
    c2 = lambda g, t: (0, 0)
    c3 = lambda g, t: (0, 0, 0)
    body = functools.partial(_s5_body, tb=tb, n_streams=n_streams, n_chunks=n_chunks)
    one = pl.Buffered(1)
    return pl.pallas_call(
        body,
        grid=grid,
        in_specs=[
            pl.BlockSpec((1, S5_PLANES, tb, LANES), lambda g, t: (g, 0, t, 0)),
            pl.BlockSpec((1, tb, S5_WIDTH), lambda g, t: (g, t, 0)),
            pl.BlockSpec((1, n_streams, nstate), lambda g, t: (g, 0, 0)),
            pl.BlockSpec((1, n_streams, nstate), lambda g, t: (g, 0, 0)),
            pl.BlockSpec((S5_GROUPS, S5_GL, S5_GL), c3, pipeline_mode=one),
            pl.BlockSpec((S5_PAIRS, 2 * S5_GL, 2 * LANES), c3, pipeline_mode=one),
            pl.BlockSpec((S5_GROUPS, S5_GL, 2 * LANES), c3, pipeline_mode=one),
            pl.BlockSpec((1, nstate), c2),
            pl.BlockSpec((1, nstate), c2),
            pl.BlockSpec((1, S5_WIDTH), c2),
            pl.BlockSpec((S5_WIDTH, S5_WIDTH), c2),
            pl.BlockSpec((1, S5_WIDTH), c2),
        ],
        out_specs=[
            pl.BlockSpec((1, tb, S5_WIDTH), lambda g, t: (g, t, 0)),
            pl.BlockSpec((1, n_streams, nstate), lambda g, t: (g, 0, 0)),
            pl.BlockSpec((1, n_streams, nstate), lambda g, t: (g, 0, 0)),
        ],
        out_shape=[
            jax.ShapeDtypeStruct((G, T, S5_WIDTH), bf16),
            jax.ShapeDtypeStruct((G, n_streams, nstate), f32),
            jax.ShapeDtypeStruct((G, n_streams, nstate), f32),
        ],
        scratch_shapes=[
            pltpu.VMEM((S5_GROUPS, S5_GL, R), bf16),
            pltpu.VMEM((R, nstate), f32),
            pltpu.VMEM((R, nstate), f32),
            pltpu.VMEM((R, nstate), f32),
            pltpu.VMEM((R, nstate), f32),
            pltpu.VMEM((S5_GROUPS, S5_GL, R), f32),
            pltpu.VMEM((S5_PLANES, tb, LANES), f32),
        ],
        compiler_params=pltpu.CompilerParams(
            dimension_semantics=("arbitrary", "arbitrary"), vmem_limit_bytes=VMEM_LIMIT),
        name="s5",
    )(u, z, xr0, xi0, tmat, wp, vt, ar, ai, dvec, gw, gb)


def _out_body(x_ref, odn_ref, os5_ref, wo_ref, fw_ref, y_ref):
    acc = (x_ref[...]
           + jnp.dot(odn_ref[...], wo_ref[0:DN_WIDTH, :], preferred_element_type=f32)
           + jnp.dot(os5_ref[...], wo_ref[DN_WIDTH:, :], preferred_element_type=f32))
    ms = jnp.mean(acc * acc, axis=-1, keepdims=True)
    y_ref[...] = acc * lax.rsqrt(ms + EPS) * fw_ref[...]


def _out_call(x2, odn2, os52, w_out16, fw, *, rows):
    n = x2.shape[0]
    return pl.pallas_call(
        _out_body,
        grid=(n // rows,),
        in_specs=[
            pl.BlockSpec((rows, D_MODEL), lambda i: (i, 0)),
            pl.BlockSpec((rows, DN_WIDTH), lambda i: (i, 0)),
            pl.BlockSpec((rows, S5_WIDTH), lambda i: (i, 0)),
            pl.BlockSpec((D_MODEL, D_MODEL), lambda i: (0, 0)),
            pl.BlockSpec((1, D_MODEL), lambda i: (0, 0)),
        ],
        out_specs=pl.BlockSpec((rows, D_MODEL), lambda i: (i, 0)),
        out_shape=jax.ShapeDtypeStruct((n, D_MODEL), f32),
        compiler_params=pltpu.CompilerParams(
            dimension_semantics=("arbitrary",), vmem_limit_bytes=VMEM_LIMIT),
        name="outproj",
    )(x2, odn2, os52, w_out16, fw)


def _s5_operands(a_re, a_im, log_dt, b_re, b_im, c_re, c_im):
    lam_re = jnp.minimum(a_re, -1e-4)
    lam_im = a_im
    dt = jnp.exp(log_dt)[:, None]
    ldt_re, ldt_im = lam_re * dt, lam_im * dt

    def lpow(k):
        mag = jnp.exp(ldt_re * k)
        return mag * jnp.cos(ldt_im * k), mag * jnp.sin(ldt_im * k)

    lb_re, lb_im = lpow(1.0)
    den = lam_re * lam_re + lam_im * lam_im
    f_re = ((lb_re - 1.0) * lam_re + lb_im * lam_im) / den
    f_im = (lb_im * lam_re - (lb_re - 1.0) * lam_im) / den
    bb_re = f_re[..., None] * b_re - f_im[..., None] * b_im
    bb_im = f_re[..., None] * b_im + f_im[..., None] * b_re

    fr = jnp.arange(S5_L, dtype=f32)
    G, N = S5_GROUPS, S5_STATE

    def cmul(ar, ai, br, bi):
        return ar * br - ai * bi, ar * bi + ai * br

    pr, pi = lpow(fr[:, None, None])
    pr, pi = pr.transpose(1, 0, 2), pi.transpose(1, 0, 2)
    cpr, cpi = cmul(c_re[:, None], c_im[:, None], pr[:, :, None], pi[:, :, None])
    cpr, cpi = cpr.reshape(G, S5_GL, N), cpi.reshape(G, S5_GL, N)
    bbr_t, bbi_t = bb_re.transpose(0, 2, 1), bb_im.transpose(0, 2, 1)
    qr, qi = lpow(-fr[:, None, None])
    qr, qi = qr.transpose(1, 0, 2), qi.transpose(1, 0, 2)
    bpr, bpi = cmul(qr[:, :, None], qi[:, :, None], bbr_t[:, None], bbi_t[:, None])
    bpr, bpi = bpr.reshape(G, S5_GL, N), bpi.reshape(G, S5_GL, N)

    wr_, wi_ = lpow((S5_L - 1.0) - fr[:, None, None])
    wr_, wi_ = wr_.transpose(1, 0, 2), wi_.transpose(1, 0, 2)
    w_re, w_im = cmul(wr_[:, :, None], wi_[:, :, None], bbr_t[:, None], bbi_t[:, None])
    w_re, w_im = w_re.reshape(G, S5_GL, N), w_im.reshape(G, S5_GL, N)
    vr_, vi_ = lpow(fr[:, None, None] + 1.0)
    vr_, vi_ = vr_.transpose(1, 0, 2), vi_.transpose(1, 0, 2)
    v_re, v_im = cmul(c_re[:, None], c_im[:, None], vr_[:, :, None], vi_[:, :, None])
    v_re, v_im = v_re.reshape(G, S5_GL, N), v_im.reshape(G, S5_GL, N)

    zeros = jnp.zeros((S5_PAIRS, S5_GL, N), f32)

    def pair_cols(re, im):
        re, im = re.reshape(S5_PAIRS, 2, S5_GL, N), im.reshape(S5_PAIRS, 2, S5_GL, N)
        even = jnp.concatenate([re[:, 0], zeros, im[:, 0], zeros], axis=-1)
        odd = jnp.concatenate([zeros, re[:, 1], zeros, im[:, 1]], axis=-1)
        return jnp.stack([even, odd], axis=1)

    wp = pair_cols(w_re, w_im).reshape(S5_PAIRS, 2 * S5_GL, 2 * LANES).astype(bf16)
    vt = pair_cols(v_re, -v_im).reshape(S5_GROUPS, S5_GL, 2 * LANES).astype(bf16)
    a16r, a16i = lpow(float(S5_L))
    cp = jnp.concatenate([cpr, cpi], axis=2)
    bp = jnp.concatenate([bpr, -bpi], axis=2)
    return cp, bp, wp, vt, a16r.reshape(1, G * N), a16i.reshape(1, G * N)


def _pad_lanes(v, start=0):
    out = jnp.zeros((1, LANES), f32)
    return out.at[0, start:start + v.shape[0]].set(v.astype(f32))


def _layer(x, conv0, s0, xr0, xi0, prm, *, delta_cfg, s5_cfg):
    B, T, _ = x.shape
    conv0p = jnp.concatenate([jnp.zeros((B, SUBLANES - (CONV_K - 1), CONV_CH), f32), conv0], axis=1)
    odn, us5, zs5, convout, s_new = _delta_call(
        x, prm["w1"], prm["w2"], prm["norm_w"], prm["conv_w"], prm["alog"], prm["dtb"], prm["dnw"],
        conv0p, s0, **delta_cfg)

    nstate = S5_GROUPS * S5_STATE
    if s5_cfg["flatten"]:
        tb = s5_cfg["tb"]
        uf = us5.transpose(1, 0, 2, 3).reshape(1, S5_PLANES, B * T, LANES)
        zf = zs5.reshape(1, B * T, S5_WIDTH)
        pad = tb - B * T
        uf = jnp.pad(uf, ((0, 0), (0, 0), (0, pad), (0, 0)))
        zf = jnp.pad(zf, ((0, 0), (0, pad), (0, 0)))
        os5, xr, xi = _s5_call(uf, zf, xr0.reshape(1, B, nstate), xi0.reshape(1, B, nstate),
                               prm["tmat"], prm["wp"], prm["vt"], prm["ar"], prm["ai"], prm["dvec"],
                               prm["gw"], prm["gb"], tb=tb, n_streams=B, n_chunks=T // S5_L)
        os5 = os5[0, :B * T].reshape(B, T, S5_WIDTH)
    else:
        tb = s5_cfg["tb"]
        os5, xr, xi = _s5_call(us5, zs5, xr0.reshape(B, 1, nstate), xi0.reshape(B, 1, nstate),
                               prm["tmat"], prm["wp"], prm["vt"], prm["ar"], prm["ai"], prm["dvec"],
                               prm["gw"], prm["gb"], tb=tb, n_streams=1, n_chunks=tb // S5_L)

    y = _out_call(x.reshape(B * T, D_MODEL), odn.reshape(B * T, DN_WIDTH), os5.reshape(B * T, S5_WIDTH),
                  prm["w_out"], prm["fw"], rows=min(OUT_ROWS, B * T))
    return (y.reshape(B, T, D_MODEL), convout[:, SUBLANES - (CONV_K - 1):, :], s_new,
            xr.reshape(B, S5_GROUPS, S5_STATE), xi.reshape(B, S5_GROUPS, S5_STATE))


def kernel(x_prompt, x_sample, cache_conv, state_dn, state_s5_re, state_s5_im, norm_w, w_in, conv_w, dn_A_log, dn_dt_bias, dn_norm_w, s5_A_re, s5_A_im, s5_log_dt, s5_B_re, s5_B_im, s5_C_re, s5_C_im, s5_D, glu_w, glu_b, w_out, final_norm_w):
    depth = norm_w.shape[0]
    assert depth == 1
    l = 0
    assert w_in.shape[-1] == IN_COLS
    w1, w2 = _win_call(w_in[l])
    cp, bp, wp, vt, ar, ai = _s5_operands(
        s5_A_re[l].astype(f32), s5_A_im[l].astype(f32), s5_log_dt[l].astype(f32),
        s5_B_re[l].astype(f32), s5_B_im[l].astype(f32), s5_C_re[l].astype(f32), s5_C_im[l].astype(f32))
    prm = dict(
        w1=w1, w2=w2,
        norm_w=norm_w[l].reshape(1, D_MODEL).astype(f32),
        conv_w=conv_w[l].astype(f32),
        alog=_pad_lanes(dn_A_log[l], AB_LANE),
        dtb=_pad_lanes(dn_dt_bias[l], AB_LANE),
        dnw=dn_norm_w[l].reshape(1, DN_D).astype(f32),
        tmat=_s5mat_call(cp, bp),
        wp=wp, vt=vt, ar=ar, ai=ai,
        dvec=s5_D[l].reshape(1, S5_WIDTH).astype(f32),
        gw=glu_w[l].astype(bf16),
        gb=glu_b[l].reshape(1, S5_WIDTH).astype(f32),
        w_out=w_out[l].astype(bf16),
        fw=final_norm_w.reshape(1, D_MODEL).astype(f32),
    )

    bp = x_prompt.shape[0]
    yp, c1, d1, r1, i1 = _layer(
        x_prompt,
        jnp.zeros((bp, CONV_K - 1, CONV_CH), f32),
        jnp.zeros((bp, DN_HEADS, DN_D, DN_D), f32),
        jnp.zeros((bp, S5_GROUPS, S5_STATE), f32),
        jnp.zeros((bp, S5_GROUPS, S5_STATE), f32),
        prm,
        delta_cfg=dict(tb=256, pipelined=True),
        s5_cfg=dict(flatten=False, tb=2048))
    ys, c2, d2, r2, i2 = _layer(
        x_sample, cache_conv[l].astype(f32), state_dn[l].astype(f32),
        state_s5_re[l].astype(f32), state_s5_im[l].astype(f32),
        prm,
        delta_cfg=dict(tb=32, pipelined=False),
        s5_cfg=dict(flatten=True, tb=2048))

    return (yp, ys, c1[None], d1[None], r1[None], i1[None], c2[None], d2[None], r2[None], i2[None])
```

```python
import functools
import math

import jax
import jax.numpy as jnp
import numpy as np
from jax import lax
from jax.experimental import pallas as pl
from jax.experimental.pallas import tpu as pltpu

bf16 = jnp.bfloat16
f32 = jnp.float32

LANES = 128
SUBLANES = 8

D_MODEL = 1024
DN_HEADS = 4
DN_D = 128
DN_WIDTH = DN_HEADS * DN_D
CONV_K = 4
CONV_CH = 3 * DN_WIDTH
S5_WIDTH = 512
S5_GROUP = 16
S5_GROUPS = 32
S5_STATE = 64
S5_PAIRS = S5_GROUPS // 2
S5_L = 16
S5_GL = S5_GROUP * S5_L
OUT_SUB = 256
OUT_ROWS = 2048
S5_PLANES = S5_WIDTH // LANES
EPS = 1e-6
L2_EPS = 1e-6

IN_COLS = CONV_CH + 2 * DN_HEADS + DN_WIDTH + 2 * S5_WIDTH
W2_COLS = LANES + DN_WIDTH + 2 * S5_WIDTH
AB_LANE = LANES - 2 * DN_HEADS
W2_ZDN = LANES
W2_US5 = W2_ZDN + DN_WIDTH
W2_ZS5 = W2_US5 + S5_WIDTH

GROUP = 256
MASK_INCL, MASK_STRICT, MASK_OFF = 0, 1, 2
FOLD_MIN_ROWS = 64
VMEM_LIMIT =56 * 1024 * 1024

_NT = (((1,), (1,)), ((), ()))
_TN = (((0,), (0,)), ((), ()))


def _bdot(a, b):
    return jnp.dot(a.astype(bf16), b.astype(bf16), preferred_element_type=f32)


def _bdot_g(a, b, dims):
    return lax.dot_general(a.astype(bf16), b.astype(bf16), dims, preferred_element_type=f32)


def _sigmoid(x):
    return 1.0 / (1.0 + jnp.exp2(x * (-math.log2(math.e))))


def _silu(x):
    return x * _sigmoid(x)


def _softplus(x):
    return jnp.maximum(x, 0.0) + jnp.log1p(jnp.exp(-jnp.abs(x)))


def _gelu_tanh(x):
    c = math.sqrt(2.0 / math.pi)
    return x * (0.5 + 0.5 * jnp.tanh(x * (c + (c * 0.044715) * (x * x))))


def _zero_after(x):
    bits = pltpu.bitcast(x[0:SUBLANES, 0:LANES].astype(f32), jnp.uint32)
    half = jnp.uint32(16)
    bits = lax.shift_right_logical(lax.shift_right_logical(bits, half), half)
    return pltpu.bitcast(bits, f32)[0:1, :]


def _unit_lower_solve(nmats, rhss, eye, m_ref, n_levels, fillers):
    d32 = functools.partial(jnp.dot, preferred_element_type=f32)

    def d16(a, b):
        return d32(a, b).astype(bf16)

    def fill(after):
        if fillers:
            fillers.pop(0)(_zero_after(after[-1]))

    def fold(x, rows):
        out = x[0:rows]
        for k in range(1, GROUP // rows):
            out = out + x[k * rows:(k + 1) * rows]
        return out

    def unfold(xf, rows):
        lane_blk = lax.broadcasted_iota(jnp.int32, (rows, GROUP), 1) >> (rows.bit_length() - 1)
        return jnp.concatenate([jnp.where(lane_blk == k, xf, jnp.zeros_like(xf))
                                for k in range(GROUP // rows)], axis=0)

    n16 = [n.astype(bf16) for n in nmats]
    invs = [(eye - n * m_ref[MASK_OFF]).astype(bf16) for n in nmats]
    substitute_top = 2 ** n_levels == GROUP
    for lvl in range(1, n_levels - 1 if substitute_top else n_levels):
        rows = min(GROUP, max(FOLD_MIN_ROWS, 2 ** (lvl + 1)))
        mask16 = m_ref[MASK_OFF + lvl].astype(bf16)
        lhs = invs if rows == GROUP else [fold(inv, rows) for inv in invs]
        t1 = [d16(l, n * mask16) for l, n in zip(lhs, n16)]
        fill(invs)
        t2 = [d16(t, inv) for t, inv in zip(t1, invs)]
        fill(t1)
        if rows != GROUP:
            t2 = [unfold(t, rows) for t in t2]
        invs = [inv - t for inv, t in zip(invs, t2)]
    if not substitute_top:
        return [d32(inv, r.astype(bf16)) for inv, r in zip(invs, rhss)]
    h = GROUP // 2
    x1 = [d32(inv[0:h, 0:h], r[0:h].astype(bf16)) for inv, r in zip(invs, rhss)]
    fill(invs)
    bx = [d32(n[h:, 0:h], x.astype(bf16)) for n, x in zip(n16, x1)]
    fill(x1)
    x2 = [d32(inv[h:, h:], (r[h:] - t).astype(bf16)) for inv, r, t in zip(invs, rhss, bx)]
    return [jnp.concatenate([a, b], axis=0) for a, b in zip(x1, x2)]


def _segment_cumsum(tri16, g):
    g1 = g.astype(bf16)
    r1 = g - g1.astype(f32)
    g2 = r1.astype(bf16)
    g3 = (r1 - g2.astype(f32)).astype(bf16)
    s = jnp.dot(tri16, jnp.concatenate([g1, g2, g3], axis=1), preferred_element_type=f32)
    n = g.shape[1]
    return s[:, :n] + (s[:, n:2 * n] + s[:, 2 * n:])


def _delta_body(x_ref, w1_ref, w2_ref, nw_ref, cw_ref, alog_ref, dtb_ref, dnw_ref, m_ref, conv0_ref, s0_ref,
                odn_ref, us5_ref, zs5_ref, convout_ref, s_ref,
                cbuf, qn_scr, kn_scr, v_scr, gc_scr, beta_scr, zdn_scr, *, nb, tb, n_t, pipelined):
    rows = nb * tb
    n_groups = rows // GROUP
    segs = GROUP // tb
    n_levels = tb.bit_length() - 1
    step = pl.program_id(0)
    fresh = step < n_t

    @pl.when(step == 0)
    def _init():
        cbuf[:, 0:SUBLANES, :] = conv0_ref[...]
        s_ref[...] = s0_ref[...]
        if pipelined:
            for scr in (qn_scr, kn_scr, v_scr, gc_scr, beta_scr, zdn_scr):
                scr[...] = jnp.zeros_like(scr)

    front = {}

    def project(w_ref, lo, width):
        return jnp.dot(front["h"], w_ref[:, lo:lo + width], preferred_element_type=f32)

    def a_norm_in():
        x = x_ref[...].reshape(rows, D_MODEL)
        ms = jnp.mean(x * x, axis=-1, keepdims=True)
        front["h"] = (x * lax.rsqrt(ms + EPS) * nw_ref[...]).astype(bf16)

    def a_project_qkv(s):
        front[("raw", s)] = project(w1_ref, s * DN_WIDTH, DN_WIDTH)

    def a_project_us5():
        u = project(w2_ref, W2_US5, S5_WIDTH)
        for j in range(S5_PLANES):
            us5_ref[:, j] = u[:, j * LANES:(j + 1) * LANES].reshape(nb, tb, LANES)

    def a_project_zs5():
        zs5_ref[...] = project(w2_ref, W2_ZS5, S5_WIDTH).reshape(nb, tb, S5_WIDTH)

    def a_project_zdn():
        front["zdn"] = project(w2_ref, W2_ZDN, DN_WIDTH)

    def a_project_gates():
        ab = project(w2_ref, 0, LANES)
        front["g"] = -jnp.exp(alog_ref[...]) * _softplus(ab + dtb_ref[...])
        front["beta"] = _sigmoid(ab)

    projections = ([functools.partial(a_project_qkv, s) for s in range(3)]
                   + [a_project_us5, a_project_zs5, a_project_zdn, a_project_gates])

    def a_conv(b, s, zero):
        cols = slice(s * DN_WIDTH, (s + 1) * DN_WIDTH)
        cbuf[b, SUBLANES:SUBLANES + tb, cols] = front[("raw", s)][b * tb:(b + 1) * tb, :]
        taps = [cw_ref[j:j + 1, cols] for j in range(CONV_K)]
        if zero is not None:
            zero_w = jnp.concatenate([zero] * (DN_WIDTH // LANES), axis=1)
            taps = [t + zero_w for t in taps]
        acc = cbuf[b, 5:5 + tb, cols] * taps[0]
        for j in range(1, CONV_K):
            acc = acc + cbuf[b, 5 + j:5 + j + tb, cols] * taps[j]
        front[("act", b, s)] = _silu(acc)
        tail = cbuf[b, tb:tb + SUBLANES, cols]
        if pipelined:
            tail = jnp.where(fresh, tail, cbuf[b, 0:SUBLANES, cols])
        cbuf[b, 0:SUBLANES, cols] = tail
        convout_ref[b, :, cols] = tail

    def a_norm(b, hd, zero):
        lo = hd * DN_D
        q = front[("act", b, 0)][:, lo:lo + DN_D]
        k = front[("act", b, 1)][:, lo:lo + DN_D]
        if zero is not None:
            q, k = q + zero, k + zero
        front[("qn", b, hd)] = (q * lax.rsqrt(jnp.sum(q * q, axis=-1, keepdims=True) + L2_EPS)
                                * (DN_D ** -0.5))
        front[("kn", b, hd)] = k * lax.rsqrt(jnp.sum(k * k, axis=-1, keepdims=True) + L2_EPS)

    def a_gates(zero):
        tri16 = m_ref[MASK_INCL].astype(bf16)
        g = front["g"] if zero is None else front["g"] + zero
        front["gc"] = [_segment_cumsum(tri16, g[gi * GROUP:(gi + 1) * GROUP])
                       for gi in range(n_groups)]

    def a_store():
        for b in range(nb):
            r = slice(b * tb, (b + 1) * tb)
            for hd in range(DN_HEADS):
                c = slice(hd * DN_D, (hd + 1) * DN_D)
                qn_scr[r, c] = front[("qn", b, hd)]
                kn_scr[r, c] = front[("kn", b, hd)]
            v_scr[r, :] = front[("act", b, 2)]
        for gi in range(n_groups):
            gc_scr[gi * GROUP:(gi + 1) * GROUP, :] = front["gc"][gi]
        beta_scr[...] = front["beta"]
        zdn_scr[...] = front["zdn"]

    fillers = ([functools.partial(a_conv, b, s) for b in range(nb) for s in range(3)] + [a_gates]
               + [functools.partial(a_norm, b, hd) for b in range(nb) for hd in range(DN_HEADS)])

    incl = m_ref[MASK_INCL]
    strict = m_ref[MASK_STRICT]
    eye = incl - strict
    dnw = dnw_ref[...]
    chains = [(gi, hd) for gi in range(n_groups) for hd in range(DN_HEADS)]
    back = {}

    def stream_of(gi, sg):
        return (gi * GROUP) // tb + sg

    for name in ("q16", "qe16", "k16", "kb16", "rhs", "dmat", "kdec", "sdecay", "nmat", "attn"):
        back[name] = [None] * len(chains)

    def b_load(ci):
        gi, hd = chains[ci]
        r = slice(gi * GROUP, (gi + 1) * GROUP)
        c = slice(hd * DN_D, (hd + 1) * DN_D)
        q, k, v = qn_scr[r, c], kn_scr[r, c], v_scr[r, c]
        beta = beta_scr[r, AB_LANE + DN_HEADS + hd:AB_LANE + DN_HEADS + hd + 1]
        g = gc_scr[r, AB_LANE + hd:AB_LANE + hd + 1]
        eg = jnp.exp(g)
        kb = k * beta
        gcb = jnp.broadcast_to(g * math.log2(math.e), (GROUP, GROUP))
        back["dmat"][ci] = jnp.exp2(jnp.minimum(gcb - gcb.T, 0.0)) * incl
        back["q16"][ci] = q.astype(bf16)
        back["qe16"][ci] = (q * eg).astype(bf16)
        back["k16"][ci] = k.astype(bf16)
        back["kb16"][ci] = kb.astype(bf16)
        back["rhs"][ci] = jnp.concatenate([v * beta, kb * eg], axis=1)
        kd, sd = [], []
        for sg in range(segs):
            a0 = sg * tb
            glast = g[a0 + tb - 1:a0 + tb, :]
            kd.append((k[a0:a0 + tb] * jnp.exp(glast - g[a0:a0 + tb])).astype(bf16))
            sd.append(jnp.exp(glast))
        back["kdec"][ci] = kd
        back["sdecay"][ci] = sd

    def b_mats(ci):
        d = functools.partial(lax.dot_general, dimension_numbers=_NT, preferred_element_type=f32)
        dm = back["dmat"][ci]
        back["nmat"][ci] = d(back["kb16"][ci], back["k16"][ci]) * (dm * strict)
        back["attn"][ci] = (d(back["q16"][ci], back["k16"][ci]) * dm).astype(bf16)

    def b_state(sols):
        s_old = {(b, hd): s_ref[b, hd] for b in range(nb) for hd in range(DN_HEADS)}
        s16 = {key: val.astype(bf16) for key, val in s_old.items()}
        v_news, o_states = [], []
        for (gi, hd), sol, qe16 in zip(chains, sols, back["qe16"]):
            u = sol[:, :DN_D]
            w16 = sol[:, DN_D:].astype(bf16)
            vn, os_ = [], []
            for sg in range(segs):
                a0 = sg * tb
                st = s16[(stream_of(gi, sg), hd)]
                vn.append(u[a0:a0 + tb] - jnp.dot(w16[a0:a0 + tb], st, preferred_element_type=f32))
                os_.append(jnp.dot(qe16[a0:a0 + tb], st, preferred_element_type=f32))
            v_news.append(vn[0] if segs == 1 else jnp.concatenate(vn, axis=0))
            o_states.append(os_[0] if segs == 1 else jnp.concatenate(os_, axis=0))
        vn16s = [vn.astype(bf16) for vn in v_news]
        outs = [os_ + jnp.dot(a, vn, preferred_element_type=f32)
                for os_, a, vn in zip(o_states, back["attn"], vn16s)]

        s_new = {}
        for (gi, hd), kd, sd, vn in zip(chains, back["kdec"], back["sdecay"], vn16s):
            for sg in range(segs):
                a0 = sg * tb
                key = (stream_of(gi, sg), hd)
                s_new[key] = (s_old[key] * sd[sg]
                              + lax.dot_general(kd[sg], vn[a0:a0 + tb], _TN, preferred_element_type=f32))

        o_rows = []
        for gi in range(n_groups):
            o_heads = []
            for hd in range(DN_HEADS):
                o = outs[gi * DN_HEADS + hd]
                on = o * lax.rsqrt(jnp.mean(o * o, axis=-1, keepdims=True) + EPS) * dnw
                zd = zdn_scr[gi * GROUP:(gi + 1) * GROUP, hd * DN_D:(hd + 1) * DN_D]
                o_heads.append(on * _silu(zd))
            o_rows.append(jnp.concatenate(o_heads, axis=1))
        o_all = o_rows[0] if n_groups == 1 else jnp.concatenate(o_rows, axis=0)
        odn_ref[...] = o_all.astype(bf16).reshape(nb, tb, DN_WIDTH)
        s_ref[...] = jnp.stack([jnp.stack([s_new[(b, hd)] for hd in range(DN_HEADS)]) for b in range(nb)])

    n_ch = len(chains)
    if pipelined:
        a_norm_in()
        for ci in range(n_ch):
            b_load(ci)
        back_setup = [functools.partial(b_mats, ci) for ci in range(n_ch)]
        convs = [fillers.pop(0) for _ in range(nb * 3)]
        per_piece = -(-len(back_setup) // len(projections))
        for idx, piece in enumerate(projections):
            piece()
            for thunk in back_setup[:per_piece]:
                thunk()
            del back_setup[:per_piece]
            if idx >= 3 and convs:
                convs.pop(0)(None)
        while back_setup or convs:
            if back_setup:
                back_setup.pop(0)()
            if convs:
                convs.pop(0)(None)
        sols = _unit_lower_solve(back["nmat"], back["rhs"], eye, m_ref, n_levels, fillers)
        for fill in fillers:
            fill(None)
        b_state(sols)
        a_store()
    else:
        a_norm_in()
        for piece in projections:
            piece()
        for fill in fillers:
            fill(None)
        a_store()
        for ci in range(n_ch):
            b_load(ci)
        for ci in range(n_ch):
            b_mats(ci)
        b_state(_unit_lower_solve(back["nmat"], back["rhs"], eye, m_ref, n_levels, []))


def _delta_masks(tb):
    r = np.arange(GROUP)[:, None]
    c = np.arange(GROUP)[None, :]
    same = (r // tb) == (c // tb)
    ms = [same & (r >= c), same & (r > c)]
    for l in range(tb.bit_length() - 1):
        ms.append(((r >> (l + 1)) == (c >> (l + 1))) & (((r >> l) & 1) == 1) & (((c >> l) & 1) == 0))
    return jnp.asarray(np.stack(ms).astype(np.float32))


def _delta_call(x, w1, w2, norm_w, conv_w, alog, dtb, dnw, conv0, s0, *, tb, pipelined):
    nb, T, _ = x.shape
    assert GROUP % tb == 0 and (nb * tb) % GROUP == 0 and T % tb == 0
    n_t = T // tb
    rows = nb * tb
    if pipelined:
        grid = (n_t + 1,)
        cur = lambda i: jnp.minimum(i, n_t - 1)
        prev = lambda i: jnp.maximum(i - 1, 0)
    else:
        grid = (n_t,)
        cur = prev = lambda i: i
    const2 = lambda i: (0, 0)
    const3 = lambda i: (0, 0, 0)
    const4 = lambda i: (0, 0, 0, 0)
    body = functools.partial(_delta_body, nb=nb, tb=tb, n_t=n_t, pipelined=pipelined)
    masks = _delta_masks(tb)
    one = pl.Buffered(1)
    return pl.pallas_call(
        body,
        grid=grid,
        in_specs=[
            pl.BlockSpec((nb, tb, D_MODEL), lambda i: (0, cur(i), 0)),
            pl.BlockSpec((D_MODEL, CONV_CH), const2, pipeline_mode=one),
            pl.BlockSpec((D_MODEL, W2_COLS), const2, pipeline_mode=one),
            pl.BlockSpec((1, D_MODEL), const2),
            pl.BlockSpec((CONV_K, CONV_CH), const2),
            pl.BlockSpec((1, LANES), const2),
            pl.BlockSpec((1, LANES), const2),
            pl.BlockSpec((1, DN_D), const2),
            pl.BlockSpec(masks.shape, const3, pipeline_mode=one),
            pl.BlockSpec((nb, SUBLANES, CONV_CH), const3),
            pl.BlockSpec((nb, DN_HEADS, DN_D, DN_D), const4),
        ],
        out_specs=[
            pl.BlockSpec((nb, tb, DN_WIDTH), lambda i: (0, prev(i), 0)),
            pl.BlockSpec((nb, S5_PLANES, tb, LANES), lambda i: (0, 0, cur(i), 0)),
            pl.BlockSpec((nb, tb, S5_WIDTH), lambda i: (0, cur(i), 0)),
            pl.BlockSpec((nb, SUBLANES, CONV_CH), const3),
            pl.BlockSpec((nb, DN_HEADS, DN_D, DN_D), const4),
        ],
        out_shape=[
            jax.ShapeDtypeStruct((nb, T, DN_WIDTH), bf16),
            jax.ShapeDtypeStruct((nb, S5_PLANES, T, LANES), f32),
            jax.ShapeDtypeStruct((nb, T, S5_WIDTH), f32),
            jax.ShapeDtypeStruct((nb, SUBLANES, CONV_CH), f32),
            jax.ShapeDtypeStruct((nb, DN_HEADS, DN_D, DN_D), f32),
        ],
        scratch_shapes=[
            pltpu.VMEM((nb, tb + SUBLANES, CONV_CH), f32),
            pltpu.VMEM((rows, DN_WIDTH), f32),
            pltpu.VMEM((rows, DN_WIDTH), f32),
            pltpu.VMEM((rows, DN_WIDTH), f32),
            pltpu.VMEM((rows, LANES), f32),
            pltpu.VMEM((rows, LANES), f32),
            pltpu.VMEM((rows, DN_WIDTH), f32),
        ],
        compiler_params=pltpu.CompilerParams(
            dimension_semantics=("arbitrary",), vmem_limit_bytes=VMEM_LIMIT),
        name="delta",
    )(x, w1, w2, norm_w, conv_w, alog, dtb, dnw, masks, conv0, s0)


S5MAT_GROUPS_PER_STEP = 8


def _split16(a):
    hi = a.astype(bf16)
    return hi, (a - hi.astype(f32)).astype(bf16)


def _s5mat_body(cp_ref, bp_ref, t_ref):
    ri = lax.broadcasted_iota(jnp.int32, (S5_GL, S5_GL), 0) >> 4
    ci = lax.broadcasted_iota(jnp.int32, (S5_GL, S5_GL), 1) >> 4
    causal = ri >= ci
    d = functools.partial(lax.dot_general, dimension_numbers=_NT, preferred_element_type=f32)
    for g in range(S5MAT_GROUPS_PER_STEP):
        ch, cl = _split16(cp_ref[g])
        bh, bl = _split16(bp_ref[g])
        t = d(ch, bh) + (d(ch, bl) + d(cl, bh))
        t_ref[g] = jnp.where(causal, t, 0.0).astype(bf16)


def _s5mat_call(cp, bp):
    n = S5MAT_GROUPS_PER_STEP
    return pl.pallas_call(
        _s5mat_body,
        grid=(S5_GROUPS // n,),
        in_specs=[
            pl.BlockSpec((n, S5_GL, 2 * S5_STATE), lambda g: (g, 0, 0)),
            pl.BlockSpec((n, S5_GL, 2 * S5_STATE), lambda g: (g, 0, 0)),
        ],
        out_specs=pl.BlockSpec((n, S5_GL, S5_GL), lambda g: (g, 0, 0)),
        out_shape=jax.ShapeDtypeStruct((S5_GROUPS, S5_GL, S5_GL), bf16),
        compiler_params=pltpu.CompilerParams(dimension_semantics=("arbitrary",)),
        name="s5mat",
    )(cp, bp)


def _s5_body(u_ref, z_ref, xr0_ref, xi0_ref, t_ref, wp_ref, vt_ref, ar_ref, ai_ref, d_ref,
             gw_ref, gb_ref,
             o_ref, xr_ref, xi_ref,
             at_scr, er_scr, ei_scr, xinr_scr, xini_scr, yt_scr, y_scr, *, tb, n_streams, n_chunks):
    R = tb // S5_L
    used = n_streams * n_chunks

    @pl.when(pl.program_id(1) == 0)
    def _init():
        xr_ref[...] = xr0_ref[...]
        xi_ref[...] = xi0_ref[...]

    for f in range(S5_L):
        for j in range(S5_PLANES):
            ut = u_ref[0, j, pl.ds(f, R, stride=S5_L), :]
            ut_t = ut.astype(bf16).T
            for gl in range(LANES // S5_GROUP):
                g = j * (LANES // S5_GROUP) + gl
                at_scr[g, f * S5_GROUP:(f + 1) * S5_GROUP, :] = ut_t[gl * S5_GROUP:(gl + 1) * S5_GROUP, :]

    for p in range(S5_PAIRS):
        a_t = at_scr[2 * p:2 * p + 2].reshape(2 * S5_GL, R)
        e = lax.dot_general(a_t, wp_ref[p], _TN, preferred_element_type=f32)
        er_scr[:, p * LANES:(p + 1) * LANES] = e[:, :LANES]
        ei_scr[:, p * LANES:(p + 1) * LANES] = e[:, LANES:]

    if used < R:
        xinr_scr[...] = jnp.zeros_like(xinr_scr)
        xini_scr[...] = jnp.zeros_like(xini_scr)

    a_r = ar_ref[...]
    a_i = ai_ref[...]
    for s in range(n_streams):
        def step(c, carry):
            xr, xi = carry
            row = s * n_chunks + c
            xinr_scr[pl.ds(row, 1), :] = xr
            xini_scr[pl.ds(row, 1), :] = xi
            er = er_scr[pl.ds(row, 1), :]
            ei = ei_scr[pl.ds(row, 1), :]
            return a_r * xr - a_i * xi + er, a_r * xi + a_i * xr + ei

        xr, xi = lax.fori_loop(0, n_chunks, step, (xr_ref[0, s:s + 1, :], xi_ref[0, s:s + 1, :]),
                               unroll=min(n_chunks, 4))
        xr_ref[0, s:s + 1, :] = xr
        xi_ref[0, s:s + 1, :] = xi

    for g in range(S5_GROUPS):
        p = g // 2
        xin = jnp.concatenate([xinr_scr[:, p * LANES:(p + 1) * LANES],
                               xini_scr[:, p * LANES:(p + 1) * LANES]], axis=1).astype(bf16)
        yt_scr[g] = (jnp.dot(t_ref[g], at_scr[g], preferred_element_type=f32)
                     + lax.dot_general(vt_ref[g], xin, _NT, preferred_element_type=f32))

    for f in range(S5_L):
        gpp = LANES // S5_GROUP
        for j in range(S5_PLANES):
            yt = yt_scr[j * gpp:(j + 1) * gpp, f * S5_GROUP:(f + 1) * S5_GROUP, :].reshape(LANES, R)
            y_scr[j, pl.ds(f, R, stride=S5_L), :] = yt.T

    for rb in range(tb // OUT_SUB):
        sl = pl.ds(rb * OUT_SUB, OUT_SUB)
        y_intra = jnp.concatenate([y_scr[j, sl, :] for j in range(S5_PLANES)], axis=1)
        u_nat = jnp.concatenate([u_ref[0, j, sl, :] for j in range(S5_PLANES)], axis=1)
        y = y_intra + d_ref[...] * u_nat
        gy = _gelu_tanh(y)
        gate = _sigmoid(_bdot(gy, gw_ref[...]) + gb_ref[...])
        o_ref[0, sl, :] = (gy * gate * _silu(z_ref[0, sl, :])).astype(bf16)


def _s5_call(u, z, xr0, xi0, tmat, wp, vt, ar, ai, dvec, gw, gb, *, tb, n_streams, n_chunks):
    G, T, _ = z.shape
    grid = (G, T // tb)
    R = tb // S5_L
    nstate = S5_GROUPS * S5_STATE
    c2 = lambda g, t: (0, 0)
    c3 = lambda g, t: (0, 0, 0)
    body = functools.partial(_s5_body, tb=tb, n_streams=n_streams, n_chunks=n_chunks)
    one = pl.Buffered(1)
    return pl.pallas_call(
        body,
        grid=grid,
        in_specs=[
            pl.BlockSpec((1, S5_PLANES, tb, LANES), lambda g, t: (g, 0, t, 0)),
            pl.BlockSpec((1, tb, S5_WIDTH), lambda g, t: (g, t, 0)),
            pl.BlockSpec((1, n_streams, nstate), lambda g, t: (g, 0, 0)),
            pl.BlockSpec((1, n_streams, nstate), lambda g, t: (g, 0, 0)),
            pl.BlockSpec((S5_GROUPS, S5_GL, S5_GL), c3, pipeline_mode=one),
            pl.BlockSpec((S5_PAIRS, 2 * S5_GL, 2 * LANES), c3, pipeline_mode=one),
            pl.BlockSpec((S5_GROUPS, S5_GL, 2 * LANES), c3, pipeline_mode=one),
            pl.BlockSpec((1, nstate), c2),
            pl.BlockSpec((1, nstate), c2),
            pl.BlockSpec((1, S5_WIDTH), c2),
            pl.BlockSpec((S5_WIDTH, S5_WIDTH), c2),
            pl.BlockSpec((1, S5_WIDTH), c2),
        ],
        out_specs=[
            pl.BlockSpec((1, tb, S5_WIDTH), lambda g, t: (g, t, 0)),
            pl.BlockSpec((1, n_streams, nstate), lambda g, t: (g, 0, 0)),
            pl.BlockSpec((1, n_streams, nstate), lambda g, t: (g, 0, 0)),
        ],
        out_shape=[
            jax.ShapeDtypeStruct((G, T, S5_WIDTH), bf16),
            jax.ShapeDtypeStruct((G, n_streams, nstate), f32),
            jax.ShapeDtypeStruct((G, n_streams, nstate), f32),
        ],
        scratch_shapes=[
            pltpu.VMEM((S5_GROUPS, S5_GL, R), bf16),
            pltpu.VMEM((R, nstate), f32),
            pltpu.VMEM((R, nstate), f32),
            pltpu.VMEM((R, nstate), f32),
            pltpu.VMEM((R, nstate), f32),
            pltpu.VMEM((S5_GROUPS, S5_GL, R), f32),
            pltpu.VMEM((S5_PLANES, tb, LANES), f32),
        ],
        compiler_params=pltpu.CompilerParams(
            dimension_semantics=("arbitrary", "arbitrary"), vmem_limit_bytes=VMEM_LIMIT),
        name="s5",
    )(u, z, xr0, xi0, tmat, wp, vt, ar, ai, dvec, gw, gb)


def _out_body(x_ref, odn_ref, os5_ref, wo_ref, fw_ref, y_ref):
    acc = (x_ref[...]
           + jnp.dot(odn_ref[...], wo_ref[0:DN_WIDTH, :], preferred_element_type=f32)
           + jnp.dot(os5_ref[...], wo_ref[DN_WIDTH:, :], preferred_element_type=f32))
    ms = jnp.mean(acc * acc, axis=-1, keepdims=True)
    y_ref[...] = acc * lax.rsqrt(ms + EPS) * fw_ref[...]


def _out_call(x2, odn2, os52, w_out16, fw, *, rows):
    n = x2.shape[0]
    return pl.pallas_call(
        _out_body,
        grid=(n // rows,),
        in_specs=[
            pl.BlockSpec((rows, D_MODEL), lambda i: (i, 0)),
            pl.BlockSpec((rows, DN_WIDTH), lambda i: (i, 0)),
            pl.BlockSpec((rows, S5_WIDTH), lambda i: (i, 0)),
            pl.BlockSpec((D_MODEL, D_MODEL), lambda i: (0, 0)),
            pl.BlockSpec((1, D_MODEL), lambda i: (0, 0)),
        ],
        out_specs=pl.BlockSpec((rows, D_MODEL), lambda i: (i, 0)),
        out_shape=jax.ShapeDtypeStruct((n, D_MODEL), f32),
        compiler_params=pltpu.CompilerParams(
            dimension_semantics=("arbitrary",), vmem_limit_bytes=VMEM_LIMIT),
        name="outproj",
    )(x2, odn2, os52, w_out16, fw)


def _s5_operands(a_re, a_im, log_dt, b_re, b_im, c_re, c_im):
    lam_re = jnp.minimum(a_re, -1e-4)
    lam_im = a_im
    dt = jnp.exp(log_dt)[:, None]
    ldt_re, ldt_im = lam_re * dt, lam_im * dt

    def lpow(k):
        mag = jnp.exp(ldt_re * k)
        return mag * jnp.cos(ldt_im * k), mag * jnp.sin(ldt_im * k)

    lb_re, lb_im = lpow(1.0)
    den = lam_re * lam_re + lam_im * lam_im
    f_re = ((lb_re - 1.0) * lam_re + lb_im * lam_im) / den
    f_im = (lb_im * lam_re - (lb_re - 1.0) * lam_im) / den
    bb_re = f_re[..., None] * b_re - f_im[..., None] * b_im
    bb_im = f_re[..., None] * b_im + f_im[..., None] * b_re

    fr = jnp.arange(S5_L, dtype=f32)
    G, N = S5_GROUPS, S5_STATE

    def cmul(ar, ai, br, bi):
        return ar * br - ai * bi, ar * bi + ai * br

    pr, pi = lpow(fr[:, None, None])
    pr, pi = pr.transpose(1, 0, 2), pi.transpose(1, 0, 2)
    cpr, cpi = cmul(c_re[:, None], c_im[:, None], pr[:, :, None], pi[:, :, None])
    cpr, cpi = cpr.reshape(G, S5_GL, N), cpi.reshape(G, S5_GL, N)
    bbr_t, bbi_t = bb_re.transpose(0, 2, 1), bb_im.transpose(0, 2, 1)
    qr, qi = lpow(-fr[:, None, None])
    qr, qi = qr.transpose(1, 0, 2), qi.transpose(1, 0, 2)
    bpr, bpi = cmul(qr[:, :, None], qi[:, :, None], bbr_t[:, None], bbi_t[:, None])
    bpr, bpi = bpr.reshape(G, S5_GL, N), bpi.reshape(G, S5_GL, N)

    wr_, wi_ = lpow((S5_L - 1.0) - fr[:, None, None])
    wr_, wi_ = wr_.transpose(1, 0, 2), wi_.transpose(1, 0, 2)
    w_re, w_im = cmul(wr_[:, :, None], wi_[:, :, None], bbr_t[:, None], bbi_t[:, None])
    w_re, w_im = w_re.reshape(G, S5_GL, N), w_im.reshape(G, S5_GL, N)
    vr_, vi_ = lpow(fr[:, None, None] + 1.0)
    vr_, vi_ = vr_.transpose(1, 0, 2), vi_.transpose(1, 0, 2)
    v_re, v_im = cmul(c_re[:, None], c_im[:, None], vr_[:, :, None], vi_[:, :, None])
    v_re, v_im = v_re.reshape(G, S5_GL, N), v_im.reshape(G, S5_GL, N)

    zeros = jnp.zeros((S5_PAIRS, S5_GL, N), f32)

    def pair_cols(re, im):
        re, im = re.reshape(S5_PAIRS, 2, S5_GL, N), im.reshape(S5_PAIRS, 2, S5_GL, N)
        even = jnp.concatenate([re[:, 0], zeros, im[:, 0], zeros], axis=-1)
        odd = jnp.concatenate([zeros, re[:, 1], zeros, im[:, 1]], axis=-1)
        return jnp.stack([even, odd], axis=1)

    wp = pair_cols(w_re, w_im).reshape(S5_PAIRS, 2 * S5_GL, 2 * LANES).astype(bf16)
    vt = pair_cols(v_re, -v_im).reshape(S5_GROUPS, S5_GL, 2 * LANES).astype(bf16)
    a16r, a16i = lpow(float(S5_L))
    cp = jnp.concatenate([cpr, cpi], axis=2)
    bp = jnp.concatenate([bpr, -bpi], axis=2)
    return cp, bp, wp, vt, a16r.reshape(1, G * N), a16i.reshape(1, G * N)


def _pad_lanes(v, start=0):
    out = jnp.zeros((1, LANES), f32)
    return out.at[0, start:start + v.shape[0]].set(v.astype(f32))


def _layer(x, conv0, s0, xr0, xi0, prm, *, delta_cfg, s5_cfg):
    B, T, _ = x.shape
    conv0p = jnp.concatenate([jnp.zeros((B, SUBLANES - (CONV_K - 1), CONV_CH), f32), conv0], axis=1)
    odn, us5, zs5, convout, s_new = _delta_call(
        x, prm["w1"], prm["w2"], prm["norm_w"], prm["conv_w"], prm["alog"], prm["dtb"], prm["dnw"],
        conv0p, s0, **delta_cfg)

    nstate = S5_GROUPS * S5_STATE
    if s5_cfg["flatten"]:
        tb = s5_cfg["tb"]
        uf = us5.transpose(1, 0, 2, 3).reshape(1, S5_PLANES, B * T, LANES)
        zf = zs5.reshape(1, B * T, S5_WIDTH)
        pad = tb - B * T
        uf = jnp.pad(uf, ((0, 0), (0, 0), (0, pad), (0, 0)))
        zf = jnp.pad(zf, ((0, 0), (0, pad), (0, 0)))
        os5, xr, xi = _s5_call(uf, zf, xr0.reshape(1, B, nstate), xi0.reshape(1, B, nstate),
                               prm["tmat"], prm["wp"], prm["vt"], prm["ar"], prm["ai"], prm["dvec"],
                               prm["gw"], prm["gb"], tb=tb, n_streams=B, n_chunks=T // S5_L)
        os5 = os5[0, :B * T].reshape(B, T, S5_WIDTH)
    else:
        tb = s5_cfg["tb"]
        os5, xr, xi = _s5_call(us5, zs5, xr0.reshape(B, 1, nstate), xi0.reshape(B, 1, nstate),
                               prm["tmat"], prm["wp"], prm["vt"], prm["ar"], prm["ai"], prm["dvec"],
                               prm["gw"], prm["gb"], tb=tb, n_streams=1, n_chunks=tb // S5_L)

    y = _out_call(x.reshape(B * T, D_MODEL), odn.reshape(B * T, DN_WIDTH), os5.reshape(B * T, S5_WIDTH),
                  prm["w_out"], prm["fw"], rows=min(OUT_ROWS, B * T))
    return (y.reshape(B, T, D_MODEL), convout[:, SUBLANES - (CONV_K - 1):, :], s_new,
            xr.reshape(B, S5_GROUPS, S5_STATE), xi.reshape(B, S5_GROUPS, S5_STATE))


def kernel(x_prompt, x_sample, cache_conv, state_dn, state_s5_re, state_s5_im, norm_w, w_in, conv_w, dn_A_log, dn_dt_bias, dn_norm_w, s5_A_re, s5_A_im, s5_log_dt, s5_B_re, s5_B_im, s5_C_re, s5_C_im, s5_D, glu_w, glu_b, w_out, final_norm_w):
    depth = norm_w.shape[0]
    assert depth == 1
    l = 0
    assert w_in.shape[-1] == IN_COLS
    w1 = w_in[l, :, :CONV_CH].astype(bf16)
    w2 = w_in[l, :, IN_COLS - W2_COLS:].astype(bf16)
    cp, bp, wp, vt, ar, ai = _s5_operands(
        s5_A_re[l].astype(f32), s5_A_im[l].astype(f32), s5_log_dt[l].astype(f32),
        s5_B_re[l].astype(f32), s5_B_im[l].astype(f32), s5_C_re[l].astype(f32), s5_C_im[l].astype(f32))
    prm = dict(
        w1=w1, w2=w2,
        norm_w=norm_w[l].reshape(1, D_MODEL).astype(f32),
        conv_w=conv_w[l].astype(f32),
        alog=_pad_lanes(dn_A_log[l], AB_LANE),
        dtb=_pad_lanes(dn_dt_bias[l], AB_LANE),
        dnw=dn_norm_w[l].reshape(1, DN_D).astype(f32),
        tmat=_s5mat_call(cp, bp),
        wp=wp, vt=vt, ar=ar, ai=ai,
        dvec=s5_D[l].reshape(1, S5_WIDTH).astype(f32),
        gw=glu_w[l].astype(bf16),
        gb=glu_b[l].reshape(1, S5_WIDTH).astype(f32),
        w_out=w_out[l].astype(bf16),
        fw=final_norm_w.reshape(1, D_MODEL).astype(f32),
    )

    bp = x_prompt.shape[0]
    yp, c1, d1, r1, i1 = _layer(
        x_prompt,
        jnp.zeros((bp, CONV_K - 1, CONV_CH), f32),
        jnp.zeros((bp, DN_HEADS, DN_D, DN_D), f32),
        jnp.zeros((bp, S5_GROUPS, S5_STATE), f32),
        jnp.zeros((bp, S5_GROUPS, S5_STATE), f32),
        prm,
        delta_cfg=dict(tb=256, pipelined=True),
        s5_cfg=dict(flatten=False, tb=2048))
    ys, c2, d2, r2, i2 = _layer(
        x_sample, cache_conv[l].astype(f32), state_dn[l].astype(f32),
        state_s5_re[l].astype(f32), state_s5_im[l].astype(f32),
        prm,
        delta_cfg=dict(tb=32, pipelined=False),
        s5_cfg=dict(flatten=True, tb=2048))

    return (yp, ys, c1[None], d1[None], r1[None], i1[None], c2[None], d2[None], r2[None], i2[None])
```

```python
import functools
import math

import jax
import jax.numpy as jnp
import numpy as np
from jax import lax
from jax.experimental import pallas as pl
from jax.experimental.pallas import tpu as pltpu

bf16 = jnp.bfloat16
f32 = jnp.float32

LANES = 128
SUBLANES = 8

D_MODEL = 1024
DN_HEADS = 4
DN_D = 128
DN_WIDTH = DN_HEADS * DN_D
CONV_K = 4
CONV_CH = 3 * DN_WIDTH
S5_WIDTH = 512
S5_GROUP = 16
S5_GROUPS = 32
S5_STATE = 64
S5_PAIRS = S5_GROUPS // 2
S5_L = 16
S5_GL = S5_GROUP * S5_L
OUT_SUB = 256
OUT_ROWS = 2048
S5_PLANES = S5_WIDTH // LANES
EPS = 1e-6
L2_EPS = 1e-6

IN_COLS = CONV_CH + 2 * DN_HEADS + DN_WIDTH + 2 * S5_WIDTH
W2_COLS = LANES + DN_WIDTH + 2 * S5_WIDTH
AB_LANE = LANES - 2 * DN_HEADS
W2_ZDN = LANES
W2_US5 = W2_ZDN + DN_WIDTH
W2_ZS5 = W2_US5 + S5_WIDTH

GROUP = 256
MASK_INCL, MASK_STRICT, MASK_OFF = 0, 1, 2
FOLD_MIN_ROWS = 64
VMEM_LIMIT =56 * 1024 * 1024

_NT = (((1,), (1,)), ((), ()))
_TN = (((0,), (0,)), ((), ()))


def _bdot(a, b):
    return jnp.dot(a.astype(bf16), b.astype(bf16), preferred_element_type=f32)


def _bdot_g(a, b, dims):
    return lax.dot_general(a.astype(bf16), b.astype(bf16), dims, preferred_element_type=f32)


def _sigmoid(x):
    return 1.0 / (1.0 + jnp.exp2(x * (-math.log2(math.e))))


def _silu(x):
    return x * _sigmoid(x)


def _softplus(x):
    return jnp.maximum(x, 0.0) + jnp.log1p(jnp.exp(-jnp.abs(x)))


def _gelu_tanh(x):
    c = math.sqrt(2.0 / math.pi)
    return x * (0.5 + 0.5 * jnp.tanh(x * (c + (c * 0.044715) * (x * x))))


def _zero_after(x):
    bits = pltpu.bitcast(x[0:SUBLANES, 0:LANES].astype(f32), jnp.uint32)
    half = jnp.uint32(16)
    bits = lax.shift_right_logical(lax.shift_right_logical(bits, half), half)
    return pltpu.bitcast(bits, f32)[0:1, :]


def _unit_lower_solve(nmats, rhss, eye, m_ref, n_levels, fillers):
    d32 = functools.partial(jnp.dot, preferred_element_type=f32)

    def d16(a, b):
        return d32(a, b).astype(bf16)

    def fill(after):
        if fillers:
            fillers.pop(0)(_zero_after(after[-1]))

    def fold(x, rows):
        out = x[0:rows]
        for k in range(1, GROUP // rows):
            out = out + x[k * rows:(k + 1) * rows]
        return out

    def unfold(xf, rows):
        lane_blk = lax.broadcasted_iota(jnp.int32, (rows, GROUP), 1) >> (rows.bit_length() - 1)
        return jnp.concatenate([jnp.where(lane_blk == k, xf, jnp.zeros_like(xf))
                                for k in range(GROUP // rows)], axis=0)

    n16 = [n.astype(bf16) for n in nmats]
    invs = [(eye - n * m_ref[MASK_OFF]).astype(bf16) for n in nmats]
    substitute_top = 2 ** n_levels == GROUP
    for lvl in range(1, n_levels - 1 if substitute_top else n_levels):
        rows = min(GROUP, max(FOLD_MIN_ROWS, 2 ** (lvl + 1)))
        mask16 = m_ref[MASK_OFF + lvl].astype(bf16)
        lhs = invs if rows == GROUP else [fold(inv, rows) for inv in invs]
        t1 = [d16(l, n * mask16) for l, n in zip(lhs, n16)]
        fill(invs)
        t2 = [d16(t, inv) for t, inv in zip(t1, invs)]
        fill(t1)
        if rows != GROUP:
            t2 = [unfold(t, rows) for t in t2]
        invs = [inv - t for inv, t in zip(invs, t2)]
    if not substitute_top:
        return [d32(inv, r.astype(bf16)) for inv, r in zip(invs, rhss)]
    h = GROUP // 2
    x1 = [d32(inv[0:h, 0:h], r[0:h].astype(bf16)) for inv, r in zip(invs, rhss)]
    fill(invs)
    bx = [d32(n[h:, 0:h], x.astype(bf16)) for n, x in zip(n16, x1)]
    fill(x1)
    x2 = [d32(inv[h:, h:], (r[h:] - t).astype(bf16)) for inv, r, t in zip(invs, rhss, bx)]
    return [jnp.concatenate([a, b], axis=0) for a, b in zip(x1, x2)]


def _segment_cumsum(tri16, g):
    g1 = g.astype(bf16)
    r1 = g - g1.astype(f32)
    g2 = r1.astype(bf16)
    g3 = (r1 - g2.astype(f32)).astype(bf16)
    s = jnp.dot(tri16, jnp.concatenate([g1, g2, g3], axis=1), preferred_element_type=f32)
    n = g.shape[1]
    return s[:, :n] + (s[:, n:2 * n] + s[:, 2 * n:])


def _delta_body(x_ref, w1_ref, w2_ref, nw_ref, cw_ref, alog_ref, dtb_ref, dnw_ref, m_ref, conv0_ref, s0_ref,
                odn_ref, us5_ref, zs5_ref, convout_ref, s_ref,
                cbuf, pbuf, qn_scr, kn_scr, v_scr, gc_scr, beta_scr, zdn_scr,
                *, nb, tb, n_t, pipelined):
    rows = nb * tb
    n_groups = rows // GROUP
    segs = GROUP // tb
    n_levels = tb.bit_length() - 1
    step = pl.program_id(0)
    fresh = step < n_t

    @pl.when(step == 0)
    def _init():
        cbuf[:, 0:SUBLANES, :] = conv0_ref[...]
        for b in range(nb):
            c8 = conv0_ref[b]
            pbuf[b, 0:SUBLANES, :] = c8 * cw_ref[1:2, :] + pltpu.roll(c8, 1, axis=0) * cw_ref[0:1, :]
        s_ref[...] = s0_ref[...]
        if pipelined:
            for scr in (qn_scr, kn_scr, v_scr, gc_scr, beta_scr, zdn_scr):
                scr[...] = jnp.zeros_like(scr)

    front = {}

    def project(w_ref, lo, width):
        return jnp.dot(front["h"], w_ref[:, lo:lo + width], preferred_element_type=f32)

    def a_norm_in():
        x = x_ref[...].reshape(rows, D_MODEL)
        ms = jnp.mean(x * x, axis=-1, keepdims=True)
        front["h"] = (x * lax.rsqrt(ms + EPS) * nw_ref[...]).astype(bf16)

    def a_project_qkv(s):
        front[("raw", s)] = project(w1_ref, s * DN_WIDTH, DN_WIDTH)

    def a_project_us5():
        u = project(w2_ref, W2_US5, S5_WIDTH)
        for j in range(S5_PLANES):
            us5_ref[:, j] = u[:, j * LANES:(j + 1) * LANES].reshape(nb, tb, LANES)

    def a_project_zs5():
        zs5_ref[...] = project(w2_ref, W2_ZS5, S5_WIDTH).reshape(nb, tb, S5_WIDTH)

    def a_project_zdn():
        front["zdn"] = project(w2_ref, W2_ZDN, DN_WIDTH)

    def a_project_gates():
        ab = project(w2_ref, 0, LANES)
        front["g"] = -jnp.exp(alog_ref[...]) * _softplus(ab + dtb_ref[...])
        front["beta"] = _sigmoid(ab)

    projections = ([functools.partial(a_project_qkv, s) for s in range(3)]
                   + [a_project_us5, a_project_zs5, a_project_zdn, a_project_gates])

    def a_conv(b, s, zero):
        cols = slice(s * DN_WIDTH, (s + 1) * DN_WIDTH)
        u0 = front[("raw", s)][b * tb:(b + 1) * tb, :]
        cbuf[b, SUBLANES:SUBLANES + tb, cols] = u0
        taps = [cw_ref[j:j + 1, cols] for j in range(CONV_K)]
        if zero is not None:
            zero_w = jnp.concatenate([zero] * (DN_WIDTH // LANES), axis=1)
            taps = [t + zero_w for t in taps]
        u1 = cbuf[b, SUBLANES - 1:SUBLANES - 1 + tb, cols]
        pbuf[b, SUBLANES:SUBLANES + tb, cols] = u0 * taps[1] + u1 * taps[0]
        acc = (u0 * taps[3] + u1 * taps[2]) + pbuf[b, SUBLANES - 2:SUBLANES - 2 + tb, cols]
        front[("act", b, s)] = _silu(acc)
        for buf in (cbuf, pbuf):
            tail = buf[b, tb:tb + SUBLANES, cols]
            if pipelined:
                tail = jnp.where(fresh, tail, buf[b, 0:SUBLANES, cols])
            buf[b, 0:SUBLANES, cols] = tail
        convout_ref[b, :, cols] = cbuf[b, 0:SUBLANES, cols]

    def a_norm(b, hd, zero):
        lo = hd * DN_D
        q = front[("act", b, 0)][:, lo:lo + DN_D]
        k = front[("act", b, 1)][:, lo:lo + DN_D]
        if zero is not None:
            q, k = q + zero, k + zero
        front[("qn", b, hd)] = (q * lax.rsqrt(jnp.sum(q * q, axis=-1, keepdims=True) + L2_EPS)
                                * (DN_D ** -0.5))
        front[("kn", b, hd)] = k * lax.rsqrt(jnp.sum(k * k, axis=-1, keepdims=True) + L2_EPS)

    def a_gates(zero):
        tri16 = m_ref[MASK_INCL].astype(bf16)
        g = front["g"] if zero is None else front["g"] + zero
        front["gc"] = [_segment_cumsum(tri16, g[gi * GROUP:(gi + 1) * GROUP])
                       for gi in range(n_groups)]

    def a_store():
        for b in range(nb):
            r = slice(b * tb, (b + 1) * tb)
            for hd in range(DN_HEADS):
                c = slice(hd * DN_D, (hd + 1) * DN_D)
                qn_scr[r, c] = front[("qn", b, hd)]
                kn_scr[r, c] = front[("kn", b, hd)]
            v_scr[r, :] = front[("act", b, 2)]
        for gi in range(n_groups):
            gc_scr[gi * GROUP:(gi + 1) * GROUP, :] = front["gc"][gi]
        beta_scr[...] = front["beta"]
        zdn_scr[...] = front["zdn"]

    fillers = ([functools.partial(a_conv, b, s) for b in range(nb) for s in range(3)] + [a_gates]
               + [functools.partial(a_norm, b, hd) for b in range(nb) for hd in range(DN_HEADS)])

    incl = m_ref[MASK_INCL]
    strict = m_ref[MASK_STRICT]
    eye = incl - strict
    dnw = dnw_ref[...]
    chains = [(gi, hd) for gi in range(n_groups) for hd in range(DN_HEADS)]
    back = {}

    def stream_of(gi, sg):
        return (gi * GROUP) // tb + sg

    for name in ("q16", "qe16", "k16", "kb16", "rhs", "dmat", "kdec", "sdecay", "nmat", "attn"):
        back[name] = [None] * len(chains)

    def b_load(ci):
        gi, hd = chains[ci]
        r = slice(gi * GROUP, (gi + 1) * GROUP)
        c = slice(hd * DN_D, (hd + 1) * DN_D)
        q, k, v = qn_scr[r, c], kn_scr[r, c], v_scr[r, c]
        beta = beta_scr[r, AB_LANE + DN_HEADS + hd:AB_LANE + DN_HEADS + hd + 1]
        g = gc_scr[r, AB_LANE + hd:AB_LANE + hd + 1]
        eg = jnp.exp(g)
        kb = k * beta
        gcb = jnp.broadcast_to(g * math.log2(math.e), (GROUP, GROUP))
        back["dmat"][ci] = jnp.exp2(jnp.minimum(gcb - gcb.T, 0.0)) * incl
        back["q16"][ci] = q.astype(bf16)
        back["qe16"][ci] = (q * eg).astype(bf16)
        back["k16"][ci] = k.astype(bf16)
        back["kb16"][ci] = kb.astype(bf16)
        back["rhs"][ci] = jnp.concatenate([v * beta, kb * eg], axis=1)
        kd, sd = [], []
        for sg in range(segs):
            a0 = sg * tb
            glast = g[a0 + tb - 1:a0 + tb, :]
            kd.append((k[a0:a0 + tb] * jnp.exp(glast - g[a0:a0 + tb])).astype(bf16))
            sd.append(jnp.exp(glast))
        back["kdec"][ci] = kd
        back["sdecay"][ci] = sd

    def b_mats(ci):
        d = functools.partial(lax.dot_general, dimension_numbers=_NT, preferred_element_type=f32)
        dm = back["dmat"][ci]
        back["nmat"][ci] = d(back["kb16"][ci], back["k16"][ci]) * (dm * strict)
        back["attn"][ci] = (d(back["q16"][ci], back["k16"][ci]) * dm).astype(bf16)

    def b_state(sols):
        s_old = {(b, hd): s_ref[b, hd] for b in range(nb) for hd in range(DN_HEADS)}
        s16 = {key: val.astype(bf16) for key, val in s_old.items()}
        v_news, o_states = [], []
        for (gi, hd), sol, qe16 in zip(chains, sols, back["qe16"]):
            u = sol[:, :DN_D]
            w16 = sol[:, DN_D:].astype(bf16)
            vn, os_ = [], []
            for sg in range(segs):
                a0 = sg * tb
                st = s16[(stream_of(gi, sg), hd)]
                vn.append(u[a0:a0 + tb] - jnp.dot(w16[a0:a0 + tb], st, preferred_element_type=f32))
                os_.append(jnp.dot(qe16[a0:a0 + tb], st, preferred_element_type=f32))
            v_news.append(vn[0] if segs == 1 else jnp.concatenate(vn, axis=0))
            o_states.append(os_[0] if segs == 1 else jnp.concatenate(os_, axis=0))
        vn16s = [vn.astype(bf16) for vn in v_news]
        outs = [os_ + jnp.dot(a, vn, preferred_element_type=f32)
                for os_, a, vn in zip(o_states, back["attn"], vn16s)]

        s_new = {}
        for (gi, hd), kd, sd, vn in zip(chains, back["kdec"], back["sdecay"], vn16s):
            for sg in range(segs):
                a0 = sg * tb
                key = (stream_of(gi, sg), hd)
                s_new[key] = (s_old[key] * sd[sg]
                              + lax.dot_general(kd[sg], vn[a0:a0 + tb], _TN, preferred_element_type=f32))

        o_rows = []
        for gi in range(n_groups):
            o_heads = []
            for hd in range(DN_HEADS):
                o = outs[gi * DN_HEADS + hd]
                on = o * lax.rsqrt(jnp.mean(o * o, axis=-1, keepdims=True) + EPS) * dnw
                zd = zdn_scr[gi * GROUP:(gi + 1) * GROUP, hd * DN_D:(hd + 1) * DN_D]
                o_heads.append(on * _silu(zd))
            o_rows.append(jnp.concatenate(o_heads, axis=1))
        o_all = o_rows[0] if n_groups == 1 else jnp.concatenate(o_rows, axis=0)
        odn_ref[...] = o_all.astype(bf16).reshape(nb, tb, DN_WIDTH)
        s_ref[...] = jnp.stack([jnp.stack([s_new[(b, hd)] for hd in range(DN_HEADS)]) for b in range(nb)])

    n_ch = len(chains)
    if pipelined:
        a_norm_in()
        for ci in range(n_ch):
            b_load(ci)
        back_setup = [functools.partial(b_mats, ci) for ci in range(n_ch)]
        convs = [fillers.pop(0) for _ in range(nb * 3)]
        per_piece = -(-len(back_setup) // len(projections))
        for idx, piece in enumerate(projections):
            piece()
            for thunk in back_setup[:per_piece]:
                thunk()
            del back_setup[:per_piece]
            if idx >= 3 and convs:
                convs.pop(0)(None)
        while back_setup or convs:
            if back_setup:
                back_setup.pop(0)()
            if convs:
                convs.pop(0)(None)
        sols = _unit_lower_solve(back["nmat"], back["rhs"], eye, m_ref, n_levels, fillers)
        for fill in fillers:
            fill(None)
        b_state(sols)
        a_store()
    else:
        a_norm_in()
        for piece in projections:
            piece()
        for fill in fillers:
            fill(None)
        a_store()
        for ci in range(n_ch):
            b_load(ci)
        for ci in range(n_ch):
            b_mats(ci)
        b_state(_unit_lower_solve(back["nmat"], back["rhs"], eye, m_ref, n_levels, []))


def _delta_masks(tb):
    r = np.arange(GROUP)[:, None]
    c = np.arange(GROUP)[None, :]
    same = (r // tb) == (c // tb)
    ms = [same & (r >= c), same & (r > c)]
    for l in range(tb.bit_length() - 1):
        ms.append(((r >> (l + 1)) == (c >> (l + 1))) & (((r >> l) & 1) == 1) & (((c >> l) & 1) == 0))
    return jnp.asarray(np.stack(ms).astype(np.float32))


def _delta_call(x, w1, w2, norm_w, conv_w, alog, dtb, dnw, conv0, s0, *, tb, pipelined):
    nb, T, _ = x.shape
    assert GROUP % tb == 0 and (nb * tb) % GROUP == 0 and T % tb == 0
    n_t = T // tb
    rows = nb * tb
    if pipelined:
        grid = (n_t + 1,)
        cur = lambda i: jnp.minimum(i, n_t - 1)
        prev = lambda i: jnp.maximum(i - 1, 0)
    else:
        grid = (n_t,)
        cur = prev = lambda i: i
    const2 = lambda i: (0, 0)
    const3 = lambda i: (0, 0, 0)
    const4 = lambda i: (0, 0, 0, 0)
    body = functools.partial(_delta_body, nb=nb, tb=tb, n_t=n_t, pipelined=pipelined)
    masks = _delta_masks(tb)
    one = pl.Buffered(1)
    return pl.pallas_call(
        body,
        grid=grid,
        in_specs=[
            pl.BlockSpec((nb, tb, D_MODEL), lambda i: (0, cur(i), 0)),
            pl.BlockSpec((D_MODEL, CONV_CH), const2, pipeline_mode=one),
            pl.BlockSpec((D_MODEL, W2_COLS), const2, pipeline_mode=one),
            pl.BlockSpec((1, D_MODEL), const2),
            pl.BlockSpec((CONV_K, CONV_CH), const2),
            pl.BlockSpec((1, LANES), const2),
            pl.BlockSpec((1, LANES), const2),
            pl.BlockSpec((1, DN_D), const2),
            pl.BlockSpec(masks.shape, const3, pipeline_mode=one),
            pl.BlockSpec((nb, SUBLANES, CONV_CH), const3),
            pl.BlockSpec((nb, DN_HEADS, DN_D, DN_D), const4),
        ],
        out_specs=[
            pl.BlockSpec((nb, tb, DN_WIDTH), lambda i: (0, prev(i), 0)),
            pl.BlockSpec((nb, S5_PLANES, tb, LANES), lambda i: (0, 0, cur(i), 0)),
            pl.BlockSpec((nb, tb, S5_WIDTH), lambda i: (0, cur(i), 0)),
            pl.BlockSpec((nb, SUBLANES, CONV_CH), const3),
            pl.BlockSpec((nb, DN_HEADS, DN_D, DN_D), const4),
        ],
        out_shape=[
            jax.ShapeDtypeStruct((nb, T, DN_WIDTH), bf16),
            jax.ShapeDtypeStruct((nb, S5_PLANES, T, LANES), f32),
            jax.ShapeDtypeStruct((nb, T, S5_WIDTH), f32),
            jax.ShapeDtypeStruct((nb, SUBLANES, CONV_CH), f32),
            jax.ShapeDtypeStruct((nb, DN_HEADS, DN_D, DN_D), f32),
        ],
        scratch_shapes=[
            pltpu.VMEM((nb, tb + SUBLANES, CONV_CH), f32),
            pltpu.VMEM((nb, tb + SUBLANES, CONV_CH), f32),
            pltpu.VMEM((rows, DN_WIDTH), f32),
            pltpu.VMEM((rows, DN_WIDTH), f32),
            pltpu.VMEM((rows, DN_WIDTH), f32),
            pltpu.VMEM((rows, LANES), f32),
            pltpu.VMEM((rows, LANES), f32),
            pltpu.VMEM((rows, DN_WIDTH), f32),
        ],
        compiler_params=pltpu.CompilerParams(
            dimension_semantics=("arbitrary",), vmem_limit_bytes=VMEM_LIMIT),
        name="delta",
    )(x, w1, w2, norm_w, conv_w, alog, dtb, dnw, masks, conv0, s0)


S5MAT_GROUPS_PER_STEP = 8


def _split16(a):
    hi = a.astype(bf16)
    return hi, (a - hi.astype(f32)).astype(bf16)


def _s5mat_body(cp_ref, bp_ref, t_ref):
    ri = lax.broadcasted_iota(jnp.int32, (S5_GL, S5_GL), 0) >> 4
    ci = lax.broadcasted_iota(jnp.int32, (S5_GL, S5_GL), 1) >> 4
    causal = ri >= ci
    d = functools.partial(lax.dot_general, dimension_numbers=_NT, preferred_element_type=f32)
    for g in range(S5MAT_GROUPS_PER_STEP):
        ch, cl = _split16(cp_ref[g])
        bh, bl = _split16(bp_ref[g])
        t = d(ch, bh) + (d(ch, bl) + d(cl, bh))
        t_ref[g] = jnp.where(causal, t, 0.0).astype(bf16)


def _s5mat_call(cp, bp):
    n = S5MAT_GROUPS_PER_STEP
    return pl.pallas_call(
        _s5mat_body,
        grid=(S5_GROUPS // n,),
        in_specs=[
            pl.BlockSpec((n, S5_GL, 2 * S5_STATE), lambda g: (g, 0, 0)),
            pl.BlockSpec((n, S5_GL, 2 * S5_STATE), lambda g: (g, 0, 0)),
        ],
        out_specs=pl.BlockSpec((n, S5_GL, S5_GL), lambda g: (g, 0, 0)),
        out_shape=jax.ShapeDtypeStruct((S5_GROUPS, S5_GL, S5_GL), bf16),
        compiler_params=pltpu.CompilerParams(dimension_semantics=("arbitrary",)),
        name="s5mat",
    )(cp, bp)


def _s5_body(u_ref, z_ref, xr0_ref, xi0_ref, t_ref, wp_ref, vt_ref, ar_ref, ai_ref, d_ref,
             gw_ref, gb_ref,
             o_ref, xr_ref, xi_ref,
             at_scr, er_scr, ei_scr, xinr_scr, xini_scr, yt_scr, y_scr, *, tb, n_streams, n_chunks):
    R = tb // S5_L
    used = n_streams * n_chunks

    @pl.when(pl.program_id(1) == 0)
    def _init():
        xr_ref[...] = xr0_ref[...]
        xi_ref[...] = xi0_ref[...]

    for f in range(S5_L):
        for j in range(S5_PLANES):
            ut = u_ref[0, j, pl.ds(f, R, stride=S5_L), :]
            ut_t = ut.astype(bf16).T
            for gl in range(LANES // S5_GROUP):
                g = j * (LANES // S5_GROUP) + gl
                at_scr[g, f * S5_GROUP:(f + 1) * S5_GROUP, :] = ut_t[gl * S5_GROUP:(gl + 1) * S5_GROUP, :]

    for p in range(S5_PAIRS):
        a_t = at_scr[2 * p:2 * p + 2].reshape(2 * S5_GL, R)
        e = lax.dot_general(a_t, wp_ref[p], _TN, preferred_element_type=f32)
        er_scr[:, p * LANES:(p + 1) * LANES] = e[:, :LANES]
        ei_scr[:, p * LANES:(p + 1) * LANES] = e[:, LANES:]

    if used < R:
        xinr_scr[...] = jnp.zeros_like(xinr_scr)
        xini_scr[...] = jnp.zeros_like(xini_scr)

    a_r = ar_ref[...]
    a_i = ai_ref[...]
    for s in range(n_streams):
        def step(c, carry):
            xr, xi = carry
            row = s * n_chunks + c
            xinr_scr[pl.ds(row, 1), :] = xr
            xini_scr[pl.ds(row, 1), :] = xi
            er = er_scr[pl.ds(row, 1), :]
            ei = ei_scr[pl.ds(row, 1), :]
            return a_r * xr - a_i * xi + er, a_r * xi + a_i * xr + ei

        xr, xi = lax.fori_loop(0, n_chunks, step, (xr_ref[0, s:s + 1, :], xi_ref[0, s:s + 1, :]),
                               unroll=min(n_chunks, 4))
        xr_ref[0, s:s + 1, :] = xr
        xi_ref[0, s:s + 1, :] = xi

    for g in range(S5_GROUPS):
        p = g // 2
        xin = jnp.concatenate([xinr_scr[:, p * LANES:(p + 1) * LANES],
                               xini_scr[:, p * LANES:(p + 1) * LANES]], axis=1).astype(bf16)
        yt_scr[g] = (jnp.dot(t_ref[g], at_scr[g], preferred_element_type=f32)
                     + lax.dot_general(vt_ref[g], xin, _NT, preferred_element_type=f32))

    for f in range(S5_L):
        gpp = LANES // S5_GROUP
        for j in range(S5_PLANES):
            yt = yt_scr[j * gpp:(j + 1) * gpp, f * S5_GROUP:(f + 1) * S5_GROUP, :].reshape(LANES, R)
            y_scr[j, pl.ds(f, R, stride=S5_L), :] = yt.T

    for rb in range(tb // OUT_SUB):
        sl = pl.ds(rb * OUT_SUB, OUT_SUB)
        y_intra = jnp.concatenate([y_scr[j, sl, :] for j in range(S5_PLANES)], axis=1)
        u_nat = jnp.concatenate([u_ref[0, j, sl, :] for j in range(S5_PLANES)], axis=1)
        y = y_intra + d_ref[...] * u_nat
        gy = _gelu_tanh(y)
        gate = _sigmoid(_bdot(gy, gw_ref[...]) + gb_ref[...])
        o_ref[0, sl, :] = (gy * gate * _silu(z_ref[0, sl, :])).astype(bf16)


def _s5_call(u, z, xr0, xi0, tmat, wp, vt, ar, ai, dvec, gw, gb, *, tb, n_streams, n_chunks):
    G, T, _ = z.shape
    grid = (G, T // tb)
    R = tb // S5_L
    nstate = S5_GROUPS * S5_STATE
    c2 = lambda g, t: (0, 0)
    c3 = lambda g, t: (0, 0, 0)
    body = functools.partial(_s5_body, tb=tb, n_streams=n_streams, n_chunks=n_chunks)
    one = pl.Buffered(1)
    return pl.pallas_call(
        body,
        grid=grid,
        in_specs=[
            pl.BlockSpec((1, S5_PLANES, tb, LANES), lambda g, t: (g, 0, t, 0)),
            pl.BlockSpec((1, tb, S5_WIDTH), lambda g, t: (g, t, 0)),
            pl.BlockSpec((1, n_streams, nstate), lambda g, t: (g, 0, 0)),
            pl.BlockSpec((1, n_streams, nstate), lambda g, t: (g, 0, 0)),
            pl.BlockSpec((S5_GROUPS, S5_GL, S5_GL), c3, pipeline_mode=one),
            pl.BlockSpec((S5_PAIRS, 2 * S5_GL, 2 * LANES), c3, pipeline_mode=one),
            pl.BlockSpec((S5_GROUPS, S5_GL, 2 * LANES), c3, pipeline_mode=one),
            pl.BlockSpec((1, nstate), c2),
            pl.BlockSpec((1, nstate), c2),
            pl.BlockSpec((1, S5_WIDTH), c2),
            pl.BlockSpec((S5_WIDTH, S5_WIDTH), c2),
            pl.BlockSpec((1, S5_WIDTH), c2),
        ],
        out_specs=[
            pl.BlockSpec((1, tb, S5_WIDTH), lambda g, t: (g, t, 0)),
            pl.BlockSpec((1, n_streams, nstate), lambda g, t: (g, 0, 0)),
            pl.BlockSpec((1, n_streams, nstate), lambda g, t: (g, 0, 0)),
        ],
        out_shape=[
            jax.ShapeDtypeStruct((G, T, S5_WIDTH), bf16),
            jax.ShapeDtypeStruct((G, n_streams, nstate), f32),
            jax.ShapeDtypeStruct((G, n_streams, nstate), f32),
        ],
        scratch_shapes=[
            pltpu.VMEM((S5_GROUPS, S5_GL, R), bf16),
            pltpu.VMEM((R, nstate), f32),
            pltpu.VMEM((R, nstate), f32),
            pltpu.VMEM((R, nstate), f32),
            pltpu.VMEM((R, nstate), f32),
            pltpu.VMEM((S5_GROUPS, S5_GL, R), f32),
            pltpu.VMEM((S5_PLANES, tb, LANES), f32),
        ],
        compiler_params=pltpu.CompilerParams(
            dimension_semantics=("arbitrary", "arbitrary"), vmem_limit_bytes=VMEM_LIMIT),
        name="s5",
    )(u, z, xr0, xi0, tmat, wp, vt, ar, ai, dvec, gw, gb)


def _out_body(x_ref, odn_ref, os5_ref, wo_ref, fw_ref, y_ref):
    acc = (x_ref[...]
           + jnp.dot(odn_ref[...], wo_ref[0:DN_WIDTH, :], preferred_element_type=f32)
           + jnp.dot(os5_ref[...], wo_ref[DN_WIDTH:, :], preferred_element_type=f32))
    ms = jnp.mean(acc * acc, axis=-1, keepdims=True)
    y_ref[...] = acc * lax.rsqrt(ms + EPS) * fw_ref[...]


def _out_call(x2, odn2, os52, w_out16, fw, *, rows):
    n = x2.shape[0]
    return pl.pallas_call(
        _out_body,
        grid=(n // rows,),
        in_specs=[
            pl.BlockSpec((rows, D_MODEL), lambda i: (i, 0)),
            pl.BlockSpec((rows, DN_WIDTH), lambda i: (i, 0)),
            pl.BlockSpec((rows, S5_WIDTH), lambda i: (i, 0)),
            pl.BlockSpec((D_MODEL, D_MODEL), lambda i: (0, 0)),
            pl.BlockSpec((1, D_MODEL), lambda i: (0, 0)),
        ],
        out_specs=pl.BlockSpec((rows, D_MODEL), lambda i: (i, 0)),
        out_shape=jax.ShapeDtypeStruct((n, D_MODEL), f32),
        compiler_params=pltpu.CompilerParams(
            dimension_semantics=("arbitrary",), vmem_limit_bytes=VMEM_LIMIT),
        name="outproj",
    )(x2, odn2, os52, w_out16, fw)


def _s5_operands(a_re, a_im, log_dt, b_re, b_im, c_re, c_im):
    lam_re = jnp.minimum(a_re, -1e-4)
    lam_im = a_im
    dt = jnp.exp(log_dt)[:, None]
    ldt_re, ldt_im = lam_re * dt, lam_im * dt

    def lpow(k):
        mag = jnp.exp(ldt_re * k)
        return mag * jnp.cos(ldt_im * k), mag * jnp.sin(ldt_im * k)

    lb_re, lb_im = lpow(1.0)
    den = lam_re * lam_re + lam_im * lam_im
    f_re = ((lb_re - 1.0) * lam_re + lb_im * lam_im) / den
    f_im = (lb_im * lam_re - (lb_re - 1.0) * lam_im) / den
    bb_re = f_re[..., None] * b_re - f_im[..., None] * b_im
    bb_im = f_re[..., None] * b_im + f_im[..., None] * b_re

    fr = jnp.arange(S5_L, dtype=f32)
    G, N = S5_GROUPS, S5_STATE

    def cmul(ar, ai, br, bi):
        return ar * br - ai * bi, ar * bi + ai * br

    pr, pi = lpow(fr[:, None, None])
    pr, pi = pr.transpose(1, 0, 2), pi.transpose(1, 0, 2)
    cpr, cpi = cmul(c_re[:, None], c_im[:, None], pr[:, :, None], pi[:, :, None])
    cpr, cpi = cpr.reshape(G, S5_GL, N), cpi.reshape(G, S5_GL, N)
    bbr_t, bbi_t = bb_re.transpose(0, 2, 1), bb_im.transpose(0, 2, 1)
    qr, qi = lpow(-fr[:, None, None])
    qr, qi = qr.transpose(1, 0, 2), qi.transpose(1, 0, 2)
    bpr, bpi = cmul(qr[:, :, None], qi[:, :, None], bbr_t[:, None], bbi_t[:, None])
    bpr, bpi = bpr.reshape(G, S5_GL, N), bpi.reshape(G, S5_GL, N)

    wr_, wi_ = lpow((S5_L - 1.0) - fr[:, None, None])
    wr_, wi_ = wr_.transpose(1, 0, 2), wi_.transpose(1, 0, 2)
    w_re, w_im = cmul(wr_[:, :, None], wi_[:, :, None], bbr_t[:, None], bbi_t[:, None])
    w_re, w_im = w_re.reshape(G, S5_GL, N), w_im.reshape(G, S5_GL, N)
    vr_, vi_ = lpow(fr[:, None, None] + 1.0)
    vr_, vi_ = vr_.transpose(1, 0, 2), vi_.transpose(1, 0, 2)
    v_re, v_im = cmul(c_re[:, None], c_im[:, None], vr_[:, :, None], vi_[:, :, None])
    v_re, v_im = v_re.reshape(G, S5_GL, N), v_im.reshape(G, S5_GL, N)

    zeros = jnp.zeros((S5_PAIRS, S5_GL, N), f32)

    def pair_cols(re, im):
        re, im = re.reshape(S5_PAIRS, 2, S5_GL, N), im.reshape(S5_PAIRS, 2, S5_GL, N)
        even = jnp.concatenate([re[:, 0], zeros, im[:, 0], zeros], axis=-1)
        odd = jnp.concatenate([zeros, re[:, 1], zeros, im[:, 1]], axis=-1)
        return jnp.stack([even, odd], axis=1)

    wp = pair_cols(w_re, w_im).reshape(S5_PAIRS, 2 * S5_GL, 2 * LANES).astype(bf16)
    vt = pair_cols(v_re, -v_im).reshape(S5_GROUPS, S5_GL, 2 * LANES).astype(bf16)
    a16r, a16i = lpow(float(S5_L))
    cp = jnp.concatenate([cpr, cpi], axis=2)
    bp = jnp.concatenate([bpr, -bpi], axis=2)
    return cp, bp, wp, vt, a16r.reshape(1, G * N), a16i.reshape(1, G * N)


def _pad_lanes(v, start=0):
    out = jnp.zeros((1, LANES), f32)
    return out.at[0, start:start + v.shape[0]].set(v.astype(f32))


def _layer(x, conv0, s0, xr0, xi0, prm, *, delta_cfg, s5_cfg):
    B, T, _ = x.shape
    conv0p = jnp.concatenate([jnp.zeros((B, SUBLANES - (CONV_K - 1), CONV_CH), f32), conv0], axis=1)
    odn, us5, zs5, convout, s_new = _delta_call(
        x, prm["w1"], prm["w2"], prm["norm_w"], prm["conv_w"], prm["alog"], prm["dtb"], prm["dnw"],
        conv0p, s0, **delta_cfg)

    nstate = S5_GROUPS * S5_STATE
    if s5_cfg["flatten"]:
        tb = s5_cfg["tb"]
        uf = us5.transpose(1, 0, 2, 3).reshape(1, S5_PLANES, B * T, LANES)
        zf = zs5.reshape(1, B * T, S5_WIDTH)
        pad = tb - B * T
        uf = jnp.pad(uf, ((0, 0), (0, 0), (0, pad), (0, 0)))
        zf = jnp.pad(zf, ((0, 0), (0, pad), (0, 0)))
        os5, xr, xi = _s5_call(uf, zf, xr0.reshape(1, B, nstate), xi0.reshape(1, B, nstate),
                               prm["tmat"], prm["wp"], prm["vt"], prm["ar"], prm["ai"], prm["dvec"],
                               prm["gw"], prm["gb"], tb=tb, n_streams=B, n_chunks=T // S5_L)
        os5 = os5[0, :B * T].reshape(B, T, S5_WIDTH)
    else:
        tb = s5_cfg["tb"]
        os5, xr, xi = _s5_call(us5, zs5, xr0.reshape(B, 1, nstate), xi0.reshape(B, 1, nstate),
                               prm["tmat"], prm["wp"], prm["vt"], prm["ar"], prm["ai"], prm["dvec"],
                               prm["gw"], prm["gb"], tb=tb, n_streams=1, n_chunks=tb // S5_L)

    y = _out_call(x.reshape(B * T, D_MODEL), odn.reshape(B * T, DN_WIDTH), os5.reshape(B * T, S5_WIDTH),
                  prm["w_out"], prm["fw"], rows=min(OUT_ROWS, B * T))
    return (y.reshape(B, T, D_MODEL), convout[:, SUBLANES - (CONV_K - 1):, :], s_new,
            xr.reshape(B, S5_GROUPS, S5_STATE), xi.reshape(B, S5_GROUPS, S5_STATE))


def kernel(x_prompt, x_sample, cache_conv, state_dn, state_s5_re, state_s5_im, norm_w, w_in, conv_w, dn_A_log, dn_dt_bias, dn_norm_w, s5_A_re, s5_A_im, s5_log_dt, s5_B_re, s5_B_im, s5_C_re, s5_C_im, s5_D, glu_w, glu_b, w_out, final_norm_w):
    depth = norm_w.shape[0]
    assert depth == 1
    l = 0
    assert w_in.shape[-1] == IN_COLS
    w1 = w_in[l, :, :CONV_CH].astype(bf16)
    w2 = w_in[l, :, IN_COLS - W2_COLS:].astype(bf16)
    cp, bp, wp, vt, ar, ai = _s5_operands(
        s5_A_re[l].astype(f32), s5_A_im[l].astype(f32), s5_log_dt[l].astype(f32),
        s5_B_re[l].astype(f32), s5_B_im[l].astype(f32), s5_C_re[l].astype(f32), s5_C_im[l].astype(f32))
    prm = dict(
        w1=w1, w2=w2,
        norm_w=norm_w[l].reshape(1, D_MODEL).astype(f32),
        conv_w=conv_w[l].astype(f32),
        alog=_pad_lanes(dn_A_log[l], AB_LANE),
        dtb=_pad_lanes(dn_dt_bias[l], AB_LANE),
        dnw=dn_norm_w[l].reshape(1, DN_D).astype(f32),
        tmat=_s5mat_call(cp, bp),
        wp=wp, vt=vt, ar=ar, ai=ai,
        dvec=s5_D[l].reshape(1, S5_WIDTH).astype(f32),
        gw=glu_w[l].astype(bf16),
        gb=glu_b[l].reshape(1, S5_WIDTH).astype(f32),
        w_out=w_out[l].astype(bf16),
        fw=final_norm_w.reshape(1, D_MODEL).astype(f32),
    )

    bp = x_prompt.shape[0]
    yp, c1, d1, r1, i1 = _layer(
        x_prompt,
        jnp.zeros((bp, CONV_K - 1, CONV_CH), f32),
        jnp.zeros((bp, DN_HEADS, DN_D, DN_D), f32),
        jnp.zeros((bp, S5_GROUPS, S5_STATE), f32),
        jnp.zeros((bp, S5_GROUPS, S5_STATE), f32),
        prm,
        delta_cfg=dict(tb=256, pipelined=True),
        s5_cfg=dict(flatten=False, tb=2048))
    ys, c2, d2, r2, i2 = _layer(
        x_sample, cache_conv[l].astype(f32), state_dn[l].astype(f32),
        state_s5_re[l].astype(f32), state_s5_im[l].astype(f32),
        prm,
        delta_cfg=dict(tb=32, pipelined=False),
        s5_cfg=dict(flatten=True, tb=2048))

    return (yp, ys, c1[None], d1[None], r1[None], i1[None], c2[None], d2[None], r2[None], i2[None])
```

```python
import functools
import math

import jax
import jax.numpy as jnp
import numpy as np
from jax import lax
from jax.experimental import pallas as pl
from jax.experimental.pallas import tpu as pltpu

bf16 = jnp.bfloat16
f32 = jnp.float32

LANES = 128
SUBLANES = 8

D_MODEL = 1024
DN_HEADS = 4
DN_D = 128
DN_WIDTH = DN_HEADS * DN_D
CONV_K = 4
CONV_CH = 3 * DN_WIDTH
S5_WIDTH = 512
S5_GROUP = 16
S5_GROUPS = 32
S5_STATE = 64
S5_PAIRS = S5_GROUPS // 2
S5_L = 16
S5_GL = S5_GROUP * S5_L
OUT_SUB = 256
OUT_ROWS = 2048
S5_PLANES = S5_WIDTH // LANES
EPS = 1e-6
L2_EPS = 1e-6

IN_COLS = CONV_CH + 2 * DN_HEADS + DN_WIDTH + 2 * S5_WIDTH
W2_COLS = LANES + DN_WIDTH + 2 * S5_WIDTH
AB_LANE = LANES - 2 * DN_HEADS
W2_ZDN = LANES
W2_US5 = W2_ZDN + DN_WIDTH
W2_ZS5 = W2_US5 + S5_WIDTH

GROUP = 256
MASK_INCL, MASK_STRICT, MASK_OFF = 0, 1, 2
FOLD_MIN_ROWS = 64
VMEM_LIMIT =56 * 1024 * 1024

_NT = (((1,), (1,)), ((), ()))
_TN = (((0,), (0,)), ((), ()))


def _bdot(a, b):
    return jnp.dot(a.astype(bf16), b.astype(bf16), preferred_element_type=f32)


def _bdot_g(a, b, dims):
    return lax.dot_general(a.astype(bf16), b.astype(bf16), dims, preferred_element_type=f32)


def _sigmoid(x):
    return 1.0 / (1.0 + jnp.exp2(x * (-math.log2(math.e))))


def _silu(x):
    return x * _sigmoid(x)


def _softplus(x):
    return jnp.maximum(x, 0.0) + jnp.log1p(jnp.exp(-jnp.abs(x)))


def _gelu_tanh(x):
    c = math.sqrt(2.0 / math.pi)
    return x * (0.5 + 0.5 * jnp.tanh(x * (c + (c * 0.044715) * (x * x))))


def _zero_after(x):
    bits = pltpu.bitcast(x[0:SUBLANES, 0:LANES].astype(f32), jnp.uint32)
    half = jnp.uint32(16)
    bits = lax.shift_right_logical(lax.shift_right_logical(bits, half), half)
    return pltpu.bitcast(bits, f32)[0:1, :]


def _unit_lower_solve(nmats, rhss, eye, m_ref, n_levels, fillers):
    d32 = functools.partial(jnp.dot, preferred_element_type=f32)

    def d16(a, b):
        return d32(a, b).astype(bf16)

    def fill(after):
        if fillers:
            fillers.pop(0)(_zero_after(after[-1]))

    def fold(x, rows):
        out = x[0:rows]
        for k in range(1, GROUP // rows):
            out = out + x[k * rows:(k + 1) * rows]
        return out

    def unfold(xf, rows):
        lane_blk = lax.broadcasted_iota(jnp.int32, (rows, GROUP), 1) >> (rows.bit_length() - 1)
        return jnp.concatenate([jnp.where(lane_blk == k, xf, jnp.zeros_like(xf))
                                for k in range(GROUP // rows)], axis=0)

    n16 = [n.astype(bf16) for n in nmats]
    invs = [(eye - n * m_ref[MASK_OFF]).astype(bf16) for n in nmats]
    substitute_top = 2 ** n_levels == GROUP
    for lvl in range(1, n_levels - 1 if substitute_top else n_levels):
        rows = min(GROUP, max(FOLD_MIN_ROWS, 2 ** (lvl + 1)))
        mask16 = m_ref[MASK_OFF + lvl].astype(bf16)
        lhs = invs if rows == GROUP else [fold(inv, rows) for inv in invs]
        t1 = [d16(l, n * mask16) for l, n in zip(lhs, n16)]
        fill(invs)
        t2 = [d16(t, inv) for t, inv in zip(t1, invs)]
        fill(t1)
        if rows != GROUP:
            t2 = [unfold(t, rows) for t in t2]
        invs = [inv - t for inv, t in zip(invs, t2)]
    if not substitute_top:
        return [d32(inv, r.astype(bf16)) for inv, r in zip(invs, rhss)]
    h = GROUP // 2
    x1 = [d32(inv[0:h, 0:h], r[0:h].astype(bf16)) for inv, r in zip(invs, rhss)]
    fill(invs)
    bx = [d32(n[h:, 0:h], x.astype(bf16)) for n, x in zip(n16, x1)]
    fill(x1)
    x2 = [d32(inv[h:, h:], (r[h:] - t).astype(bf16)) for inv, r, t in zip(invs, rhss, bx)]
    return [jnp.concatenate([a, b], axis=0) for a, b in zip(x1, x2)]


def _segment_cumsum(tri16, g):
    g1 = g.astype(bf16)
    r1 = g - g1.astype(f32)
    g2 = r1.astype(bf16)
    g3 = (r1 - g2.astype(f32)).astype(bf16)
    s = jnp.dot(tri16, jnp.concatenate([g1, g2, g3], axis=1), preferred_element_type=f32)
    n = g.shape[1]
    return s[:, :n] + (s[:, n:2 * n] + s[:, 2 * n:])


def _delta_body(x_ref, w1_ref, w2_ref, nw_ref, cw_ref, alog_ref, dtb_ref, dnw_ref, m_ref, conv0_ref, s0_ref,
                odn_ref, us5_ref, zs5_ref, convout_ref, s_ref,
                cbuf, pbuf, qn_scr, kn_scr, v_scr, gc_scr, beta_scr, zdn_scr,
                *, nb, tb, n_t, pipelined):
    rows = nb * tb
    n_groups = rows // GROUP
    segs = GROUP // tb
    n_levels = tb.bit_length() - 1
    step = pl.program_id(0)
    fresh = step < n_t

    @pl.when(step == 0)
    def _init():
        cbuf[:, 0:SUBLANES, :] = conv0_ref[...]
        for b in range(nb):
            c8 = conv0_ref[b]
            pbuf[b, 0:SUBLANES, :] = c8 * cw_ref[1:2, :] + pltpu.roll(c8, 1, axis=0) * cw_ref[0:1, :]
        s_ref[...] = s0_ref[...]
        if pipelined:
            for scr in (qn_scr, kn_scr, v_scr, gc_scr, beta_scr, zdn_scr):
                scr[...] = jnp.zeros_like(scr)

    front = {}

    def project(w_ref, lo, width):
        return jnp.dot(front["h"], w_ref[:, lo:lo + width], preferred_element_type=f32)

    def a_norm_in():
        x = x_ref[...].reshape(rows, D_MODEL)
        ms = jnp.mean(x * x, axis=-1, keepdims=True)
        front["h"] = (x * lax.rsqrt(ms + EPS) * nw_ref[...]).astype(bf16)

    def a_project_qkv(s):
        front[("raw", s)] = project(w1_ref, s * DN_WIDTH, DN_WIDTH)

    def a_project_us5():
        u = project(w2_ref, W2_US5, S5_WIDTH)
        for j in range(S5_PLANES):
            us5_ref[:, j] = u[:, j * LANES:(j + 1) * LANES].reshape(nb, tb, LANES)

    def a_project_zs5():
        zs5_ref[...] = project(w2_ref, W2_ZS5, S5_WIDTH).reshape(nb, tb, S5_WIDTH)

    def a_project_zdn():
        front["zdn"] = project(w2_ref, W2_ZDN, DN_WIDTH)

    def a_project_gates():
        ab = project(w2_ref, 0, LANES)
        front["g"] = -jnp.exp(alog_ref[...]) * _softplus(ab + dtb_ref[...])
        front["beta"] = _sigmoid(ab)

    projections = ([functools.partial(a_project_qkv, s) for s in range(3)]
                   + [a_project_us5, a_project_zs5, a_project_zdn, a_project_gates])

    def a_conv(b, s, zero):
        cols = slice(s * DN_WIDTH, (s + 1) * DN_WIDTH)
        u0 = front[("raw", s)][b * tb:(b + 1) * tb, :]
        cbuf[b, SUBLANES:SUBLANES + tb, cols] = u0
        taps = [cw_ref[j:j + 1, cols] for j in range(CONV_K)]
        if zero is not None:
            zero_w = jnp.concatenate([zero] * (DN_WIDTH // LANES), axis=1)
            taps = [t + zero_w for t in taps]
        u1 = cbuf[b, SUBLANES - 1:SUBLANES - 1 + tb, cols]
        pbuf[b, SUBLANES:SUBLANES + tb, cols] = u0 * taps[1] + u1 * taps[0]
        acc = (u0 * taps[3] + u1 * taps[2]) + pbuf[b, SUBLANES - 2:SUBLANES - 2 + tb, cols]
        front[("act", b, s)] = _silu(acc)
        for buf in (cbuf, pbuf):
            tail = buf[b, tb:tb + SUBLANES, cols]
            if pipelined:
                tail = jnp.where(fresh, tail, buf[b, 0:SUBLANES, cols])
            buf[b, 0:SUBLANES, cols] = tail
        convout_ref[b, :, cols] = cbuf[b, 0:SUBLANES, cols]

    def a_norm(b, hd, zero):
        lo = hd * DN_D
        q = front[("act", b, 0)][:, lo:lo + DN_D]
        k = front[("act", b, 1)][:, lo:lo + DN_D]
        if zero is not None:
            q, k = q + zero, k + zero
        front[("qn", b, hd)] = (q * lax.rsqrt(jnp.sum(q * q, axis=-1, keepdims=True) + L2_EPS)
                                * (DN_D ** -0.5))
        front[("kn", b, hd)] = k * lax.rsqrt(jnp.sum(k * k, axis=-1, keepdims=True) + L2_EPS)

    def a_gates(zero):
        tri16 = m_ref[MASK_INCL].astype(bf16)
        g = front["g"] if zero is None else front["g"] + zero
        front["gc"] = [_segment_cumsum(tri16, g[gi * GROUP:(gi + 1) * GROUP])
                       for gi in range(n_groups)]

    def a_store():
        for b in range(nb):
            r = slice(b * tb, (b + 1) * tb)
            for hd in range(DN_HEADS):
                c = slice(hd * DN_D, (hd + 1) * DN_D)
                qn_scr[r, c] = front[("qn", b, hd)]
                kn_scr[r, c] = front[("kn", b, hd)]
            v_scr[r, :] = front[("act", b, 2)]
        for gi in range(n_groups):
            gc_scr[gi * GROUP:(gi + 1) * GROUP, :] = front["gc"][gi]
        beta_scr[...] = front["beta"]
        zdn_scr[...] = front["zdn"]

    fillers = ([functools.partial(a_conv, b, s) for b in range(nb) for s in range(3)] + [a_gates]
               + [functools.partial(a_norm, b, hd) for b in range(nb) for hd in range(DN_HEADS)])

    incl = m_ref[MASK_INCL]
    strict = m_ref[MASK_STRICT]
    eye = incl - strict
    dnw = dnw_ref[...]
    chains = [(gi, hd) for gi in range(n_groups) for hd in range(DN_HEADS)]
    back = {}

    def stream_of(gi, sg):
        return (gi * GROUP) // tb + sg

    for name in ("q16", "qe16", "k16", "kb16", "rhs", "dmat", "kdec", "sdecay", "nmat", "attn"):
        back[name] = [None] * len(chains)

    def b_load(ci):
        gi, hd = chains[ci]
        r = slice(gi * GROUP, (gi + 1) * GROUP)
        c = slice(hd * DN_D, (hd + 1) * DN_D)
        q, k, v = qn_scr[r, c], kn_scr[r, c], v_scr[r, c]
        beta = beta_scr[r, AB_LANE + DN_HEADS + hd:AB_LANE + DN_HEADS + hd + 1]
        g = gc_scr[r, AB_LANE + hd:AB_LANE + hd + 1]
        eg = jnp.exp(g)
        kb = k * beta
        gcb = jnp.broadcast_to(g * math.log2(math.e), (GROUP, GROUP))
        back["dmat"][ci] = jnp.exp2(jnp.minimum(gcb - gcb.T, 0.0)) * incl
        back["q16"][ci] = q.astype(bf16)
        back["qe16"][ci] = (q * eg).astype(bf16)
        back["k16"][ci] = k.astype(bf16)
        back["kb16"][ci] = kb.astype(bf16)
        back["rhs"][ci] = jnp.concatenate([v * beta, kb * eg], axis=1)
        kd, sd = [], []
        for sg in range(segs):
            a0 = sg * tb
            glast = g[a0 + tb - 1:a0 + tb, :]
            kd.append((k[a0:a0 + tb] * jnp.exp(glast - g[a0:a0 + tb])).astype(bf16))
            sd.append(jnp.exp(glast))
        back["kdec"][ci] = kd
        back["sdecay"][ci] = sd

    def b_mats(ci):
        d = functools.partial(lax.dot_general, dimension_numbers=_NT, preferred_element_type=f32)
        dm = back["dmat"][ci]
        back["nmat"][ci] = d(back["kb16"][ci], back["k16"][ci]) * (dm * strict)
        back["attn"][ci] = (d(back["q16"][ci], back["k16"][ci]) * dm).astype(bf16)

    def b_state(sols):
        s_old = {(b, hd): s_ref[b, hd] for b in range(nb) for hd in range(DN_HEADS)}
        s16 = {key: val.astype(bf16) for key, val in s_old.items()}
        v_news, o_states = [], []
        for (gi, hd), sol, qe16 in zip(chains, sols, back["qe16"]):
            u = sol[:, :DN_D]
            w16 = sol[:, DN_D:].astype(bf16)
            vn, os_ = [], []
            for sg in range(segs):
                a0 = sg * tb
                st = s16[(stream_of(gi, sg), hd)]
                vn.append(u[a0:a0 + tb] - jnp.dot(w16[a0:a0 + tb], st, preferred_element_type=f32))
                os_.append(jnp.dot(qe16[a0:a0 + tb], st, preferred_element_type=f32))
            v_news.append(vn[0] if segs == 1 else jnp.concatenate(vn, axis=0))
            o_states.append(os_[0] if segs == 1 else jnp.concatenate(os_, axis=0))
        vn16s = [vn.astype(bf16) for vn in v_news]
        outs = [os_ + jnp.dot(a, vn, preferred_element_type=f32)
                for os_, a, vn in zip(o_states, back["attn"], vn16s)]

        s_new = {}
        for (gi, hd), kd, sd, vn in zip(chains, back["kdec"], back["sdecay"], vn16s):
            for sg in range(segs):
                a0 = sg * tb
                key = (stream_of(gi, sg), hd)
                s_new[key] = (s_old[key] * sd[sg]
                              + lax.dot_general(kd[sg], vn[a0:a0 + tb], _TN, preferred_element_type=f32))

        o_rows = []
        for gi in range(n_groups):
            o_heads = []
            for hd in range(DN_HEADS):
                o = outs[gi * DN_HEADS + hd]
                on = o * lax.rsqrt(jnp.mean(o * o, axis=-1, keepdims=True) + EPS) * dnw
                zd = zdn_scr[gi * GROUP:(gi + 1) * GROUP, hd * DN_D:(hd + 1) * DN_D]
                o_heads.append(on * _silu(zd))
            o_rows.append(jnp.concatenate(o_heads, axis=1))
        o_all = o_rows[0] if n_groups == 1 else jnp.concatenate(o_rows, axis=0)
        odn_ref[...] = o_all.astype(bf16).reshape(nb, tb, DN_WIDTH)
        s_ref[...] = jnp.stack([jnp.stack([s_new[(b, hd)] for hd in range(DN_HEADS)]) for b in range(nb)])

    n_ch = len(chains)
    if pipelined:
        a_norm_in()
        for ci in range(n_ch):
            b_load(ci)
        back_setup = [functools.partial(b_mats, ci) for ci in range(n_ch)]
        convs = [fillers.pop(0) for _ in range(nb * 3)]
        per_piece = -(-len(back_setup) // len(projections))
        for idx, piece in enumerate(projections):
            piece()
            for thunk in back_setup[:per_piece]:
                thunk()
            del back_setup[:per_piece]
            if idx >= 3 and convs:
                convs.pop(0)(None)
        while back_setup or convs:
            if back_setup:
                back_setup.pop(0)()
            if convs:
                convs.pop(0)(None)
        sols = _unit_lower_solve(back["nmat"], back["rhs"], eye, m_ref, n_levels, fillers)
        for fill in fillers:
            fill(None)
        b_state(sols)
        a_store()
    else:
        a_norm_in()
        for piece in projections:
            piece()
        for fill in fillers:
            fill(None)
        a_store()
        for ci in range(n_ch):
            b_load(ci)
        for ci in range(n_ch):
            b_mats(ci)
        b_state(_unit_lower_solve(back["nmat"], back["rhs"], eye, m_ref, n_levels, []))


def _delta_masks(tb):
    r = np.arange(GROUP)[:, None]
    c = np.arange(GROUP)[None, :]
    same = (r // tb) == (c // tb)
    ms = [same & (r >= c), same & (r > c)]
    for l in range(tb.bit_length() - 1):
        ms.append(((r >> (l + 1)) == (c >> (l + 1))) & (((r >> l) & 1) == 1) & (((c >> l) & 1) == 0))
    return jnp.asarray(np.stack(ms).astype(np.float32))


def _delta_call(x, w1, w2, norm_w, conv_w, alog, dtb, dnw, conv0, s0, *, tb, pipelined):
    nb, T, _ = x.shape
    assert GROUP % tb == 0 and (nb * tb) % GROUP == 0 and T % tb == 0
    n_t = T // tb
    rows = nb * tb
    if pipelined:
        grid = (n_t + 1,)
        cur = lambda i: jnp.minimum(i, n_t - 1)
        prev = lambda i: jnp.maximum(i - 1, 0)
    else:
        grid = (n_t,)
        cur = prev = lambda i: i
    const2 = lambda i: (0, 0)
    const3 = lambda i: (0, 0, 0)
    const4 = lambda i: (0, 0, 0, 0)
    body = functools.partial(_delta_body, nb=nb, tb=tb, n_t=n_t, pipelined=pipelined)
    masks = _delta_masks(tb)
    one = pl.Buffered(1)
    return pl.pallas_call(
        body,
        grid=grid,
        in_specs=[
            pl.BlockSpec((nb, tb, D_MODEL), lambda i: (0, cur(i), 0)),
            pl.BlockSpec((D_MODEL, CONV_CH), const2, pipeline_mode=one),
            pl.BlockSpec((D_MODEL, W2_COLS), const2, pipeline_mode=one),
            pl.BlockSpec((1, D_MODEL), const2),
            pl.BlockSpec((CONV_K, CONV_CH), const2),
            pl.BlockSpec((1, LANES), const2),
            pl.BlockSpec((1, LANES), const2),
            pl.BlockSpec((1, DN_D), const2),
            pl.BlockSpec(masks.shape, const3, pipeline_mode=one),
            pl.BlockSpec((nb, SUBLANES, CONV_CH), const3),
            pl.BlockSpec((nb, DN_HEADS, DN_D, DN_D), const4),
        ],
        out_specs=[
            pl.BlockSpec((nb, tb, DN_WIDTH), lambda i: (0, prev(i), 0)),
            pl.BlockSpec((nb, S5_PLANES, tb, LANES), lambda i: (0, 0, cur(i), 0)),
            pl.BlockSpec((nb, tb, S5_WIDTH), lambda i: (0, cur(i), 0)),
            pl.BlockSpec((nb, SUBLANES, CONV_CH), const3),
            pl.BlockSpec((nb, DN_HEADS, DN_D, DN_D), const4),
        ],
        out_shape=[
            jax.ShapeDtypeStruct((nb, T, DN_WIDTH), bf16),
            jax.ShapeDtypeStruct((nb, S5_PLANES, T, LANES), f32),
            jax.ShapeDtypeStruct((nb, T, S5_WIDTH), f32),
            jax.ShapeDtypeStruct((nb, SUBLANES, CONV_CH), f32),
            jax.ShapeDtypeStruct((nb, DN_HEADS, DN_D, DN_D), f32),
        ],
        scratch_shapes=[
            pltpu.VMEM((nb, tb + SUBLANES, CONV_CH), f32),
            pltpu.VMEM((nb, tb + SUBLANES, CONV_CH), f32),
            pltpu.VMEM((rows, DN_WIDTH), f32),
            pltpu.VMEM((rows, DN_WIDTH), f32),
            pltpu.VMEM((rows, DN_WIDTH), f32),
            pltpu.VMEM((rows, LANES), f32),
            pltpu.VMEM((rows, LANES), f32),
            pltpu.VMEM((rows, DN_WIDTH), f32),
        ],
        compiler_params=pltpu.CompilerParams(
            dimension_semantics=("arbitrary",), vmem_limit_bytes=VMEM_LIMIT),
        name="delta",
    )(x, w1, w2, norm_w, conv_w, alog, dtb, dnw, masks, conv0, s0)


S5MAT_GROUPS_PER_STEP = 8


def _split16(a):
    hi = a.astype(bf16)
    return hi, (a - hi.astype(f32)).astype(bf16)


def _s5mat_body(cp_ref, bp_ref, t_ref):
    ri = lax.broadcasted_iota(jnp.int32, (S5_GL, S5_GL), 0) >> 4
    ci = lax.broadcasted_iota(jnp.int32, (S5_GL, S5_GL), 1) >> 4
    causal = ri >= ci
    d = functools.partial(lax.dot_general, dimension_numbers=_NT, preferred_element_type=f32)
    for g in range(S5MAT_GROUPS_PER_STEP):
        ch, cl = _split16(cp_ref[g])
        bh, bl = _split16(bp_ref[g])
        t = d(ch, bh) + (d(ch, bl) + d(cl, bh))
        t_ref[g] = jnp.where(causal, t, 0.0).astype(bf16)


def _s5mat_call(cp, bp):
    n = S5MAT_GROUPS_PER_STEP
    return pl.pallas_call(
        _s5mat_body,
        grid=(S5_GROUPS // n,),
        in_specs=[
            pl.BlockSpec((n, S5_GL, 2 * S5_STATE), lambda g: (g, 0, 0)),
            pl.BlockSpec((n, S5_GL, 2 * S5_STATE), lambda g: (g, 0, 0)),
        ],
        out_specs=pl.BlockSpec((n, S5_GL, S5_GL), lambda g: (g, 0, 0)),
        out_shape=jax.ShapeDtypeStruct((S5_GROUPS, S5_GL, S5_GL), bf16),
        compiler_params=pltpu.CompilerParams(dimension_semantics=("arbitrary",)),
        name="s5mat",
    )(cp, bp)


def _s5_body(u_ref, z_ref, xr0_ref, xi0_ref, t_ref, wp_ref, vt_ref, ar_ref, ai_ref, d_ref,
             gw_ref, gb_ref,
             o_ref, xr_ref, xi_ref,
             at_scr, er_scr, ei_scr, xinr_scr, xini_scr, yt_scr, y_scr, *, tb, n_streams, n_chunks):
    R = tb // S5_L
    used = n_streams * n_chunks

    @pl.when(pl.program_id(1) == 0)
    def _init():
        xr_ref[...] = xr0_ref[...]
        xi_ref[...] = xi0_ref[...]

    for f in range(S5_L):
        for j in range(S5_PLANES):
            ut = u_ref[0, j, pl.ds(f, R, stride=S5_L), :]
            ut_t = ut.astype(bf16).T
            for gl in range(LANES // S5_GROUP):
                g = j * (LANES // S5_GROUP) + gl
                at_scr[g, f * S5_GROUP:(f + 1) * S5_GROUP, :] = ut_t[gl * S5_GROUP:(gl + 1) * S5_GROUP, :]

    for p in range(S5_PAIRS):
        a_t = at_scr[2 * p:2 * p + 2].reshape(2 * S5_GL, R)
        e = lax.dot_general(a_t, wp_ref[p], _TN, preferred_element_type=f32)
        er_scr[:, p * LANES:(p + 1) * LANES] = e[:, :LANES]
        ei_scr[:, p * LANES:(p + 1) * LANES] = e[:, LANES:]

    if used < R:
        xinr_scr[...] = jnp.zeros_like(xinr_scr)
        xini_scr[...] = jnp.zeros_like(xini_scr)

    a_r = ar_ref[...]
    a_i = ai_ref[...]
    for s in range(n_streams):
        def step(c, carry):
            xr, xi = carry
            row = s * n_chunks + c
            xinr_scr[pl.ds(row, 1), :] = xr
            xini_scr[pl.ds(row, 1), :] = xi
            er = er_scr[pl.ds(row, 1), :]
            ei = ei_scr[pl.ds(row, 1), :]
            return a_r * xr - a_i * xi + er, a_r * xi + a_i * xr + ei

        xr, xi = lax.fori_loop(0, n_chunks, step, (xr_ref[0, s:s + 1, :], xi_ref[0, s:s + 1, :]),
                               unroll=min(n_chunks, 4))
        xr_ref[0, s:s + 1, :] = xr
        xi_ref[0, s:s + 1, :] = xi

    for g in range(S5_GROUPS):
        p = g // 2
        xin = jnp.concatenate([xinr_scr[:, p * LANES:(p + 1) * LANES],
                               xini_scr[:, p * LANES:(p + 1) * LANES]], axis=1).astype(bf16)
        yt_scr[g] = (jnp.dot(t_ref[g], at_scr[g], preferred_element_type=f32)
                     + lax.dot_general(vt_ref[g], xin, _NT, preferred_element_type=f32))

    for f in range(S5_L):
        gpp = LANES // S5_GROUP
        for j in range(S5_PLANES):
            yt = yt_scr[j * gpp:(j + 1) * gpp, f * S5_GROUP:(f + 1) * S5_GROUP, :].reshape(LANES, R)
            y_scr[j, pl.ds(f, R, stride=S5_L), :] = yt.T

    for rb in range(tb // OUT_SUB):
        sl = pl.ds(rb * OUT_SUB, OUT_SUB)
        y_intra = jnp.concatenate([y_scr[j, sl, :] for j in range(S5_PLANES)], axis=1)
        u_nat = jnp.concatenate([u_ref[0, j, sl, :] for j in range(S5_PLANES)], axis=1)
        y = y_intra + d_ref[...] * u_nat
        gy = _gelu_tanh(y)
        gate = _sigmoid(_bdot(gy, gw_ref[...]) + gb_ref[...])
        o_ref[0, sl, :] = (gy * gate * _silu(z_ref[0, sl, :])).astype(bf16)


def _s5_call(u, z, xr0, xi0, tmat, wp, vt, ar, ai, dvec, gw, gb, *, tb, n_streams, n_chunks):
    G, T, _ = z.shape
    grid = (G, T // tb)
    R = tb // S5_L
    nstate = S5_GROUPS * S5_STATE
    c2 = lambda g, t: (0, 0)
    c3 = lambda g, t: (0, 0, 0)
    body = functools.partial(_s5_body, tb=tb, n_streams=n_streams, n_chunks=n_chunks)
    one = pl.Buffered(1)
    return pl.pallas_call(
        body,
        grid=grid,
        in_specs=[
            pl.BlockSpec((1, S5_PLANES, tb, LANES), lambda g, t: (g, 0, t, 0)),
            pl.BlockSpec((1, tb, S5_WIDTH), lambda g, t: (g, t, 0)),
            pl.BlockSpec((1, n_streams, nstate), lambda g, t: (g, 0, 0)),
            pl.BlockSpec((1, n_streams, nstate), lambda g, t: (g, 0, 0)),
            pl.BlockSpec((S5_GROUPS, S5_GL, S5_GL), c3, pipeline_mode=one),
            pl.BlockSpec((S5_PAIRS, 2 * S5_GL, 2 * LANES), c3, pipeline_mode=one),
            pl.BlockSpec((S5_GROUPS, S5_GL, 2 * LANES), c3, pipeline_mode=one),
            pl.BlockSpec((1, nstate), c2),
            pl.BlockSpec((1, nstate), c2),
            pl.BlockSpec((1, S5_WIDTH), c2),
            pl.BlockSpec((S5_WIDTH, S5_WIDTH), c2),
            pl.BlockSpec((1, S5_WIDTH), c2),
        ],
        out_specs=[
            pl.BlockSpec((1, tb, S5_WIDTH), lambda g, t: (g, t, 0)),
            pl.BlockSpec((1, n_streams, nstate), lambda g, t: (g, 0, 0)),
            pl.BlockSpec((1, n_streams, nstate), lambda g, t: (g, 0, 0)),
        ],
        out_shape=[
            jax.ShapeDtypeStruct((G, T, S5_WIDTH), bf16),
            jax.ShapeDtypeStruct((G, n_streams, nstate), f32),
            jax.ShapeDtypeStruct((G, n_streams, nstate), f32),
        ],
        scratch_shapes=[
            pltpu.VMEM((S5_GROUPS, S5_GL, R), bf16),
            pltpu.VMEM((R, nstate), f32),
            pltpu.VMEM((R, nstate), f32),
            pltpu.VMEM((R, nstate), f32),
            pltpu.VMEM((R, nstate), f32),
            pltpu.VMEM((S5_GROUPS, S5_GL, R), f32),
            pltpu.VMEM((S5_PLANES, tb, LANES), f32),
        ],
        compiler_params=pltpu.CompilerParams(
            dimension_semantics=("arbitrary", "arbitrary"), vmem_limit_bytes=VMEM_LIMIT),
        name="s5",
    )(u, z, xr0, xi0, tmat, wp, vt, ar, ai, dvec, gw, gb)


def _out_body(x_ref, odn_ref, os5_ref, wo_ref, fw_ref, y_ref):
    acc = (x_ref[...]
           + jnp.dot(odn_ref[...], wo_ref[0:DN_WIDTH, :], preferred_element_type=f32)
           + jnp.dot(os5_ref[...], wo_ref[DN_WIDTH:, :], preferred_element_type=f32))
    ms = jnp.mean(acc * acc, axis=-1, keepdims=True)
    y_ref[...] = acc * lax.rsqrt(ms + EPS) * fw_ref[...]


def _out_call(x2, odn2, os52, w_out16, fw, *, rows):
    n = x2.shape[0]
    return pl.pallas_call(
        _out_body,
        grid=(n // rows,),
        in_specs=[
            pl.BlockSpec((rows, D_MODEL), lambda i: (i, 0)),
            pl.BlockSpec((rows, DN_WIDTH), lambda i: (i, 0)),
            pl.BlockSpec((rows, S5_WIDTH), lambda i: (i, 0)),
            pl.BlockSpec((D_MODEL, D_MODEL), lambda i: (0, 0)),
            pl.BlockSpec((1, D_MODEL), lambda i: (0, 0)),
        ],
        out_specs=pl.BlockSpec((rows, D_MODEL), lambda i: (i, 0)),
        out_shape=jax.ShapeDtypeStruct((n, D_MODEL), f32),
        compiler_params=pltpu.CompilerParams(
            dimension_semantics=("arbitrary",), vmem_limit_bytes=VMEM_LIMIT),
        name="outproj",
    )(x2, odn2, os52, w_out16, fw)


def _s5_operands(a_re, a_im, log_dt, b_re, b_im, c_re, c_im):
    lam_re = jnp.minimum(a_re, -1e-4)
    lam_im = a_im
    dt = jnp.exp(log_dt)[:, None]
    ldt_re, ldt_im = lam_re * dt, lam_im * dt

    def lpow(k):
        mag = jnp.exp(ldt_re * k)
        return mag * jnp.cos(ldt_im * k), mag * jnp.sin(ldt_im * k)

    lb_re, lb_im = lpow(1.0)
    den = lam_re * lam_re + lam_im * lam_im
    f_re = ((lb_re - 1.0) * lam_re + lb_im * lam_im) / den
    f_im = (lb_im * lam_re - (lb_re - 1.0) * lam_im) / den
    bb_re = f_re[..., None] * b_re - f_im[..., None] * b_im
    bb_im = f_re[..., None] * b_im + f_im[..., None] * b_re

    fr = jnp.arange(S5_L, dtype=f32)
    G, N = S5_GROUPS, S5_STATE

    def cmul(ar, ai, br, bi):
        return ar * br - ai * bi, ar * bi + ai * br

    pr, pi = lpow(fr[:, None, None])
    pr, pi = pr.transpose(1, 0, 2), pi.transpose(1, 0, 2)
    cpr, cpi = cmul(c_re[:, None], c_im[:, None], pr[:, :, None], pi[:, :, None])
    cpr, cpi = cpr.reshape(G, S5_GL, N), cpi.reshape(G, S5_GL, N)
    bbr_t, bbi_t = bb_re.transpose(0, 2, 1), bb_im.transpose(0, 2, 1)
    qr, qi = lpow(-fr[:, None, None])
    qr, qi = qr.transpose(1, 0, 2), qi.transpose(1, 0, 2)
    bpr, bpi = cmul(qr[:, :, None], qi[:, :, None], bbr_t[:, None], bbi_t[:, None])
    bpr, bpi = bpr.reshape(G, S5_GL, N), bpi.reshape(G, S5_GL, N)

    wr_, wi_ = lpow((S5_L - 1.0) - fr[:, None, None])
    wr_, wi_ = wr_.transpose(1, 0, 2), wi_.transpose(1, 0, 2)
    w_re, w_im = cmul(wr_[:, :, None], wi_[:, :, None], bbr_t[:, None], bbi_t[:, None])
    w_re, w_im = w_re.reshape(G, S5_GL, N), w_im.reshape(G, S5_GL, N)
    vr_, vi_ = lpow(fr[:, None, None] + 1.0)
    vr_, vi_ = vr_.transpose(1, 0, 2), vi_.transpose(1, 0, 2)
    v_re, v_im = cmul(c_re[:, None], c_im[:, None], vr_[:, :, None], vi_[:, :, None])
    v_re, v_im = v_re.reshape(G, S5_GL, N), v_im.reshape(G, S5_GL, N)

    zeros = jnp.zeros((S5_PAIRS, S5_GL, N), f32)

    def pair_cols(re, im):
        re, im = re.reshape(S5_PAIRS, 2, S5_GL, N), im.reshape(S5_PAIRS, 2, S5_GL, N)
        even = jnp.concatenate([re[:, 0], zeros, im[:, 0], zeros], axis=-1)
        odd = jnp.concatenate([zeros, re[:, 1], zeros, im[:, 1]], axis=-1)
        return jnp.stack([even, odd], axis=1)

    wp = pair_cols(w_re, w_im).reshape(S5_PAIRS, 2 * S5_GL, 2 * LANES).astype(bf16)
    vt = pair_cols(v_re, -v_im).reshape(S5_GROUPS, S5_GL, 2 * LANES).astype(bf16)
    a16r, a16i = lpow(float(S5_L))
    cp = jnp.concatenate([cpr, cpi], axis=2)
    bp = jnp.concatenate([bpr, -bpi], axis=2)
    return cp, bp, wp, vt, a16r.reshape(1, G * N), a16i.reshape(1, G * N)


def _pad_lanes(v, start=0):
    out = jnp.zeros((1, LANES), f32)
    return out.at[0, start:start + v.shape[0]].set(v.astype(f32))


def _layer(x, conv0, s0, xr0, xi0, prm, *, delta_cfg, s5_cfg):
    B, T, _ = x.shape
    conv0p = jnp.concatenate([jnp.zeros((B, SUBLANES - (CONV_K - 1), CONV_CH), f32), conv0], axis=1)
    odn, us5, zs5, convout, s_new = _delta_call(
        x, prm["w1"], prm["w2"], prm["norm_w"], prm["conv_w"], prm["alog"], prm["dtb"], prm["dnw"],
        conv0p, s0, **delta_cfg)

    nstate = S5_GROUPS * S5_STATE
    if s5_cfg["flatten"]:
        tb = s5_cfg["tb"]
        uf = us5.transpose(1, 0, 2, 3).reshape(1, S5_PLANES, B * T, LANES)
        zf = zs5.reshape(1, B * T, S5_WIDTH)
        pad = tb - B * T
        uf = jnp.pad(uf, ((0, 0), (0, 0), (0, pad), (0, 0)))
        zf = jnp.pad(zf, ((0, 0), (0, pad), (0, 0)))
        os5, xr, xi = _s5_call(uf, zf, xr0.reshape(1, B, nstate), xi0.reshape(1, B, nstate),
                               prm["tmat"], prm["wp"], prm["vt"], prm["ar"], prm["ai"], prm["dvec"],
                               prm["gw"], prm["gb"], tb=tb, n_streams=B, n_chunks=T // S5_L)
        os5 = os5[0, :B * T].reshape(B, T, S5_WIDTH)
    else:
        tb = s5_cfg["tb"]
        os5, xr, xi = _s5_call(us5, zs5, xr0.reshape(B, 1, nstate), xi0.reshape(B, 1, nstate),
                               prm["tmat"], prm["wp"], prm["vt"], prm["ar"], prm["ai"], prm["dvec"],
                               prm["gw"], prm["gb"], tb=tb, n_streams=1, n_chunks=tb // S5_L)

    y = _out_call(x.reshape(B * T, D_MODEL), odn.reshape(B * T, DN_WIDTH), os5.reshape(B * T, S5_WIDTH),
                  prm["w_out"], prm["fw"], rows=min(OUT_ROWS, B * T))
    return (y.reshape(B, T, D_MODEL), convout[:, SUBLANES - (CONV_K - 1):, :], s_new,
            xr.reshape(B, S5_GROUPS, S5_STATE), xi.reshape(B, S5_GROUPS, S5_STATE))


def kernel(x_prompt, x_sample, cache_conv, state_dn, state_s5_re, state_s5_im, norm_w, w_in, conv_w, dn_A_log, dn_dt_bias, dn_norm_w, s5_A_re, s5_A_im, s5_log_dt, s5_B_re, s5_B_im, s5_C_re, s5_C_im, s5_D, glu_w, glu_b, w_out, final_norm_w):
    depth = norm_w.shape[0]
    assert depth == 1
    l = 0
    assert w_in.shape[-1] == IN_COLS
    w1 = w_in[l, :, :CONV_CH].astype(bf16)
    w2 = w_in[l, :, IN_COLS - W2_COLS:].astype(bf16)
    cp, bp, wp, vt, ar, ai = _s5_operands(
        s5_A_re[l].astype(f32), s5_A_im[l].astype(f32), s5_log_dt[l].astype(f32),
        s5_B_re[l].astype(f32), s5_B_im[l].astype(f32), s5_C_re[l].astype(f32), s5_C_im[l].astype(f32))
    prm = dict(
        w1=w1, w2=w2,
        norm_w=norm_w[l].reshape(1, D_MODEL).astype(f32),
        conv_w=conv_w[l].astype(f32),
        alog=_pad_lanes(dn_A_log[l], AB_LANE),
        dtb=_pad_lanes(dn_dt_bias[l], AB_LANE),
        dnw=dn_norm_w[l].reshape(1, DN_D).astype(f32),
        tmat=_s5mat_call(cp, bp),
        wp=wp, vt=vt, ar=ar, ai=ai,
        dvec=s5_D[l].reshape(1, S5_WIDTH).astype(f32),
        gw=glu_w[l].astype(bf16),
        gb=glu_b[l].reshape(1, S5_WIDTH).astype(f32),
        w_out=w_out[l].astype(bf16),
        fw=final_norm_w.reshape(1, D_MODEL).astype(f32),
    )

    bp = x_prompt.shape[0]
    yp, c1, d1, r1, i1 = _layer(
        x_prompt,
        jnp.zeros((bp, CONV_K - 1, CONV_CH), f32),
        jnp.zeros((bp, DN_HEADS, DN_D, DN_D), f32),
        jnp.zeros((bp, S5_GROUPS, S5_STATE), f32),
        jnp.zeros((bp, S5_GROUPS, S5_STATE), f32),
        prm,
        delta_cfg=dict(tb=256, pipelined=True),
        s5_cfg=dict(flatten=False, tb=2048))
    ys, c2, d2, r2, i2 = _layer(
        x_sample, cache_conv[l].astype(f32), state_dn[l].astype(f32),
        state_s5_re[l].astype(f32), state_s5_im[l].astype(f32),
        prm,
        delta_cfg=dict(tb=32, pipelined=False),
        s5_cfg=dict(flatten=True, tb=512))

    return (yp, ys, c1[None], d1[None], r1[None], i1[None], c2[None], d2[None], r2[None], i2[None])
```

```python
import functools
import math

import jax
import jax.numpy as jnp
import numpy as np
from jax import lax
from jax.experimental import pallas as pl
from jax.experimental.pallas import tpu as pltpu

bf16 = jnp.bfloat16
f32 = jnp.float32

LANES = 128
SUBLANES = 8

D_MODEL = 1024
DN_HEADS = 4
DN_D = 128
DN_WIDTH = DN_HEADS * DN_D
CONV_K = 4
CONV_CH = 3 * DN_WIDTH
S5_WIDTH = 512
S5_GROUP = 16
S5_GROUPS = 32
S5_STATE = 64
S5_PAIRS = S5_GROUPS // 2
S5_L = 16
S5_GL = S5_GROUP * S5_L
OUT_SUB = 256
OUT_ROWS = 1024
S5_PLANES = S5_WIDTH // LANES
EPS = 1e-6
L2_EPS = 1e-6

IN_COLS = CONV_CH + 2 * DN_HEADS + DN_WIDTH + 2 * S5_WIDTH
W2_COLS = LANES + DN_WIDTH + 2 * S5_WIDTH
AB_LANE = LANES - 2 * DN_HEADS
W2_ZDN = LANES
W2_US5 = W2_ZDN + DN_WIDTH
W2_ZS5 = W2_US5 + S5_WIDTH

GROUP = 256
MASK_INCL, MASK_STRICT, MASK_OFF = 0, 1, 2
FOLD_MIN_ROWS = 64
VMEM_LIMIT =56 * 1024 * 1024

_NT = (((1,), (1,)), ((), ()))
_TN = (((0,), (0,)), ((), ()))


def _bdot(a, b):
    return jnp.dot(a.astype(bf16), b.astype(bf16), preferred_element_type=f32)


def _bdot_g(a, b, dims):
    return lax.dot_general(a.astype(bf16), b.astype(bf16), dims, preferred_element_type=f32)


def _sigmoid(x):
    return 1.0 / (1.0 + jnp.exp2(x * (-math.log2(math.e))))


def _silu(x):
    return x * _sigmoid(x)


def _softplus(x):
    return jnp.maximum(x, 0.0) + jnp.log1p(jnp.exp(-jnp.abs(x)))


def _gelu_tanh(x):
    c = math.sqrt(2.0 / math.pi)
    return x * (0.5 + 0.5 * jnp.tanh(x * (c + (c * 0.044715) * (x * x))))


def _zero_after(x):
    bits = pltpu.bitcast(x[0:SUBLANES, 0:LANES].astype(f32), jnp.uint32)
    half = jnp.uint32(16)
    bits = lax.shift_right_logical(lax.shift_right_logical(bits, half), half)
    return pltpu.bitcast(bits, f32)[0:1, :]


def _unit_lower_solve(nmats, rhss, eye, m_ref, n_levels, fillers):
    d32 = functools.partial(jnp.dot, preferred_element_type=f32)

    def d16(a, b):
        return d32(a, b).astype(bf16)

    def fill(after):
        if fillers:
            fillers.pop(0)(_zero_after(after[-1]))

    def fold(x, rows):
        out = x[0:rows]
        for k in range(1, GROUP // rows):
            out = out + x[k * rows:(k + 1) * rows]
        return out

    def unfold(xf, rows):
        lane_blk = lax.broadcasted_iota(jnp.int32, (rows, GROUP), 1) >> (rows.bit_length() - 1)
        return jnp.concatenate([jnp.where(lane_blk == k, xf, jnp.zeros_like(xf))
                                for k in range(GROUP // rows)], axis=0)

    n16 = [n.astype(bf16) for n in nmats]
    invs = [(eye - n * m_ref[MASK_OFF]).astype(bf16) for n in nmats]
    substitute_top = 2 ** n_levels == GROUP
    for lvl in range(1, n_levels - 1 if substitute_top else n_levels):
        rows = min(GROUP, max(FOLD_MIN_ROWS, 2 ** (lvl + 1)))
        mask16 = m_ref[MASK_OFF + lvl].astype(bf16)
        lhs = invs if rows == GROUP else [fold(inv, rows) for inv in invs]
        t1 = [d16(l, n * mask16) for l, n in zip(lhs, n16)]
        fill(invs)
        t2 = [d16(t, inv) for t, inv in zip(t1, invs)]
        fill(t1)
        if rows != GROUP:
            t2 = [unfold(t, rows) for t in t2]
        invs = [inv - t for inv, t in zip(invs, t2)]
    if not substitute_top:
        return [d32(inv, r.astype(bf16)) for inv, r in zip(invs, rhss)]
    h = GROUP // 2
    x1 = [d32(inv[0:h, 0:h], r[0:h].astype(bf16)) for inv, r in zip(invs, rhss)]
    fill(invs)
    bx = [d32(n[h:, 0:h], x.astype(bf16)) for n, x in zip(n16, x1)]
    fill(x1)
    x2 = [d32(inv[h:, h:], (r[h:] - t).astype(bf16)) for inv, r, t in zip(invs, rhss, bx)]
    return [jnp.concatenate([a, b], axis=0) for a, b in zip(x1, x2)]


def _segment_cumsum(tri16, g):
    g1 = g.astype(bf16)
    r1 = g - g1.astype(f32)
    g2 = r1.astype(bf16)
    g3 = (r1 - g2.astype(f32)).astype(bf16)
    s = jnp.dot(tri16, jnp.concatenate([g1, g2, g3], axis=1), preferred_element_type=f32)
    n = g.shape[1]
    return s[:, :n] + (s[:, n:2 * n] + s[:, 2 * n:])


def _delta_body(x_ref, w1_ref, w2_ref, nw_ref, cw_ref, alog_ref, dtb_ref, dnw_ref, m_ref, conv0_ref, s0_ref,
                odn_ref, us5_ref, zs5_ref, convout_ref, s_ref,
                cbuf, pbuf, qn_scr, kn_scr, v_scr, gc_scr, beta_scr, zdn_scr,
                *, nb, tb, n_t, pipelined):
    rows = nb * tb
    n_groups = rows // GROUP
    segs = GROUP // tb
    n_levels = tb.bit_length() - 1
    step = pl.program_id(0)
    fresh = step < n_t

    @pl.when(step == 0)
    def _init():
        cbuf[:, 0:SUBLANES, :] = conv0_ref[...]
        for b in range(nb):
            c8 = conv0_ref[b]
            pbuf[b, 0:SUBLANES, :] = c8 * cw_ref[1:2, :] + pltpu.roll(c8, 1, axis=0) * cw_ref[0:1, :]
        s_ref[...] = s0_ref[...]
        if pipelined:
            for scr in (qn_scr, kn_scr, v_scr, gc_scr, beta_scr, zdn_scr):
                scr[...] = jnp.zeros_like(scr)

    front = {}

    def project(w_ref, lo, width):
        return jnp.dot(front["h"], w_ref[:, lo:lo + width], preferred_element_type=f32)

    def a_norm_in():
        x = x_ref[...].reshape(rows, D_MODEL)
        ms = jnp.mean(x * x, axis=-1, keepdims=True)
        front["h"] = (x * lax.rsqrt(ms + EPS) * nw_ref[...]).astype(bf16)

    def a_project_qkv(s):
        front[("raw", s)] = project(w1_ref, s * DN_WIDTH, DN_WIDTH)

    def a_project_us5():
        u = project(w2_ref, W2_US5, S5_WIDTH)
        for j in range(S5_PLANES):
            us5_ref[:, j] = u[:, j * LANES:(j + 1) * LANES].reshape(nb, tb, LANES)

    def a_project_zs5():
        zs5_ref[...] = project(w2_ref, W2_ZS5, S5_WIDTH).reshape(nb, tb, S5_WIDTH)

    def a_project_zdn():
        front["zdn"] = project(w2_ref, W2_ZDN, DN_WIDTH)

    def a_project_gates():
        ab = project(w2_ref, 0, LANES)
        front["g"] = -jnp.exp(alog_ref[...]) * _softplus(ab + dtb_ref[...])
        front["beta"] = _sigmoid(ab)

    projections = ([functools.partial(a_project_qkv, s) for s in range(3)]
                   + [a_project_us5, a_project_zs5, a_project_zdn, a_project_gates])

    def a_conv(b, s, zero):
        cols = slice(s * DN_WIDTH, (s + 1) * DN_WIDTH)
        u0 = front[("raw", s)][b * tb:(b + 1) * tb, :]
        cbuf[b, SUBLANES:SUBLANES + tb, cols] = u0
        taps = [cw_ref[j:j + 1, cols] for j in range(CONV_K)]
        if zero is not None:
            zero_w = jnp.concatenate([zero] * (DN_WIDTH // LANES), axis=1)
            taps = [t + zero_w for t in taps]
        u1 = cbuf[b, SUBLANES - 1:SUBLANES - 1 + tb, cols]
        pbuf[b, SUBLANES:SUBLANES + tb, cols] = u0 * taps[1] + u1 * taps[0]
        acc = (u0 * taps[3] + u1 * taps[2]) + pbuf[b, SUBLANES - 2:SUBLANES - 2 + tb, cols]
        front[("act", b, s)] = _silu(acc)
        for buf in (cbuf, pbuf):
            tail = buf[b, tb:tb + SUBLANES, cols]
            if pipelined:
                tail = jnp.where(fresh, tail, buf[b, 0:SUBLANES, cols])
            buf[b, 0:SUBLANES, cols] = tail
        convout_ref[b, :, cols] = cbuf[b, 0:SUBLANES, cols]

    def a_norm(b, hd, zero):
        lo = hd * DN_D
        q = front[("act", b, 0)][:, lo:lo + DN_D]
        k = front[("act", b, 1)][:, lo:lo + DN_D]
        if zero is not None:
            q, k = q + zero, k + zero
        front[("qn", b, hd)] = (q * lax.rsqrt(jnp.sum(q * q, axis=-1, keepdims=True) + L2_EPS)
                                * (DN_D ** -0.5))
        front[("kn", b, hd)] = k * lax.rsqrt(jnp.sum(k * k, axis=-1, keepdims=True) + L2_EPS)

    def a_gates(zero):
        tri16 = m_ref[MASK_INCL].astype(bf16)
        g = front["g"] if zero is None else front["g"] + zero
        front["gc"] = [_segment_cumsum(tri16, g[gi * GROUP:(gi + 1) * GROUP])
                       for gi in range(n_groups)]

    def a_store():
        for b in range(nb):
            r = slice(b * tb, (b + 1) * tb)
            for hd in range(DN_HEADS):
                c = slice(hd * DN_D, (hd + 1) * DN_D)
                qn_scr[r, c] = front[("qn", b, hd)]
                kn_scr[r, c] = front[("kn", b, hd)]
            v_scr[r, :] = front[("act", b, 2)]
        for gi in range(n_groups):
            gc_scr[gi * GROUP:(gi + 1) * GROUP, :] = front["gc"][gi]
        beta_scr[...] = front["beta"]
        zdn_scr[...] = front["zdn"]

    fillers = ([functools.partial(a_conv, b, s) for b in range(nb) for s in range(3)] + [a_gates]
               + [functools.partial(a_norm, b, hd) for b in range(nb) for hd in range(DN_HEADS)])

    incl = m_ref[MASK_INCL]
    strict = m_ref[MASK_STRICT]
    eye = incl - strict
    dnw = dnw_ref[...]
    chains = [(gi, hd) for gi in range(n_groups) for hd in range(DN_HEADS)]
    back = {}

    def stream_of(gi, sg):
        return (gi * GROUP) // tb + sg

    for name in ("q16", "qe16", "k16", "kb16", "rhs", "dmat", "kdec", "sdecay", "nmat", "attn"):
        back[name] = [None] * len(chains)

    def b_load(ci):
        gi, hd = chains[ci]
        r = slice(gi * GROUP, (gi + 1) * GROUP)
        c = slice(hd * DN_D, (hd + 1) * DN_D)
        q, k, v = qn_scr[r, c], kn_scr[r, c], v_scr[r, c]
        beta = beta_scr[r, AB_LANE + DN_HEADS + hd:AB_LANE + DN_HEADS + hd + 1]
        g = gc_scr[r, AB_LANE + hd:AB_LANE + hd + 1]
        eg = jnp.exp(g)
        kb = k * beta
        gcb = jnp.broadcast_to(g * math.log2(math.e), (GROUP, GROUP))
        back["dmat"][ci] = jnp.exp2(jnp.minimum(gcb - gcb.T, 0.0)) * incl
        back["q16"][ci] = q.astype(bf16)
        back["qe16"][ci] = (q * eg).astype(bf16)
        back["k16"][ci] = k.astype(bf16)
        back["kb16"][ci] = kb.astype(bf16)
        back["rhs"][ci] = jnp.concatenate([v * beta, kb * eg], axis=1)
        kd, sd = [], []
        for sg in range(segs):
            a0 = sg * tb
            glast = g[a0 + tb - 1:a0 + tb, :]
            kd.append((k[a0:a0 + tb] * jnp.exp(glast - g[a0:a0 + tb])).astype(bf16))
            sd.append(jnp.exp(glast))
        back["kdec"][ci] = kd
        back["sdecay"][ci] = sd

    def b_mats(ci):
        d = functools.partial(lax.dot_general, dimension_numbers=_NT, preferred_element_type=f32)
        dm = back["dmat"][ci]
        back["nmat"][ci] = d(back["kb16"][ci], back["k16"][ci]) * (dm * strict)
        back["attn"][ci] = (d(back["q16"][ci], back["k16"][ci]) * dm).astype(bf16)

    def b_state(sols):
        s_old = {(b, hd): s_ref[b, hd] for b in range(nb) for hd in range(DN_HEADS)}
        s16 = {key: val.astype(bf16) for key, val in s_old.items()}
        v_news, o_states = [], []
        for (gi, hd), sol, qe16 in zip(chains, sols, back["qe16"]):
            u = sol[:, :DN_D]
            w16 = sol[:, DN_D:].astype(bf16)
            vn, os_ = [], []
            for sg in range(segs):
                a0 = sg * tb
                st = s16[(stream_of(gi, sg), hd)]
                vn.append(u[a0:a0 + tb] - jnp.dot(w16[a0:a0 + tb], st, preferred_element_type=f32))
                os_.append(jnp.dot(qe16[a0:a0 + tb], st, preferred_element_type=f32))
            v_news.append(vn[0] if segs == 1 else jnp.concatenate(vn, axis=0))
            o_states.append(os_[0] if segs == 1 else jnp.concatenate(os_, axis=0))
        vn16s = [vn.astype(bf16) for vn in v_news]
        outs = [os_ + jnp.dot(a, vn, preferred_element_type=f32)
                for os_, a, vn in zip(o_states, back["attn"], vn16s)]

        s_new = {}
        for (gi, hd), kd, sd, vn in zip(chains, back["kdec"], back["sdecay"], vn16s):
            for sg in range(segs):
                a0 = sg * tb
                key = (stream_of(gi, sg), hd)
                s_new[key] = (s_old[key] * sd[sg]
                              + lax.dot_general(kd[sg], vn[a0:a0 + tb], _TN, preferred_element_type=f32))

        o_rows = []
        for gi in range(n_groups):
            o_heads = []
            for hd in range(DN_HEADS):
                o = outs[gi * DN_HEADS + hd]
                on = o * lax.rsqrt(jnp.mean(o * o, axis=-1, keepdims=True) + EPS) * dnw
                zd = zdn_scr[gi * GROUP:(gi + 1) * GROUP, hd * DN_D:(hd + 1) * DN_D]
                o_heads.append(on * _silu(zd))
            o_rows.append(jnp.concatenate(o_heads, axis=1))
        o_all = o_rows[0] if n_groups == 1 else jnp.concatenate(o_rows, axis=0)
        odn_ref[...] = o_all.astype(bf16).reshape(nb, tb, DN_WIDTH)
        s_ref[...] = jnp.stack([jnp.stack([s_new[(b, hd)] for hd in range(DN_HEADS)]) for b in range(nb)])

    n_ch = len(chains)
    if pipelined:
        a_norm_in()
        for ci in range(n_ch):
            b_load(ci)
        back_setup = [functools.partial(b_mats, ci) for ci in range(n_ch)]
        convs = [fillers.pop(0) for _ in range(nb * 3)]
        per_piece = -(-len(back_setup) // len(projections))
        for idx, piece in enumerate(projections):
            piece()
            for thunk in back_setup[:per_piece]:
                thunk()
            del back_setup[:per_piece]
            if idx >= 3 and convs:
                convs.pop(0)(None)
        while back_setup or convs:
            if back_setup:
                back_setup.pop(0)()
            if convs:
                convs.pop(0)(None)
        sols = _unit_lower_solve(back["nmat"], back["rhs"], eye, m_ref, n_levels, fillers)
        for fill in fillers:
            fill(None)
        b_state(sols)
        a_store()
    else:
        a_norm_in()
        for piece in projections:
            piece()
        for fill in fillers:
            fill(None)
        a_store()
        for ci in range(n_ch):
            b_load(ci)
        for ci in range(n_ch):
            b_mats(ci)
        b_state(_unit_lower_solve(back["nmat"], back["rhs"], eye, m_ref, n_levels, []))


def _delta_masks(tb):
    r = np.arange(GROUP)[:, None]
    c = np.arange(GROUP)[None, :]
    same = (r // tb) == (c // tb)
    ms = [same & (r >= c), same & (r > c)]
    for l in range(tb.bit_length() - 1):
        ms.append(((r >> (l + 1)) == (c >> (l + 1))) & (((r >> l) & 1) == 1) & (((c >> l) & 1) == 0))
    return jnp.asarray(np.stack(ms).astype(np.float32))


def _delta_call(x, w1, w2, norm_w, conv_w, alog, dtb, dnw, conv0, s0, *, tb, pipelined):
    nb, T, _ = x.shape
    assert GROUP % tb == 0 and (nb * tb) % GROUP == 0 and T % tb == 0
    n_t = T // tb
    rows = nb * tb
    if pipelined:
        grid = (n_t + 1,)
        cur = lambda i: jnp.minimum(i, n_t - 1)
        prev = lambda i: jnp.maximum(i - 1, 0)
    else:
        grid = (n_t,)
        cur = prev = lambda i: i
    const2 = lambda i: (0, 0)
    const3 = lambda i: (0, 0, 0)
    const4 = lambda i: (0, 0, 0, 0)
    body = functools.partial(_delta_body, nb=nb, tb=tb, n_t=n_t, pipelined=pipelined)
    masks = _delta_masks(tb)
    one = pl.Buffered(1)
    return pl.pallas_call(
        body,
        grid=grid,
        in_specs=[
            pl.BlockSpec((nb, tb, D_MODEL), lambda i: (0, cur(i), 0)),
            pl.BlockSpec((D_MODEL, CONV_CH), const2, pipeline_mode=one),
            pl.BlockSpec((D_MODEL, W2_COLS), const2, pipeline_mode=one),
            pl.BlockSpec((1, D_MODEL), const2),
            pl.BlockSpec((CONV_K, CONV_CH), const2),
            pl.BlockSpec((1, LANES), const2),
            pl.BlockSpec((1, LANES), const2),
            pl.BlockSpec((1, DN_D), const2),
            pl.BlockSpec(masks.shape, const3, pipeline_mode=one),
            pl.BlockSpec((nb, SUBLANES, CONV_CH), const3),
            pl.BlockSpec((nb, DN_HEADS, DN_D, DN_D), const4),
        ],
        out_specs=[
            pl.BlockSpec((nb, tb, DN_WIDTH), lambda i: (0, prev(i), 0)),
            pl.BlockSpec((nb, S5_PLANES, tb, LANES), lambda i: (0, 0, cur(i), 0)),
            pl.BlockSpec((nb, tb, S5_WIDTH), lambda i: (0, cur(i), 0)),
            pl.BlockSpec((nb, SUBLANES, CONV_CH), const3),
            pl.BlockSpec((nb, DN_HEADS, DN_D, DN_D), const4),
        ],
        out_shape=[
            jax.ShapeDtypeStruct((nb, T, DN_WIDTH), bf16),
            jax.ShapeDtypeStruct((nb, S5_PLANES, T, LANES), f32),
            jax.ShapeDtypeStruct((nb, T, S5_WIDTH), f32),
            jax.ShapeDtypeStruct((nb, SUBLANES, CONV_CH), f32),
            jax.ShapeDtypeStruct((nb, DN_HEADS, DN_D, DN_D), f32),
        ],
        scratch_shapes=[
            pltpu.VMEM((nb, tb + SUBLANES, CONV_CH), f32),
            pltpu.VMEM((nb, tb + SUBLANES, CONV_CH), f32),
            pltpu.VMEM((rows, DN_WIDTH), f32),
            pltpu.VMEM((rows, DN_WIDTH), f32),
            pltpu.VMEM((rows, DN_WIDTH), f32),
            pltpu.VMEM((rows, LANES), f32),
            pltpu.VMEM((rows, LANES), f32),
            pltpu.VMEM((rows, DN_WIDTH), f32),
        ],
        compiler_params=pltpu.CompilerParams(
            dimension_semantics=("arbitrary",), vmem_limit_bytes=VMEM_LIMIT),
        name="delta",
    )(x, w1, w2, norm_w, conv_w, alog, dtb, dnw, masks, conv0, s0)


S5MAT_GROUPS_PER_STEP = 8


def _split16(a):
    hi = a.astype(bf16)
    return hi, (a - hi.astype(f32)).astype(bf16)


def _s5mat_body(cp_ref, bp_ref, t_ref):
    ri = lax.broadcasted_iota(jnp.int32, (S5_GL, S5_GL), 0) >> 4
    ci = lax.broadcasted_iota(jnp.int32, (S5_GL, S5_GL), 1) >> 4
    causal = ri >= ci
    d = functools.partial(lax.dot_general, dimension_numbers=_NT, preferred_element_type=f32)
    for g in range(S5MAT_GROUPS_PER_STEP):
        ch, cl = _split16(cp_ref[g])
        bh, bl = _split16(bp_ref[g])
        t = d(ch, bh) + (d(ch, bl) + d(cl, bh))
        t_ref[g] = jnp.where(causal, t, 0.0).astype(bf16)


def _s5mat_call(cp, bp):
    n = S5MAT_GROUPS_PER_STEP
    return pl.pallas_call(
        _s5mat_body,
        grid=(S5_GROUPS // n,),
        in_specs=[
            pl.BlockSpec((n, S5_GL, 2 * S5_STATE), lambda g: (g, 0, 0)),
            pl.BlockSpec((n, S5_GL, 2 * S5_STATE), lambda g: (g, 0, 0)),
        ],
        out_specs=pl.BlockSpec((n, S5_GL, S5_GL), lambda g: (g, 0, 0)),
        out_shape=jax.ShapeDtypeStruct((S5_GROUPS, S5_GL, S5_GL), bf16),
        compiler_params=pltpu.CompilerParams(dimension_semantics=("arbitrary",)),
        name="s5mat",
    )(cp, bp)


def _s5_body(u_ref, z_ref, xr0_ref, xi0_ref, t_ref, wp_ref, vt_ref, ar_ref, ai_ref, d_ref,
             gw_ref, gb_ref,
             o_ref, xr_ref, xi_ref,
             at_scr, er_scr, ei_scr, xinr_scr, xini_scr, yt_scr, y_scr, *, tb, n_streams, n_chunks):
    R = tb // S5_L
    used = n_streams * n_chunks

    @pl.when(pl.program_id(1) == 0)
    def _init():
        xr_ref[...] = xr0_ref[...]
        xi_ref[...] = xi0_ref[...]

    for f in range(S5_L):
        for j in range(S5_PLANES):
            ut = u_ref[0, j, pl.ds(f, R, stride=S5_L), :]
            ut_t = ut.astype(bf16).T
            for gl in range(LANES // S5_GROUP):
                g = j * (LANES // S5_GROUP) + gl
                at_scr[g, f * S5_GROUP:(f + 1) * S5_GROUP, :] = ut_t[gl * S5_GROUP:(gl + 1) * S5_GROUP, :]

    for p in range(S5_PAIRS):
        a_t = at_scr[2 * p:2 * p + 2].reshape(2 * S5_GL, R)
        e = lax.dot_general(a_t, wp_ref[p], _TN, preferred_element_type=f32)
        er_scr[:, p * LANES:(p + 1) * LANES] = e[:, :LANES]
        ei_scr[:, p * LANES:(p + 1) * LANES] = e[:, LANES:]

    if used < R:
        xinr_scr[...] = jnp.zeros_like(xinr_scr)
        xini_scr[...] = jnp.zeros_like(xini_scr)

    a_r = ar_ref[...]
    a_i = ai_ref[...]
    for s in range(n_streams):
        def step(c, carry):
            xr, xi = carry
            row = s * n_chunks + c
            xinr_scr[pl.ds(row, 1), :] = xr
            xini_scr[pl.ds(row, 1), :] = xi
            er = er_scr[pl.ds(row, 1), :]
            ei = ei_scr[pl.ds(row, 1), :]
            return a_r * xr - a_i * xi + er, a_r * xi + a_i * xr + ei

        xr, xi = lax.fori_loop(0, n_chunks, step, (xr_ref[0, s:s + 1, :], xi_ref[0, s:s + 1, :]),
                               unroll=min(n_chunks, 4))
        xr_ref[0, s:s + 1, :] = xr
        xi_ref[0, s:s + 1, :] = xi

    for g in range(S5_GROUPS):
        p = g // 2
        xin = jnp.concatenate([xinr_scr[:, p * LANES:(p + 1) * LANES],
                               xini_scr[:, p * LANES:(p + 1) * LANES]], axis=1).astype(bf16)
        yt_scr[g] = (jnp.dot(t_ref[g], at_scr[g], preferred_element_type=f32)
                     + lax.dot_general(vt_ref[g], xin, _NT, preferred_element_type=f32))

    for f in range(S5_L):
        gpp = LANES // S5_GROUP
        for j in range(S5_PLANES):
            yt = yt_scr[j * gpp:(j + 1) * gpp, f * S5_GROUP:(f + 1) * S5_GROUP, :].reshape(LANES, R)
            y_scr[j, pl.ds(f, R, stride=S5_L), :] = yt.T

    for rb in range(tb // OUT_SUB):
        sl = pl.ds(rb * OUT_SUB, OUT_SUB)
        y_intra = jnp.concatenate([y_scr[j, sl, :] for j in range(S5_PLANES)], axis=1)
        u_nat = jnp.concatenate([u_ref[0, j, sl, :] for j in range(S5_PLANES)], axis=1)
        y = y_intra + d_ref[...] * u_nat
        gy = _gelu_tanh(y)
        gate = _sigmoid(_bdot(gy, gw_ref[...]) + gb_ref[...])
        o_ref[0, sl, :] = (gy * gate * _silu(z_ref[0, sl, :])).astype(bf16)


def _s5_call(u, z, xr0, xi0, tmat, wp, vt, ar, ai, dvec, gw, gb, *, tb, n_streams, n_chunks):
    G, T, _ = z.shape
    grid = (G, T // tb)
    R = tb // S5_L
    nstate = S5_GROUPS * S5_STATE
    c2 = lambda g, t: (0, 0)
    c3 = lambda g, t: (0, 0, 0)
    body = functools.partial(_s5_body, tb=tb, n_streams=n_streams, n_chunks=n_chunks)
    one = pl.Buffered(1)
    return pl.pallas_call(
        body,
        grid=grid,
        in_specs=[
            pl.BlockSpec((1, S5_PLANES, tb, LANES), lambda g, t: (g, 0, t, 0)),
            pl.BlockSpec((1, tb, S5_WIDTH), lambda g, t: (g, t, 0)),
            pl.BlockSpec((1, n_streams, nstate), lambda g, t: (g, 0, 0)),
            pl.BlockSpec((1, n_streams, nstate), lambda g, t: (g, 0, 0)),
            pl.BlockSpec((S5_GROUPS, S5_GL, S5_GL), c3, pipeline_mode=one),
            pl.BlockSpec((S5_PAIRS, 2 * S5_GL, 2 * LANES), c3, pipeline_mode=one),
            pl.BlockSpec((S5_GROUPS, S5_GL, 2 * LANES), c3, pipeline_mode=one),
            pl.BlockSpec((1, nstate), c2),
            pl.BlockSpec((1, nstate), c2),
            pl.BlockSpec((1, S5_WIDTH), c2),
            pl.BlockSpec((S5_WIDTH, S5_WIDTH), c2),
            pl.BlockSpec((1, S5_WIDTH), c2),
        ],
        out_specs=[
            pl.BlockSpec((1, tb, S5_WIDTH), lambda g, t: (g, t, 0)),
            pl.BlockSpec((1, n_streams, nstate), lambda g, t: (g, 0, 0)),
            pl.BlockSpec((1, n_streams, nstate), lambda g, t: (g, 0, 0)),
        ],
        out_shape=[
            jax.ShapeDtypeStruct((G, T, S5_WIDTH), bf16),
            jax.ShapeDtypeStruct((G, n_streams, nstate), f32),
            jax.ShapeDtypeStruct((G, n_streams, nstate), f32),
        ],
        scratch_shapes=[
            pltpu.VMEM((S5_GROUPS, S5_GL, R), bf16),
            pltpu.VMEM((R, nstate), f32),
            pltpu.VMEM((R, nstate), f32),
            pltpu.VMEM((R, nstate), f32),
            pltpu.VMEM((R, nstate), f32),
            pltpu.VMEM((S5_GROUPS, S5_GL, R), f32),
            pltpu.VMEM((S5_PLANES, tb, LANES), f32),
        ],
        compiler_params=pltpu.CompilerParams(
            dimension_semantics=("arbitrary", "arbitrary"), vmem_limit_bytes=VMEM_LIMIT),
        name="s5",
    )(u, z, xr0, xi0, tmat, wp, vt, ar, ai, dvec, gw, gb)


X_RING = 3


def _out_body(x_hbm, odn_ref, os5_ref, wo_ref, fw_ref, y_ref, xring, sem, *, rows, n_steps):
    i = pl.program_id(0)

    def fetch(step):
        slot = lax.rem(step, X_RING)
        src = x_hbm.at[pl.ds(pl.multiple_of(step * rows, rows), rows), :]
        return pltpu.make_async_copy(src, xring.at[slot], sem.at[slot])

    @pl.when(i == 0)
    def _prime():
        for s in range(min(X_RING - 1, n_steps)):
            fetch(s).start()

    @pl.when(i + (X_RING - 1) < n_steps)
    def _ahead():
        fetch(i + (X_RING - 1)).start()

    fetch(i).wait()
    acc = (xring[lax.rem(i, X_RING)]
           + jnp.dot(odn_ref[...], wo_ref[0:DN_WIDTH, :], preferred_element_type=f32)
           + jnp.dot(os5_ref[...], wo_ref[DN_WIDTH:, :], preferred_element_type=f32))
    ms = jnp.mean(acc * acc, axis=-1, keepdims=True)
    y_ref[...] = acc * lax.rsqrt(ms + EPS) * fw_ref[...]


def _out_call(x2, odn2, os52, w_out16, fw, *, rows):
    n = x2.shape[0]
    n_steps = n // rows
    return pl.pallas_call(
        functools.partial(_out_body, rows=rows, n_steps=n_steps),
        grid=(n_steps,),
        in_specs=[
            pl.BlockSpec(memory_space=pl.ANY),
            pl.BlockSpec((rows, DN_WIDTH), lambda i: (i, 0)),
            pl.BlockSpec((rows, S5_WIDTH), lambda i: (i, 0)),
            pl.BlockSpec((D_MODEL, D_MODEL), lambda i: (0, 0)),
            pl.BlockSpec((1, D_MODEL), lambda i: (0, 0)),
        ],
        out_specs=pl.BlockSpec((rows, D_MODEL), lambda i: (i, 0)),
        out_shape=jax.ShapeDtypeStruct((n, D_MODEL), f32),
        scratch_shapes=[pltpu.VMEM((X_RING, rows, D_MODEL), f32),
                        pltpu.SemaphoreType.DMA((X_RING,))],
        compiler_params=pltpu.CompilerParams(
            dimension_semantics=("arbitrary",), vmem_limit_bytes=VMEM_LIMIT),
        name="outproj",
    )(x2, odn2, os52, w_out16, fw)


def _s5_operands(a_re, a_im, log_dt, b_re, b_im, c_re, c_im):
    lam_re = jnp.minimum(a_re, -1e-4)
    lam_im = a_im
    dt = jnp.exp(log_dt)[:, None]
    ldt_re, ldt_im = lam_re * dt, lam_im * dt

    def lpow(k):
        mag = jnp.exp(ldt_re * k)
        return mag * jnp.cos(ldt_im * k), mag * jnp.sin(ldt_im * k)

    lb_re, lb_im = lpow(1.0)
    den = lam_re * lam_re + lam_im * lam_im
    f_re = ((lb_re - 1.0) * lam_re + lb_im * lam_im) / den
    f_im = (lb_im * lam_re - (lb_re - 1.0) * lam_im) / den
    bb_re = f_re[..., None] * b_re - f_im[..., None] * b_im
    bb_im = f_re[..., None] * b_im + f_im[..., None] * b_re

    fr = jnp.arange(S5_L, dtype=f32)
    G, N = S5_GROUPS, S5_STATE

    def cmul(ar, ai, br, bi):
        return ar * br - ai * bi, ar * bi + ai * br

    pr, pi = lpow(fr[:, None, None])
    pr, pi = pr.transpose(1, 0, 2), pi.transpose(1, 0, 2)
    cpr, cpi = cmul(c_re[:, None], c_im[:, None], pr[:, :, None], pi[:, :, None])
    cpr, cpi = cpr.reshape(G, S5_GL, N), cpi.reshape(G, S5_GL, N)
    bbr_t, bbi_t = bb_re.transpose(0, 2, 1), bb_im.transpose(0, 2, 1)
    qr, qi = lpow(-fr[:, None, None])
    qr, qi = qr.transpose(1, 0, 2), qi.transpose(1, 0, 2)
    bpr, bpi = cmul(qr[:, :, None], qi[:, :, None], bbr_t[:, None], bbi_t[:, None])
    bpr, bpi = bpr.reshape(G, S5_GL, N), bpi.reshape(G, S5_GL, N)

    wr_, wi_ = lpow((S5_L - 1.0) - fr[:, None, None])
    wr_, wi_ = wr_.transpose(1, 0, 2), wi_.transpose(1, 0, 2)
    w_re, w_im = cmul(wr_[:, :, None], wi_[:, :, None], bbr_t[:, None], bbi_t[:, None])
    w_re, w_im = w_re.reshape(G, S5_GL, N), w_im.reshape(G, S5_GL, N)
    vr_, vi_ = lpow(fr[:, None, None] + 1.0)
    vr_, vi_ = vr_.transpose(1, 0, 2), vi_.transpose(1, 0, 2)
    v_re, v_im = cmul(c_re[:, None], c_im[:, None], vr_[:, :, None], vi_[:, :, None])
    v_re, v_im = v_re.reshape(G, S5_GL, N), v_im.reshape(G, S5_GL, N)

    zeros = jnp.zeros((S5_PAIRS, S5_GL, N), f32)

    def pair_cols(re, im):
        re, im = re.reshape(S5_PAIRS, 2, S5_GL, N), im.reshape(S5_PAIRS, 2, S5_GL, N)
        even = jnp.concatenate([re[:, 0], zeros, im[:, 0], zeros], axis=-1)
        odd = jnp.concatenate([zeros, re[:, 1], zeros, im[:, 1]], axis=-1)
        return jnp.stack([even, odd], axis=1)

    wp = pair_cols(w_re, w_im).reshape(S5_PAIRS, 2 * S5_GL, 2 * LANES).astype(bf16)
    vt = pair_cols(v_re, -v_im).reshape(S5_GROUPS, S5_GL, 2 * LANES).astype(bf16)
    a16r, a16i = lpow(float(S5_L))
    cp = jnp.concatenate([cpr, cpi], axis=2)
    bp = jnp.concatenate([bpr, -bpi], axis=2)
    return cp, bp, wp, vt, a16r.reshape(1, G * N), a16i.reshape(1, G * N)


def _pad_lanes(v, start=0):
    out = jnp.zeros((1, LANES), f32)
    return out.at[0, start:start + v.shape[0]].set(v.astype(f32))


def _layer(x, conv0, s0, xr0, xi0, prm, *, delta_cfg, s5_cfg):
    B, T, _ = x.shape
    conv0p = jnp.concatenate([jnp.zeros((B, SUBLANES - (CONV_K - 1), CONV_CH), f32), conv0], axis=1)
    odn, us5, zs5, convout, s_new = _delta_call(
        x, prm["w1"], prm["w2"], prm["norm_w"], prm["conv_w"], prm["alog"], prm["dtb"], prm["dnw"],
        conv0p, s0, **delta_cfg)

    nstate = S5_GROUPS * S5_STATE
    if s5_cfg["flatten"]:
        tb = s5_cfg["tb"]
        uf = us5.transpose(1, 0, 2, 3).reshape(1, S5_PLANES, B * T, LANES)
        zf = zs5.reshape(1, B * T, S5_WIDTH)
        pad = tb - B * T
        uf = jnp.pad(uf, ((0, 0), (0, 0), (0, pad), (0, 0)))
        zf = jnp.pad(zf, ((0, 0), (0, pad), (0, 0)))
        os5, xr, xi = _s5_call(uf, zf, xr0.reshape(1, B, nstate), xi0.reshape(1, B, nstate),
                               prm["tmat"], prm["wp"], prm["vt"], prm["ar"], prm["ai"], prm["dvec"],
                               prm["gw"], prm["gb"], tb=tb, n_streams=B, n_chunks=T // S5_L)
        os5 = os5[0, :B * T].reshape(B, T, S5_WIDTH)
    else:
        tb = s5_cfg["tb"]
        os5, xr, xi = _s5_call(us5, zs5, xr0.reshape(B, 1, nstate), xi0.reshape(B, 1, nstate),
                               prm["tmat"], prm["wp"], prm["vt"], prm["ar"], prm["ai"], prm["dvec"],
                               prm["gw"], prm["gb"], tb=tb, n_streams=1, n_chunks=tb // S5_L)

    y = _out_call(x.reshape(B * T, D_MODEL), odn.reshape(B * T, DN_WIDTH), os5.reshape(B * T, S5_WIDTH),
                  prm["w_out"], prm["fw"], rows=min(OUT_ROWS, B * T))
    return (y.reshape(B, T, D_MODEL), convout[:, SUBLANES - (CONV_K - 1):, :], s_new,
            xr.reshape(B, S5_GROUPS, S5_STATE), xi.reshape(B, S5_GROUPS, S5_STATE))


def kernel(x_prompt, x_sample, cache_conv, state_dn, state_s5_re, state_s5_im, norm_w, w_in, conv_w, dn_A_log, dn_dt_bias, dn_norm_w, s5_A_re, s5_A_im, s5_log_dt, s5_B_re, s5_B_im, s5_C_re, s5_C_im, s5_D, glu_w, glu_b, w_out, final_norm_w):
    depth = norm_w.shape[0]
    assert depth == 1
    l = 0
    assert w_in.shape[-1] == IN_COLS
    w1 = w_in[l, :, :CONV_CH].astype(bf16)
    w2 = w_in[l, :, IN_COLS - W2_COLS:].astype(bf16)
    cp, bp, wp, vt, ar, ai = _s5_operands(
        s5_A_re[l].astype(f32), s5_A_im[l].astype(f32), s5_log_dt[l].astype(f32),
        s5_B_re[l].astype(f32), s5_B_im[l].astype(f32), s5_C_re[l].astype(f32), s5_C_im[l].astype(f32))
    prm = dict(
        w1=w1, w2=w2,
        norm_w=norm_w[l].reshape(1, D_MODEL).astype(f32),
        conv_w=conv_w[l].astype(f32),
        alog=_pad_lanes(dn_A_log[l], AB_LANE),
        dtb=_pad_lanes(dn_dt_bias[l], AB_LANE),
        dnw=dn_norm_w[l].reshape(1, DN_D).astype(f32),
        tmat=_s5mat_call(cp, bp),
        wp=wp, vt=vt, ar=ar, ai=ai,
        dvec=s5_D[l].reshape(1, S5_WIDTH).astype(f32),
        gw=glu_w[l].astype(bf16),
        gb=glu_b[l].reshape(1, S5_WIDTH).astype(f32),
        w_out=w_out[l].astype(bf16),
        fw=final_norm_w.reshape(1, D_MODEL).astype(f32),
    )

    bp = x_prompt.shape[0]
    yp, c1, d1, r1, i1 = _layer(
        x_prompt,
        jnp.zeros((bp, CONV_K - 1, CONV_CH), f32),
        jnp.zeros((bp, DN_HEADS, DN_D, DN_D), f32),
        jnp.zeros((bp, S5_GROUPS, S5_STATE), f32),
        jnp.zeros((bp, S5_GROUPS, S5_STATE), f32),
        prm,
        delta_cfg=dict(tb=256, pipelined=True),
        s5_cfg=dict(flatten=False, tb=2048))
    ys, c2, d2, r2, i2 = _layer(
        x_sample, cache_conv[l].astype(f32), state_dn[l].astype(f32),
        state_s5_re[l].astype(f32), state_s5_im[l].astype(f32),
        prm,
        delta_cfg=dict(tb=32, pipelined=False),
        s5_cfg=dict(flatten=True, tb=512))

    return (yp, ys, c1[None], d1[None], r1[None], i1[None], c2[None], d2[None], r2[None], i2[None])
```

```python
import functools
import math

import jax
import jax.numpy as jnp
import numpy as np
from jax import lax
from jax.experimental import pallas as pl
from jax.experimental.pallas import tpu as pltpu

bf16 = jnp.bfloat16
f32 = jnp.float32

LANES = 128
SUBLANES = 8

D_MODEL = 1024
DN_HEADS = 4
DN_D = 128
DN_WIDTH = DN_HEADS * DN_D
CONV_K = 4
CONV_CH = 3 * DN_WIDTH
S5_WIDTH = 512
S5_GROUP = 16
S5_GROUPS = 32
S5_STATE = 64
S5_PAIRS = S5_GROUPS // 2
S5_L = 16
S5_GL = S5_GROUP * S5_L
OUT_SUB = 256
OUT_ROWS = 1024
S5_PLANES = S5_WIDTH // LANES
EPS = 1e-6
L2_EPS = 1e-6

IN_COLS = CONV_CH + 2 * DN_HEADS + DN_WIDTH + 2 * S5_WIDTH
W2_COLS = LANES + DN_WIDTH + 2 * S5_WIDTH
AB_LANE = LANES - 2 * DN_HEADS
W2_ZDN = LANES
W2_US5 = W2_ZDN + DN_WIDTH
W2_ZS5 = W2_US5 + S5_WIDTH

GROUP = 256
MASK_INCL, MASK_STRICT, MASK_OFF = 0, 1, 2
FOLD_MIN_ROWS = 64
VMEM_LIMIT =56 * 1024 * 1024

_NT = (((1,), (1,)), ((), ()))
_TN = (((0,), (0,)), ((), ()))


def _bdot(a, b):
    return jnp.dot(a.astype(bf16), b.astype(bf16), preferred_element_type=f32)


def _bdot_g(a, b, dims):
    return lax.dot_general(a.astype(bf16), b.astype(bf16), dims, preferred_element_type=f32)


def _sigmoid(x):
    return 1.0 / (1.0 + jnp.exp2(x * (-math.log2(math.e))))


def _silu(x):
    return x * _sigmoid(x)


def _softplus(x):
    return jnp.maximum(x, 0.0) + jnp.log1p(jnp.exp(-jnp.abs(x)))


def _gelu_tanh(x):
    c = math.sqrt(2.0 / math.pi)
    return x * (0.5 + 0.5 * jnp.tanh(x * (c + (c * 0.044715) * (x * x))))


def _zero_after(x):
    bits = pltpu.bitcast(x[0:SUBLANES, 0:LANES].astype(f32), jnp.uint32)
    half = jnp.uint32(16)
    bits = lax.shift_right_logical(lax.shift_right_logical(bits, half), half)
    return pltpu.bitcast(bits, f32)[0:1, :]


def _unit_lower_solve(nmats, rhss, eye, m_ref, n_levels, fillers):
    d32 = functools.partial(jnp.dot, preferred_element_type=f32)

    def d16(a, b):
        return d32(a, b).astype(bf16)

    def fill(after):
        if fillers:
            fillers.pop(0)(_zero_after(after[-1]))

    def fold(x, rows):
        out = x[0:rows]
        for k in range(1, GROUP // rows):
            out = out + x[k * rows:(k + 1) * rows]
        return out

    def unfold(xf, rows):
        lane_blk = lax.broadcasted_iota(jnp.int32, (rows, GROUP), 1) >> (rows.bit_length() - 1)
        return jnp.concatenate([jnp.where(lane_blk == k, xf, jnp.zeros_like(xf))
                                for k in range(GROUP // rows)], axis=0)

    n16 = [n.astype(bf16) for n in nmats]
    invs = [(eye - n * m_ref[MASK_OFF]).astype(bf16) for n in nmats]
    substitute_top = 2 ** n_levels == GROUP
    for lvl in range(1, n_levels - 1 if substitute_top else n_levels):
        rows = min(GROUP, max(FOLD_MIN_ROWS, 2 ** (lvl + 1)))
        mask16 = m_ref[MASK_OFF + lvl].astype(bf16)
        lhs = invs if rows == GROUP else [fold(inv, rows) for inv in invs]
        t1 = [d16(l, n * mask16) for l, n in zip(lhs, n16)]
        fill(invs)
        t2 = [d16(t, inv) for t, inv in zip(t1, invs)]
        fill(t1)
        if rows != GROUP:
            t2 = [unfold(t, rows) for t in t2]
        invs = [inv - t for inv, t in zip(invs, t2)]
    if not substitute_top:
        return [d32(inv, r.astype(bf16)) for inv, r in zip(invs, rhss)]
    h = GROUP // 2
    x1 = [d32(inv[0:h, 0:h], r[0:h].astype(bf16)) for inv, r in zip(invs, rhss)]
    fill(invs)
    bx = [d32(n[h:, 0:h], x.astype(bf16)) for n, x in zip(n16, x1)]
    fill(x1)
    x2 = [d32(inv[h:, h:], (r[h:] - t).astype(bf16)) for inv, r, t in zip(invs, rhss, bx)]
    return [jnp.concatenate([a, b], axis=0) for a, b in zip(x1, x2)]


def _segment_cumsum(tri16, g):
    g1 = g.astype(bf16)
    r1 = g - g1.astype(f32)
    g2 = r1.astype(bf16)
    g3 = (r1 - g2.astype(f32)).astype(bf16)
    s = jnp.dot(tri16, jnp.concatenate([g1, g2, g3], axis=1), preferred_element_type=f32)
    n = g.shape[1]
    return s[:, :n] + (s[:, n:2 * n] + s[:, 2 * n:])


def _delta_body(x_ref, w1_ref, w2_ref, nw_ref, cw_ref, alog_ref, dtb_ref, dnw_ref, m_ref, conv0_ref, s0_ref,
                odn_ref, us5_ref, zs5_ref, convout_ref, s_ref,
                cbuf, pbuf, qn_scr, kn_scr, v_scr, gc_scr, beta_scr, zdn_scr,
                *, nb, tb, n_t, pipelined):
    rows = nb * tb
    n_groups = rows // GROUP
    segs = GROUP // tb
    n_levels = tb.bit_length() - 1
    step = pl.program_id(0)
    fresh = step < n_t

    @pl.when(step == 0)
    def _init():
        cbuf[:, 0:SUBLANES, :] = conv0_ref[...]
        for b in range(nb):
            c8 = conv0_ref[b]
            pbuf[b, 0:SUBLANES, :] = c8 * cw_ref[1:2, :] + pltpu.roll(c8, 1, axis=0) * cw_ref[0:1, :]
        s_ref[...] = s0_ref[...]
        if pipelined:
            for scr in (qn_scr, kn_scr, v_scr, gc_scr, beta_scr, zdn_scr):
                scr[...] = jnp.zeros_like(scr)

    front = {}

    def project(w_ref, lo, width):
        return jnp.dot(front["h"], w_ref[:, lo:lo + width], preferred_element_type=f32)

    def a_norm_in():
        x = x_ref[...].reshape(rows, D_MODEL)
        ms = jnp.mean(x * x, axis=-1, keepdims=True)
        front["h"] = (x * lax.rsqrt(ms + EPS) * nw_ref[...]).astype(bf16)

    def a_project_qkv(s):
        front[("raw", s)] = project(w1_ref, s * DN_WIDTH, DN_WIDTH)

    def a_project_us5():
        u = project(w2_ref, W2_US5, S5_WIDTH)
        for j in range(S5_PLANES):
            us5_ref[:, j] = u[:, j * LANES:(j + 1) * LANES].reshape(nb, tb, LANES)

    def a_project_zs5():
        zs5_ref[...] = project(w2_ref, W2_ZS5, S5_WIDTH).reshape(nb, tb, S5_WIDTH)

    def a_project_zdn():
        front["zdn"] = project(w2_ref, W2_ZDN, DN_WIDTH)

    def a_project_gates():
        ab = project(w2_ref, 0, LANES)
        front["g"] = -jnp.exp(alog_ref[...]) * _softplus(ab + dtb_ref[...])
        front["beta"] = _sigmoid(ab)

    projections = ([functools.partial(a_project_qkv, s) for s in range(3)]
                   + [a_project_us5, a_project_zs5, a_project_zdn, a_project_gates])

    def a_conv(b, s, zero):
        cols = slice(s * DN_WIDTH, (s + 1) * DN_WIDTH)
        u0 = front[("raw", s)][b * tb:(b + 1) * tb, :]
        cbuf[b, SUBLANES:SUBLANES + tb, cols] = u0
        taps = [cw_ref[j:j + 1, cols] for j in range(CONV_K)]
        if zero is not None:
            zero_w = jnp.concatenate([zero] * (DN_WIDTH // LANES), axis=1)
            taps = [t + zero_w for t in taps]
        u1 = cbuf[b, SUBLANES - 1:SUBLANES - 1 + tb, cols]
        pbuf[b, SUBLANES:SUBLANES + tb, cols] = u0 * taps[1] + u1 * taps[0]
        acc = (u0 * taps[3] + u1 * taps[2]) + pbuf[b, SUBLANES - 2:SUBLANES - 2 + tb, cols]
        front[("act", b, s)] = _silu(acc)
        for buf in (cbuf, pbuf):
            tail = buf[b, tb:tb + SUBLANES, cols]
            if pipelined:
                tail = jnp.where(fresh, tail, buf[b, 0:SUBLANES, cols])
            buf[b, 0:SUBLANES, cols] = tail
        convout_ref[b, :, cols] = cbuf[b, 0:SUBLANES, cols]

    def a_norm(b, hd, zero):
        lo = hd * DN_D
        q = front[("act", b, 0)][:, lo:lo + DN_D]
        k = front[("act", b, 1)][:, lo:lo + DN_D]
        if zero is not None:
            q, k = q + zero, k + zero
        front[("qn", b, hd)] = (q * lax.rsqrt(jnp.sum(q * q, axis=-1, keepdims=True) + L2_EPS)
                                * (DN_D ** -0.5))
        front[("kn", b, hd)] = k * lax.rsqrt(jnp.sum(k * k, axis=-1, keepdims=True) + L2_EPS)

    def a_gates(zero):
        tri16 = m_ref[MASK_INCL].astype(bf16)
        g = front["g"] if zero is None else front["g"] + zero
        front["gc"] = [_segment_cumsum(tri16, g[gi * GROUP:(gi + 1) * GROUP])
                       for gi in range(n_groups)]

    def a_store():
        for b in range(nb):
            r = slice(b * tb, (b + 1) * tb)
            for hd in range(DN_HEADS):
                c = slice(hd * DN_D, (hd + 1) * DN_D)
                qn_scr[r, c] = front[("qn", b, hd)]
                kn_scr[r, c] = front[("kn", b, hd)]
            v_scr[r, :] = front[("act", b, 2)]
        for gi in range(n_groups):
            gc_scr[gi * GROUP:(gi + 1) * GROUP, :] = front["gc"][gi]
        beta_scr[...] = front["beta"]
        zdn_scr[...] = front["zdn"]

    fillers = ([functools.partial(a_conv, b, s) for b in range(nb) for s in range(3)] + [a_gates]
               + [functools.partial(a_norm, b, hd) for b in range(nb) for hd in range(DN_HEADS)])

    incl = m_ref[MASK_INCL]
    strict = m_ref[MASK_STRICT]
    eye = incl - strict
    dnw = dnw_ref[...]
    chains = [(gi, hd) for gi in range(n_groups) for hd in range(DN_HEADS)]
    back = {}

    def stream_of(gi, sg):
        return (gi * GROUP) // tb + sg

    for name in ("q16", "qe16", "k16", "kb16", "rhs", "dmat", "kdec", "sdecay", "nmat", "attn"):
        back[name] = [None] * len(chains)

    def b_load(ci):
        gi, hd = chains[ci]
        r = slice(gi * GROUP, (gi + 1) * GROUP)
        c = slice(hd * DN_D, (hd + 1) * DN_D)
        q, k, v = qn_scr[r, c], kn_scr[r, c], v_scr[r, c]
        beta = beta_scr[r, AB_LANE + DN_HEADS + hd:AB_LANE + DN_HEADS + hd + 1]
        g = gc_scr[r, AB_LANE + hd:AB_LANE + hd + 1]
        eg = jnp.exp(g)
        kb = k * beta
        gcb = jnp.broadcast_to(g * math.log2(math.e), (GROUP, GROUP))
        back["dmat"][ci] = jnp.exp2(jnp.minimum(gcb - gcb.T, 0.0)) * incl
        back["q16"][ci] = q.astype(bf16)
        back["qe16"][ci] = (q * eg).astype(bf16)
        back["k16"][ci] = k.astype(bf16)
        back["kb16"][ci] = kb.astype(bf16)
        back["rhs"][ci] = jnp.concatenate([v * beta, kb * eg], axis=1)
        kd, sd = [], []
        for sg in range(segs):
            a0 = sg * tb
            glast = g[a0 + tb - 1:a0 + tb, :]
            kd.append((k[a0:a0 + tb] * jnp.exp(glast - g[a0:a0 + tb])).astype(bf16))
            sd.append(jnp.exp(glast))
        back["kdec"][ci] = kd
        back["sdecay"][ci] = sd

    def b_mats(ci):
        d = functools.partial(lax.dot_general, dimension_numbers=_NT, preferred_element_type=f32)
        dm = back["dmat"][ci]
        back["nmat"][ci] = d(back["kb16"][ci], back["k16"][ci]) * (dm * strict)
        back["attn"][ci] = (d(back["q16"][ci], back["k16"][ci]) * dm).astype(bf16)

    def b_state(sols):
        s_old = {(b, hd): s_ref[b, hd] for b in range(nb) for hd in range(DN_HEADS)}
        s16 = {key: val.astype(bf16) for key, val in s_old.items()}
        v_news, o_states = [], []
        for (gi, hd), sol, qe16 in zip(chains, sols, back["qe16"]):
            u = sol[:, :DN_D]
            w16 = sol[:, DN_D:].astype(bf16)
            vn, os_ = [], []
            for sg in range(segs):
                a0 = sg * tb
                st = s16[(stream_of(gi, sg), hd)]
                vn.append(u[a0:a0 + tb] - jnp.dot(w16[a0:a0 + tb], st, preferred_element_type=f32))
                os_.append(jnp.dot(qe16[a0:a0 + tb], st, preferred_element_type=f32))
            v_news.append(vn[0] if segs == 1 else jnp.concatenate(vn, axis=0))
            o_states.append(os_[0] if segs == 1 else jnp.concatenate(os_, axis=0))
        vn16s = [vn.astype(bf16) for vn in v_news]
        outs = [os_ + jnp.dot(a, vn, preferred_element_type=f32)
                for os_, a, vn in zip(o_states, back["attn"], vn16s)]

        s_new = {}
        for (gi, hd), kd, sd, vn in zip(chains, back["kdec"], back["sdecay"], vn16s):
            for sg in range(segs):
                a0 = sg * tb
                key = (stream_of(gi, sg), hd)
                s_new[key] = (s_old[key] * sd[sg]
                              + lax.dot_general(kd[sg], vn[a0:a0 + tb], _TN, preferred_element_type=f32))

        o_rows = []
        for gi in range(n_groups):
            o_heads = []
            for hd in range(DN_HEADS):
                o = outs[gi * DN_HEADS + hd]
                on = o * lax.rsqrt(jnp.mean(o * o, axis=-1, keepdims=True) + EPS) * dnw
                zd = zdn_scr[gi * GROUP:(gi + 1) * GROUP, hd * DN_D:(hd + 1) * DN_D]
                o_heads.append(on * _silu(zd))
            o_rows.append(jnp.concatenate(o_heads, axis=1))
        o_all = o_rows[0] if n_groups == 1 else jnp.concatenate(o_rows, axis=0)
        odn_ref[...] = o_all.astype(bf16).reshape(nb, tb, DN_WIDTH)
        s_ref[...] = jnp.stack([jnp.stack([s_new[(b, hd)] for hd in range(DN_HEADS)]) for b in range(nb)])

    n_ch = len(chains)
    if pipelined:
        a_norm_in()
        for ci in range(n_ch):
            b_load(ci)
        back_setup = [functools.partial(b_mats, ci) for ci in range(n_ch)]
        convs = [fillers.pop(0) for _ in range(nb * 3)]
        per_piece = -(-len(back_setup) // len(projections))
        for idx, piece in enumerate(projections):
            piece()
            for thunk in back_setup[:per_piece]:
                thunk()
            del back_setup[:per_piece]
            if idx >= 3 and convs:
                convs.pop(0)(None)
        while back_setup or convs:
            if back_setup:
                back_setup.pop(0)()
            if convs:
                convs.pop(0)(None)
        sols = _unit_lower_solve(back["nmat"], back["rhs"], eye, m_ref, n_levels, fillers)
        for fill in fillers:
            fill(None)
        b_state(sols)
        a_store()
    else:
        a_norm_in()
        for piece in projections:
            piece()
        for fill in fillers:
            fill(None)
        a_store()
        for ci in range(n_ch):
            b_load(ci)
        for ci in range(n_ch):
            b_mats(ci)
        b_state(_unit_lower_solve(back["nmat"], back["rhs"], eye, m_ref, n_levels, []))


def _delta_masks(tb):
    r = np.arange(GROUP)[:, None]
    c = np.arange(GROUP)[None, :]
    same = (r // tb) == (c // tb)
    ms = [same & (r >= c), same & (r > c)]
    for l in range(tb.bit_length() - 1):
        ms.append(((r >> (l + 1)) == (c >> (l + 1))) & (((r >> l) & 1) == 1) & (((c >> l) & 1) == 0))
    return jnp.asarray(np.stack(ms).astype(np.float32))


def _delta_call(x, w1, w2, norm_w, conv_w, alog, dtb, dnw, conv0, s0, *, tb, pipelined):
    nb, T, _ = x.shape
    assert GROUP % tb == 0 and (nb * tb) % GROUP == 0 and T % tb == 0
    n_t = T // tb
    rows = nb * tb
    if pipelined:
        grid = (n_t + 1,)
        cur = lambda i: jnp.minimum(i, n_t - 1)
        prev = lambda i: jnp.maximum(i - 1, 0)
    else:
        grid = (n_t,)
        cur = prev = lambda i: i
    const2 = lambda i: (0, 0)
    const3 = lambda i: (0, 0, 0)
    const4 = lambda i: (0, 0, 0, 0)
    body = functools.partial(_delta_body, nb=nb, tb=tb, n_t=n_t, pipelined=pipelined)
    masks = _delta_masks(tb)
    one = pl.Buffered(1)
    return pl.pallas_call(
        body,
        grid=grid,
        in_specs=[
            pl.BlockSpec((nb, tb, D_MODEL), lambda i: (0, cur(i), 0)),
            pl.BlockSpec((D_MODEL, CONV_CH), const2, pipeline_mode=one),
            pl.BlockSpec((D_MODEL, W2_COLS), const2, pipeline_mode=one),
            pl.BlockSpec((1, D_MODEL), const2),
            pl.BlockSpec((CONV_K, CONV_CH), const2),
            pl.BlockSpec((1, LANES), const2),
            pl.BlockSpec((1, LANES), const2),
            pl.BlockSpec((1, DN_D), const2),
            pl.BlockSpec(masks.shape, const3, pipeline_mode=one),
            pl.BlockSpec((nb, SUBLANES, CONV_CH), const3),
            pl.BlockSpec((nb, DN_HEADS, DN_D, DN_D), const4),
        ],
        out_specs=[
            pl.BlockSpec((nb, tb, DN_WIDTH), lambda i: (0, prev(i), 0)),
            pl.BlockSpec((nb, S5_PLANES, tb, LANES), lambda i: (0, 0, cur(i), 0)),
            pl.BlockSpec((nb, tb, S5_WIDTH), lambda i: (0, cur(i), 0)),
            pl.BlockSpec((nb, SUBLANES, CONV_CH), const3),
            pl.BlockSpec((nb, DN_HEADS, DN_D, DN_D), const4),
        ],
        out_shape=[
            jax.ShapeDtypeStruct((nb, T, DN_WIDTH), bf16),
            jax.ShapeDtypeStruct((nb, S5_PLANES, T, LANES), f32),
            jax.ShapeDtypeStruct((nb, T, S5_WIDTH), f32),
            jax.ShapeDtypeStruct((nb, SUBLANES, CONV_CH), f32),
            jax.ShapeDtypeStruct((nb, DN_HEADS, DN_D, DN_D), f32),
        ],
        scratch_shapes=[
            pltpu.VMEM((nb, tb + SUBLANES, CONV_CH), f32),
            pltpu.VMEM((nb, tb + SUBLANES, CONV_CH), f32),
            pltpu.VMEM((rows, DN_WIDTH), f32),
            pltpu.VMEM((rows, DN_WIDTH), f32),
            pltpu.VMEM((rows, DN_WIDTH), f32),
            pltpu.VMEM((rows, LANES), f32),
            pltpu.VMEM((rows, LANES), f32),
            pltpu.VMEM((rows, DN_WIDTH), f32),
        ],
        compiler_params=pltpu.CompilerParams(
            dimension_semantics=("arbitrary",), vmem_limit_bytes=VMEM_LIMIT),
        name="delta",
    )(x, w1, w2, norm_w, conv_w, alog, dtb, dnw, masks, conv0, s0)


S5MAT_GROUPS_PER_STEP = 8


def _split16(a):
    hi = a.astype(bf16)
    return hi, (a - hi.astype(f32)).astype(bf16)


def _s5mat_body(cp_ref, bp_ref, t_ref):
    ri = lax.broadcasted_iota(jnp.int32, (S5_GL, S5_GL), 0) >> 4
    ci = lax.broadcasted_iota(jnp.int32, (S5_GL, S5_GL), 1) >> 4
    causal = ri >= ci
    d = functools.partial(lax.dot_general, dimension_numbers=_NT, preferred_element_type=f32)
    for g in range(S5MAT_GROUPS_PER_STEP):
        ch, cl = _split16(cp_ref[g])
        bh, bl = _split16(bp_ref[g])
        t = d(ch, bh) + (d(ch, bl) + d(cl, bh))
        t_ref[g] = jnp.where(causal, t, 0.0).astype(bf16)


def _s5mat_call(cp, bp):
    n = S5MAT_GROUPS_PER_STEP
    return pl.pallas_call(
        _s5mat_body,
        grid=(S5_GROUPS // n,),
        in_specs=[
            pl.BlockSpec((n, S5_GL, 2 * S5_STATE), lambda g: (g, 0, 0)),
            pl.BlockSpec((n, S5_GL, 2 * S5_STATE), lambda g: (g, 0, 0)),
        ],
        out_specs=pl.BlockSpec((n, S5_GL, S5_GL), lambda g: (g, 0, 0)),
        out_shape=jax.ShapeDtypeStruct((S5_GROUPS, S5_GL, S5_GL), bf16),
        compiler_params=pltpu.CompilerParams(dimension_semantics=("arbitrary",)),
        name="s5mat",
    )(cp, bp)


def _s5_body(u_ref, z_ref, xr0_ref, xi0_ref, t_ref, wp_ref, vt_ref, ar_ref, ai_ref, d_ref,
             gw_ref, gb_ref,
             o_ref, xr_ref, xi_ref,
             at_scr, er_scr, ei_scr, xinr_scr, xini_scr, yt_scr, y_scr, *, tb, n_streams, n_chunks):
    R = tb // S5_L
    used = n_streams * n_chunks

    @pl.when(pl.program_id(1) == 0)
    def _init():
        xr_ref[...] = xr0_ref[...]
        xi_ref[...] = xi0_ref[...]

    for f in range(S5_L):
        for j in range(S5_PLANES):
            ut = u_ref[0, j, pl.ds(f, R, stride=S5_L), :]
            ut_t = ut.astype(bf16).T
            for gl in range(LANES // S5_GROUP):
                g = j * (LANES // S5_GROUP) + gl
                at_scr[g, f * S5_GROUP:(f + 1) * S5_GROUP, :] = ut_t[gl * S5_GROUP:(gl + 1) * S5_GROUP, :]

    for p in range(S5_PAIRS):
        a_t = at_scr[2 * p:2 * p + 2].reshape(2 * S5_GL, R)
        e = lax.dot_general(a_t, wp_ref[p], _TN, preferred_element_type=f32)
        er_scr[:, p * LANES:(p + 1) * LANES] = e[:, :LANES]
        ei_scr[:, p * LANES:(p + 1) * LANES] = e[:, LANES:]

    if used < R:
        xinr_scr[...] = jnp.zeros_like(xinr_scr)
        xini_scr[...] = jnp.zeros_like(xini_scr)

    a_r = ar_ref[...]
    a_i = ai_ref[...]
    for s in range(n_streams):
        def step(c, carry):
            xr, xi = carry
            row = s * n_chunks + c
            xinr_scr[pl.ds(row, 1), :] = xr
            xini_scr[pl.ds(row, 1), :] = xi
            er = er_scr[pl.ds(row, 1), :]
            ei = ei_scr[pl.ds(row, 1), :]
            return a_r * xr - a_i * xi + er, a_r * xi + a_i * xr + ei

        xr, xi = lax.fori_loop(0, n_chunks, step, (xr_ref[0, s:s + 1, :], xi_ref[0, s:s + 1, :]),
                               unroll=min(n_chunks, 4))
        xr_ref[0, s:s + 1, :] = xr
        xi_ref[0, s:s + 1, :] = xi

    for g in range(S5_GROUPS):
        p = g // 2
        xin = jnp.concatenate([xinr_scr[:, p * LANES:(p + 1) * LANES],
                               xini_scr[:, p * LANES:(p + 1) * LANES]], axis=1).astype(bf16)
        yt_scr[g] = (jnp.dot(t_ref[g], at_scr[g], preferred_element_type=f32)
                     + lax.dot_general(vt_ref[g], xin, _NT, preferred_element_type=f32))

    for f in range(S5_L):
        gpp = LANES // S5_GROUP
        for j in range(S5_PLANES):
            yt = yt_scr[j * gpp:(j + 1) * gpp, f * S5_GROUP:(f + 1) * S5_GROUP, :].reshape(LANES, R)
            y_scr[j, pl.ds(f, R, stride=S5_L), :] = yt.T

    for rb in range(tb // OUT_SUB):
        sl = pl.ds(rb * OUT_SUB, OUT_SUB)
        y_intra = jnp.concatenate([y_scr[j, sl, :] for j in range(S5_PLANES)], axis=1)
        u_nat = jnp.concatenate([u_ref[0, j, sl, :] for j in range(S5_PLANES)], axis=1)
        y = y_intra + d_ref[...] * u_nat
        gy = _gelu_tanh(y)
        gate = _sigmoid(_bdot(gy, gw_ref[...]) + gb_ref[...])
        o_ref[0, sl, :] = (gy * gate * _silu(z_ref[0, sl, :])).astype(bf16)


def _s5_call(u, z, xr0, xi0, tmat, wp, vt, ar, ai, dvec, gw, gb, *, tb, n_streams, n_chunks):
    G, T, _ = z.shape
    grid = (G, T // tb)
    R = tb // S5_L
    nstate = S5_GROUPS * S5_STATE
    c2 = lambda g, t: (0, 0)
    c3 = lambda g, t: (0, 0, 0)
    body = functools.partial(_s5_body, tb=tb, n_streams=n_streams, n_chunks=n_chunks)
    one = pl.Buffered(1)
    return pl.pallas_call(
        body,
        grid=grid,
        in_specs=[
            pl.BlockSpec((1, S5_PLANES, tb, LANES), lambda g, t: (g, 0, t, 0)),
            pl.BlockSpec((1, tb, S5_WIDTH), lambda g, t: (g, t, 0)),
            pl.BlockSpec((1, n_streams, nstate), lambda g, t: (g, 0, 0)),
            pl.BlockSpec((1, n_streams, nstate), lambda g, t: (g, 0, 0)),
            pl.BlockSpec((S5_GROUPS, S5_GL, S5_GL), c3, pipeline_mode=one),
            pl.BlockSpec((S5_PAIRS, 2 * S5_GL, 2 * LANES), c3, pipeline_mode=one),
            pl.BlockSpec((S5_GROUPS, S5_GL, 2 * LANES), c3, pipeline_mode=one),
            pl.BlockSpec((1, nstate), c2),
            pl.BlockSpec((1, nstate), c2),
            pl.BlockSpec((1, S5_WIDTH), c2),
            pl.BlockSpec((S5_WIDTH, S5_WIDTH), c2),
            pl.BlockSpec((1, S5_WIDTH), c2),
        ],
        out_specs=[
            pl.BlockSpec((1, tb, S5_WIDTH), lambda g, t: (g, t, 0)),
            pl.BlockSpec((1, n_streams, nstate), lambda g, t: (g, 0, 0)),
            pl.BlockSpec((1, n_streams, nstate), lambda g, t: (g, 0, 0)),
        ],
        out_shape=[
            jax.ShapeDtypeStruct((G, T, S5_WIDTH), bf16),
            jax.ShapeDtypeStruct((G, n_streams, nstate), f32),
            jax.ShapeDtypeStruct((G, n_streams, nstate), f32),
        ],
        scratch_shapes=[
            pltpu.VMEM((S5_GROUPS, S5_GL, R), bf16),
            pltpu.VMEM((R, nstate), f32),
            pltpu.VMEM((R, nstate), f32),
            pltpu.VMEM((R, nstate), f32),
            pltpu.VMEM((R, nstate), f32),
            pltpu.VMEM((S5_GROUPS, S5_GL, R), f32),
            pltpu.VMEM((S5_PLANES, tb, LANES), f32),
        ],
        compiler_params=pltpu.CompilerParams(
            dimension_semantics=("arbitrary", "arbitrary"), vmem_limit_bytes=VMEM_LIMIT),
        name="s5",
    )(u, z, xr0, xi0, tmat, wp, vt, ar, ai, dvec, gw, gb)


X_RING = 3


def _out_body(x_hbm, odn_ref, os5_ref, wo_ref, fw_ref, y_ref, xring, sem, *, rows, n_steps):
    i = pl.program_id(0)

    def fetch(step):
        slot = lax.rem(step, X_RING)
        src = x_hbm.at[pl.ds(pl.multiple_of(step * rows, rows), rows), :]
        return pltpu.make_async_copy(src, xring.at[slot], sem.at[slot])

    @pl.when(i == 0)
    def _prime():
        for s in range(min(X_RING - 1, n_steps)):
            fetch(s).start(priority=1)

    @pl.when(i + (X_RING - 1) < n_steps)
    def _ahead():
        fetch(i + (X_RING - 1)).start(priority=1)

    fetch(i).wait()
    acc = (xring[lax.rem(i, X_RING)]
           + jnp.dot(odn_ref[...], wo_ref[0:DN_WIDTH, :], preferred_element_type=f32)
           + jnp.dot(os5_ref[...], wo_ref[DN_WIDTH:, :], preferred_element_type=f32))
    ms = jnp.mean(acc * acc, axis=-1, keepdims=True)
    y_ref[...] = acc * lax.rsqrt(ms + EPS) * fw_ref[...]


def _out_call(x2, odn2, os52, w_out16, fw, *, rows):
    n = x2.shape[0]
    n_steps = n // rows
    return pl.pallas_call(
        functools.partial(_out_body, rows=rows, n_steps=n_steps),
        grid=(n_steps,),
        in_specs=[
            pl.BlockSpec(memory_space=pl.ANY),
            pl.BlockSpec((rows, DN_WIDTH), lambda i: (i, 0)),
            pl.BlockSpec((rows, S5_WIDTH), lambda i: (i, 0)),
            pl.BlockSpec((D_MODEL, D_MODEL), lambda i: (0, 0)),
            pl.BlockSpec((1, D_MODEL), lambda i: (0, 0)),
        ],
        out_specs=pl.BlockSpec((rows, D_MODEL), lambda i: (i, 0)),
        out_shape=jax.ShapeDtypeStruct((n, D_MODEL), f32),
        scratch_shapes=[pltpu.VMEM((X_RING, rows, D_MODEL), f32),
                        pltpu.SemaphoreType.DMA((X_RING,))],
        compiler_params=pltpu.CompilerParams(
            dimension_semantics=("arbitrary",), vmem_limit_bytes=VMEM_LIMIT),
        name="outproj",
    )(x2, odn2, os52, w_out16, fw)


def _s5_operands(a_re, a_im, log_dt, b_re, b_im, c_re, c_im):
    lam_re = jnp.minimum(a_re, -1e-4)
    lam_im = a_im
    dt = jnp.exp(log_dt)[:, None]
    ldt_re, ldt_im = lam_re * dt, lam_im * dt

    def lpow(k):
        mag = jnp.exp(ldt_re * k)
        return mag * jnp.cos(ldt_im * k), mag * jnp.sin(ldt_im * k)

    lb_re, lb_im = lpow(1.0)
    den = lam_re * lam_re + lam_im * lam_im
    f_re = ((lb_re - 1.0) * lam_re + lb_im * lam_im) / den
    f_im = (lb_im * lam_re - (lb_re - 1.0) * lam_im) / den
    bb_re = f_re[..., None] * b_re - f_im[..., None] * b_im
    bb_im = f_re[..., None] * b_im + f_im[..., None] * b_re

    fr = jnp.arange(S5_L, dtype=f32)
    G, N = S5_GROUPS, S5_STATE

    def cmul(ar, ai, br, bi):
        return ar * br - ai * bi, ar * bi + ai * br

    pr, pi = lpow(fr[:, None, None])
    pr, pi = pr.transpose(1, 0, 2), pi.transpose(1, 0, 2)
    cpr, cpi = cmul(c_re[:, None], c_im[:, None], pr[:, :, None], pi[:, :, None])
    cpr, cpi = cpr.reshape(G, S5_GL, N), cpi.reshape(G, S5_GL, N)
    bbr_t, bbi_t = bb_re.transpose(0, 2, 1), bb_im.transpose(0, 2, 1)
    qr, qi = lpow(-fr[:, None, None])
    qr, qi = qr.transpose(1, 0, 2), qi.transpose(1, 0, 2)
    bpr, bpi = cmul(qr[:, :, None], qi[:, :, None], bbr_t[:, None], bbi_t[:, None])
    bpr, bpi = bpr.reshape(G, S5_GL, N), bpi.reshape(G, S5_GL, N)

    wr_, wi_ = lpow((S5_L - 1.0) - fr[:, None, None])
    wr_, wi_ = wr_.transpose(1, 0, 2), wi_.transpose(1, 0, 2)
    w_re, w_im = cmul(wr_[:, :, None], wi_[:, :, None], bbr_t[:, None], bbi_t[:, None])
    w_re, w_im = w_re.reshape(G, S5_GL, N), w_im.reshape(G, S5_GL, N)
    vr_, vi_ = lpow(fr[:, None, None] + 1.0)
    vr_, vi_ = vr_.transpose(1, 0, 2), vi_.transpose(1, 0, 2)
    v_re, v_im = cmul(c_re[:, None], c_im[:, None], vr_[:, :, None], vi_[:, :, None])
    v_re, v_im = v_re.reshape(G, S5_GL, N), v_im.reshape(G, S5_GL, N)

    zeros = jnp.zeros((S5_PAIRS, S5_GL, N), f32)

    def pair_cols(re, im):
        re, im = re.reshape(S5_PAIRS, 2, S5_GL, N), im.reshape(S5_PAIRS, 2, S5_GL, N)
        even = jnp.concatenate([re[:, 0], zeros, im[:, 0], zeros], axis=-1)
        odd = jnp.concatenate([zeros, re[:, 1], zeros, im[:, 1]], axis=-1)
        return jnp.stack([even, odd], axis=1)

    wp = pair_cols(w_re, w_im).reshape(S5_PAIRS, 2 * S5_GL, 2 * LANES).astype(bf16)
    vt = pair_cols(v_re, -v_im).reshape(S5_GROUPS, S5_GL, 2 * LANES).astype(bf16)
    a16r, a16i = lpow(float(S5_L))
    cp = jnp.concatenate([cpr, cpi], axis=2)
    bp = jnp.concatenate([bpr, -bpi], axis=2)
    return cp, bp, wp, vt, a16r.reshape(1, G * N), a16i.reshape(1, G * N)


def _pad_lanes(v, start=0):
    out = jnp.zeros((1, LANES), f32)
    return out.at[0, start:start + v.shape[0]].set(v.astype(f32))


def _layer(x, conv0, s0, xr0, xi0, prm, *, delta_cfg, s5_cfg):
    B, T, _ = x.shape
    conv0p = jnp.concatenate([jnp.zeros((B, SUBLANES - (CONV_K - 1), CONV_CH), f32), conv0], axis=1)
    odn, us5, zs5, convout, s_new = _delta_call(
        x, prm["w1"], prm["w2"], prm["norm_w"], prm["conv_w"], prm["alog"], prm["dtb"], prm["dnw"],
        conv0p, s0, **delta_cfg)

    nstate = S5_GROUPS * S5_STATE
    if s5_cfg["flatten"]:
        tb = s5_cfg["tb"]
        uf = us5.transpose(1, 0, 2, 3).reshape(1, S5_PLANES, B * T, LANES)
        zf = zs5.reshape(1, B * T, S5_WIDTH)
        pad = tb - B * T
        uf = jnp.pad(uf, ((0, 0), (0, 0), (0, pad), (0, 0)))
        zf = jnp.pad(zf, ((0, 0), (0, pad), (0, 0)))
        os5, xr, xi = _s5_call(uf, zf, xr0.reshape(1, B, nstate), xi0.reshape(1, B, nstate),
                               prm["tmat"], prm["wp"], prm["vt"], prm["ar"], prm["ai"], prm["dvec"],
                               prm["gw"], prm["gb"], tb=tb, n_streams=B, n_chunks=T // S5_L)
        os5 = os5[0, :B * T].reshape(B, T, S5_WIDTH)
    else:
        tb = s5_cfg["tb"]
        os5, xr, xi = _s5_call(us5, zs5, xr0.reshape(B, 1, nstate), xi0.reshape(B, 1, nstate),
                               prm["tmat"], prm["wp"], prm["vt"], prm["ar"], prm["ai"], prm["dvec"],
                               prm["gw"], prm["gb"], tb=tb, n_streams=1, n_chunks=tb // S5_L)

    y = _out_call(x.reshape(B * T, D_MODEL), odn.reshape(B * T, DN_WIDTH), os5.reshape(B * T, S5_WIDTH),
                  prm["w_out"], prm["fw"], rows=min(OUT_ROWS, B * T))
    return (y.reshape(B, T, D_MODEL), convout[:, SUBLANES - (CONV_K - 1):, :], s_new,
            xr.reshape(B, S5_GROUPS, S5_STATE), xi.reshape(B, S5_GROUPS, S5_STATE))


def kernel(x_prompt, x_sample, cache_conv, state_dn, state_s5_re, state_s5_im, norm_w, w_in, conv_w, dn_A_log, dn_dt_bias, dn_norm_w, s5_A_re, s5_A_im, s5_log_dt, s5_B_re, s5_B_im, s5_C_re, s5_C_im, s5_D, glu_w, glu_b, w_out, final_norm_w):
    depth = norm_w.shape[0]
    assert depth == 1
    l = 0
    assert w_in.shape[-1] == IN_COLS
    w1 = w_in[l, :, :CONV_CH].astype(bf16)
    w2 = w_in[l, :, IN_COLS - W2_COLS:].astype(bf16)
    cp, bp, wp, vt, ar, ai = _s5_operands(
        s5_A_re[l].astype(f32), s5_A_im[l].astype(f32), s5_log_dt[l].astype(f32),
        s5_B_re[l].astype(f32), s5_B_im[l].astype(f32), s5_C_re[l].astype(f32), s5_C_im[l].astype(f32))
    prm = dict(
        w1=w1, w2=w2,
        norm_w=norm_w[l].reshape(1, D_MODEL).astype(f32),
        conv_w=conv_w[l].astype(f32),
        alog=_pad_lanes(dn_A_log[l], AB_LANE),
        dtb=_pad_lanes(dn_dt_bias[l], AB_LANE),
        dnw=dn_norm_w[l].reshape(1, DN_D).astype(f32),
        tmat=_s5mat_call(cp, bp),
        wp=wp, vt=vt, ar=ar, ai=ai,
        dvec=s5_D[l].reshape(1, S5_WIDTH).astype(f32),
        gw=glu_w[l].astype(bf16),
        gb=glu_b[l].reshape(1, S5_WIDTH).astype(f32),
        w_out=w_out[l].astype(bf16),
        fw=final_norm_w.reshape(1, D_MODEL).astype(f32),
    )

    bp = x_prompt.shape[0]
    yp, c1, d1, r1, i1 = _layer(
        x_prompt,
        jnp.zeros((bp, CONV_K - 1, CONV_CH), f32),
        jnp.zeros((bp, DN_HEADS, DN_D, DN_D), f32),
        jnp.zeros((bp, S5_GROUPS, S5_STATE), f32),
        jnp.zeros((bp, S5_GROUPS, S5_STATE), f32),
        prm,
        delta_cfg=dict(tb=256, pipelined=True),
        s5_cfg=dict(flatten=False, tb=2048))
    ys, c2, d2, r2, i2 = _layer(
        x_sample, cache_conv[l].astype(f32), state_dn[l].astype(f32),
        state_s5_re[l].astype(f32), state_s5_im[l].astype(f32),
        prm,
        delta_cfg=dict(tb=32, pipelined=False),
        s5_cfg=dict(flatten=True, tb=512))

    return (yp, ys, c1[None], d1[None], r1[None], i1[None], c2[None], d2[None], r2[None], i2[None])
```
